```python
import jax, jax.numpy as jnp
from jax import lax
import numpy as np

D_MODEL = 2048
BATCH = 8
SEQ = 8192
DEPTH = 2

CHUNK = 64
N_MIXERS = 2
N_A_LAYERS = (DEPTH + 1) // 2
N_B_LAYERS = DEPTH // 2
EPS = 1e-6

D_FF = 5632

SGU_BLOCK = 128
SGU_WIDTH = 2 * D_MODEL
SGU_GROUPS = 8
SGU_GROUP_DIM = SGU_WIDTH // SGU_GROUPS

MLA_HEADS = 16
Q_LORA = 512
KV_LORA = 512
QK_NOPE = 128
QK_ROPE = 64
V_DIM = 128
QK_DIM = QK_NOPE + QK_ROPE
ROPE_THETA = 10000.0
Q_BLOCK = 128

kernel_name = "hybrid_sgu_mla_macaron_encoder"


def rmsnorm(x, g):
    xf = x.astype(jnp.float32)
    y = xf * lax.rsqrt(jnp.mean(xf * xf, axis=-1, keepdims=True) + EPS)
    return (y * g.astype(jnp.float32)).astype(x.dtype)


def layernorm(x, g, b):
    xf = x.astype(jnp.float32)
    mu = jnp.mean(xf, axis=-1, keepdims=True)
    var = jnp.mean(jnp.square(xf - mu), axis=-1, keepdims=True)
    y = (xf - mu) * lax.rsqrt(var + EPS)
    return (y * g.astype(jnp.float32) + b.astype(jnp.float32)).astype(x.dtype)


def swiglu(h, w_in, w_out):
    gate, up = jnp.split(h @ w_in, 2, axis=-1)
    return (jax.nn.silu(gate) * up) @ w_out


def rope(x, positions):
    half = x.shape[-1] // 2
    inv_freq = 1.0 / (ROPE_THETA ** (jnp.arange(half, dtype=jnp.float32) / half))
    ang = positions.astype(jnp.float32)[..., None] * inv_freq
    cos = jnp.cos(ang)[:, :, None, :]
    sin = jnp.sin(ang)[:, :, None, :]
    xf = x.astype(jnp.float32)
    x1, x2 = xf[..., :half], xf[..., half:]
    out = jnp.concatenate([x1 * cos - x2 * sin, x1 * sin + x2 * cos], axis=-1)
    return out.astype(x.dtype)


def sgu_mixer(h, w_in, v_gain, v_bias, w_spatial, b_spatial, w_out):
    B, S, _ = h.shape
    uv = jax.nn.gelu(h @ w_in)
    u, v = jnp.split(uv, 2, axis=-1)
    v = layernorm(v, v_gain, v_bias)
    nb = S // SGU_BLOCK
    v = v.reshape(B, nb, SGU_BLOCK, SGU_GROUPS, SGU_GROUP_DIM)
    pos_chunk = jnp.arange(SGU_BLOCK) // CHUNK
    mask = pos_chunk[:, None] >= pos_chunk[None, :]
    ws = jnp.where(mask[None], w_spatial, jnp.zeros_like(w_spatial))
    mixed = jnp.einsum('gij,bnjgc->bnigc', ws, v)
    mixed = mixed + b_spatial.T[None, None, :, :, None]
    gated = u * mixed.reshape(B, S, SGU_WIDTH)
    return gated @ w_out


def mla_mixer(h, positions, w_in, q_norm_g, w_q_up, kv_norm_g, w_kv_up, w_out):
    B, S, _ = h.shape
    proj = h @ w_in
    q_lat, kv_lat, k_rope = jnp.split(proj, [Q_LORA, Q_LORA + KV_LORA], axis=-1)
    q = (rmsnorm(q_lat, q_norm_g) @ w_q_up).reshape(B, S, MLA_HEADS, QK_DIM)
    q = jnp.concatenate([q[..., :QK_NOPE], rope(q[..., QK_NOPE:], positions)], axis=-1)
    q = q * (QK_DIM ** -0.5)
    k_rope = rope(k_rope[:, :, None, :], positions)
    kv = (rmsnorm(kv_lat, kv_norm_g) @ w_kv_up).reshape(B, S, MLA_HEADS, QK_NOPE + V_DIM)
    k_nope, v = kv[..., :QK_NOPE], kv[..., QK_NOPE:]
    k = jnp.concatenate(
        [k_nope, jnp.broadcast_to(k_rope, (B, S, MLA_HEADS, QK_ROPE))], axis=-1)

    nq = S // Q_BLOCK
    q_blocks = q.reshape(B, nq, Q_BLOCK, MLA_HEADS, QK_DIM).transpose(1, 0, 2, 3, 4)
    key_chunk = jnp.arange(S) // CHUNK

    def attend(args):
        qb, idx = args
        q_chunk = (idx * Q_BLOCK + jnp.arange(Q_BLOCK)) // CHUNK
        mask = key_chunk[None, :] <= q_chunk[:, None]
        s = jnp.einsum('bqhd,bkhd->bhqk', qb, k).astype(jnp.float32)
        s = jnp.where(mask[None, None], s, -jnp.inf)
        p = jax.nn.softmax(s, axis=-1).astype(v.dtype)
        return jnp.einsum('bhqk,bkhd->bqhd', p, v)

    o = lax.map(attend, (q_blocks, jnp.arange(nq)))
    o = o.transpose(1, 0, 2, 3, 4).reshape(B, S, MLA_HEADS * V_DIM)
    return o @ w_out


def _fwd_setup_inputs(seed: int = 0) -> dict:
    key = jax.random.key(seed)
    ks = jax.random.split(key, 24)

    def nrm(k, shape, scale):
        return jax.random.normal(k, shape, jnp.float32) * scale

    def gain(k, shape):
        return 1.0 + 0.02 * jax.random.normal(k, shape, jnp.float32)

    x = jax.random.normal(ks[0], (BATCH, SEQ, D_MODEL), jnp.float32)
    offset = jax.random.randint(ks[1], (BATCH, 1), 0, 4096, dtype=jnp.int32)
    positions = offset + jnp.arange(SEQ, dtype=jnp.int32)[None, :]
    return {
        "x": x,
        "positions": positions,
        "ln_ffn1": gain(ks[2], (DEPTH, D_MODEL)),
        "ffn1_w_in": nrm(ks[3], (DEPTH, D_MODEL, 2 * D_FF), D_MODEL ** -0.5),
        "ffn1_w_out": nrm(ks[4], (DEPTH, D_FF, D_MODEL), D_FF ** -0.5),
        "ln_mix": gain(ks[5], (DEPTH, D_MODEL)),
        "ln_ffn2": gain(ks[6], (DEPTH, D_MODEL)),
        "ffn2_w_in": nrm(ks[7], (DEPTH, D_MODEL, 2 * D_FF), D_MODEL ** -0.5),
        "ffn2_w_out": nrm(ks[8], (DEPTH, D_FF, D_MODEL), D_FF ** -0.5),
        "sgu_w_in": nrm(ks[9], (N_A_LAYERS, D_MODEL, 2 * SGU_WIDTH), D_MODEL ** -0.5),
        "sgu_v_gain": gain(ks[10], (N_A_LAYERS, SGU_WIDTH)),
        "sgu_v_bias": nrm(ks[11], (N_A_LAYERS, SGU_WIDTH), 0.02),
        "sgu_w_spatial": nrm(ks[12], (N_A_LAYERS, SGU_GROUPS, SGU_BLOCK, SGU_BLOCK), SGU_BLOCK ** -0.5),
        "sgu_b_spatial": gain(ks[13], (N_A_LAYERS, SGU_GROUPS, SGU_BLOCK)),
        "sgu_w_out": nrm(ks[14], (N_A_LAYERS, SGU_WIDTH, D_MODEL), SGU_WIDTH ** -0.5),
        "mla_w_in": nrm(ks[15], (N_B_LAYERS, D_MODEL, Q_LORA + KV_LORA + QK_ROPE), D_MODEL ** -0.5),
        "mla_q_norm": gain(ks[16], (N_B_LAYERS, Q_LORA)),
        "mla_w_q_up": nrm(ks[17], (N_B_LAYERS, Q_LORA, MLA_HEADS * QK_DIM), Q_LORA ** -0.5),
        "mla_kv_norm": gain(ks[18], (N_B_LAYERS, KV_LORA)),
        "mla_w_kv_up": nrm(ks[19], (N_B_LAYERS, KV_LORA, MLA_HEADS * (QK_NOPE + V_DIM)), KV_LORA ** -0.5),
        "mla_w_out": nrm(ks[20], (N_B_LAYERS, MLA_HEADS * V_DIM, D_MODEL), (MLA_HEADS * V_DIM) ** -0.5),
        "ln_final": gain(ks[21], (D_MODEL,)),
    }


def _fwd_reference(x, positions, ln_ffn1, ffn1_w_in, ffn1_w_out, ln_mix, ln_ffn2, ffn2_w_in, ffn2_w_out,
              sgu_w_in, sgu_v_gain, sgu_v_bias, sgu_w_spatial, sgu_b_spatial, sgu_w_out,
              mla_w_in, mla_q_norm, mla_w_q_up, mla_kv_norm, mla_w_kv_up, mla_w_out, ln_final):
    for i in range(DEPTH):
        x = x + 0.5 * swiglu(rmsnorm(x, ln_ffn1[i]), ffn1_w_in[i], ffn1_w_out[i])
        h = rmsnorm(x, ln_mix[i])
        j = i // N_MIXERS
        if i % N_MIXERS == 0:
            x = x + sgu_mixer(h, sgu_w_in[j], sgu_v_gain[j], sgu_v_bias[j],
                              sgu_w_spatial[j], sgu_b_spatial[j], sgu_w_out[j])
        else:
            x = x + mla_mixer(h, positions, mla_w_in[j], mla_q_norm[j], mla_w_q_up[j],
                              mla_kv_norm[j], mla_w_kv_up[j], mla_w_out[j])
        x = x + 0.5 * swiglu(rmsnorm(x, ln_ffn2[i]), ffn2_w_in[i], ffn2_w_out[i])
    return rmsnorm(x, ln_final)


import jax as _jax
import jax.numpy as _jnp

TWIN_FORMAT = 'train_step'
FWD_PARAMS = ['x', 'positions', 'ln_ffn1', 'ffn1_w_in', 'ffn1_w_out', 'ln_mix', 'ln_ffn2', 'ffn2_w_in', 'ffn2_w_out', 'sgu_w_in', 'sgu_v_gain', 'sgu_v_bias', 'sgu_w_spatial', 'sgu_b_spatial', 'sgu_w_out', 'mla_w_in', 'mla_q_norm', 'mla_w_q_up', 'mla_kv_norm', 'mla_w_kv_up', 'mla_w_out', 'ln_final']
TWIN_WEIGHTS = ['ln_ffn1', 'ffn1_w_in', 'ffn1_w_out', 'ln_mix', 'ln_ffn2', 'ffn2_w_in', 'ffn2_w_out', 'sgu_w_in', 'sgu_v_gain', 'sgu_v_bias', 'sgu_w_spatial', 'sgu_b_spatial', 'sgu_w_out', 'mla_w_in', 'mla_q_norm', 'mla_w_q_up', 'mla_kv_norm', 'mla_w_kv_up', 'mla_w_out', 'ln_final']
TWIN_DIFF_INPUT = 'x'
TWIN_INPUTS = ['x', 'positions', 'ln_ffn1', 'ffn1_w_in', 'ffn1_w_out', 'ln_mix', 'ln_ffn2', 'ffn2_w_in', 'ffn2_w_out', 'sgu_w_in', 'sgu_v_gain', 'sgu_v_bias', 'sgu_w_spatial', 'sgu_b_spatial', 'sgu_w_out', 'mla_w_in', 'mla_q_norm', 'mla_w_q_up', 'mla_kv_norm', 'mla_w_kv_up', 'mla_w_out', 'ln_final', 'loss_target', 'm_ln_ffn1', 'm_ffn1_w_in', 'm_ffn1_w_out', 'm_ln_mix', 'm_ln_ffn2', 'm_ffn2_w_in', 'm_ffn2_w_out', 'm_sgu_w_in', 'm_sgu_v_gain', 'm_sgu_v_bias', 'm_sgu_w_spatial', 'm_sgu_b_spatial', 'm_sgu_w_out', 'm_mla_w_in', 'm_mla_q_norm', 'm_mla_w_q_up', 'm_mla_kv_norm', 'm_mla_w_kv_up', 'm_mla_w_out', 'm_ln_final', 'v_ln_ffn1', 'v_ffn1_w_in', 'v_ffn1_w_out', 'v_ln_mix', 'v_ln_ffn2', 'v_ffn2_w_in', 'v_ffn2_w_out', 'v_sgu_w_in', 'v_sgu_v_gain', 'v_sgu_v_bias', 'v_sgu_w_spatial', 'v_sgu_b_spatial', 'v_sgu_w_out', 'v_mla_w_in', 'v_mla_q_norm', 'v_mla_w_q_up', 'v_mla_kv_norm', 'v_mla_w_kv_up', 'v_mla_w_out', 'v_ln_final']
TWIN_OUTPUTS = ['loss', 'grad_x', 'grad_ln_ffn1', 'grad_ffn1_w_in', 'grad_ffn1_w_out', 'grad_ln_mix', 'grad_ln_ffn2', 'grad_ffn2_w_in', 'grad_ffn2_w_out', 'grad_sgu_w_in', 'grad_sgu_v_gain', 'grad_sgu_v_bias', 'grad_sgu_w_spatial', 'grad_sgu_b_spatial', 'grad_sgu_w_out', 'grad_mla_w_in', 'grad_mla_q_norm', 'grad_mla_w_q_up', 'grad_mla_kv_norm', 'grad_mla_w_kv_up', 'grad_mla_w_out', 'grad_ln_final', 'delta_ln_ffn1', 'delta_ffn1_w_in', 'delta_ffn1_w_out', 'delta_ln_mix', 'delta_ln_ffn2', 'delta_ffn2_w_in', 'delta_ffn2_w_out', 'delta_sgu_w_in', 'delta_sgu_v_gain', 'delta_sgu_v_bias', 'delta_sgu_w_spatial', 'delta_sgu_b_spatial', 'delta_sgu_w_out', 'delta_mla_w_in', 'delta_mla_q_norm', 'delta_mla_w_q_up', 'delta_mla_kv_norm', 'delta_mla_w_kv_up', 'delta_mla_w_out', 'delta_ln_final', 'new_m_ln_ffn1', 'new_m_ffn1_w_in', 'new_m_ffn1_w_out', 'new_m_ln_mix', 'new_m_ln_ffn2', 'new_m_ffn2_w_in', 'new_m_ffn2_w_out', 'new_m_sgu_w_in', 'new_m_sgu_v_gain', 'new_m_sgu_v_bias', 'new_m_sgu_w_spatial', 'new_m_sgu_b_spatial', 'new_m_sgu_w_out', 'new_m_mla_w_in', 'new_m_mla_q_norm', 'new_m_mla_w_q_up', 'new_m_mla_kv_norm', 'new_m_mla_w_kv_up', 'new_m_mla_w_out', 'new_m_ln_final', 'new_v_ln_ffn1', 'new_v_ffn1_w_in', 'new_v_ffn1_w_out', 'new_v_ln_mix', 'new_v_ln_ffn2', 'new_v_ffn2_w_in', 'new_v_ffn2_w_out', 'new_v_sgu_w_in', 'new_v_sgu_v_gain', 'new_v_sgu_v_bias', 'new_v_sgu_w_spatial', 'new_v_sgu_b_spatial', 'new_v_sgu_w_out', 'new_v_mla_w_in', 'new_v_mla_q_norm', 'new_v_mla_w_q_up', 'new_v_mla_kv_norm', 'new_v_mla_w_kv_up', 'new_v_mla_w_out', 'new_v_ln_final']
TWIN_LEAF_KINDS = {'loss': 'loss', 'grad_x': 'grad_x', 'grad_ln_ffn1': 'grad_w', 'grad_ffn1_w_in': 'grad_w', 'grad_ffn1_w_out': 'grad_w', 'grad_ln_mix': 'grad_w', 'grad_ln_ffn2': 'grad_w', 'grad_ffn2_w_in': 'grad_w', 'grad_ffn2_w_out': 'grad_w', 'grad_sgu_w_in': 'grad_w', 'grad_sgu_v_gain': 'grad_w', 'grad_sgu_v_bias': 'grad_w', 'grad_sgu_w_spatial': 'grad_w', 'grad_sgu_b_spatial': 'grad_w', 'grad_sgu_w_out': 'grad_w', 'grad_mla_w_in': 'grad_w', 'grad_mla_q_norm': 'grad_w', 'grad_mla_w_q_up': 'grad_w', 'grad_mla_kv_norm': 'grad_w', 'grad_mla_w_kv_up': 'grad_w', 'grad_mla_w_out': 'grad_w', 'grad_ln_final': 'grad_w', 'delta_ln_ffn1': 'delta_w', 'delta_ffn1_w_in': 'delta_w', 'delta_ffn1_w_out': 'delta_w', 'delta_ln_mix': 'delta_w', 'delta_ln_ffn2': 'delta_w', 'delta_ffn2_w_in': 'delta_w', 'delta_ffn2_w_out': 'delta_w', 'delta_sgu_w_in': 'delta_w', 'delta_sgu_v_gain': 'delta_w', 'delta_sgu_v_bias': 'delta_w', 'delta_sgu_w_spatial': 'delta_w', 'delta_sgu_b_spatial': 'delta_w', 'delta_sgu_w_out': 'delta_w', 'delta_mla_w_in': 'delta_w', 'delta_mla_q_norm': 'delta_w', 'delta_mla_w_q_up': 'delta_w', 'delta_mla_kv_norm': 'delta_w', 'delta_mla_w_kv_up': 'delta_w', 'delta_mla_w_out': 'delta_w', 'delta_ln_final': 'delta_w', 'new_m_ln_ffn1': 'new_m', 'new_m_ffn1_w_in': 'new_m', 'new_m_ffn1_w_out': 'new_m', 'new_m_ln_mix': 'new_m', 'new_m_ln_ffn2': 'new_m', 'new_m_ffn2_w_in': 'new_m', 'new_m_ffn2_w_out': 'new_m', 'new_m_sgu_w_in': 'new_m', 'new_m_sgu_v_gain': 'new_m', 'new_m_sgu_v_bias': 'new_m', 'new_m_sgu_w_spatial': 'new_m', 'new_m_sgu_b_spatial': 'new_m', 'new_m_sgu_w_out': 'new_m', 'new_m_mla_w_in': 'new_m', 'new_m_mla_q_norm': 'new_m', 'new_m_mla_w_q_up': 'new_m', 'new_m_mla_kv_norm': 'new_m', 'new_m_mla_w_kv_up': 'new_m', 'new_m_mla_w_out': 'new_m', 'new_m_ln_final': 'new_m', 'new_v_ln_ffn1': 'new_v', 'new_v_ffn1_w_in': 'new_v', 'new_v_ffn1_w_out': 'new_v', 'new_v_ln_mix': 'new_v', 'new_v_ln_ffn2': 'new_v', 'new_v_ffn2_w_in': 'new_v', 'new_v_ffn2_w_out': 'new_v', 'new_v_sgu_w_in': 'new_v', 'new_v_sgu_v_gain': 'new_v', 'new_v_sgu_v_bias': 'new_v', 'new_v_sgu_w_spatial': 'new_v', 'new_v_sgu_b_spatial': 'new_v', 'new_v_sgu_w_out': 'new_v', 'new_v_mla_w_in': 'new_v', 'new_v_mla_q_norm': 'new_v', 'new_v_mla_w_q_up': 'new_v', 'new_v_mla_kv_norm': 'new_v', 'new_v_mla_w_kv_up': 'new_v', 'new_v_mla_w_out': 'new_v', 'new_v_ln_final': 'new_v'}


def _forward(args):
    return _fwd_reference(*[args[k] for k in FWD_PARAMS])


def _output_shape():
    def fwd():
        inp = _fwd_setup_inputs(0)
        return _fwd_reference(*[inp[k] for k in FWD_PARAMS])
    out = _jax.eval_shape(fwd)
    return out.shape, out.dtype

N_MICROBATCH = 1
ADAM_LR = 0.001
ADAM_B1 = 0.9
ADAM_B2 = 0.999
ADAM_EPS = 1e-08
ADAM_WD = 0.01
ADAM_STEP = 10
PER_EXAMPLE_BATCH_AXIS = {'x': 0, 'positions': 0, 'loss_target': 0}
SHARED_INPUTS = []
_WEIGHT_DTYPES = {'ln_ffn1': _jnp.float32, 'ffn1_w_in': _jnp.float32, 'ffn1_w_out': _jnp.float32, 'ln_mix': _jnp.float32, 'ln_ffn2': _jnp.float32, 'ffn2_w_in': _jnp.float32, 'ffn2_w_out': _jnp.float32, 'sgu_w_in': _jnp.float32, 'sgu_v_gain': _jnp.float32, 'sgu_v_bias': _jnp.float32, 'sgu_w_spatial': _jnp.float32, 'sgu_b_spatial': _jnp.float32, 'sgu_w_out': _jnp.float32, 'mla_w_in': _jnp.float32, 'mla_q_norm': _jnp.float32, 'mla_w_q_up': _jnp.float32, 'mla_kv_norm': _jnp.float32, 'mla_w_kv_up': _jnp.float32, 'mla_w_out': _jnp.float32, 'ln_final': _jnp.float32}
MOMENT_SCALE = {'ln_ffn1': 5.213827e-02, 'ffn1_w_in': 2.201190e-02, 'ffn1_w_out': 3.589465e-02, 'ln_mix': 8.184036e-02, 'ln_ffn2': 3.947480e-02, 'ffn2_w_in': 1.655470e-02, 'ffn2_w_out': 2.715124e-02, 'sgu_w_in': 5.443742e-02, 'sgu_v_gain': 3.905636e-02, 'sgu_v_bias': 4.054088e-02, 'sgu_w_spatial': 7.826726e-02, 'sgu_b_spatial': 9.238088e-02, 'sgu_w_out': 9.630223e-02, 'mla_w_in': 3.837131e-02, 'mla_q_norm': 2.456188e-02, 'mla_w_q_up': 9.983328e-03, 'mla_kv_norm': 6.332272e-02, 'mla_w_kv_up': 2.012154e-02, 'mla_w_out': 2.789537e-02, 'ln_final': 3.210114e+01}


def _to_microbatches(a, axis):
    t = _jnp.moveaxis(a, axis, 0)
    t = t.reshape((N_MICROBATCH, t.shape[0] // N_MICROBATCH) + t.shape[1:])
    return _jnp.moveaxis(t, 1, axis + 1)


def setup_inputs(seed: int = 0) -> dict:
    inp = _fwd_setup_inputs(seed)
    key = _jax.random.fold_in(_jax.random.key(seed), 7919)
    shape, _ = _output_shape()
    out = dict(inp)
    out["loss_target"] = _jax.random.normal(_jax.random.fold_in(key, 0), shape, _jnp.float32)
    for i, name in enumerate(TWIN_WEIGHTS):
        w = inp[name].astype(_jnp.float32)
        if MOMENT_SCALE is None:
            s = _jnp.sqrt(_jnp.mean(_jnp.square(w)) + 1e-30)
        else:
            s = MOMENT_SCALE[name]
        km, kv = _jax.random.split(_jax.random.fold_in(key, i + 1))
        out[name] = w
        out["m_" + name] = s * _jax.random.normal(km, w.shape, _jnp.float32)
        out["v_" + name] = (s * s) * _jax.random.uniform(kv, w.shape, _jnp.float32, 0.5, 1.5)
    if N_MICROBATCH > 1:
        for name, axis in PER_EXAMPLE_BATCH_AXIS.items():
            out[name] = _to_microbatches(out[name], axis)
    return {'x': out['x'], 'positions': out['positions'], 'ln_ffn1': out['ln_ffn1'], 'ffn1_w_in': out['ffn1_w_in'], 'ffn1_w_out': out['ffn1_w_out'], 'ln_mix': out['ln_mix'], 'ln_ffn2': out['ln_ffn2'], 'ffn2_w_in': out['ffn2_w_in'], 'ffn2_w_out': out['ffn2_w_out'], 'sgu_w_in': out['sgu_w_in'], 'sgu_v_gain': out['sgu_v_gain'], 'sgu_v_bias': out['sgu_v_bias'], 'sgu_w_spatial': out['sgu_w_spatial'], 'sgu_b_spatial': out['sgu_b_spatial'], 'sgu_w_out': out['sgu_w_out'], 'mla_w_in': out['mla_w_in'], 'mla_q_norm': out['mla_q_norm'], 'mla_w_q_up': out['mla_w_q_up'], 'mla_kv_norm': out['mla_kv_norm'], 'mla_w_kv_up': out['mla_w_kv_up'], 'mla_w_out': out['mla_w_out'], 'ln_final': out['ln_final'], 'loss_target': out['loss_target'], 'm_ln_ffn1': out['m_ln_ffn1'], 'm_ffn1_w_in': out['m_ffn1_w_in'], 'm_ffn1_w_out': out['m_ffn1_w_out'], 'm_ln_mix': out['m_ln_mix'], 'm_ln_ffn2': out['m_ln_ffn2'], 'm_ffn2_w_in': out['m_ffn2_w_in'], 'm_ffn2_w_out': out['m_ffn2_w_out'], 'm_sgu_w_in': out['m_sgu_w_in'], 'm_sgu_v_gain': out['m_sgu_v_gain'], 'm_sgu_v_bias': out['m_sgu_v_bias'], 'm_sgu_w_spatial': out['m_sgu_w_spatial'], 'm_sgu_b_spatial': out['m_sgu_b_spatial'], 'm_sgu_w_out': out['m_sgu_w_out'], 'm_mla_w_in': out['m_mla_w_in'], 'm_mla_q_norm': out['m_mla_q_norm'], 'm_mla_w_q_up': out['m_mla_w_q_up'], 'm_mla_kv_norm': out['m_mla_kv_norm'], 'm_mla_w_kv_up': out['m_mla_w_kv_up'], 'm_mla_w_out': out['m_mla_w_out'], 'm_ln_final': out['m_ln_final'], 'v_ln_ffn1': out['v_ln_ffn1'], 'v_ffn1_w_in': out['v_ffn1_w_in'], 'v_ffn1_w_out': out['v_ffn1_w_out'], 'v_ln_mix': out['v_ln_mix'], 'v_ln_ffn2': out['v_ln_ffn2'], 'v_ffn2_w_in': out['v_ffn2_w_in'], 'v_ffn2_w_out': out['v_ffn2_w_out'], 'v_sgu_w_in': out['v_sgu_w_in'], 'v_sgu_v_gain': out['v_sgu_v_gain'], 'v_sgu_v_bias': out['v_sgu_v_bias'], 'v_sgu_w_spatial': out['v_sgu_w_spatial'], 'v_sgu_b_spatial': out['v_sgu_b_spatial'], 'v_sgu_w_out': out['v_sgu_w_out'], 'v_mla_w_in': out['v_mla_w_in'], 'v_mla_q_norm': out['v_mla_q_norm'], 'v_mla_w_q_up': out['v_mla_w_q_up'], 'v_mla_kv_norm': out['v_mla_kv_norm'], 'v_mla_w_kv_up': out['v_mla_w_kv_up'], 'v_mla_w_out': out['v_mla_w_out'], 'v_ln_final': out['v_ln_final']}


def _loss(weights, diff, rest, loss_target):
    with _jax.named_scope("forward"):
        args = {**rest, TWIN_DIFF_INPUT: diff, **{k: w.astype(_WEIGHT_DTYPES[k]) for k, w in weights.items()}}
        y = _forward(args)
    with _jax.named_scope("loss_head"):
        err = _jnp.square(y.astype(_jnp.float32) - loss_target)
        return 0.5 * _jnp.sum(_jnp.mean(err, axis=-1)) if err.ndim else 0.5 * err


def _adamw(w, g, m, v):
    m = ADAM_B1 * m + (1.0 - ADAM_B1) * g
    v = ADAM_B2 * v + (1.0 - ADAM_B2) * _jnp.square(g)
    m_hat = m / (1.0 - ADAM_B1 ** ADAM_STEP)
    v_hat = v / (1.0 - ADAM_B2 ** ADAM_STEP)
    delta = -ADAM_LR * (m_hat / (_jnp.sqrt(v_hat) + ADAM_EPS) + ADAM_WD * w)
    return delta, m, v


def reference(x, positions, ln_ffn1, ffn1_w_in, ffn1_w_out, ln_mix, ln_ffn2, ffn2_w_in, ffn2_w_out, sgu_w_in, sgu_v_gain, sgu_v_bias, sgu_w_spatial, sgu_b_spatial, sgu_w_out, mla_w_in, mla_q_norm, mla_w_q_up, mla_kv_norm, mla_w_kv_up, mla_w_out, ln_final, loss_target, m_ln_ffn1, m_ffn1_w_in, m_ffn1_w_out, m_ln_mix, m_ln_ffn2, m_ffn2_w_in, m_ffn2_w_out, m_sgu_w_in, m_sgu_v_gain, m_sgu_v_bias, m_sgu_w_spatial, m_sgu_b_spatial, m_sgu_w_out, m_mla_w_in, m_mla_q_norm, m_mla_w_q_up, m_mla_kv_norm, m_mla_w_kv_up, m_mla_w_out, m_ln_final, v_ln_ffn1, v_ffn1_w_in, v_ffn1_w_out, v_ln_mix, v_ln_ffn2, v_ffn2_w_in, v_ffn2_w_out, v_sgu_w_in, v_sgu_v_gain, v_sgu_v_bias, v_sgu_w_spatial, v_sgu_b_spatial, v_sgu_w_out, v_mla_w_in, v_mla_q_norm, v_mla_w_q_up, v_mla_kv_norm, v_mla_w_kv_up, v_mla_w_out, v_ln_final):
    given = dict(x=x, positions=positions, ln_ffn1=ln_ffn1, ffn1_w_in=ffn1_w_in, ffn1_w_out=ffn1_w_out, ln_mix=ln_mix, ln_ffn2=ln_ffn2, ffn2_w_in=ffn2_w_in, ffn2_w_out=ffn2_w_out, sgu_w_in=sgu_w_in, sgu_v_gain=sgu_v_gain, sgu_v_bias=sgu_v_bias, sgu_w_spatial=sgu_w_spatial, sgu_b_spatial=sgu_b_spatial, sgu_w_out=sgu_w_out, mla_w_in=mla_w_in, mla_q_norm=mla_q_norm, mla_w_q_up=mla_w_q_up, mla_kv_norm=mla_kv_norm, mla_w_kv_up=mla_w_kv_up, mla_w_out=mla_w_out, ln_final=ln_final, loss_target=loss_target, m_ln_ffn1=m_ln_ffn1, m_ffn1_w_in=m_ffn1_w_in, m_ffn1_w_out=m_ffn1_w_out, m_ln_mix=m_ln_mix, m_ln_ffn2=m_ln_ffn2, m_ffn2_w_in=m_ffn2_w_in, m_ffn2_w_out=m_ffn2_w_out, m_sgu_w_in=m_sgu_w_in, m_sgu_v_gain=m_sgu_v_gain, m_sgu_v_bias=m_sgu_v_bias, m_sgu_w_spatial=m_sgu_w_spatial, m_sgu_b_spatial=m_sgu_b_spatial, m_sgu_w_out=m_sgu_w_out, m_mla_w_in=m_mla_w_in, m_mla_q_norm=m_mla_q_norm, m_mla_w_q_up=m_mla_w_q_up, m_mla_kv_norm=m_mla_kv_norm, m_mla_w_kv_up=m_mla_w_kv_up, m_mla_w_out=m_mla_w_out, m_ln_final=m_ln_final, v_ln_ffn1=v_ln_ffn1, v_ffn1_w_in=v_ffn1_w_in, v_ffn1_w_out=v_ffn1_w_out, v_ln_mix=v_ln_mix, v_ln_ffn2=v_ln_ffn2, v_ffn2_w_in=v_ffn2_w_in, v_ffn2_w_out=v_ffn2_w_out, v_sgu_w_in=v_sgu_w_in, v_sgu_v_gain=v_sgu_v_gain, v_sgu_v_bias=v_sgu_v_bias, v_sgu_w_spatial=v_sgu_w_spatial, v_sgu_b_spatial=v_sgu_b_spatial, v_sgu_w_out=v_sgu_w_out, v_mla_w_in=v_mla_w_in, v_mla_q_norm=v_mla_q_norm, v_mla_w_q_up=v_mla_w_q_up, v_mla_kv_norm=v_mla_kv_norm, v_mla_w_kv_up=v_mla_w_kv_up, v_mla_w_out=v_mla_w_out, v_ln_final=v_ln_final)
    weights = {n: given[n] for n in TWIN_WEIGHTS}
    shared = {n: given[n] for n in SHARED_INPUTS}
    per_example = {n: given[n] for n in ['x', 'positions']}
    grad_fn = _jax.value_and_grad(_loss, argnums=(0, 1))

    def one_microbatch(ex, loss_target):
        ex = dict(ex)
        diff = ex.pop(TWIN_DIFF_INPUT)
        return grad_fn(weights, diff, {**shared, **ex}, loss_target)

    if N_MICROBATCH == 1:
        loss, (grad_w, grad_x) = one_microbatch(per_example, given["loss_target"])
    else:
        def body(carry, xs):
            loss_sum, grad_sum = carry
            l_k, (gw_k, gx_k) = one_microbatch(xs[0], xs[1])
            with _jax.named_scope("update"):
                return (loss_sum + l_k, _jax.tree.map(_jnp.add, grad_sum, gw_k)), gx_k

        init = (_jnp.zeros((), _jnp.float32), _jax.tree.map(_jnp.zeros_like, weights))
        (loss, grad_w), grad_x = _jax.lax.scan(body, init, (per_example, given["loss_target"]))
    with _jax.named_scope("update"):
        delta_w, new_m, new_v = {}, {}, {}
        for n in TWIN_WEIGHTS:
            delta_w[n], new_m[n], new_v[n] = _adamw(weights[n], grad_w[n], given["m_" + n], given["v_" + n])
    return (loss, grad_x, *[grad_w[n] for n in TWIN_WEIGHTS], *[delta_w[n] for n in TWIN_WEIGHTS],
            *[new_m[n] for n in TWIN_WEIGHTS], *[new_v[n] for n in TWIN_WEIGHTS])
```

```python
import functools
import math

import jax
import jax.numpy as jnp
from jax import lax
from jax.experimental import pallas as pl
from jax.experimental.pallas import tpu as pltpu

F32 = jnp.float32
BF = jnp.bfloat16
MESH = pl.DeviceIdType.MESH

N_DEV = 8
EPS = 1e-6
CHUNK = 64
SGU_BLOCK = 128
SGU_GROUPS = 8
Q_LORA = 512
KV_LORA = 512
QK_NOPE = 128
QK_ROPE = 64
V_DIM = 128
ROPE_THETA = 10000.0
HEAD_PAD = 128
LANES = 128
ADAM_LR = 0.001
ADAM_B1 = 0.9
ADAM_B2 = 0.999
ADAM_EPS = 1e-08
ADAM_WD = 0.01
ADAM_STEP = 10
V7X_VMEM_BYTES = 64 * 1024 * 1024
VMEM_CAP = V7X_VMEM_BYTES - 6 * 1024 * 1024
VMEM_FLOOR = 32 * 1024 * 1024
NEG = -1e30


def _pick(n, cands):
    for c in cands:
        if n % c == 0:
            return c
    return n


def _nbytes(shape, dtype):
    return math.prod(int(s) for s in shape if s is not None) * jnp.dtype(dtype).itemsize


def _params(sem, block_bytes):
    limit = int(min(VMEM_CAP, max(VMEM_FLOOR, block_bytes)))
    return pltpu.CompilerParams(dimension_semantics=sem, vmem_limit_bytes=limit)


def _spec(shape, fn):
    return pl.BlockSpec(shape, fn)


def _mm(name, grid, ops, pairs, extras, outs, epilogue, acc_shapes, alias=None):
    nk = grid[2]
    n_ops, n_ex, n_out = len(ops), len(extras), len(outs)

    def load(refs, idx):
        loader = ops[idx][2] if len(ops[idx]) > 2 else None
        return (refs[idx][...] if loader is None else loader(refs[idx])).astype(BF)

    def prod(refs, p):
        ia, ib, ta, tb, _ = p
        a = load(refs, ia)
        b = load(refs, ib)
        dims = (((0 if ta else 1,), (1 if tb else 0,)), ((), ()))
        return lax.dot_general(a, b, dims, preferred_element_type=F32)

    def body(*refs):
        op_refs = refs[:n_ops]
        ex_refs = refs[n_ops:n_ops + n_ex]
        n_in = n_ops + n_ex + (1 if alias is not None else 0)
        out_refs = refs[n_in:n_in + n_out]
        acc_refs = refs[n_in + n_out:]
        ids = (pl.program_id(0), pl.program_id(1))

        def finish(vals):
            epilogue(vals, [e[...] for e in ex_refs], out_refs, ids)

        if nk == 1:
            vals = [None] * len(acc_shapes)
            for p in pairs:
                r = prod(op_refs, p)
                vals[p[4]] = r if vals[p[4]] is None else vals[p[4]] + r
            finish(vals)
        else:
            k = pl.program_id(2)

            @pl.when(k == 0)
            def _():
                for a in acc_refs:
                    a[...] = jnp.zeros_like(a)

            for p in pairs:
                acc_refs[p[4]][...] += prod(op_refs, p)

            @pl.when(k == nk - 1)
            def _():
                finish([a[...] for a in acc_refs])

    in_arrays = [o[0] for o in ops] + [e[0] for e in extras]
    in_specs = [o[1] for o in ops] + [e[1] for e in extras]
    kwargs = {}
    if alias is not None:
        in_arrays.append(alias[0])
        in_specs.append(pl.BlockSpec(memory_space=pl.ANY))
        kwargs["input_output_aliases"] = {len(in_arrays) - 1: alias[1]}
    blk = 0
    for entry in ops + extras:
        blk += 2 * _nbytes(entry[1].block_shape, entry[0].dtype)
    for sd, sp in outs:
        blk += 2 * _nbytes(sp.block_shape, sd.dtype)
    acc_b = sum(_nbytes(s, F32) for s in acc_shapes)
    blk += 6 * acc_b
    scratch = [pltpu.VMEM(s, F32) for s in acc_shapes] if nk > 1 else []
    res = pl.pallas_call(
        body, name=name, grid=grid, in_specs=in_specs,
        out_specs=[o[1] for o in outs], out_shape=[o[0] for o in outs],
        scratch_shapes=scratch,
        compiler_params=_params(("parallel", "parallel", "arbitrary"), blk),
        **kwargs)(*in_arrays)
    return res


def _store(scale=None):
    def epi(accs, ex, outs, ids):
        v = accs[0]
        if scale is not None:
            v = v * scale
        outs[0][...] = v.astype(outs[0].dtype)
    return epi


def _store_residual(scale):
    def epi(accs, ex, outs, ids):
        outs[0][...] = ex[0] + scale * accs[0]
    return epi


def _sd(shape, dtype):
    return jax.ShapeDtypeStruct(tuple(shape), dtype)


def _pair_in(name, h, w_sm, l, out_dtype, act):
    S, D = h.shape
    c = w_sm.shape[-1]
    tm = _pick(S, (256, 128))
    half = N_DEV // 2
    ops = [(h, _spec((tm, D), lambda j, i, k: (i, 0))),
           (w_sm, _spec((None, None, D, c), lambda j, i, k: (j, l, 0, 0))),
           (w_sm, _spec((None, None, D, c), lambda j, i, k: (j + half, l, 0, 0)))]
    outs = [(_sd((2, S, half * c), out_dtype), _spec((2, tm, c), lambda j, i, k: (0, i, j)))]
    if act is not None:
        outs.append((_sd((S, half * c), BF), _spec((tm, c), lambda j, i, k: (i, j))))

    def epi(accs, ex, orefs, ids):
        orefs[0][0] = accs[0].astype(out_dtype)
        orefs[0][1] = accs[1].astype(out_dtype)
        if act is not None:
            orefs[1][...] = act(accs[0], accs[1]).astype(BF)

    return _mm(name, (half, S // tm, 1), ops, [(0, 1, False, False, 0), (0, 2, False, False, 1)], [], outs, epi,
               [(tm, c), (tm, c)])


def _two_slabs(ref):
    return jnp.concatenate([ref[0], ref[1]], axis=0)


def _rows_out(name, a, w_sm, l, res, scale):
    S = a.shape[0]
    r, D = w_sm.shape[-2], w_sm.shape[-1]
    tm = _pick(S, (1024, 512, 256, 128))
    tn = _pick(D, (1024, 512, 256, 128))
    ops = [(a, _spec((tm, 2 * r), lambda i, j, k: (i, k))),
           (w_sm, _spec((2, None, r, tn), lambda i, j, k: (k, l, 0, j)), _two_slabs)]
    extras = [(res, _spec((tm, tn), lambda i, j, k: (i, j)))]
    outs = [(_sd((S, D), F32), _spec((tm, tn), lambda i, j, k: (i, j)))]
    return _mm(name, (S // tm, D // tn, N_DEV // 2), ops, [(0, 1, False, False, 0)], extras, outs,
               _store_residual(scale), [(tm, tn)])[0]


def _rows_dact(name, d_bf, w_sm, l, extras_arrays, out_shapes, epi):
    S, D = d_bf.shape
    r = w_sm.shape[-2]
    tm = _pick(S, (1024, 512, 256, 128))
    ops = [(d_bf, _spec((tm, D), lambda j, i, k: (i, 0))),
           (w_sm, _spec((2, None, r, D), lambda j, i, k: (j, l, 0, 0)), _two_slabs)]
    extras = []
    for arr in extras_arrays:
        if arr.ndim == 3:
            extras.append((arr, _spec((arr.shape[0], tm, 2 * r), lambda j, i, k: (0, i, j))))
        else:
            extras.append((arr, _spec((tm, 2 * r), lambda j, i, k: (i, j))))
    outs = []
    for sd in out_shapes:
        if len(sd.shape) == 3:
            outs.append((sd, _spec((sd.shape[0], tm, 2 * r), lambda j, i, k: (0, i, j))))
        else:
            outs.append((sd, _spec((tm, 2 * r), lambda j, i, k: (i, j))))
    return _mm(name, (N_DEV // 2, S // tm, 1), ops, [(0, 1, False, True, 0)], extras, outs, epi, [(tm, 2 * r)])


def _rows_wgrad(name, a, d_bf, scale, l, n_layers, prev):
    S, D = d_bf.shape
    r = a.shape[1] // N_DEV
    tn = _pick(D, (2048, 1024, 512, 256, 128))
    tk = _pick(S, (1024, 512, 256, 128))
    ops = [(a, _spec((tk, 2 * r), lambda s, j, k: (k, s))),
           (d_bf, _spec((tk, tn), lambda s, j, k: (k, j)))]
    outs = [(_sd((N_DEV, n_layers, r, D), BF), _spec((2, None, r, tn), lambda s, j, k: (s, l, 0, j)))]

    def epi(accs, ex, orefs, ids):
        v = accs[0] if scale is None else accs[0] * scale
        orefs[0][0] = v[:r].astype(BF)
        orefs[0][1] = v[r:].astype(BF)

    return _mm(name, (N_DEV // 2, D // tn, S // tk), ops, [(0, 1, True, False, 0)], [], outs, epi, [(2 * r, tn)],
               alias=None if prev is None else (prev, 0))[0]


def _cols_dh(name, dpair, w_sm, l):
    _, S, _ = dpair.shape
    D, c = w_sm.shape[-2], w_sm.shape[-1]
    half = N_DEV // 2
    tm = _pick(S, (1024, 512, 256, 128))
    tn = _pick(D, (1024, 512, 256, 128))
    ops = [(dpair, _spec((None, tm, c), lambda i, j, k: (k // half, i, k % half))),
           (w_sm, _spec((None, None, tn, c), lambda i, j, k: (k, l, j, 0)))]
    outs = [(_sd((S, D), F32), _spec((tm, tn), lambda i, j, k: (i, j)))]
    return _mm(name, (S // tm, D // tn, N_DEV), ops, [(0, 1, False, True, 0)], [], outs, _store(), [(tm, tn)])[0]


def _cols_wgrad(name, h, dpair, l, n_layers, prev):
    S, D = h.shape
    half = N_DEV // 2
    c = dpair.shape[2] // half
    tm = _pick(D, (1024, 512, 256, 128))
    tk = _pick(S, (1024, 512, 256, 128))
    ops = [(h, _spec((tk, tm), lambda s, i, k: (k, i))),
           (dpair, _spec((None, tk, c), lambda s, i, k: (s // half, k, s % half)))]
    outs = [(_sd((N_DEV, n_layers, D, c), BF), _spec((None, None, tm, c), lambda s, i, k: (s, l, i, 0)))]
    return _mm(name, (N_DEV, D // tm, S // tk), ops, [(0, 1, True, False, 0)], [], outs, _store(), [(tm, c)],
               alias=None if prev is None else (prev, 0))[0]


def _mm2(name, a, b, ta, tb, out_dtype, epi=None, extras=(), res=None, tn_cands=(512, 384, 256, 128)):
    M = a.shape[1] if ta else a.shape[0]
    K = a.shape[0] if ta else a.shape[1]
    N = b.shape[0] if tb else b.shape[1]
    tm = _pick(M, (1024, 512, 256, 128))
    tn = _pick(N, tn_cands)
    tk = _pick(K, (2048, 1152, 1024, 512, 256, 128))
    a_spec = _spec((tk, tm), lambda i, j, k: (k, i)) if ta else _spec((tm, tk), lambda i, j, k: (i, k))
    b_spec = _spec((tn, tk), lambda i, j, k: (j, k)) if tb else _spec((tk, tn), lambda i, j, k: (k, j))
    ex = [(e, _spec((tm, e.shape[1]), lambda i, j, k: (i, 0))) for e in extras]
    if res is not None:
        ex = [(res, _spec((tm, tn), lambda i, j, k: (i, j)))]
        epi = _store_residual(1.0)
    outs = [(_sd((M, N), out_dtype), _spec((tm, tn), lambda i, j, k: (i, j)))]
    return _mm(name, (M // tm, N // tn, K // tk), [(a, a_spec), (b, b_spec)], [(0, 1, ta, tb, 0)], ex, outs,
               epi or _store(), [(tm, tn)])[0]


def _rms_fwd(name, x, g):
    S, D = x.shape
    ts = _pick(S, (512, 256, 128))

    def body(x_ref, g_ref, h_ref):
        xv = x_ref[...]
        r = lax.rsqrt(jnp.mean(xv * xv, axis=-1, keepdims=True) + EPS)
        h_ref[...] = (xv * r * g_ref[...]).astype(BF)

    return pl.pallas_call(
        body, name=name, grid=(S // ts,),
        in_specs=[_spec((ts, D), lambda i: (i, 0)), _spec((1, D), lambda i: (0, 0))],
        out_specs=_spec((ts, D), lambda i: (i, 0)), out_shape=_sd((S, D), BF),
        compiler_params=_params(("parallel",), 12 * ts * D * 4))(x, g)


def _rms_bwd(name, dh, x, g, dres):
    S, D = x.shape
    ts = _pick(S, (256, 128))

    def body(dh_ref, x_ref, g_ref, dres_ref, dx_ref, dxb_ref, dg_ref):
        xv = x_ref[...]
        dhv = dh_ref[...]
        r = lax.rsqrt(jnp.mean(xv * xv, axis=-1, keepdims=True) + EPS)
        xhat = xv * r
        dxh = dhv * g_ref[...]
        cm = jnp.mean(dxh * xhat, axis=-1, keepdims=True)
        dx = r * (dxh - xhat * cm) + dres_ref[...]
        dx_ref[...] = dx
        dxb_ref[...] = dx.astype(BF)

        @pl.when(pl.program_id(0) == 0)
        def _():
            dg_ref[...] = jnp.zeros_like(dg_ref)

        dg_ref[...] += jnp.sum(dhv * xhat, axis=0, keepdims=True)

    row = _spec((ts, D), lambda i: (i, 0))
    vec = _spec((1, D), lambda i: (0, 0))
    return pl.pallas_call(
        body, name=name, grid=(S // ts,),
        in_specs=[row, row, vec, row], out_specs=[row, row, vec],
        out_shape=[_sd((S, D), F32), _sd((S, D), BF), _sd((1, D), F32)],
        compiler_params=_params(("arbitrary",), 20 * ts * D * 4))(dh, x, g, dres)


def _final_loss(name, x, g, target):
    S, D = x.shape
    ts = _pick(S, (256, 128))

    def body(x_ref, g_ref, t_ref, loss_ref, dx_ref, dxb_ref, dg_ref):
        xv = x_ref[...]
        gv = g_ref[...]
        r = lax.rsqrt(jnp.mean(xv * xv, axis=-1, keepdims=True) + EPS)
        xhat = xv * r
        err = xhat * gv - t_ref[...]
        part = 0.5 * jnp.sum(jnp.mean(err * err, axis=-1, keepdims=True), axis=0, keepdims=True)
        dy = err * (1.0 / D)
        dxh = dy * gv
        cm = jnp.mean(dxh * xhat, axis=-1, keepdims=True)
        dx = r * (dxh - xhat * cm)
        dx_ref[...] = dx
        dxb_ref[...] = dx.astype(BF)

        @pl.when(pl.program_id(0) == 0)
        def _():
            dg_ref[...] = jnp.zeros_like(dg_ref)
            loss_ref[...] = jnp.zeros_like(loss_ref)

        dg_ref[...] += jnp.sum(dy * xhat, axis=0, keepdims=True)
        loss_ref[...] += jnp.broadcast_to(part, loss_ref.shape)

    row = _spec((ts, D), lambda i: (i, 0))
    vec = _spec((1, D), lambda i: (0, 0))
    return pl.pallas_call(
        body, name=name, grid=(S // ts,),
        in_specs=[row, vec, row], out_specs=[_spec((1, LANES), lambda i: (0, 0)), row, row, vec],
        out_shape=[_sd((1, LANES), F32), _sd((S, D), F32), _sd((S, D), BF), _sd((1, D), F32)],
        compiler_params=_params(("arbitrary",), 20 * ts * D * 4))(x, g, target)


def _swiglu(gate, up):
    return gate * jax.nn.sigmoid(gate) * up


def _swiglu_bwd_epi(accs, ex, orefs, ids):
    da = 0.5 * accs[0]
    gate = ex[0][0].astype(F32)
    up = ex[0][1].astype(F32)
    sg = jax.nn.sigmoid(gate)
    orefs[0][0] = (da * up * (sg * (1.0 + gate * (1.0 - sg)))).astype(BF)
    orefs[0][1] = (da * gate * sg).astype(BF)


_GELU_C = math.sqrt(2.0 / math.pi)


def _gelu(x):
    return x * (0.5 * (1.0 + jnp.tanh(_GELU_C * (x + 0.044715 * (x * x * x)))))


def _gelu_grad(x):
    t = jnp.tanh(_GELU_C * (x + 0.044715 * (x * x * x)))
    return 0.5 * (1.0 + t) + x * (0.5 * (1.0 - t * t) * _GELU_C * (1.0 + 3.0 * 0.044715 * (x * x)))


def _causal_block_mask():
    row = lax.broadcasted_iota(jnp.int32, (SGU_BLOCK, SGU_BLOCK), 0) // CHUNK
    col = lax.broadcasted_iota(jnp.int32, (SGU_BLOCK, SGU_BLOCK), 1) // CHUNK
    return row >= col


def _sgu_mid_fwd(name, puv, gain, bias, w_sp, b_sp):
    _, S, W = puv.shape
    G = SGU_GROUPS
    C = W // G
    T = SGU_BLOCK

    def body(puv_ref, gain_ref, bias_ref, w_ref, b_ref, out_ref):
        mask = _causal_block_mask()
        v = _gelu(puv_ref[1])
        mu = jnp.mean(v, axis=-1, keepdims=True)
        vc = v - mu
        rs = lax.rsqrt(jnp.mean(vc * vc, axis=-1, keepdims=True) + EPS)
        vln = (vc * rs * gain_ref[...] + bias_ref[...]).astype(BF)
        for g in range(G):
            wg = jnp.where(mask, w_ref[g], 0.0).astype(BF)
            mixed = jnp.dot(wg, vln[:, g * C:(g + 1) * C], preferred_element_type=F32) + b_ref[g]
            out_ref[:, g * C:(g + 1) * C] = (_gelu(puv_ref[0, :, g * C:(g + 1) * C]) * mixed).astype(BF)

    return pl.pallas_call(
        body, name=name, grid=(S // T,),
        in_specs=[_spec((2, T, W), lambda i: (0, i, 0)), _spec((1, W), lambda i: (0, 0)), _spec((1, W), lambda i: (0, 0)),
                  _spec((G, T, T), lambda i: (0, 0, 0)), _spec((G, T, 1), lambda i: (0, 0, 0))],
        out_specs=_spec((T, W), lambda i: (i, 0)), out_shape=_sd((S, W), BF),
        compiler_params=_params(("parallel",), 16 * T * W * 4))(puv, gain, bias, w_sp, b_sp)


def _sgu_mid_bwd(name, puv, dgated, gain, bias, w_sp, b_sp):
    _, S, W = puv.shape
    G = SGU_GROUPS
    C = W // G
    T = SGU_BLOCK

    def body(puv_ref, dg_ref, gain_ref, bias_ref, w_ref, b_ref, dpuv_ref, dgain_ref, dbias_ref, dw_ref, db_ref, dvln_ref):
        @pl.when(pl.program_id(0) == 0)
        def _():
            dgain_ref[...] = jnp.zeros_like(dgain_ref)
            dbias_ref[...] = jnp.zeros_like(dbias_ref)
            dw_ref[...] = jnp.zeros_like(dw_ref)
            db_ref[...] = jnp.zeros_like(db_ref)

        mask = _causal_block_mask()
        pv = puv_ref[1]
        v = _gelu(pv)
        mu = jnp.mean(v, axis=-1, keepdims=True)
        vc = v - mu
        rs = lax.rsqrt(jnp.mean(vc * vc, axis=-1, keepdims=True) + EPS)
        vhat = vc * rs
        gain_v = gain_ref[...]
        vln = (vhat * gain_v + bias_ref[...]).astype(BF)
        for g in range(G):
            sl = slice(g * C, (g + 1) * C)
            wg = jnp.where(mask, w_ref[g], 0.0).astype(BF)
            vg = vln[:, sl]
            mixed = jnp.dot(wg, vg, preferred_element_type=F32) + b_ref[g]
            pu = puv_ref[0, :, sl]
            dgt = dg_ref[:, sl].astype(F32)
            dpuv_ref[0, :, sl] = (dgt * mixed * _gelu_grad(pu)).astype(BF)
            dmix = dgt * _gelu(pu)
            db_ref[g] += jnp.sum(dmix, axis=-1, keepdims=True)
            dmb = dmix.astype(BF)
            dwg = lax.dot_general(dmb, vg, (((1,), (1,)), ((), ())), preferred_element_type=F32)
            dw_ref[g] += jnp.where(mask, dwg, 0.0)
            dvln_ref[:, sl] = lax.dot_general(wg, dmb, (((0,), (0,)), ((), ())), preferred_element_type=F32)
        dvln = dvln_ref[...]
        dgain_ref[...] += jnp.sum(dvln * vhat, axis=0, keepdims=True)
        dbias_ref[...] += jnp.sum(dvln, axis=0, keepdims=True)
        dvh = dvln * gain_v
        m1 = jnp.mean(dvh, axis=-1, keepdims=True)
        m2 = jnp.mean(dvh * vhat, axis=-1, keepdims=True)
        dv = rs * (dvh - m1 - vhat * m2)
        dpuv_ref[1] = (dv * _gelu_grad(pv)).astype(BF)

    vec = _spec((1, W), lambda i: (0, 0))
    wsp = _spec((G, T, T), lambda i: (0, 0, 0))
    bsp = _spec((G, T, 1), lambda i: (0, 0, 0))
    return pl.pallas_call(
        body, name=name, grid=(S // T,),
        in_specs=[_spec((2, T, W), lambda i: (0, i, 0)), _spec((T, W), lambda i: (i, 0)), vec, vec, wsp, bsp],
        out_specs=[_spec((2, T, W), lambda i: (0, i, 0)), vec, vec, wsp, bsp],
        out_shape=[_sd((2, S, W), BF), _sd((1, W), F32), _sd((1, W), F32), _sd((G, T, T), F32), _sd((G, T, 1), F32)],
        scratch_shapes=[pltpu.VMEM((T, W), F32)],
        compiler_params=_params(("arbitrary",), 24 * T * W * 4))(puv, dgated, gain, bias, w_sp, b_sp)


def _rope_tables(positions):
    half = QK_ROPE // 2
    inv_freq = 1.0 / (ROPE_THETA ** (jnp.arange(half, dtype=F32) / half))
    ang = positions.astype(F32)[:, None] * inv_freq[None, :]
    cos, sin = jnp.cos(ang), jnp.sin(ang)
    z = jnp.zeros_like(cos)
    return (jnp.concatenate([cos, cos, z, z], axis=1), jnp.concatenate([-sin, z, z, z], axis=1),
            jnp.concatenate([z, sin, z, z], axis=1))


def _rope(x, cos, sa, sb):
    return x * cos + pltpu.roll(x, HEAD_PAD - QK_ROPE // 2, 1) * sa + pltpu.roll(x, QK_ROPE // 2, 1) * sb


def _rope_t(dy, cos, sa, sb):
    return dy * cos + pltpu.roll(dy * sa, QK_ROPE // 2, 1) + pltpu.roll(dy * sb, HEAD_PAD - QK_ROPE // 2, 1)


def _rms_rows(x, g):
    r = lax.rsqrt(jnp.mean(x * x, axis=-1, keepdims=True) + EPS)
    return x * r * g


def _rms_rows_bwd(dy, x, g):
    r = lax.rsqrt(jnp.mean(x * x, axis=-1, keepdims=True) + EPS)
    xhat = x * r
    dxh = dy * g
    cm = jnp.mean(dxh * xhat, axis=-1, keepdims=True)
    return r * (dxh - xhat * cm), jnp.sum(dy * xhat, axis=0, keepdims=True)


def _mla_mid_fwd(name, proj, qg, kvg, cos, sa, sb):
    S, P = proj.shape
    ts = _pick(S, (512, 256, 128))

    def body(p_ref, qg_ref, kvg_ref, cos_ref, sa_ref, sb_ref, lat_ref, kr_ref):
        lat_ref[0] = _rms_rows(p_ref[:, :Q_LORA], qg_ref[...]).astype(BF)
        lat_ref[1] = _rms_rows(p_ref[:, Q_LORA:Q_LORA + KV_LORA], kvg_ref[...]).astype(BF)
        kr_ref[...] = _rope(p_ref[:, Q_LORA + KV_LORA:], cos_ref[...], sa_ref[...], sb_ref[...]).astype(BF)

    tab = _spec((ts, HEAD_PAD), lambda i: (i, 0))
    return pl.pallas_call(
        body, name=name, grid=(S // ts,),
        in_specs=[_spec((ts, P), lambda i: (i, 0)), _spec((1, Q_LORA), lambda i: (0, 0)),
                  _spec((1, KV_LORA), lambda i: (0, 0)), tab, tab, tab],
        out_specs=[_spec((2, ts, Q_LORA), lambda i: (0, i, 0)), tab],
        out_shape=[_sd((2, S, Q_LORA), BF), _sd((S, HEAD_PAD), BF)],
        compiler_params=_params(("parallel",), 16 * ts * P * 4))(proj, qg, kvg, cos, sa, sb)


def _mla_mid_bwd(name, proj, dqn, dkvn, dkr_heads, qg, kvg, cos, sa, sb):
    S, P = proj.shape
    H = dkr_heads.shape[0]
    ts = _pick(S, (256, 128))

    def body(p_ref, dqn_ref, dkvn_ref, dkr_ref, qg_ref, kvg_ref, cos_ref, sa_ref, sb_ref, dp_ref, dqg_ref, dkvg_ref):
        @pl.when(pl.program_id(0) == 0)
        def _():
            dqg_ref[...] = jnp.zeros_like(dqg_ref)
            dkvg_ref[...] = jnp.zeros_like(dkvg_ref)

        dq, dqg = _rms_rows_bwd(dqn_ref[...], p_ref[:, :Q_LORA], qg_ref[...])
        dkv, dkvg = _rms_rows_bwd(dkvn_ref[...], p_ref[:, Q_LORA:Q_LORA + KV_LORA], kvg_ref[...])
        dqg_ref[...] += dqg
        dkvg_ref[...] += dkvg
        dkr = dkr_ref[0]
        for h in range(1, H):
            dkr = dkr + dkr_ref[h]
        dp_ref[:, :Q_LORA] = dq.astype(BF)
        dp_ref[:, Q_LORA:Q_LORA + KV_LORA] = dkv.astype(BF)
        dp_ref[:, Q_LORA + KV_LORA:] = _rope_t(dkr, cos_ref[...], sa_ref[...], sb_ref[...]).astype(BF)

    tab = _spec((ts, HEAD_PAD), lambda i: (i, 0))
    lat = _spec((ts, Q_LORA), lambda i: (i, 0))
    gq = _spec((1, Q_LORA), lambda i: (0, 0))
    return pl.pallas_call(
        body, name=name, grid=(S // ts,),
        in_specs=[_spec((ts, P), lambda i: (i, 0)), lat, lat, _spec((H, ts, HEAD_PAD), lambda i: (0, i, 0)),
                  gq, gq, tab, tab, tab],
        out_specs=[_spec((ts, P), lambda i: (i, 0)), gq, gq],
        out_shape=[_sd((S, P), BF), _sd((1, Q_LORA), F32), _sd((1, KV_LORA), F32)],
        compiler_params=_params(("arbitrary",), 24 * ts * P * 4))(proj, dqn, dkvn, dkr_heads, qg, kvg, cos, sa, sb)


def _attn_tile(S):
    return _pick(S, (512,)) if S >= 2048 else _pick(S, (128,))


def _diag_mask(t, transposed):
    q = lax.broadcasted_iota(jnp.int32, (t, t), 1 if transposed else 0) // CHUNK
    k = lax.broadcasted_iota(jnp.int32, (t, t), 0 if transposed else 1) // CHUNK
    return k <= q


_NT = (((1,), (1,)), ((), ()))


def _attn_fwd(name, q_all, kv_all, kr):
    _, S, HP = q_all.shape
    H = HP // HEAD_PAD
    t = _attn_tile(S)
    nq = S // t

    def body(q_ref, kv_ref, kr_ref, o_ref, lse_ref, kcat_ref):
        i = pl.program_id(1)

        @pl.when(i == 0)
        def _():
            kcat_ref[:, :HEAD_PAD] = kv_ref[0]
            kcat_ref[:, HEAD_PAD:] = kr_ref[...]

        q = jnp.concatenate([q_ref[0], q_ref[1]], axis=1)

        def step(j, carry, masked):
            m, l, acc = carry
            off = pl.multiple_of(j * t, t)
            kj = kcat_ref[pl.ds(off, t), :]
            vj = kv_ref[1, pl.ds(off, t), :]
            s = lax.dot_general(q, kj, _NT, preferred_element_type=F32)
            if masked:
                s = jnp.where(_diag_mask(t, False), s, NEG)
            m2 = jnp.maximum(m, jnp.max(s, axis=-1, keepdims=True))
            al = jnp.exp(m - m2)
            p = jnp.exp(s - m2)
            l2 = al * l + jnp.sum(p, axis=-1, keepdims=True)
            acc2 = al * acc + jnp.dot(p.astype(BF), vj, preferred_element_type=F32)
            return m2, l2, acc2

        init = (jnp.full((t, 1), NEG, F32), jnp.zeros((t, 1), F32), jnp.zeros((t, V_DIM), F32))
        carry = lax.fori_loop(0, i, lambda j, c: step(j, c, False), init)
        m, l, acc = step(i, carry, True)
        o_ref[...] = acc / l
        lse_ref[...] = jnp.broadcast_to(m + jnp.log(l), (t, LANES))

    return pl.pallas_call(
        body, name=name, grid=(H, nq),
        in_specs=[_spec((2, t, HEAD_PAD), lambda h, i: (0, i, h)), _spec((2, S, HEAD_PAD), lambda h, i: (0, 0, h)),
                  _spec((S, HEAD_PAD), lambda h, i: (0, 0))],
        out_specs=[_spec((t, HEAD_PAD), lambda h, i: (i, h)), _spec((None, t, LANES), lambda h, i: (h, i, 0))],
        out_shape=[_sd((S, HP), F32), _sd((H, S, LANES), F32)],
        scratch_shapes=[pltpu.VMEM((S, 2 * HEAD_PAD), BF)],
        compiler_params=_params(("parallel", "arbitrary"), 8 * S * HEAD_PAD * 2 + 24 * t * t * 4))(q_all, kv_all, kr)


def _attn_delta(name, do, o):
    S, HP = o.shape
    H = HP // HEAD_PAD
    ts = _pick(S, (512, 256, 128))

    def body(do_ref, o_ref, d_ref):
        d_ref[...] = jnp.broadcast_to(jnp.sum(do_ref[...] * o_ref[...], axis=-1, keepdims=True), (ts, LANES))

    tile = _spec((ts, HEAD_PAD), lambda h, i: (i, h))
    return pl.pallas_call(
        body, name=name, grid=(H, S // ts), in_specs=[tile, tile],
        out_specs=_spec((None, ts, LANES), lambda h, i: (h, i, 0)), out_shape=_sd((H, S, LANES), F32),
        compiler_params=_params(("parallel", "parallel"), VMEM_FLOOR))(do, o)


def _attn_dq(name, q_all, kv_all, kr, do, lse, delta, cos, sa, sb, scale):
    _, S, HP = q_all.shape
    H = HP // HEAD_PAD
    t = _attn_tile(S)
    nq = S // t

    def body(q_ref, kv_ref, kr_ref, do_ref, lse_ref, dl_ref, cos_ref, sa_ref, sb_ref, dq_ref, kcat_ref):
        i = pl.program_id(1)

        @pl.when(i == 0)
        def _():
            kcat_ref[:, :HEAD_PAD] = kv_ref[0]
            kcat_ref[:, HEAD_PAD:] = kr_ref[...]

        q = jnp.concatenate([q_ref[0], q_ref[1]], axis=1)
        dob = do_ref[...].astype(BF)
        lse_c = lse_ref[:, 0:1]
        dl_c = dl_ref[:, 0:1]

        def step(j, dq, masked):
            off = pl.multiple_of(j * t, t)
            kj = kcat_ref[pl.ds(off, t), :]
            vj = kv_ref[1, pl.ds(off, t), :]
            s = lax.dot_general(q, kj, _NT, preferred_element_type=F32)
            if masked:
                s = jnp.where(_diag_mask(t, False), s, NEG)
            p = jnp.exp(s - lse_c)
            dp = lax.dot_general(dob, vj, _NT, preferred_element_type=F32)
            ds = (p * (dp - dl_c)).astype(BF)
            return dq + jnp.dot(ds, kj, preferred_element_type=F32)

        dq = lax.fori_loop(0, i, lambda j, c: step(j, c, False), jnp.zeros((t, 2 * HEAD_PAD), F32))
        dq = step(i, dq, True)
        dq_ref[0] = (dq[:, :HEAD_PAD] * scale).astype(BF)
        dq_ref[1] = (_rope_t(dq[:, HEAD_PAD:], cos_ref[...], sa_ref[...], sb_ref[...]) * scale).astype(BF)

    tab = _spec((t, HEAD_PAD), lambda h, i: (i, 0))
    stat = _spec((None, t, LANES), lambda h, i: (h, i, 0))
    return pl.pallas_call(
        body, name=name, grid=(H, nq),
        in_specs=[_spec((2, t, HEAD_PAD), lambda h, i: (0, i, h)), _spec((2, S, HEAD_PAD), lambda h, i: (0, 0, h)),
                  _spec((S, HEAD_PAD), lambda h, i: (0, 0)), _spec((t, HEAD_PAD), lambda h, i: (i, h)), stat, stat,
                  tab, tab, tab],
        out_specs=_spec((2, t, HEAD_PAD), lambda h, i: (0, i, h)), out_shape=_sd((2, S, HP), BF),
        scratch_shapes=[pltpu.VMEM((S, 2 * HEAD_PAD), BF)],
        compiler_params=_params(("parallel", "arbitrary"), 8 * S * HEAD_PAD * 2 + 32 * t * t * 4))(
            q_all, kv_all, kr, do, lse, delta, cos, sa, sb)


def _attn_dkv(name, q_all, kv_all, kr, do, lse_row, delta_row):
    _, S, HP = q_all.shape
    H = HP // HEAD_PAD
    t = _attn_tile(S)
    nq = S // t

    def body(q_ref, kv_ref, kr_ref, do_ref, lse_ref, dl_ref, dkv_ref, dkr_ref, qcat_ref):
        j = pl.program_id(1)

        @pl.when(j == 0)
        def _():
            qcat_ref[:, :HEAD_PAD] = q_ref[0]
            qcat_ref[:, HEAD_PAD:] = q_ref[1]

        kj = jnp.concatenate([kv_ref[0], kr_ref[...]], axis=1)
        vj = kv_ref[1]

        def step(i, carry, masked):
            dk, dv = carry
            off = pl.multiple_of(i * t, t)
            qi = qcat_ref[pl.ds(off, t), :]
            doi = do_ref[pl.ds(off, t), :].astype(BF)
            st = lax.dot_general(kj, qi, _NT, preferred_element_type=F32)
            if masked:
                st = jnp.where(_diag_mask(t, True), st, NEG)
            pt = jnp.exp(st - lse_ref[i])
            dv2 = dv + jnp.dot(pt.astype(BF), doi, preferred_element_type=F32)
            dpt = lax.dot_general(vj, doi, _NT, preferred_element_type=F32)
            dst = (pt * (dpt - dl_ref[i])).astype(BF)
            return dk + jnp.dot(dst, qi, preferred_element_type=F32), dv2

        carry = step(j, (jnp.zeros((t, 2 * HEAD_PAD), F32), jnp.zeros((t, V_DIM), F32)), True)
        dk, dv = lax.fori_loop(j + 1, nq, lambda i, c: step(i, c, False), carry)
        dkv_ref[0] = dk[:, :HEAD_PAD].astype(BF)
        dkv_ref[1] = dv.astype(BF)
        dkr_ref[...] = dk[:, HEAD_PAD:]

    stat = _spec((None, nq, 1, t), lambda h, j: (h, 0, 0, 0))
    return pl.pallas_call(
        body, name=name, grid=(H, nq),
        in_specs=[_spec((2, S, HEAD_PAD), lambda h, j: (0, 0, h)), _spec((2, t, HEAD_PAD), lambda h, j: (0, j, h)),
                  _spec((t, HEAD_PAD), lambda h, j: (j, 0)), _spec((S, HEAD_PAD), lambda h, j: (0, h)), stat, stat],
        out_specs=[_spec((2, t, HEAD_PAD), lambda h, j: (0, j, h)), _spec((None, t, HEAD_PAD), lambda h, j: (h, j, 0))],
        out_shape=[_sd((2, S, HP), BF), _sd((H, S, HEAD_PAD), F32)],
        scratch_shapes=[pltpu.VMEM((S, 2 * HEAD_PAD), BF)],
        compiler_params=_params(("parallel", "arbitrary"), 8 * S * HEAD_PAD * 4 + 32 * t * t * 4))(
            q_all, kv_all, kr, do, lse_row, delta_row)


def _place():
    x, y, c = lax.axis_index("x"), lax.axis_index("y"), lax.axis_index("c")
    return x, y, c


_ANY = pl.BlockSpec(memory_space=pl.ANY)


def _all_gather(blocks):
    n = len(blocks)

    def body(*refs):
        ins, outs = refs[:n], refs[n:2 * n]
        send_sems, recv_sems, local_sems = refs[2 * n:]
        x, y, c = _place()
        me = 4 * x + 2 * y + c
        sibling = (x, y, 1 - c)
        chips = [(1 - x, y), (x, 1 - y), (1 - x, 1 - y)]

        def slab(a, px, py, pc):
            return outs[a].at[4 * px + 2 * py + pc]

        def copy(a, k, src, dst, to):
            return pltpu.make_async_remote_copy(src_ref=src, dst_ref=dst, send_sem=send_sems.at[a, k],
                                                recv_sem=recv_sems.at[a, k], device_id=to, device_id_type=MESH)

        local = [pltpu.make_async_copy(ins[a], outs[a].at[me], local_sems.at[a]) for a in range(n)]
        for cp in local:
            cp.start()
        sends = []
        for a in range(n):
            mine = slab(a, x, y, c)
            sends.append(copy(a, 0, ins[a], mine, sibling))
            for j, chip in enumerate(chips):
                sends.append(copy(a, 1 + j, ins[a], mine, (*chip, c)))
        for cp in sends:
            cp.start()
        for j, chip in enumerate(chips):
            for a in range(n):
                got = slab(a, *chip, c)
                copy(a, 1 + j, got, got, (x, y, c)).wait_recv()
                fwd = copy(a, 4 + j, got, got, sibling)
                fwd.start()
                sends.append(fwd)
        for a in range(n):
            got = slab(a, x, y, 1 - c)
            copy(a, 0, got, got, (x, y, c)).wait_recv()
            for j, chip in enumerate(chips):
                got = slab(a, *chip, 1 - c)
                copy(a, 4 + j, got, got, (x, y, c)).wait_recv()
        for cp in sends:
            cp.wait_send()
        for cp in local:
            cp.wait()

    return pl.pallas_call(
        body, name="weights_all_gather", in_specs=[_ANY] * n, out_specs=[_ANY] * n,
        out_shape=[_sd((N_DEV,) + b.shape, b.dtype) for b in blocks],
        scratch_shapes=[pltpu.SemaphoreType.DMA((n, 7)), pltpu.SemaphoreType.DMA((n, 7)), pltpu.SemaphoreType.DMA((n,))],
    )(*blocks)


def _pair_exchange(grads):
    n = len(grads)
    half = N_DEV // 2

    def body(*refs):
        ins, own, got = refs[:n], refs[n:2 * n], refs[2 * n:3 * n]
        send_sems, recv_sems, local_sems = refs[3 * n:]
        x, y, c = _place()
        sibling = (x, y, 1 - c)
        cps, local = [], []
        for a in range(n):
            for j in range(half):
                local.append(pltpu.make_async_copy(ins[a].at[2 * j + c], own[a].at[j], local_sems.at[a, j]))
                cps.append(pltpu.make_async_remote_copy(
                    src_ref=ins[a].at[2 * j + (1 - c)], dst_ref=got[a].at[j], send_sem=send_sems.at[a, j],
                    recv_sem=recv_sems.at[a, j], device_id=sibling, device_id_type=MESH))
        for cp in local + cps:
            cp.start()
        for cp in cps:
            cp.wait_recv()
        for cp in cps:
            cp.wait_send()
        for cp in local:
            cp.wait()

    shapes = [_sd((half,) + g.shape[1:], g.dtype) for g in grads]
    res = pl.pallas_call(
        body, name="grads_pair_exchange", in_specs=[_ANY] * n, out_specs=[_ANY] * (2 * n), out_shape=shapes + shapes,
        scratch_shapes=[pltpu.SemaphoreType.DMA((n, half)), pltpu.SemaphoreType.DMA((n, half)),
                        pltpu.SemaphoreType.DMA((n, half))],
    )(*grads)
    return res[:n], res[n:]


def _chip_exchange(parts):
    n = len(parts)

    def body(*refs):
        ins, outs = refs[:n], refs[n:2 * n]
        send_sems, recv_sems, local_sems = refs[2 * n:]
        x, y, c = _place()
        mine = 2 * x + y
        chips = [(1 - x, y), (x, 1 - y), (1 - x, 1 - y)]
        cps, local = [], []
        for a in range(n):
            local.append(pltpu.make_async_copy(ins[a].at[mine], outs[a].at[mine], local_sems.at[a]))
            for k, (px, py) in enumerate(chips):
                cps.append(pltpu.make_async_remote_copy(
                    src_ref=ins[a].at[2 * px + py], dst_ref=outs[a].at[mine], send_sem=send_sems.at[a, k],
                    recv_sem=recv_sems.at[a, k], device_id=(px, py, c), device_id_type=MESH))
        for cp in local + cps:
            cp.start()
        for a in range(n):
            for k, (px, py) in enumerate(chips):
                theirs = outs[a].at[2 * px + py]
                pltpu.make_async_remote_copy(
                    src_ref=theirs, dst_ref=theirs, send_sem=send_sems.at[a, k], recv_sem=recv_sems.at[a, k],
                    device_id=(x, y, c), device_id_type=MESH).wait_recv()
        for cp in cps:
            cp.wait_send()
        for cp in local:
            cp.wait()

    return pl.pallas_call(
        body, name="grads_chip_exchange", in_specs=[_ANY] * n, out_specs=[_ANY] * n,
        out_shape=[_sd(p.shape, p.dtype) for p in parts],
        scratch_shapes=[pltpu.SemaphoreType.DMA((n, 3)), pltpu.SemaphoreType.DMA((n, 3)), pltpu.SemaphoreType.DMA((n,))],
    )(*parts)


def _all_reduce_small(name, part):
    R = part.shape[0]

    def body(p_ref, out_ref, gath_ref, send_sems, recv_sems):
        x, y, c = _place()
        me = 4 * x + 2 * y + c
        gath_ref[me] = p_ref[...]
        cps = []
        for m in range(1, N_DEV):
            to = (x ^ (m >> 2), y ^ ((m >> 1) & 1), c ^ (m & 1))
            cps.append(pltpu.make_async_remote_copy(
                src_ref=p_ref, dst_ref=gath_ref.at[me], send_sem=send_sems.at[m - 1], recv_sem=recv_sems.at[m - 1],
                device_id=to, device_id_type=MESH))
        for cp in cps:
            cp.start()
        for m in range(1, N_DEV):
            frm = 4 * (x ^ (m >> 2)) + 2 * (y ^ ((m >> 1) & 1)) + (c ^ (m & 1))
            pltpu.make_async_remote_copy(
                src_ref=p_ref, dst_ref=gath_ref.at[frm], send_sem=send_sems.at[m - 1], recv_sem=recv_sems.at[m - 1],
                device_id=(x, y, c), device_id_type=MESH).wait_recv()
        for cp in cps:
            cp.wait_send()
        tot = gath_ref[0]
        for k in range(1, N_DEV):
            tot = tot + gath_ref[k]
        out_ref[...] = tot

    vm = pl.BlockSpec(memory_space=pltpu.VMEM)
    return pl.pallas_call(
        body, name=name, in_specs=[vm], out_specs=vm, out_shape=_sd((R, LANES), F32),
        scratch_shapes=[pltpu.VMEM((N_DEV, R, LANES), F32), pltpu.SemaphoreType.DMA((N_DEV - 1,)),
                        pltpu.SemaphoreType.DMA((N_DEV - 1,))],
        compiler_params=pltpu.CompilerParams(vmem_limit_bytes=VMEM_FLOOR),
    )(part)


def _add_pairs(name, a, b):
    n, R, C = a.shape
    tr = _pick(R, (256, 128, 64, 32, 16))

    def body(a_ref, b_ref, o_ref):
        o_ref[...] = (a_ref[...].astype(F32) + b_ref[...].astype(F32)).astype(BF)

    blk = _spec((None, tr, C), lambda j, i: (j, i, 0))
    return pl.pallas_call(
        body, name=name, grid=(n, R // tr), in_specs=[blk, blk], out_specs=blk, out_shape=_sd(a.shape, BF),
        compiler_params=_params(("parallel", "parallel"), 16 * tr * C * 4))(a, b)


def _adam_math(w, g, m, v):
    m2 = ADAM_B1 * m + (1.0 - ADAM_B1) * g
    v2 = ADAM_B2 * v + (1.0 - ADAM_B2) * (g * g)
    m_hat = m2 / (1.0 - ADAM_B1 ** ADAM_STEP)
    v_hat = v2 / (1.0 - ADAM_B2 ** ADAM_STEP)
    delta = -ADAM_LR * (m_hat / (jnp.sqrt(v_hat) + ADAM_EPS) + ADAM_WD * w)
    return delta, m2, v2


def _adamw_sharded(name, parts, w, m, v):
    R, C = w.shape
    tr = _pick(R, (256, 128, 64, 32, 16, 8))

    def body(p_ref, w_ref, m_ref, v_ref, g_ref, d_ref, m2_ref, v2_ref):
        g = p_ref[0].astype(F32)
        for j in range(1, p_ref.shape[0]):
            g = g + p_ref[j].astype(F32)
        delta, m2, v2 = _adam_math(w_ref[...], g, m_ref[...], v_ref[...])
        g_ref[...] = g
        d_ref[...] = delta
        m2_ref[...] = m2
        v2_ref[...] = v2

    blk = _spec((tr, C), lambda i: (i, 0))
    return pl.pallas_call(
        body, name=name, grid=(R // tr,),
        in_specs=[_spec((parts.shape[0], tr, C), lambda i: (0, i, 0)), blk, blk, blk],
        out_specs=[blk] * 4, out_shape=[_sd((R, C), F32)] * 4,
        compiler_params=_params(("parallel",), 32 * tr * C * 4))(parts, w, m, v)


def _adamw_packed(name, w, g, m, v):
    R = w.shape[0]

    def body(w_ref, g_ref, m_ref, v_ref, d_ref, m2_ref, v2_ref):
        delta, m2, v2 = _adam_math(w_ref[...], g_ref[...], m_ref[...], v_ref[...])
        d_ref[...] = delta
        m2_ref[...] = m2
        v2_ref[...] = v2

    vm = pl.BlockSpec(memory_space=pltpu.VMEM)
    return pl.pallas_call(
        body, name=name, in_specs=[vm] * 4, out_specs=[vm] * 3, out_shape=[_sd((R, LANES), F32)] * 3,
        compiler_params=pltpu.CompilerParams(vmem_limit_bytes=VMEM_FLOOR))(w, g, m, v)


def _pack(arrays):
    flat = jnp.concatenate([a.reshape(-1).astype(F32) for a in arrays])
    pad = (-flat.shape[0]) % (8 * LANES)
    return jnp.pad(flat, (0, pad)).reshape(-1, LANES)


def _unpack(packed, like):
    flat = packed.reshape(-1)
    out, pos = [], 0
    for a in like:
        n = math.prod(a.shape)
        out.append(flat[pos:pos + n].reshape(a.shape))
        pos += n
    return out


def _ffn_fwd(tag, x, gain, w_in_sm, w_out_sm, l):
    h = _rms_fwd(tag + "_norm", x, gain)
    gu, act = _pair_in(tag + "_in", h, w_in_sm, l, BF, _swiglu)
    x_new = _rows_out(tag + "_out", act, w_out_sm, l, x, 0.5)
    return x_new, (x, h, gu, act)


def _ffn_bwd(tag, d, d_bf, saved, gain, w_in_sm, w_out_sm, l, n_layers, prev_in, prev_out):
    x, h, gu, act = saved
    dgu = _rows_dact(tag + "_dact", d_bf, w_out_sm, l, [gu], [_sd(gu.shape, BF)], _swiglu_bwd_epi)[0]
    g_out = _rows_wgrad(tag + "_wgrad_out", act, d_bf, 0.5, l, n_layers, prev_out)
    dh = _cols_dh(tag + "_dh", dgu, w_in_sm, l)
    g_in = _cols_wgrad(tag + "_wgrad_in", h, dgu, l, n_layers, prev_in)
    dx, dx_bf, dgain = _rms_bwd(tag + "_dnorm", dh, x, gain, d)
    return dx, dx_bf, dgain, g_in, g_out


def kernel(x, positions, ln_ffn1, ffn1_w_in, ffn1_w_out, ln_mix, ln_ffn2, ffn2_w_in, ffn2_w_out, sgu_w_in, sgu_v_gain, sgu_v_bias, sgu_w_spatial, sgu_b_spatial, sgu_w_out, mla_w_in, mla_q_norm, mla_w_q_up, mla_kv_norm, mla_w_kv_up, mla_w_out, ln_final, loss_target, m_ln_ffn1, m_ffn1_w_in, m_ffn1_w_out, m_ln_mix, m_ln_ffn2, m_ffn2_w_in, m_ffn2_w_out, m_sgu_w_in, m_sgu_v_gain, m_sgu_v_bias, m_sgu_w_spatial, m_sgu_b_spatial, m_sgu_w_out, m_mla_w_in, m_mla_q_norm, m_mla_w_q_up, m_mla_kv_norm, m_mla_w_kv_up, m_mla_w_out, m_ln_final, v_ln_ffn1, v_ffn1_w_in, v_ffn1_w_out, v_ln_mix, v_ln_ffn2, v_ffn2_w_in, v_ffn2_w_out, v_sgu_w_in, v_sgu_v_gain, v_sgu_v_bias, v_sgu_w_spatial, v_sgu_b_spatial, v_sgu_w_out, v_mla_w_in, v_mla_q_norm, v_mla_w_q_up, v_mla_kv_norm, v_mla_w_kv_up, v_mla_w_out, v_ln_final):
    S, D = x.shape[1], x.shape[2]
    L = ln_ffn1.shape[0]
    H = mla_w_q_up.shape[-1] * N_DEV // (QK_NOPE + QK_ROPE)
    xi, yi, ci = _place()
    me = 4 * xi + 2 * yi + ci
    big = dict(ffn1_w_in=ffn1_w_in, ffn1_w_out=ffn1_w_out, ffn2_w_in=ffn2_w_in, ffn2_w_out=ffn2_w_out,
               sgu_w_in=sgu_w_in, sgu_w_out=sgu_w_out, mla_w_in=mla_w_in, mla_w_q_up=mla_w_q_up,
               mla_w_kv_up=mla_w_kv_up, mla_w_out=mla_w_out)
    big_m = dict(ffn1_w_in=m_ffn1_w_in, ffn1_w_out=m_ffn1_w_out, ffn2_w_in=m_ffn2_w_in, ffn2_w_out=m_ffn2_w_out,
                 sgu_w_in=m_sgu_w_in, sgu_w_out=m_sgu_w_out, mla_w_in=m_mla_w_in, mla_w_q_up=m_mla_w_q_up,
                 mla_w_kv_up=m_mla_w_kv_up, mla_w_out=m_mla_w_out)
    big_v = dict(ffn1_w_in=v_ffn1_w_in, ffn1_w_out=v_ffn1_w_out, ffn2_w_in=v_ffn2_w_in, ffn2_w_out=v_ffn2_w_out,
                 sgu_w_in=v_sgu_w_in, sgu_w_out=v_sgu_w_out, mla_w_in=v_mla_w_in, mla_w_q_up=v_mla_w_q_up,
                 mla_w_kv_up=v_mla_w_kv_up, mla_w_out=v_mla_w_out)
    names = list(big)

    gathered = dict(zip(names, _all_gather([big[k].astype(BF) for k in names])))
    norm_rows = jnp.zeros((N_DEV, LANES), F32)
    mine = jnp.concatenate([mla_q_norm[0], mla_kv_norm[0]])
    norm_rows = lax.dynamic_update_slice(norm_rows, mine[None, :], (me, 0))
    norm_all = _all_reduce_small("norm_gains_gather", norm_rows)
    nq_sh = mla_q_norm.shape[1]
    q_gain = norm_all[:, :nq_sh].reshape(1, Q_LORA)
    kv_gain = norm_all[:, nq_sh:2 * nq_sh].reshape(1, KV_LORA)

    w_in_nat = gathered["mla_w_in"].reshape(D, Q_LORA + KV_LORA + QK_ROPE)
    w_in_pad = jnp.pad(w_in_nat, ((0, 0), (0, HEAD_PAD - QK_ROPE)))
    wq_nat = jnp.transpose(gathered["mla_w_q_up"][:, 0], (1, 0, 2)).reshape(Q_LORA, H, QK_NOPE + QK_ROPE)
    wq_t = jnp.stack([wq_nat[:, :, :QK_NOPE].reshape(Q_LORA, H * HEAD_PAD),
                      jnp.pad(wq_nat[:, :, QK_NOPE:], ((0, 0), (0, 0), (0, HEAD_PAD - QK_ROPE))).reshape(Q_LORA, H * HEAD_PAD)])
    wkv_nat = jnp.transpose(gathered["mla_w_kv_up"][:, 0], (1, 0, 2)).reshape(KV_LORA, H, QK_NOPE + V_DIM)
    wkv_t = jnp.stack([wkv_nat[:, :, :QK_NOPE].reshape(KV_LORA, H * HEAD_PAD),
                       wkv_nat[:, :, QK_NOPE:].reshape(KV_LORA, H * HEAD_PAD)])
    w_o_nat = gathered["mla_w_out"].reshape(H * V_DIM, D)
    cos, sa, sb = _rope_tables(positions[0])
    scale = float((QK_NOPE + QK_ROPE) ** -0.5)

    xs = x[0]
    w_sp = sgu_w_spatial[0]
    b_sp = sgu_b_spatial[0][:, :, None]
    saved = []
    for i in range(L):
        xs, s1 = _ffn_fwd(f"l{i}_ffn1", xs, ln_ffn1[i:i + 1], gathered["ffn1_w_in"], gathered["ffn1_w_out"], i)
        x_mix = xs
        h = _rms_fwd(f"l{i}_mix_norm", xs, ln_mix[i:i + 1])
        if i % 2 == 0:
            puv = _pair_in(f"l{i}_sgu_in", h, gathered["sgu_w_in"], i // 2, F32, None)[0]
            gated = _sgu_mid_fwd(f"l{i}_sgu_mid", puv, sgu_v_gain, sgu_v_bias, w_sp, b_sp)
            xs = _rows_out(f"l{i}_sgu_out", gated, gathered["sgu_w_out"], i // 2, xs, 1.0)
            sm = (x_mix, h, puv, gated)
        else:
            proj = _mm2(f"l{i}_mla_in", h, w_in_pad, False, False, F32, tn_cands=(384, 128))
            lat, kr = _mla_mid_fwd(f"l{i}_mla_mid", proj, q_gain, kv_gain, cos, sa, sb)

            def q_epi(accs, ex, orefs, ids):
                orefs[0][...] = (accs[0] * scale).astype(BF)

            def qr_epi(accs, ex, orefs, ids):
                for hh in range(accs[0].shape[1] // HEAD_PAD):
                    sl = slice(hh * HEAD_PAD, (hh + 1) * HEAD_PAD)
                    orefs[0][:, sl] = (_rope(accs[0][:, sl], *ex) * scale).astype(BF)

            q_nope = _mm2(f"l{i}_mla_q", lat[0], wq_t[0], False, False, BF, epi=q_epi)
            q_rope = _mm2(f"l{i}_mla_qr", lat[0], wq_t[1], False, False, BF, epi=qr_epi, extras=(cos, sa, sb))
            q_all = jnp.stack([q_nope, q_rope])
            kv_all = jnp.stack([_mm2(f"l{i}_mla_k", lat[1], wkv_t[0], False, False, BF),
                                _mm2(f"l{i}_mla_v", lat[1], wkv_t[1], False, False, BF)])
            o, lse = _attn_fwd(f"l{i}_attn", q_all, kv_all, kr)
            xs = _mm2(f"l{i}_mla_out", o, w_o_nat, False, False, F32, res=xs)
            sm = (x_mix, h, proj, lat, kr, q_all, kv_all, o, lse)
        xs, s2 = _ffn_fwd(f"l{i}_ffn2", xs, ln_ffn2[i:i + 1], gathered["ffn2_w_in"], gathered["ffn2_w_out"], i)
        saved.append((s1, sm, s2))

    loss_row, d, d_bf, g_ln_final = _final_loss("final_loss", xs, ln_final[None, :], loss_target[0])
    loss = lax.psum(loss_row[0, 0], ("x", "y", "c"))

    gw = {k: None for k in names}
    g_ln1, g_ln2, g_lnm = [None] * L, [None] * L, [None] * L
    small_g = {}
    for i in reversed(range(L)):
        s1, sm, s2 = saved[i]
        d, d_bf, g_ln2[i], gw["ffn2_w_in"], gw["ffn2_w_out"] = _ffn_bwd(
            f"l{i}_ffn2", d, d_bf, s2, ln_ffn2[i:i + 1], gathered["ffn2_w_in"], gathered["ffn2_w_out"], i, L,
            gw["ffn2_w_in"], gw["ffn2_w_out"])
        if i % 2 == 0:
            x_mix, h, puv, gated = sm
            dgated = _rows_dact(f"l{i}_sgu_dgated", d_bf, gathered["sgu_w_out"], i // 2, [], [_sd(gated.shape, BF)],
                                _store())[0]
            gw["sgu_w_out"] = _rows_wgrad(f"l{i}_sgu_wgrad_out", gated, d_bf, None, i // 2, 1, gw["sgu_w_out"])
            dpuv, dgain, dbias, dwsp, dbsp = _sgu_mid_bwd(f"l{i}_sgu_mid_bwd", puv, dgated, sgu_v_gain, sgu_v_bias,
                                                          w_sp, b_sp)
            small_g.update(sgu_v_gain=dgain, sgu_v_bias=dbias, sgu_w_spatial=dwsp[None], sgu_b_spatial=dbsp[None, :, :, 0])
            dh = _cols_dh(f"l{i}_sgu_dh", dpuv, gathered["sgu_w_in"], i // 2)
            gw["sgu_w_in"] = _cols_wgrad(f"l{i}_sgu_wgrad_in", h, dpuv, i // 2, 1, gw["sgu_w_in"])
        else:
            x_mix, h, proj, lat, kr, q_all, kv_all, o, lse = sm
            t = _attn_tile(S)
            do = _mm2(f"l{i}_mla_do", d_bf, w_o_nat, False, True, F32)
            g_wo = _mm2(f"l{i}_mla_wgrad_out", o, d_bf, True, False, BF)
            delta = _attn_delta(f"l{i}_attn_delta", do, o)
            dq_all = _attn_dq(f"l{i}_attn_dq", q_all, kv_all, kr, do, lse, delta, cos, sa, sb, scale)
            lse_row = lse[:, :, 0].reshape(H, S // t, 1, t)
            delta_row = delta[:, :, 0].reshape(H, S // t, 1, t)
            dkv_all, dkr_heads = _attn_dkv(f"l{i}_attn_dkv", q_all, kv_all, kr, do, lse_row, delta_row)
            dqn = _mm2(f"l{i}_mla_dqn", dq_all[0], wq_t[0], False, True, F32)
            dqn = _mm2(f"l{i}_mla_dqn2", dq_all[1], wq_t[1], False, True, F32, res=dqn)
            dkvn = _mm2(f"l{i}_mla_dkvn", dkv_all[0], wkv_t[0], False, True, F32)
            dkvn = _mm2(f"l{i}_mla_dkvn2", dkv_all[1], wkv_t[1], False, True, F32, res=dkvn)
            g_wq = [_mm2(f"l{i}_mla_wgrad_q{t2}", lat[0], dq_all[t2], True, False, BF) for t2 in range(2)]
            g_wkv = [_mm2(f"l{i}_mla_wgrad_kv{t2}", lat[1], dkv_all[t2], True, False, BF) for t2 in range(2)]
            dproj, g_qn, g_kvn = _mla_mid_bwd(f"l{i}_mla_mid_bwd", proj, dqn, dkvn, dkr_heads, q_gain, kv_gain, cos, sa, sb)
            dh = _mm2(f"l{i}_mla_dh", dproj, w_in_pad, False, True, F32, tn_cands=(512, 256, 128))
            g_win = _mm2(f"l{i}_mla_wgrad_in", h, dproj, True, False, BF, tn_cands=(384, 128))
            gw["mla_w_in"] = g_win[:, :Q_LORA + KV_LORA + QK_ROPE].reshape(N_DEV, 1, D // N_DEV, Q_LORA + KV_LORA + QK_ROPE)
            gq_nat = jnp.concatenate([g_wq[0].reshape(Q_LORA, H, HEAD_PAD),
                                      g_wq[1].reshape(Q_LORA, H, HEAD_PAD)[:, :, :QK_ROPE]], axis=2)
            gw["mla_w_q_up"] = jnp.transpose(gq_nat.reshape(Q_LORA, N_DEV, -1), (1, 0, 2))[:, None]
            gkv_nat = jnp.concatenate([g_wkv[0].reshape(KV_LORA, H, HEAD_PAD), g_wkv[1].reshape(KV_LORA, H, HEAD_PAD)], axis=2)
            gw["mla_w_kv_up"] = jnp.transpose(gkv_nat.reshape(KV_LORA, N_DEV, -1), (1, 0, 2))[:, None]
            gw["mla_w_out"] = g_wo.reshape(N_DEV, 1, H * V_DIM // N_DEV, D)
            small_g.update(mla_q_norm=g_qn, mla_kv_norm=g_kvn)
        d, d_bf, g_lnm[i] = _rms_bwd(f"l{i}_mix_dnorm", dh, x_mix, ln_mix[i:i + 1], d)
        d, d_bf, g_ln1[i], gw["ffn1_w_in"], gw["ffn1_w_out"] = _ffn_bwd(
            f"l{i}_ffn1", d, d_bf, s1, ln_ffn1[i:i + 1], gathered["ffn1_w_in"], gathered["ffn1_w_out"], i, L,
            gw["ffn1_w_in"], gw["ffn1_w_out"])
    grad_x = d[None]

    own, got = _pair_exchange([gw[k] for k in names])
    parts = []
    for k, a, b in zip(names, own, got):
        flat = (a.shape[0], math.prod(a.shape[1:-1]), a.shape[-1])
        parts.append(_add_pairs(f"pair_sum_{k}", a.reshape(flat), b.reshape(flat)).reshape(a.shape))
    landed = _chip_exchange(parts)
    big_out = {}
    for k, p in zip(names, landed):
        w = big[k]
        flat = (math.prod(w.shape[:-1]), w.shape[-1])
        res = _adamw_sharded(f"adamw_{k}", p.reshape((p.shape[0],) + flat), w.reshape(flat), big_m[k].reshape(flat),
                             big_v[k].reshape(flat))
        big_out[k] = [r.reshape(w.shape) for r in res]

    small_g.update(ln_ffn1=jnp.concatenate(g_ln1), ln_mix=jnp.concatenate(g_lnm), ln_ffn2=jnp.concatenate(g_ln2),
                   ln_final=g_ln_final[0])
    small_names = ["ln_ffn1", "ln_mix", "ln_ffn2", "sgu_v_gain", "sgu_v_bias", "sgu_w_spatial", "sgu_b_spatial",
                   "ln_final", "mla_q_norm", "mla_kv_norm"]
    summed = _unpack(_all_reduce_small("small_grads_all_reduce", _pack([small_g[k] for k in small_names])),
                     [small_g[k] for k in small_names])
    small_grad = dict(zip(small_names, summed))
    for k in ("mla_q_norm", "mla_kv_norm"):
        small_grad[k] = lax.dynamic_slice(small_grad[k], (0, me * nq_sh), (1, nq_sh))
    small_w = dict(ln_ffn1=ln_ffn1, ln_mix=ln_mix, ln_ffn2=ln_ffn2, sgu_v_gain=sgu_v_gain, sgu_v_bias=sgu_v_bias,
                   sgu_w_spatial=sgu_w_spatial, sgu_b_spatial=sgu_b_spatial, ln_final=ln_final, mla_q_norm=mla_q_norm,
                   mla_kv_norm=mla_kv_norm)
    small_m = dict(ln_ffn1=m_ln_ffn1, ln_mix=m_ln_mix, ln_ffn2=m_ln_ffn2, sgu_v_gain=m_sgu_v_gain, sgu_v_bias=m_sgu_v_bias,
                   sgu_w_spatial=m_sgu_w_spatial, sgu_b_spatial=m_sgu_b_spatial, ln_final=m_ln_final,
                   mla_q_norm=m_mla_q_norm, mla_kv_norm=m_mla_kv_norm)
    small_v = dict(ln_ffn1=v_ln_ffn1, ln_mix=v_ln_mix, ln_ffn2=v_ln_ffn2, sgu_v_gain=v_sgu_v_gain, sgu_v_bias=v_sgu_v_bias,
                   sgu_w_spatial=v_sgu_w_spatial, sgu_b_spatial=v_sgu_b_spatial, ln_final=v_ln_final,
                   mla_q_norm=v_mla_q_norm, mla_kv_norm=v_mla_kv_norm)
    like = [small_w[k] for k in small_names]
    packed = _adamw_packed("adamw_small", _pack(like), _pack([small_grad[k] for k in small_names]),
                           _pack([small_m[k] for k in small_names]), _pack([small_v[k] for k in small_names]))
    small_out = {}
    unpacked = [_unpack(p, like) for p in packed]
    for idx, k in enumerate(small_names):
        small_out[k] = [small_grad[k].reshape(small_w[k].shape)] + [u[idx] for u in unpacked]

    order = ["ln_ffn1", "ffn1_w_in", "ffn1_w_out", "ln_mix", "ln_ffn2", "ffn2_w_in", "ffn2_w_out", "sgu_w_in",
             "sgu_v_gain", "sgu_v_bias", "sgu_w_spatial", "sgu_b_spatial", "sgu_w_out", "mla_w_in", "mla_q_norm",
             "mla_w_q_up", "mla_kv_norm", "mla_w_kv_up", "mla_w_out", "ln_final"]
    res = {k: (big_out[k] if k in big_out else small_out[k]) for k in order}
    outs = [loss, grad_x]
    for part in range(4):
        outs.extend(res[k][part] for k in order)
    return tuple(outs)
```

```python
import functools
import math

import jax
import jax.numpy as jnp
from jax import lax
from jax.experimental import pallas as pl
from jax.experimental.pallas import tpu as pltpu

F32 = jnp.float32
BF = jnp.bfloat16
MESH = pl.DeviceIdType.MESH

N_DEV = 8
EPS = 1e-6
CHUNK = 64
SGU_BLOCK = 128
SGU_GROUPS = 8
Q_LORA = 512
KV_LORA = 512
QK_NOPE = 128
QK_ROPE = 64
V_DIM = 128
ROPE_THETA = 10000.0
HEAD_PAD = 128
LANES = 128
ADAM_LR = 0.001
ADAM_B1 = 0.9
ADAM_B2 = 0.999
ADAM_EPS = 1e-08
ADAM_WD = 0.01
ADAM_STEP = 10
V7X_VMEM_BYTES = 64 * 1024 * 1024
VMEM_CAP = V7X_VMEM_BYTES - 6 * 1024 * 1024
VMEM_FLOOR = 32 * 1024 * 1024
NEG = -1e30


def _pick(n, cands):
    for c in cands:
        if n % c == 0:
            return c
    return n


def _nbytes(shape, dtype):
    return math.prod(int(s) for s in shape if s is not None) * jnp.dtype(dtype).itemsize


def _params(sem, block_bytes):
    limit = int(min(VMEM_CAP, max(VMEM_FLOOR, block_bytes)))
    return pltpu.CompilerParams(dimension_semantics=sem, vmem_limit_bytes=limit)


def _spec(shape, fn):
    return pl.BlockSpec(shape, fn)


def _mm(name, grid, ops, pairs, extras, outs, epilogue, acc_shapes, alias=None):
    nk = grid[2]
    n_ops, n_ex, n_out = len(ops), len(extras), len(outs)

    def load(refs, idx):
        loader = ops[idx][2] if len(ops[idx]) > 2 else None
        return (refs[idx][...] if loader is None else loader(refs[idx])).astype(BF)

    def prod(refs, p):
        ia, ib, ta, tb, _ = p
        a = load(refs, ia)
        b = load(refs, ib)
        dims = (((0 if ta else 1,), (1 if tb else 0,)), ((), ()))
        return lax.dot_general(a, b, dims, preferred_element_type=F32)

    def body(*refs):
        op_refs = refs[:n_ops]
        ex_refs = refs[n_ops:n_ops + n_ex]
        n_in = n_ops + n_ex + (1 if alias is not None else 0)
        out_refs = refs[n_in:n_in + n_out]
        acc_refs = refs[n_in + n_out:]
        ids = (pl.program_id(0), pl.program_id(1))

        def finish(vals):
            epilogue(vals, [e[...] for e in ex_refs], out_refs, ids)

        if nk == 1:
            vals = [None] * len(acc_shapes)
            for p in pairs:
                r = prod(op_refs, p)
                vals[p[4]] = r if vals[p[4]] is None else vals[p[4]] + r
            finish(vals)
        else:
            k = pl.program_id(2)

            @pl.when(k == 0)
            def _():
                for a in acc_refs:
                    a[...] = jnp.zeros_like(a)

            for p in pairs:
                acc_refs[p[4]][...] += prod(op_refs, p)

            @pl.when(k == nk - 1)
            def _():
                finish([a[...] for a in acc_refs])

    in_arrays = [o[0] for o in ops] + [e[0] for e in extras]
    in_specs = [o[1] for o in ops] + [e[1] for e in extras]
    kwargs = {}
    if alias is not None:
        in_arrays.append(alias[0])
        in_specs.append(pl.BlockSpec(memory_space=pl.ANY))
        kwargs["input_output_aliases"] = {len(in_arrays) - 1: alias[1]}
    blk = 0
    for entry in ops + extras:
        blk += 2 * _nbytes(entry[1].block_shape, entry[0].dtype)
    for sd, sp in outs:
        blk += 2 * _nbytes(sp.block_shape, sd.dtype)
    acc_b = sum(_nbytes(s, F32) for s in acc_shapes)
    blk += 6 * acc_b
    scratch = [pltpu.VMEM(s, F32) for s in acc_shapes] if nk > 1 else []
    res = pl.pallas_call(
        body, name=name, grid=grid, in_specs=in_specs,
        out_specs=[o[1] for o in outs], out_shape=[o[0] for o in outs],
        scratch_shapes=scratch,
        compiler_params=_params(("parallel", "parallel", "arbitrary"), blk),
        **kwargs)(*in_arrays)
    return res


def _store(scale=None):
    def epi(accs, ex, outs, ids):
        v = accs[0]
        if scale is not None:
            v = v * scale
        outs[0][...] = v.astype(outs[0].dtype)
    return epi


def _store_residual(scale):
    def epi(accs, ex, outs, ids):
        outs[0][...] = ex[0] + scale * accs[0]
    return epi


def _sd(shape, dtype):
    return jax.ShapeDtypeStruct(tuple(shape), dtype)


def _pair_in(name, h, w_sm, l, out_dtype, act):
    S, D = h.shape
    c = w_sm.shape[-1]
    tm = _pick(S, (256, 128))
    half = N_DEV // 2
    ops = [(h, _spec((tm, D), lambda j, i, k: (i, 0))),
           (w_sm, _spec((None, None, D, c), lambda j, i, k: (j, l, 0, 0))),
           (w_sm, _spec((None, None, D, c), lambda j, i, k: (j + half, l, 0, 0)))]
    outs = [(_sd((2, S, half * c), out_dtype), _spec((2, tm, c), lambda j, i, k: (0, i, j)))]
    if act is not None:
        outs.append((_sd((S, half * c), BF), _spec((tm, c), lambda j, i, k: (i, j))))

    def epi(accs, ex, orefs, ids):
        orefs[0][0] = accs[0].astype(out_dtype)
        orefs[0][1] = accs[1].astype(out_dtype)
        if act is not None:
            orefs[1][...] = act(accs[0], accs[1]).astype(BF)

    return _mm(name, (half, S // tm, 1), ops, [(0, 1, False, False, 0), (0, 2, False, False, 1)], [], outs, epi,
               [(tm, c), (tm, c)])


def _two_slabs(ref):
    return jnp.concatenate([ref[0], ref[1]], axis=0)


def _rows_out(name, a, w_sm, l, res, scale):
    S = a.shape[0]
    r, D = w_sm.shape[-2], w_sm.shape[-1]
    tm = _pick(S, (1024, 512, 256, 128))
    tn = _pick(D, (1024, 512, 256, 128))
    ops = [(a, _spec((tm, 2 * r), lambda i, j, k: (i, k))),
           (w_sm, _spec((2, None, r, tn), lambda i, j, k: (k, l, 0, j)), _two_slabs)]
    extras = [(res, _spec((tm, tn), lambda i, j, k: (i, j)))]
    outs = [(_sd((S, D), F32), _spec((tm, tn), lambda i, j, k: (i, j)))]
    return _mm(name, (S // tm, D // tn, N_DEV // 2), ops, [(0, 1, False, False, 0)], extras, outs,
               _store_residual(scale), [(tm, tn)])[0]


def _rows_dact(name, d_bf, w_sm, l, extras_arrays, out_shapes, epi):
    S, D = d_bf.shape
    r = w_sm.shape[-2]
    tm = _pick(S, (1024, 512, 256, 128))
    ops = [(d_bf, _spec((tm, D), lambda j, i, k: (i, 0))),
           (w_sm, _spec((2, None, r, D), lambda j, i, k: (j, l, 0, 0)), _two_slabs)]
    extras = []
    for arr in extras_arrays:
        if arr.ndim == 3:
            extras.append((arr, _spec((arr.shape[0], tm, 2 * r), lambda j, i, k: (0, i, j))))
        else:
            extras.append((arr, _spec((tm, 2 * r), lambda j, i, k: (i, j))))
    outs = []
    for sd in out_shapes:
        if len(sd.shape) == 3:
            outs.append((sd, _spec((sd.shape[0], tm, 2 * r), lambda j, i, k: (0, i, j))))
        else:
            outs.append((sd, _spec((tm, 2 * r), lambda j, i, k: (i, j))))
    return _mm(name, (N_DEV // 2, S // tm, 1), ops, [(0, 1, False, True, 0)], extras, outs, epi, [(tm, 2 * r)])


def _rows_wgrad(name, a, d_bf, scale, l, n_layers, prev):
    S, D = d_bf.shape
    r = a.shape[1] // N_DEV
    tn = _pick(D, (2048, 1024, 512, 256, 128))
    tk = _pick(S, (1024, 512, 256, 128))
    ops = [(a, _spec((tk, 2 * r), lambda s, j, k: (k, s))),
           (d_bf, _spec((tk, tn), lambda s, j, k: (k, j)))]
    outs = [(_sd((N_DEV, n_layers, r, D), BF), _spec((2, None, r, tn), lambda s, j, k: (s, l, 0, j)))]

    def epi(accs, ex, orefs, ids):
        v = accs[0] if scale is None else accs[0] * scale
        orefs[0][0] = v[:r].astype(BF)
        orefs[0][1] = v[r:].astype(BF)

    return _mm(name, (N_DEV // 2, D // tn, S // tk), ops, [(0, 1, True, False, 0)], [], outs, epi, [(2 * r, tn)],
               alias=None if prev is None else (prev, 0))[0]


def _cols_dh(name, dpair, w_sm, l):
    _, S, _ = dpair.shape
    D, c = w_sm.shape[-2], w_sm.shape[-1]
    half = N_DEV // 2
    tm = _pick(S, (1024, 512, 256, 128))
    tn = _pick(D, (1024, 512, 256, 128))
    ops = [(dpair, _spec((None, tm, c), lambda i, j, k: (k // half, i, k % half))),
           (w_sm, _spec((None, None, tn, c), lambda i, j, k: (k, l, j, 0)))]
    outs = [(_sd((S, D), F32), _spec((tm, tn), lambda i, j, k: (i, j)))]
    return _mm(name, (S // tm, D // tn, N_DEV), ops, [(0, 1, False, True, 0)], [], outs, _store(), [(tm, tn)])[0]


def _cols_wgrad(name, h, dpair, l, n_layers, prev):
    S, D = h.shape
    half = N_DEV // 2
    c = dpair.shape[2] // half
    tm = _pick(D, (1024, 512, 256, 128))
    tk = _pick(S, (1024, 512, 256, 128))
    ops = [(h, _spec((tk, tm), lambda s, i, k: (k, i))),
           (dpair, _spec((None, tk, c), lambda s, i, k: (s // half, k, s % half)))]
    outs = [(_sd((N_DEV, n_layers, D, c), BF), _spec((None, None, tm, c), lambda s, i, k: (s, l, i, 0)))]
    return _mm(name, (N_DEV, D // tm, S // tk), ops, [(0, 1, True, False, 0)], [], outs, _store(), [(tm, c)],
               alias=None if prev is None else (prev, 0))[0]


def _mm2(name, a, b, ta, tb, out_dtype, epi=None, extras=(), res=None, tn_cands=(512, 384, 256, 128)):
    M = a.shape[1] if ta else a.shape[0]
    K = a.shape[0] if ta else a.shape[1]
    N = b.shape[0] if tb else b.shape[1]
    tm = _pick(M, (1024, 512, 256, 128))
    tn = _pick(N, tn_cands)
    tk = _pick(K, (2048, 1152, 1024, 512, 256, 128))
    a_spec = _spec((tk, tm), lambda i, j, k: (k, i)) if ta else _spec((tm, tk), lambda i, j, k: (i, k))
    b_spec = _spec((tn, tk), lambda i, j, k: (j, k)) if tb else _spec((tk, tn), lambda i, j, k: (k, j))
    ex = [(e, _spec((tm, e.shape[1]), lambda i, j, k: (i, 0))) for e in extras]
    if res is not None:
        ex = [(res, _spec((tm, tn), lambda i, j, k: (i, j)))]
        epi = _store_residual(1.0)
    outs = [(_sd((M, N), out_dtype), _spec((tm, tn), lambda i, j, k: (i, j)))]
    return _mm(name, (M // tm, N // tn, K // tk), [(a, a_spec), (b, b_spec)], [(0, 1, ta, tb, 0)], ex, outs,
               epi or _store(), [(tm, tn)])[0]


def _rms_fwd(name, x, g):
    S, D = x.shape
    ts = _pick(S, (512, 256, 128))

    def body(x_ref, g_ref, h_ref):
        xv = x_ref[...]
        r = lax.rsqrt(jnp.mean(xv * xv, axis=-1, keepdims=True) + EPS)
        h_ref[...] = (xv * r * g_ref[...]).astype(BF)

    return pl.pallas_call(
        body, name=name, grid=(S // ts,),
        in_specs=[_spec((ts, D), lambda i: (i, 0)), _spec((1, D), lambda i: (0, 0))],
        out_specs=_spec((ts, D), lambda i: (i, 0)), out_shape=_sd((S, D), BF),
        compiler_params=_params(("parallel",), 12 * ts * D * 4))(x, g)


def _rms_bwd(name, dh, x, g, dres):
    S, D = x.shape
    ts = _pick(S, (256, 128))

    def body(dh_ref, x_ref, g_ref, dres_ref, dx_ref, dxb_ref, dg_ref):
        xv = x_ref[...]
        dhv = dh_ref[...]
        r = lax.rsqrt(jnp.mean(xv * xv, axis=-1, keepdims=True) + EPS)
        xhat = xv * r
        dxh = dhv * g_ref[...]
        cm = jnp.mean(dxh * xhat, axis=-1, keepdims=True)
        dx = r * (dxh - xhat * cm) + dres_ref[...]
        dx_ref[...] = dx
        dxb_ref[...] = dx.astype(BF)

        @pl.when(pl.program_id(0) == 0)
        def _():
            dg_ref[...] = jnp.zeros_like(dg_ref)

        dg_ref[...] += jnp.sum(dhv * xhat, axis=0, keepdims=True)

    row = _spec((ts, D), lambda i: (i, 0))
    vec = _spec((1, D), lambda i: (0, 0))
    return pl.pallas_call(
        body, name=name, grid=(S // ts,),
        in_specs=[row, row, vec, row], out_specs=[row, row, vec],
        out_shape=[_sd((S, D), F32), _sd((S, D), BF), _sd((1, D), F32)],
        compiler_params=_params(("arbitrary",), 20 * ts * D * 4))(dh, x, g, dres)


def _final_loss(name, x, g, target):
    S, D = x.shape
    ts = _pick(S, (256, 128))

    def body(x_ref, g_ref, t_ref, loss_ref, dx_ref, dxb_ref, dg_ref):
        xv = x_ref[...]
        gv = g_ref[...]
        r = lax.rsqrt(jnp.mean(xv * xv, axis=-1, keepdims=True) + EPS)
        xhat = xv * r
        err = xhat * gv - t_ref[...]
        part = 0.5 * jnp.sum(jnp.mean(err * err, axis=-1, keepdims=True), axis=0, keepdims=True)
        dy = err * (1.0 / D)
        dxh = dy * gv
        cm = jnp.mean(dxh * xhat, axis=-1, keepdims=True)
        dx = r * (dxh - xhat * cm)
        dx_ref[...] = dx
        dxb_ref[...] = dx.astype(BF)

        @pl.when(pl.program_id(0) == 0)
        def _():
            dg_ref[...] = jnp.zeros_like(dg_ref)
            loss_ref[...] = jnp.zeros_like(loss_ref)

        dg_ref[...] += jnp.sum(dy * xhat, axis=0, keepdims=True)
        loss_ref[...] += jnp.broadcast_to(part, loss_ref.shape)

    row = _spec((ts, D), lambda i: (i, 0))
    vec = _spec((1, D), lambda i: (0, 0))
    return pl.pallas_call(
        body, name=name, grid=(S // ts,),
        in_specs=[row, vec, row], out_specs=[_spec((1, LANES), lambda i: (0, 0)), row, row, vec],
        out_shape=[_sd((1, LANES), F32), _sd((S, D), F32), _sd((S, D), BF), _sd((1, D), F32)],
        compiler_params=_params(("arbitrary",), 20 * ts * D * 4))(x, g, target)


def _swiglu(gate, up):
    return gate * jax.nn.sigmoid(gate) * up


def _swiglu_bwd_epi(accs, ex, orefs, ids):
    da = 0.5 * accs[0]
    gate = ex[0][0].astype(F32)
    up = ex[0][1].astype(F32)
    sg = jax.nn.sigmoid(gate)
    orefs[0][0] = (da * up * (sg * (1.0 + gate * (1.0 - sg)))).astype(BF)
    orefs[0][1] = (da * gate * sg).astype(BF)


_GELU_C = math.sqrt(2.0 / math.pi)


def _gelu(x):
    return x * (0.5 * (1.0 + jnp.tanh(_GELU_C * (x + 0.044715 * (x * x * x)))))


def _gelu_grad(x):
    t = jnp.tanh(_GELU_C * (x + 0.044715 * (x * x * x)))
    return 0.5 * (1.0 + t) + x * (0.5 * (1.0 - t * t) * _GELU_C * (1.0 + 3.0 * 0.044715 * (x * x)))


def _causal_block_mask():
    row = lax.broadcasted_iota(jnp.int32, (SGU_BLOCK, SGU_BLOCK), 0) // CHUNK
    col = lax.broadcasted_iota(jnp.int32, (SGU_BLOCK, SGU_BLOCK), 1) // CHUNK
    return row >= col


def _sgu_mid_fwd(name, puv, gain, bias, w_sp, b_sp):
    _, S, W = puv.shape
    G = SGU_GROUPS
    C = W // G
    T = SGU_BLOCK

    def body(puv_ref, gain_ref, bias_ref, w_ref, b_ref, out_ref):
        mask = _causal_block_mask()
        v = _gelu(puv_ref[1])
        mu = jnp.mean(v, axis=-1, keepdims=True)
        vc = v - mu
        rs = lax.rsqrt(jnp.mean(vc * vc, axis=-1, keepdims=True) + EPS)
        vln = (vc * rs * gain_ref[...] + bias_ref[...]).astype(BF)
        for g in range(G):
            wg = jnp.where(mask, w_ref[g], 0.0).astype(BF)
            mixed = jnp.dot(wg, vln[:, g * C:(g + 1) * C], preferred_element_type=F32) + b_ref[g]
            out_ref[:, g * C:(g + 1) * C] = (_gelu(puv_ref[0, :, g * C:(g + 1) * C]) * mixed).astype(BF)

    return pl.pallas_call(
        body, name=name, grid=(S // T,),
        in_specs=[_spec((2, T, W), lambda i: (0, i, 0)), _spec((1, W), lambda i: (0, 0)), _spec((1, W), lambda i: (0, 0)),
                  _spec((G, T, T), lambda i: (0, 0, 0)), _spec((G, T, 1), lambda i: (0, 0, 0))],
        out_specs=_spec((T, W), lambda i: (i, 0)), out_shape=_sd((S, W), BF),
        compiler_params=_params(("parallel",), 16 * T * W * 4))(puv, gain, bias, w_sp, b_sp)


def _sgu_mid_bwd(name, puv, dgated, gain, bias, w_sp, b_sp):
    _, S, W = puv.shape
    G = SGU_GROUPS
    C = W // G
    T = SGU_BLOCK

    def body(puv_ref, dg_ref, gain_ref, bias_ref, w_ref, b_ref, dpuv_ref, dgain_ref, dbias_ref, dw_ref, db_ref, dvln_ref):
        @pl.when(pl.program_id(0) == 0)
        def _():
            dgain_ref[...] = jnp.zeros_like(dgain_ref)
            dbias_ref[...] = jnp.zeros_like(dbias_ref)
            dw_ref[...] = jnp.zeros_like(dw_ref)
            db_ref[...] = jnp.zeros_like(db_ref)

        mask = _causal_block_mask()
        pv = puv_ref[1]
        v = _gelu(pv)
        mu = jnp.mean(v, axis=-1, keepdims=True)
        vc = v - mu
        rs = lax.rsqrt(jnp.mean(vc * vc, axis=-1, keepdims=True) + EPS)
        vhat = vc * rs
        gain_v = gain_ref[...]
        vln = (vhat * gain_v + bias_ref[...]).astype(BF)
        for g in range(G):
            sl = slice(g * C, (g + 1) * C)
            wg = jnp.where(mask, w_ref[g], 0.0).astype(BF)
            vg = vln[:, sl]
            mixed = jnp.dot(wg, vg, preferred_element_type=F32) + b_ref[g]
            pu = puv_ref[0, :, sl]
            dgt = dg_ref[:, sl].astype(F32)
            dpuv_ref[0, :, sl] = (dgt * mixed * _gelu_grad(pu)).astype(BF)
            dmix = dgt * _gelu(pu)
            db_ref[g] += jnp.sum(dmix, axis=-1, keepdims=True)
            dmb = dmix.astype(BF)
            dwg = lax.dot_general(dmb, vg, (((1,), (1,)), ((), ())), preferred_element_type=F32)
            dw_ref[g] += jnp.where(mask, dwg, 0.0)
            dvln_ref[:, sl] = lax.dot_general(wg, dmb, (((0,), (0,)), ((), ())), preferred_element_type=F32)
        dvln = dvln_ref[...]
        dgain_ref[...] += jnp.sum(dvln * vhat, axis=0, keepdims=True)
        dbias_ref[...] += jnp.sum(dvln, axis=0, keepdims=True)
        dvh = dvln * gain_v
        m1 = jnp.mean(dvh, axis=-1, keepdims=True)
        m2 = jnp.mean(dvh * vhat, axis=-1, keepdims=True)
        dv = rs * (dvh - m1 - vhat * m2)
        dpuv_ref[1] = (dv * _gelu_grad(pv)).astype(BF)

    vec = _spec((1, W), lambda i: (0, 0))
    wsp = _spec((G, T, T), lambda i: (0, 0, 0))
    bsp = _spec((G, T, 1), lambda i: (0, 0, 0))
    return pl.pallas_call(
        body, name=name, grid=(S // T,),
        in_specs=[_spec((2, T, W), lambda i: (0, i, 0)), _spec((T, W), lambda i: (i, 0)), vec, vec, wsp, bsp],
        out_specs=[_spec((2, T, W), lambda i: (0, i, 0)), vec, vec, wsp, bsp],
        out_shape=[_sd((2, S, W), BF), _sd((1, W), F32), _sd((1, W), F32), _sd((G, T, T), F32), _sd((G, T, 1), F32)],
        scratch_shapes=[pltpu.VMEM((T, W), F32)],
        compiler_params=_params(("arbitrary",), 24 * T * W * 4))(puv, dgated, gain, bias, w_sp, b_sp)


def _rope_tables(positions):
    half = QK_ROPE // 2
    inv_freq = 1.0 / (ROPE_THETA ** (jnp.arange(half, dtype=F32) / half))
    ang = positions.astype(F32)[:, None] * inv_freq[None, :]
    cos, sin = jnp.cos(ang), jnp.sin(ang)
    z = jnp.zeros_like(cos)
    return (jnp.concatenate([cos, cos, z, z], axis=1), jnp.concatenate([-sin, z, z, z], axis=1),
            jnp.concatenate([z, sin, z, z], axis=1))


def _rope(x, cos, sa, sb):
    return x * cos + pltpu.roll(x, HEAD_PAD - QK_ROPE // 2, 1) * sa + pltpu.roll(x, QK_ROPE // 2, 1) * sb


def _rope_t(dy, cos, sa, sb):
    return dy * cos + pltpu.roll(dy * sa, QK_ROPE // 2, 1) + pltpu.roll(dy * sb, HEAD_PAD - QK_ROPE // 2, 1)


def _rms_rows(x, g):
    r = lax.rsqrt(jnp.mean(x * x, axis=-1, keepdims=True) + EPS)
    return x * r * g


def _rms_rows_bwd(dy, x, g):
    r = lax.rsqrt(jnp.mean(x * x, axis=-1, keepdims=True) + EPS)
    xhat = x * r
    dxh = dy * g
    cm = jnp.mean(dxh * xhat, axis=-1, keepdims=True)
    return r * (dxh - xhat * cm), jnp.sum(dy * xhat, axis=0, keepdims=True)


def _mla_mid_fwd(name, proj, qg, kvg, cos, sa, sb):
    S, P = proj.shape
    ts = _pick(S, (512, 256, 128))

    def body(p_ref, qg_ref, kvg_ref, cos_ref, sa_ref, sb_ref, lat_ref, kr_ref):
        lat_ref[0] = _rms_rows(p_ref[:, :Q_LORA], qg_ref[...]).astype(BF)
        lat_ref[1] = _rms_rows(p_ref[:, Q_LORA:Q_LORA + KV_LORA], kvg_ref[...]).astype(BF)
        kr_ref[...] = _rope(p_ref[:, Q_LORA + KV_LORA:], cos_ref[...], sa_ref[...], sb_ref[...]).astype(BF)

    tab = _spec((ts, HEAD_PAD), lambda i: (i, 0))
    return pl.pallas_call(
        body, name=name, grid=(S // ts,),
        in_specs=[_spec((ts, P), lambda i: (i, 0)), _spec((1, Q_LORA), lambda i: (0, 0)),
                  _spec((1, KV_LORA), lambda i: (0, 0)), tab, tab, tab],
        out_specs=[_spec((2, ts, Q_LORA), lambda i: (0, i, 0)), tab],
        out_shape=[_sd((2, S, Q_LORA), BF), _sd((S, HEAD_PAD), BF)],
        compiler_params=_params(("parallel",), 16 * ts * P * 4))(proj, qg, kvg, cos, sa, sb)


def _mla_mid_bwd(name, proj, dqn, dkvn, dkr_heads, qg, kvg, cos, sa, sb):
    S, P = proj.shape
    H = dkr_heads.shape[0]
    ts = _pick(S, (256, 128))

    def body(p_ref, dqn_ref, dkvn_ref, dkr_ref, qg_ref, kvg_ref, cos_ref, sa_ref, sb_ref, dp_ref, dqg_ref, dkvg_ref):
        @pl.when(pl.program_id(0) == 0)
        def _():
            dqg_ref[...] = jnp.zeros_like(dqg_ref)
            dkvg_ref[...] = jnp.zeros_like(dkvg_ref)

        dq, dqg = _rms_rows_bwd(dqn_ref[...], p_ref[:, :Q_LORA], qg_ref[...])
        dkv, dkvg = _rms_rows_bwd(dkvn_ref[...], p_ref[:, Q_LORA:Q_LORA + KV_LORA], kvg_ref[...])
        dqg_ref[...] += dqg
        dkvg_ref[...] += dkvg
        dkr = dkr_ref[0]
        for h in range(1, H):
            dkr = dkr + dkr_ref[h]
        dp_ref[:, :Q_LORA] = dq.astype(BF)
        dp_ref[:, Q_LORA:Q_LORA + KV_LORA] = dkv.astype(BF)
        dp_ref[:, Q_LORA + KV_LORA:] = _rope_t(dkr, cos_ref[...], sa_ref[...], sb_ref[...]).astype(BF)

    tab = _spec((ts, HEAD_PAD), lambda i: (i, 0))
    lat = _spec((ts, Q_LORA), lambda i: (i, 0))
    gq = _spec((1, Q_LORA), lambda i: (0, 0))
    return pl.pallas_call(
        body, name=name, grid=(S // ts,),
        in_specs=[_spec((ts, P), lambda i: (i, 0)), lat, lat, _spec((H, ts, HEAD_PAD), lambda i: (0, i, 0)),
                  gq, gq, tab, tab, tab],
        out_specs=[_spec((ts, P), lambda i: (i, 0)), gq, gq],
        out_shape=[_sd((S, P), BF), _sd((1, Q_LORA), F32), _sd((1, KV_LORA), F32)],
        compiler_params=_params(("arbitrary",), 24 * ts * P * 4))(proj, dqn, dkvn, dkr_heads, qg, kvg, cos, sa, sb)


def _attn_tile(S):
    return _pick(S, (512,)) if S >= 2048 else _pick(S, (128,))


def _diag_mask(t, transposed):
    q = lax.broadcasted_iota(jnp.int32, (t, t), 1 if transposed else 0) // CHUNK
    k = lax.broadcasted_iota(jnp.int32, (t, t), 0 if transposed else 1) // CHUNK
    return k <= q


_NT = (((1,), (1,)), ((), ()))


def _attn_fwd(name, q_all, kv_all, kr):
    _, S, HP = q_all.shape
    H = HP // HEAD_PAD
    t = _attn_tile(S)
    nq = S // t

    def body(q_ref, kv_ref, kr_ref, o_ref, lse_ref, kcat_ref):
        i = pl.program_id(1)

        @pl.when(i == 0)
        def _():
            kcat_ref[:, :HEAD_PAD] = kv_ref[0]
            kcat_ref[:, HEAD_PAD:] = kr_ref[...]

        q = jnp.concatenate([q_ref[0], q_ref[1]], axis=1)

        def step(j, carry, masked):
            m, l, acc = carry
            off = pl.multiple_of(j * t, t)
            kj = kcat_ref[pl.ds(off, t), :]
            vj = kv_ref[1, pl.ds(off, t), :]
            s = lax.dot_general(q, kj, _NT, preferred_element_type=F32)
            if masked:
                s = jnp.where(_diag_mask(t, False), s, NEG)
            m2 = jnp.maximum(m, jnp.max(s, axis=-1, keepdims=True))
            al = jnp.exp(m - m2)
            p = jnp.exp(s - m2)
            l2 = al * l + jnp.sum(p, axis=-1, keepdims=True)
            acc2 = al * acc + jnp.dot(p.astype(BF), vj, preferred_element_type=F32)
            return m2, l2, acc2

        init = (jnp.full((t, 1), NEG, F32), jnp.zeros((t, 1), F32), jnp.zeros((t, V_DIM), F32))
        carry = lax.fori_loop(0, i, lambda j, c: step(j, c, False), init)
        m, l, acc = step(i, carry, True)
        o_ref[...] = acc / l
        lse_ref[...] = jnp.broadcast_to(m + jnp.log(l), (t, LANES))

    return pl.pallas_call(
        body, name=name, grid=(H, nq),
        in_specs=[_spec((2, t, HEAD_PAD), lambda h, i: (0, i, h)), _spec((2, S, HEAD_PAD), lambda h, i: (0, 0, h)),
                  _spec((S, HEAD_PAD), lambda h, i: (0, 0))],
        out_specs=[_spec((t, HEAD_PAD), lambda h, i: (i, h)), _spec((None, t, LANES), lambda h, i: (h, i, 0))],
        out_shape=[_sd((S, HP), F32), _sd((H, S, LANES), F32)],
        scratch_shapes=[pltpu.VMEM((S, 2 * HEAD_PAD), BF)],
        compiler_params=_params(("parallel", "arbitrary"), 8 * S * HEAD_PAD * 2 + 24 * t * t * 4))(q_all, kv_all, kr)


def _attn_delta(name, do, o):
    S, HP = o.shape
    H = HP // HEAD_PAD
    ts = _pick(S, (512, 256, 128))

    def body(do_ref, o_ref, d_ref):
        d_ref[...] = jnp.broadcast_to(jnp.sum(do_ref[...] * o_ref[...], axis=-1, keepdims=True), (ts, LANES))

    tile = _spec((ts, HEAD_PAD), lambda h, i: (i, h))
    return pl.pallas_call(
        body, name=name, grid=(H, S // ts), in_specs=[tile, tile],
        out_specs=_spec((None, ts, LANES), lambda h, i: (h, i, 0)), out_shape=_sd((H, S, LANES), F32),
        compiler_params=_params(("parallel", "parallel"), VMEM_FLOOR))(do, o)


def _attn_dq(name, q_all, kv_all, kr, do, lse, delta, cos, sa, sb, scale):
    _, S, HP = q_all.shape
    H = HP // HEAD_PAD
    t = _attn_tile(S)
    nq = S // t

    def body(q_ref, kv_ref, kr_ref, do_ref, lse_ref, dl_ref, cos_ref, sa_ref, sb_ref, dq_ref, kcat_ref):
        i = pl.program_id(1)

        @pl.when(i == 0)
        def _():
            kcat_ref[:, :HEAD_PAD] = kv_ref[0]
            kcat_ref[:, HEAD_PAD:] = kr_ref[...]

        q = jnp.concatenate([q_ref[0], q_ref[1]], axis=1)
        dob = do_ref[...].astype(BF)
        lse_c = lse_ref[:, 0:1]
        dl_c = dl_ref[:, 0:1]

        def step(j, dq, masked):
            off = pl.multiple_of(j * t, t)
            kj = kcat_ref[pl.ds(off, t), :]
            vj = kv_ref[1, pl.ds(off, t), :]
            s = lax.dot_general(q, kj, _NT, preferred_element_type=F32)
            if masked:
                s = jnp.where(_diag_mask(t, False), s, NEG)
            p = jnp.exp(s - lse_c)
            dp = lax.dot_general(dob, vj, _NT, preferred_element_type=F32)
            ds = (p * (dp - dl_c)).astype(BF)
            return dq + jnp.dot(ds, kj, preferred_element_type=F32)

        dq = lax.fori_loop(0, i, lambda j, c: step(j, c, False), jnp.zeros((t, 2 * HEAD_PAD), F32))
        dq = step(i, dq, True)
        dq_ref[0] = (dq[:, :HEAD_PAD] * scale).astype(BF)
        dq_ref[1] = (_rope_t(dq[:, HEAD_PAD:], cos_ref[...], sa_ref[...], sb_ref[...]) * scale).astype(BF)

    tab = _spec((t, HEAD_PAD), lambda h, i: (i, 0))
    stat = _spec((None, t, LANES), lambda h, i: (h, i, 0))
    return pl.pallas_call(
        body, name=name, grid=(H, nq),
        in_specs=[_spec((2, t, HEAD_PAD), lambda h, i: (0, i, h)), _spec((2, S, HEAD_PAD), lambda h, i: (0, 0, h)),
                  _spec((S, HEAD_PAD), lambda h, i: (0, 0)), _spec((t, HEAD_PAD), lambda h, i: (i, h)), stat, stat,
                  tab, tab, tab],
        out_specs=_spec((2, t, HEAD_PAD), lambda h, i: (0, i, h)), out_shape=_sd((2, S, HP), BF),
        scratch_shapes=[pltpu.VMEM((S, 2 * HEAD_PAD), BF)],
        compiler_params=_params(("parallel", "arbitrary"), 8 * S * HEAD_PAD * 2 + 32 * t * t * 4))(
            q_all, kv_all, kr, do, lse, delta, cos, sa, sb)


def _attn_dkv(name, q_all, kv_all, kr, do, lse_row, delta_row):
    _, S, HP = q_all.shape
    H = HP // HEAD_PAD
    t = _attn_tile(S)
    nq = S // t

    def body(q_ref, kv_ref, kr_ref, do_ref, lse_ref, dl_ref, dkv_ref, dkr_ref, qcat_ref):
        j = pl.program_id(1)

        @pl.when(j == 0)
        def _():
            qcat_ref[:, :HEAD_PAD] = q_ref[0]
            qcat_ref[:, HEAD_PAD:] = q_ref[1]

        kj = jnp.concatenate([kv_ref[0], kr_ref[...]], axis=1)
        vj = kv_ref[1]

        def step(i, carry, masked):
            dk, dv = carry
            off = pl.multiple_of(i * t, t)
            qi = qcat_ref[pl.ds(off, t), :]
            doi = do_ref[pl.ds(off, t), :].astype(BF)
            st = lax.dot_general(kj, qi, _NT, preferred_element_type=F32)
            if masked:
                st = jnp.where(_diag_mask(t, True), st, NEG)
            pt = jnp.exp(st - lse_ref[i])
            dv2 = dv + jnp.dot(pt.astype(BF), doi, preferred_element_type=F32)
            dpt = lax.dot_general(vj, doi, _NT, preferred_element_type=F32)
            dst = (pt * (dpt - dl_ref[i])).astype(BF)
            return dk + jnp.dot(dst, qi, preferred_element_type=F32), dv2

        carry = step(j, (jnp.zeros((t, 2 * HEAD_PAD), F32), jnp.zeros((t, V_DIM), F32)), True)
        dk, dv = lax.fori_loop(j + 1, nq, lambda i, c: step(i, c, False), carry)
        dkv_ref[0] = dk[:, :HEAD_PAD].astype(BF)
        dkv_ref[1] = dv.astype(BF)
        dkr_ref[...] = dk[:, HEAD_PAD:]

    stat = _spec((None, nq, 1, t), lambda h, j: (h, 0, 0, 0))
    return pl.pallas_call(
        body, name=name, grid=(H, nq),
        in_specs=[_spec((2, S, HEAD_PAD), lambda h, j: (0, 0, h)), _spec((2, t, HEAD_PAD), lambda h, j: (0, j, h)),
                  _spec((t, HEAD_PAD), lambda h, j: (j, 0)), _spec((S, HEAD_PAD), lambda h, j: (0, h)), stat, stat],
        out_specs=[_spec((2, t, HEAD_PAD), lambda h, j: (0, j, h)), _spec((None, t, HEAD_PAD), lambda h, j: (h, j, 0))],
        out_shape=[_sd((2, S, HP), BF), _sd((H, S, HEAD_PAD), F32)],
        scratch_shapes=[pltpu.VMEM((S, 2 * HEAD_PAD), BF)],
        compiler_params=_params(("parallel", "arbitrary"), 8 * S * HEAD_PAD * 4 + 32 * t * t * 4))(
            q_all, kv_all, kr, do, lse_row, delta_row)


def _place():
    x, y, c = lax.axis_index("x"), lax.axis_index("y"), lax.axis_index("c")
    return x, y, c


_ANY = pl.BlockSpec(memory_space=pl.ANY)


def _all_gather(blocks):
    n = len(blocks)

    def body(*refs):
        ins, outs = refs[:n], refs[n:2 * n]
        send_sems, recv_sems, local_sems = refs[2 * n:]
        x, y, c = _place()
        me = 4 * x + 2 * y + c
        sibling = (x, y, 1 - c)
        chips = [(1 - x, y), (x, 1 - y), (1 - x, 1 - y)]

        def slab(a, px, py, pc):
            return outs[a].at[4 * px + 2 * py + pc]

        def copy(a, k, src, dst, to):
            return pltpu.make_async_remote_copy(src_ref=src, dst_ref=dst, send_sem=send_sems.at[a, k],
                                                recv_sem=recv_sems.at[a, k], device_id=to, device_id_type=MESH)

        local = [pltpu.make_async_copy(ins[a], outs[a].at[me], local_sems.at[a]) for a in range(n)]
        for cp in local:
            cp.start()
        sends = []
        for a in range(n):
            mine = slab(a, x, y, c)
            sends.append(copy(a, 0, ins[a], mine, sibling))
            for j, chip in enumerate(chips):
                sends.append(copy(a, 1 + j, ins[a], mine, (*chip, c)))
        for cp in sends:
            cp.start()
        for j, chip in enumerate(chips):
            for a in range(n):
                got = slab(a, *chip, c)
                copy(a, 1 + j, got, got, (x, y, c)).wait_recv()
                fwd = copy(a, 4 + j, got, got, sibling)
                fwd.start()
                sends.append(fwd)
        for a in range(n):
            got = slab(a, x, y, 1 - c)
            copy(a, 0, got, got, (x, y, c)).wait_recv()
            for j, chip in enumerate(chips):
                got = slab(a, *chip, 1 - c)
                copy(a, 4 + j, got, got, (x, y, c)).wait_recv()
        for cp in sends:
            cp.wait_send()
        for cp in local:
            cp.wait()

    return pl.pallas_call(
        body, name="weights_all_gather", in_specs=[_ANY] * n, out_specs=[_ANY] * n,
        out_shape=[_sd((N_DEV,) + b.shape, b.dtype) for b in blocks],
        scratch_shapes=[pltpu.SemaphoreType.DMA((n, 7)), pltpu.SemaphoreType.DMA((n, 7)), pltpu.SemaphoreType.DMA((n,))],
    )(*blocks)


def _pair_exchange(grads):
    n = len(grads)
    half = N_DEV // 2

    def body(*refs):
        ins, got = refs[:n], refs[n:2 * n]
        send_sems, recv_sems = refs[2 * n:]
        x, y, c = _place()
        sibling = (x, y, 1 - c)
        cps = []
        for a in range(n):
            for j in range(half):
                cps.append(pltpu.make_async_remote_copy(
                    src_ref=ins[a].at[2 * j + (1 - c)], dst_ref=got[a].at[j], send_sem=send_sems.at[a, j],
                    recv_sem=recv_sems.at[a, j], device_id=sibling, device_id_type=MESH))
        for cp in cps:
            cp.start()
        for cp in cps:
            cp.wait_recv()
        for cp in cps:
            cp.wait_send()

    return pl.pallas_call(
        body, name="grads_pair_exchange", in_specs=[_ANY] * n, out_specs=[_ANY] * n,
        out_shape=[_sd((half,) + g.shape[1:], g.dtype) for g in grads],
        scratch_shapes=[pltpu.SemaphoreType.DMA((n, half)), pltpu.SemaphoreType.DMA((n, half))],
    )(*grads)


def _chip_exchange(parts):
    n = len(parts)

    def body(*refs):
        ins, outs = refs[:n], refs[n:2 * n]
        send_sems, recv_sems, local_sems = refs[2 * n:]
        x, y, c = _place()
        mine = 2 * x + y
        chips = [(1 - x, y), (x, 1 - y), (1 - x, 1 - y)]
        cps, local = [], []
        for a in range(n):
            local.append(pltpu.make_async_copy(ins[a].at[mine], outs[a].at[mine], local_sems.at[a]))
            for k, (px, py) in enumerate(chips):
                cps.append(pltpu.make_async_remote_copy(
                    src_ref=ins[a].at[2 * px + py], dst_ref=outs[a].at[mine], send_sem=send_sems.at[a, k],
                    recv_sem=recv_sems.at[a, k], device_id=(px, py, c), device_id_type=MESH))
        for cp in local + cps:
            cp.start()
        for a in range(n):
            for k, (px, py) in enumerate(chips):
                theirs = outs[a].at[2 * px + py]
                pltpu.make_async_remote_copy(
                    src_ref=theirs, dst_ref=theirs, send_sem=send_sems.at[a, k], recv_sem=recv_sems.at[a, k],
                    device_id=(x, y, c), device_id_type=MESH).wait_recv()
        for cp in cps:
            cp.wait_send()
        for cp in local:
            cp.wait()

    return pl.pallas_call(
        body, name="grads_chip_exchange", in_specs=[_ANY] * n, out_specs=[_ANY] * n,
        out_shape=[_sd(p.shape, p.dtype) for p in parts],
        scratch_shapes=[pltpu.SemaphoreType.DMA((n, 3)), pltpu.SemaphoreType.DMA((n, 3)), pltpu.SemaphoreType.DMA((n,))],
    )(*parts)


def _all_reduce_small(name, part):
    R = part.shape[0]

    def body(p_ref, out_ref, gath_ref, send_sems, recv_sems):
        x, y, c = _place()
        me = 4 * x + 2 * y + c
        gath_ref[me] = p_ref[...]
        cps = []
        for m in range(1, N_DEV):
            to = (x ^ (m >> 2), y ^ ((m >> 1) & 1), c ^ (m & 1))
            cps.append(pltpu.make_async_remote_copy(
                src_ref=p_ref, dst_ref=gath_ref.at[me], send_sem=send_sems.at[m - 1], recv_sem=recv_sems.at[m - 1],
                device_id=to, device_id_type=MESH))
        for cp in cps:
            cp.start()
        for m in range(1, N_DEV):
            frm = 4 * (x ^ (m >> 2)) + 2 * (y ^ ((m >> 1) & 1)) + (c ^ (m & 1))
            pltpu.make_async_remote_copy(
                src_ref=p_ref, dst_ref=gath_ref.at[frm], send_sem=send_sems.at[m - 1], recv_sem=recv_sems.at[m - 1],
                device_id=(x, y, c), device_id_type=MESH).wait_recv()
        for cp in cps:
            cp.wait_send()
        tot = gath_ref[0]
        for k in range(1, N_DEV):
            tot = tot + gath_ref[k]
        out_ref[...] = tot

    vm = pl.BlockSpec(memory_space=pltpu.VMEM)
    return pl.pallas_call(
        body, name=name, in_specs=[vm], out_specs=vm, out_shape=_sd((R, LANES), F32),
        scratch_shapes=[pltpu.VMEM((N_DEV, R, LANES), F32), pltpu.SemaphoreType.DMA((N_DEV - 1,)),
                        pltpu.SemaphoreType.DMA((N_DEV - 1,))],
        compiler_params=pltpu.CompilerParams(vmem_limit_bytes=VMEM_FLOOR),
    )(part)


def _add_pairs(name, full, got, core):
    n, R, C = got.shape
    tr = _pick(R, (256, 128, 64, 32, 16))

    def body(core_ref, a_ref, b_ref, o_ref):
        o_ref[...] = (a_ref[...].astype(F32) + b_ref[...].astype(F32)).astype(BF)

    blk = _spec((None, tr, C), lambda j, i, core_ref: (j, i, 0))
    mine = _spec((None, tr, C), lambda j, i, core_ref: (2 * j + core_ref[0], i, 0))
    return pl.pallas_call(
        body, name=name, out_shape=_sd(got.shape, BF),
        grid_spec=pltpu.PrefetchScalarGridSpec(num_scalar_prefetch=1, grid=(n, R // tr), in_specs=[mine, blk],
                                               out_specs=blk),
        compiler_params=_params(("parallel", "parallel"), 16 * tr * C * 4))(core, full, got)


def _adam_math(w, g, m, v):
    m2 = ADAM_B1 * m + (1.0 - ADAM_B1) * g
    v2 = ADAM_B2 * v + (1.0 - ADAM_B2) * (g * g)
    m_hat = m2 / (1.0 - ADAM_B1 ** ADAM_STEP)
    v_hat = v2 / (1.0 - ADAM_B2 ** ADAM_STEP)
    delta = -ADAM_LR * (m_hat / (jnp.sqrt(v_hat) + ADAM_EPS) + ADAM_WD * w)
    return delta, m2, v2


def _adamw_sharded(name, parts, w, m, v):
    R, C = w.shape
    tr = _pick(R, (256, 128, 64, 32, 16, 8))

    def body(p_ref, w_ref, m_ref, v_ref, g_ref, d_ref, m2_ref, v2_ref):
        g = p_ref[0].astype(F32)
        for j in range(1, p_ref.shape[0]):
            g = g + p_ref[j].astype(F32)
        delta, m2, v2 = _adam_math(w_ref[...], g, m_ref[...], v_ref[...])
        g_ref[...] = g
        d_ref[...] = delta
        m2_ref[...] = m2
        v2_ref[...] = v2

    blk = _spec((tr, C), lambda i: (i, 0))
    return pl.pallas_call(
        body, name=name, grid=(R // tr,),
        in_specs=[_spec((parts.shape[0], tr, C), lambda i: (0, i, 0)), blk, blk, blk],
        out_specs=[blk] * 4, out_shape=[_sd((R, C), F32)] * 4,
        compiler_params=_params(("parallel",), 32 * tr * C * 4))(parts, w, m, v)


def _adamw_packed(name, w, g, m, v):
    R = w.shape[0]

    def body(w_ref, g_ref, m_ref, v_ref, d_ref, m2_ref, v2_ref):
        delta, m2, v2 = _adam_math(w_ref[...], g_ref[...], m_ref[...], v_ref[...])
        d_ref[...] = delta
        m2_ref[...] = m2
        v2_ref[...] = v2

    vm = pl.BlockSpec(memory_space=pltpu.VMEM)
    return pl.pallas_call(
        body, name=name, in_specs=[vm] * 4, out_specs=[vm] * 3, out_shape=[_sd((R, LANES), F32)] * 3,
        compiler_params=pltpu.CompilerParams(vmem_limit_bytes=VMEM_FLOOR))(w, g, m, v)


def _pack(arrays):
    flat = jnp.concatenate([a.reshape(-1).astype(F32) for a in arrays])
    pad = (-flat.shape[0]) % (8 * LANES)
    return jnp.pad(flat, (0, pad)).reshape(-1, LANES)


def _unpack(packed, like):
    flat = packed.reshape(-1)
    out, pos = [], 0
    for a in like:
        n = math.prod(a.shape)
        out.append(flat[pos:pos + n].reshape(a.shape))
        pos += n
    return out


def _ffn_fwd(tag, x, gain, w_in_sm, w_out_sm, l):
    h = _rms_fwd(tag + "_norm", x, gain)
    gu, act = _pair_in(tag + "_in", h, w_in_sm, l, BF, _swiglu)
    x_new = _rows_out(tag + "_out", act, w_out_sm, l, x, 0.5)
    return x_new, (x, h, gu, act)


def _ffn_bwd(tag, d, d_bf, saved, gain, w_in_sm, w_out_sm, l, n_layers, prev_in, prev_out):
    x, h, gu, act = saved
    dgu = _rows_dact(tag + "_dact", d_bf, w_out_sm, l, [gu], [_sd(gu.shape, BF)], _swiglu_bwd_epi)[0]
    g_out = _rows_wgrad(tag + "_wgrad_out", act, d_bf, 0.5, l, n_layers, prev_out)
    dh = _cols_dh(tag + "_dh", dgu, w_in_sm, l)
    g_in = _cols_wgrad(tag + "_wgrad_in", h, dgu, l, n_layers, prev_in)
    dx, dx_bf, dgain = _rms_bwd(tag + "_dnorm", dh, x, gain, d)
    return dx, dx_bf, dgain, g_in, g_out


def kernel(x, positions, ln_ffn1, ffn1_w_in, ffn1_w_out, ln_mix, ln_ffn2, ffn2_w_in, ffn2_w_out, sgu_w_in, sgu_v_gain, sgu_v_bias, sgu_w_spatial, sgu_b_spatial, sgu_w_out, mla_w_in, mla_q_norm, mla_w_q_up, mla_kv_norm, mla_w_kv_up, mla_w_out, ln_final, loss_target, m_ln_ffn1, m_ffn1_w_in, m_ffn1_w_out, m_ln_mix, m_ln_ffn2, m_ffn2_w_in, m_ffn2_w_out, m_sgu_w_in, m_sgu_v_gain, m_sgu_v_bias, m_sgu_w_spatial, m_sgu_b_spatial, m_sgu_w_out, m_mla_w_in, m_mla_q_norm, m_mla_w_q_up, m_mla_kv_norm, m_mla_w_kv_up, m_mla_w_out, m_ln_final, v_ln_ffn1, v_ffn1_w_in, v_ffn1_w_out, v_ln_mix, v_ln_ffn2, v_ffn2_w_in, v_ffn2_w_out, v_sgu_w_in, v_sgu_v_gain, v_sgu_v_bias, v_sgu_w_spatial, v_sgu_b_spatial, v_sgu_w_out, v_mla_w_in, v_mla_q_norm, v_mla_w_q_up, v_mla_kv_norm, v_mla_w_kv_up, v_mla_w_out, v_ln_final):
    S, D = x.shape[1], x.shape[2]
    L = ln_ffn1.shape[0]
    H = mla_w_q_up.shape[-1] * N_DEV // (QK_NOPE + QK_ROPE)
    xi, yi, ci = _place()
    me = 4 * xi + 2 * yi + ci
    big = dict(ffn1_w_in=ffn1_w_in, ffn1_w_out=ffn1_w_out, ffn2_w_in=ffn2_w_in, ffn2_w_out=ffn2_w_out,
               sgu_w_in=sgu_w_in, sgu_w_out=sgu_w_out, mla_w_in=mla_w_in, mla_w_q_up=mla_w_q_up,
               mla_w_kv_up=mla_w_kv_up, mla_w_out=mla_w_out)
    big_m = dict(ffn1_w_in=m_ffn1_w_in, ffn1_w_out=m_ffn1_w_out, ffn2_w_in=m_ffn2_w_in, ffn2_w_out=m_ffn2_w_out,
                 sgu_w_in=m_sgu_w_in, sgu_w_out=m_sgu_w_out, mla_w_in=m_mla_w_in, mla_w_q_up=m_mla_w_q_up,
                 mla_w_kv_up=m_mla_w_kv_up, mla_w_out=m_mla_w_out)
    big_v = dict(ffn1_w_in=v_ffn1_w_in, ffn1_w_out=v_ffn1_w_out, ffn2_w_in=v_ffn2_w_in, ffn2_w_out=v_ffn2_w_out,
                 sgu_w_in=v_sgu_w_in, sgu_w_out=v_sgu_w_out, mla_w_in=v_mla_w_in, mla_w_q_up=v_mla_w_q_up,
                 mla_w_kv_up=v_mla_w_kv_up, mla_w_out=v_mla_w_out)
    names = list(big)

    gathered = dict(zip(names, _all_gather([big[k].astype(BF) for k in names])))
    norm_rows = jnp.zeros((N_DEV, LANES), F32)
    mine = jnp.concatenate([mla_q_norm[0], mla_kv_norm[0]])
    norm_rows = lax.dynamic_update_slice(norm_rows, mine[None, :], (me, 0))
    norm_all = _all_reduce_small("norm_gains_gather", norm_rows)
    nq_sh = mla_q_norm.shape[1]
    q_gain = norm_all[:, :nq_sh].reshape(1, Q_LORA)
    kv_gain = norm_all[:, nq_sh:2 * nq_sh].reshape(1, KV_LORA)

    w_in_nat = gathered["mla_w_in"].reshape(D, Q_LORA + KV_LORA + QK_ROPE)
    w_in_pad = jnp.pad(w_in_nat, ((0, 0), (0, HEAD_PAD - QK_ROPE)))
    wq_nat = jnp.transpose(gathered["mla_w_q_up"][:, 0], (1, 0, 2)).reshape(Q_LORA, H, QK_NOPE + QK_ROPE)
    wq_t = jnp.stack([wq_nat[:, :, :QK_NOPE].reshape(Q_LORA, H * HEAD_PAD),
                      jnp.pad(wq_nat[:, :, QK_NOPE:], ((0, 0), (0, 0), (0, HEAD_PAD - QK_ROPE))).reshape(Q_LORA, H * HEAD_PAD)])
    wkv_nat = jnp.transpose(gathered["mla_w_kv_up"][:, 0], (1, 0, 2)).reshape(KV_LORA, H, QK_NOPE + V_DIM)
    wkv_t = jnp.stack([wkv_nat[:, :, :QK_NOPE].reshape(KV_LORA, H * HEAD_PAD),
                       wkv_nat[:, :, QK_NOPE:].reshape(KV_LORA, H * HEAD_PAD)])
    w_o_nat = gathered["mla_w_out"].reshape(H * V_DIM, D)
    cos, sa, sb = _rope_tables(positions[0])
    scale = float((QK_NOPE + QK_ROPE) ** -0.5)

    xs = x[0]
    w_sp = sgu_w_spatial[0]
    b_sp = sgu_b_spatial[0][:, :, None]
    saved = []
    for i in range(L):
        xs, s1 = _ffn_fwd(f"l{i}_ffn1", xs, ln_ffn1[i:i + 1], gathered["ffn1_w_in"], gathered["ffn1_w_out"], i)
        x_mix = xs
        h = _rms_fwd(f"l{i}_mix_norm", xs, ln_mix[i:i + 1])
        if i % 2 == 0:
            puv = _pair_in(f"l{i}_sgu_in", h, gathered["sgu_w_in"], i // 2, F32, None)[0]
            gated = _sgu_mid_fwd(f"l{i}_sgu_mid", puv, sgu_v_gain, sgu_v_bias, w_sp, b_sp)
            xs = _rows_out(f"l{i}_sgu_out", gated, gathered["sgu_w_out"], i // 2, xs, 1.0)
            sm = (x_mix, h, puv, gated)
        else:
            proj = _mm2(f"l{i}_mla_in", h, w_in_pad, False, False, F32, tn_cands=(384, 128))
            lat, kr = _mla_mid_fwd(f"l{i}_mla_mid", proj, q_gain, kv_gain, cos, sa, sb)

            def q_epi(accs, ex, orefs, ids):
                orefs[0][...] = (accs[0] * scale).astype(BF)

            def qr_epi(accs, ex, orefs, ids):
                for hh in range(accs[0].shape[1] // HEAD_PAD):
                    sl = slice(hh * HEAD_PAD, (hh + 1) * HEAD_PAD)
                    orefs[0][:, sl] = (_rope(accs[0][:, sl], *ex) * scale).astype(BF)

            q_nope = _mm2(f"l{i}_mla_q", lat[0], wq_t[0], False, False, BF, epi=q_epi)
            q_rope = _mm2(f"l{i}_mla_qr", lat[0], wq_t[1], False, False, BF, epi=qr_epi, extras=(cos, sa, sb))
            q_all = jnp.stack([q_nope, q_rope])
            kv_all = jnp.stack([_mm2(f"l{i}_mla_k", lat[1], wkv_t[0], False, False, BF),
                                _mm2(f"l{i}_mla_v", lat[1], wkv_t[1], False, False, BF)])
            o, lse = _attn_fwd(f"l{i}_attn", q_all, kv_all, kr)
            xs = _mm2(f"l{i}_mla_out", o, w_o_nat, False, False, F32, res=xs)
            sm = (x_mix, h, proj, lat, kr, q_all, kv_all, o, lse)
        xs, s2 = _ffn_fwd(f"l{i}_ffn2", xs, ln_ffn2[i:i + 1], gathered["ffn2_w_in"], gathered["ffn2_w_out"], i)
        saved.append((s1, sm, s2))

    loss_row, d, d_bf, g_ln_final = _final_loss("final_loss", xs, ln_final[None, :], loss_target[0])
    loss = lax.psum(loss_row[0, 0], ("x", "y", "c"))

    gw = {k: None for k in names}
    g_ln1, g_ln2, g_lnm = [None] * L, [None] * L, [None] * L
    small_g = {}
    for i in reversed(range(L)):
        s1, sm, s2 = saved[i]
        d, d_bf, g_ln2[i], gw["ffn2_w_in"], gw["ffn2_w_out"] = _ffn_bwd(
            f"l{i}_ffn2", d, d_bf, s2, ln_ffn2[i:i + 1], gathered["ffn2_w_in"], gathered["ffn2_w_out"], i, L,
            gw["ffn2_w_in"], gw["ffn2_w_out"])
        if i % 2 == 0:
            x_mix, h, puv, gated = sm
            dgated = _rows_dact(f"l{i}_sgu_dgated", d_bf, gathered["sgu_w_out"], i // 2, [], [_sd(gated.shape, BF)],
                                _store())[0]
            gw["sgu_w_out"] = _rows_wgrad(f"l{i}_sgu_wgrad_out", gated, d_bf, None, i // 2, 1, gw["sgu_w_out"])
            dpuv, dgain, dbias, dwsp, dbsp = _sgu_mid_bwd(f"l{i}_sgu_mid_bwd", puv, dgated, sgu_v_gain, sgu_v_bias,
                                                          w_sp, b_sp)
            small_g.update(sgu_v_gain=dgain, sgu_v_bias=dbias, sgu_w_spatial=dwsp[None], sgu_b_spatial=dbsp[None, :, :, 0])
            dh = _cols_dh(f"l{i}_sgu_dh", dpuv, gathered["sgu_w_in"], i // 2)
            gw["sgu_w_in"] = _cols_wgrad(f"l{i}_sgu_wgrad_in", h, dpuv, i // 2, 1, gw["sgu_w_in"])
        else:
            x_mix, h, proj, lat, kr, q_all, kv_all, o, lse = sm
            t = _attn_tile(S)
            do = _mm2(f"l{i}_mla_do", d_bf, w_o_nat, False, True, F32)
            g_wo = _mm2(f"l{i}_mla_wgrad_out", o, d_bf, True, False, BF)
            delta = _attn_delta(f"l{i}_attn_delta", do, o)
            dq_all = _attn_dq(f"l{i}_attn_dq", q_all, kv_all, kr, do, lse, delta, cos, sa, sb, scale)
            lse_row = lse[:, :, 0].reshape(H, S // t, 1, t)
            delta_row = delta[:, :, 0].reshape(H, S // t, 1, t)
            dkv_all, dkr_heads = _attn_dkv(f"l{i}_attn_dkv", q_all, kv_all, kr, do, lse_row, delta_row)
            dqn = _mm2(f"l{i}_mla_dqn", dq_all[0], wq_t[0], False, True, F32)
            dqn = _mm2(f"l{i}_mla_dqn2", dq_all[1], wq_t[1], False, True, F32, res=dqn)
            dkvn = _mm2(f"l{i}_mla_dkvn", dkv_all[0], wkv_t[0], False, True, F32)
            dkvn = _mm2(f"l{i}_mla_dkvn2", dkv_all[1], wkv_t[1], False, True, F32, res=dkvn)
            g_wq = [_mm2(f"l{i}_mla_wgrad_q{t2}", lat[0], dq_all[t2], True, False, BF) for t2 in range(2)]
            g_wkv = [_mm2(f"l{i}_mla_wgrad_kv{t2}", lat[1], dkv_all[t2], True, False, BF) for t2 in range(2)]
            dproj, g_qn, g_kvn = _mla_mid_bwd(f"l{i}_mla_mid_bwd", proj, dqn, dkvn, dkr_heads, q_gain, kv_gain, cos, sa, sb)
            dh = _mm2(f"l{i}_mla_dh", dproj, w_in_pad, False, True, F32, tn_cands=(512, 256, 128))
            g_win = _mm2(f"l{i}_mla_wgrad_in", h, dproj, True, False, BF, tn_cands=(384, 128))
            gw["mla_w_in"] = g_win[:, :Q_LORA + KV_LORA + QK_ROPE].reshape(N_DEV, 1, D // N_DEV, Q_LORA + KV_LORA + QK_ROPE)
            gq_nat = jnp.concatenate([g_wq[0].reshape(Q_LORA, H, HEAD_PAD),
                                      g_wq[1].reshape(Q_LORA, H, HEAD_PAD)[:, :, :QK_ROPE]], axis=2)
            gw["mla_w_q_up"] = jnp.transpose(gq_nat.reshape(Q_LORA, N_DEV, -1), (1, 0, 2))[:, None]
            gkv_nat = jnp.concatenate([g_wkv[0].reshape(KV_LORA, H, HEAD_PAD), g_wkv[1].reshape(KV_LORA, H, HEAD_PAD)], axis=2)
            gw["mla_w_kv_up"] = jnp.transpose(gkv_nat.reshape(KV_LORA, N_DEV, -1), (1, 0, 2))[:, None]
            gw["mla_w_out"] = g_wo.reshape(N_DEV, 1, H * V_DIM // N_DEV, D)
            small_g.update(mla_q_norm=g_qn, mla_kv_norm=g_kvn)
        d, d_bf, g_lnm[i] = _rms_bwd(f"l{i}_mix_dnorm", dh, x_mix, ln_mix[i:i + 1], d)
        d, d_bf, g_ln1[i], gw["ffn1_w_in"], gw["ffn1_w_out"] = _ffn_bwd(
            f"l{i}_ffn1", d, d_bf, s1, ln_ffn1[i:i + 1], gathered["ffn1_w_in"], gathered["ffn1_w_out"], i, L,
            gw["ffn1_w_in"], gw["ffn1_w_out"])
    grad_x = d[None]

    got = _pair_exchange([gw[k] for k in names])
    core = jnp.reshape(ci, (1,)).astype(jnp.int32)
    parts = []
    for k, b in zip(names, got):
        rc = (math.prod(b.shape[1:-1]), b.shape[-1])
        parts.append(_add_pairs(f"pair_sum_{k}", gw[k].reshape((N_DEV,) + rc), b.reshape((b.shape[0],) + rc),
                                core).reshape(b.shape))
    landed = _chip_exchange(parts)
    big_out = {}
    for k, p in zip(names, landed):
        w = big[k]
        flat = (math.prod(w.shape[:-1]), w.shape[-1])
        res = _adamw_sharded(f"adamw_{k}", p.reshape((p.shape[0],) + flat), w.reshape(flat), big_m[k].reshape(flat),
                             big_v[k].reshape(flat))
        big_out[k] = [r.reshape(w.shape) for r in res]

    small_g.update(ln_ffn1=jnp.concatenate(g_ln1), ln_mix=jnp.concatenate(g_lnm), ln_ffn2=jnp.concatenate(g_ln2),
                   ln_final=g_ln_final[0])
    small_names = ["ln_ffn1", "ln_mix", "ln_ffn2", "sgu_v_gain", "sgu_v_bias", "sgu_w_spatial", "sgu_b_spatial",
                   "ln_final", "mla_q_norm", "mla_kv_norm"]
    summed = _unpack(_all_reduce_small("small_grads_all_reduce", _pack([small_g[k] for k in small_names])),
                     [small_g[k] for k in small_names])
    small_grad = dict(zip(small_names, summed))
    for k in ("mla_q_norm", "mla_kv_norm"):
        small_grad[k] = lax.dynamic_slice(small_grad[k], (0, me * nq_sh), (1, nq_sh))
    small_w = dict(ln_ffn1=ln_ffn1, ln_mix=ln_mix, ln_ffn2=ln_ffn2, sgu_v_gain=sgu_v_gain, sgu_v_bias=sgu_v_bias,
                   sgu_w_spatial=sgu_w_spatial, sgu_b_spatial=sgu_b_spatial, ln_final=ln_final, mla_q_norm=mla_q_norm,
                   mla_kv_norm=mla_kv_norm)
    small_m = dict(ln_ffn1=m_ln_ffn1, ln_mix=m_ln_mix, ln_ffn2=m_ln_ffn2, sgu_v_gain=m_sgu_v_gain, sgu_v_bias=m_sgu_v_bias,
                   sgu_w_spatial=m_sgu_w_spatial, sgu_b_spatial=m_sgu_b_spatial, ln_final=m_ln_final,
                   mla_q_norm=m_mla_q_norm, mla_kv_norm=m_mla_kv_norm)
    small_v = dict(ln_ffn1=v_ln_ffn1, ln_mix=v_ln_mix, ln_ffn2=v_ln_ffn2, sgu_v_gain=v_sgu_v_gain, sgu_v_bias=v_sgu_v_bias,
                   sgu_w_spatial=v_sgu_w_spatial, sgu_b_spatial=v_sgu_b_spatial, ln_final=v_ln_final,
                   mla_q_norm=v_mla_q_norm, mla_kv_norm=v_mla_kv_norm)
    like = [small_w[k] for k in small_names]
    packed = _adamw_packed("adamw_small", _pack(like), _pack([small_grad[k] for k in small_names]),
                           _pack([small_m[k] for k in small_names]), _pack([small_v[k] for k in small_names]))
    small_out = {}
    unpacked = [_unpack(p, like) for p in packed]
    for idx, k in enumerate(small_names):
        small_out[k] = [small_grad[k].reshape(small_w[k].shape)] + [u[idx] for u in unpacked]

    order = ["ln_ffn1", "ffn1_w_in", "ffn1_w_out", "ln_mix", "ln_ffn2", "ffn2_w_in", "ffn2_w_out", "sgu_w_in",
             "sgu_v_gain", "sgu_v_bias", "sgu_w_spatial", "sgu_b_spatial", "sgu_w_out", "mla_w_in", "mla_q_norm",
             "mla_w_q_up", "mla_kv_norm", "mla_w_kv_up", "mla_w_out", "ln_final"]
    res = {k: (big_out[k] if k in big_out else small_out[k]) for k in order}
    outs = [loss, grad_x]
    for part in range(4):
        outs.extend(res[k][part] for k in order)
    return tuple(outs)
```

```python
import math

import jax
import jax.numpy as jnp
from jax import lax
from jax.experimental import pallas as pl
from jax.experimental.pallas import tpu as pltpu

F32 = jnp.float32
BF = jnp.bfloat16
MESH = pl.DeviceIdType.MESH

N_DEV = 8
EPS = 1e-6
CHUNK = 64
SGU_BLOCK = 128
SGU_GROUPS = 8
Q_LORA = 512
KV_LORA = 512
QK_NOPE = 128
QK_ROPE = 64
V_DIM = 128
ROPE_THETA = 10000.0
HEAD_PAD = 128
LANES = 128
ADAM_LR = 0.001
ADAM_B1 = 0.9
ADAM_B2 = 0.999
ADAM_EPS = 1e-08
ADAM_WD = 0.01
ADAM_STEP = 10
V7X_VMEM_BYTES = 64 * 1024 * 1024
VMEM_CAP = V7X_VMEM_BYTES - 6 * 1024 * 1024
VMEM_FLOOR = 32 * 1024 * 1024
NEG = -1e30
ATTN_GROUPS = 1


def _pick(n, cands):
    for c in cands:
        if n % c == 0:
            return c
    return n


def _nbytes(shape, dtype):
    return math.prod(int(s) for s in shape if s is not None) * jnp.dtype(dtype).itemsize


def _params(sem, block_bytes):
    limit = int(min(VMEM_CAP, max(VMEM_FLOOR, block_bytes)))
    return pltpu.CompilerParams(dimension_semantics=sem, vmem_limit_bytes=limit)


def _spec(shape, fn):
    return pl.BlockSpec(shape, fn)


_ANY = pl.BlockSpec(memory_space=pl.ANY)


def _mm(name, grid, ops, pairs, extras, outs, epilogue, acc_shapes, deps=()):
    nk = grid[2]
    n_ops, n_ex, n_out = len(ops), len(extras), len(outs)

    def load(refs, idx):
        loader = ops[idx][2] if len(ops[idx]) > 2 else None
        return (refs[idx][...] if loader is None else loader(refs[idx])).astype(BF)

    def prod(refs, p):
        ia, ib, ta, tb, _ = p
        a = load(refs, ia)
        b = load(refs, ib)
        dims = (((0 if ta else 1,), (1 if tb else 0,)), ((), ()))
        return lax.dot_general(a, b, dims, preferred_element_type=F32)

    def body(*refs):
        op_refs = refs[:n_ops]
        ex_refs = refs[n_ops:n_ops + n_ex]
        n_in = n_ops + n_ex + len(deps)
        out_refs = refs[n_in:n_in + n_out]
        acc_refs = refs[n_in + n_out:]
        ids = (pl.program_id(0), pl.program_id(1))

        def finish(vals):
            epilogue(vals, [e[...] for e in ex_refs], out_refs, ids)

        if nk == 1:
            vals = [None] * len(acc_shapes)
            for p in pairs:
                r = prod(op_refs, p)
                vals[p[4]] = r if vals[p[4]] is None else vals[p[4]] + r
            finish(vals)
        else:
            k = pl.program_id(2)

            @pl.when(k == 0)
            def _():
                for a in acc_refs:
                    a[...] = jnp.zeros_like(a)

            for p in pairs:
                acc_refs[p[4]][...] += prod(op_refs, p)

            @pl.when(k == nk - 1)
            def _():
                finish([a[...] for a in acc_refs])

    in_arrays = [o[0] for o in ops] + [e[0] for e in extras]
    in_specs = [o[1] for o in ops] + [e[1] for e in extras]
    in_arrays += list(deps)
    in_specs += [_ANY] * len(deps)
    blk = 0
    for entry in ops + extras:
        blk += 2 * _nbytes(entry[1].block_shape, entry[0].dtype)
    for sd, sp in outs:
        blk += 2 * _nbytes(sp.block_shape, sd.dtype)
    acc_b = sum(_nbytes(s, F32) for s in acc_shapes)
    blk += 6 * acc_b
    scratch = [pltpu.VMEM(s, F32) for s in acc_shapes] if nk > 1 else []
    res = pl.pallas_call(
        body, name=name, grid=grid, in_specs=in_specs,
        out_specs=[o[1] for o in outs], out_shape=[o[0] for o in outs],
        scratch_shapes=scratch,
        compiler_params=_params(("parallel", "parallel", "arbitrary"), blk))(*in_arrays)
    return res


def _store(scale=None):
    def epi(accs, ex, outs, ids):
        v = accs[0]
        if scale is not None:
            v = v * scale
        outs[0][...] = v.astype(outs[0].dtype)
    return epi


def _store_residual(scale):
    def epi(accs, ex, outs, ids):
        outs[0][...] = ex[0] + scale * accs[0]
    return epi


def _sd(shape, dtype):
    return jax.ShapeDtypeStruct(tuple(shape), dtype)


def _pair_in(name, h, w_sm, out_dtype, act):
    S, D = h.shape
    c = w_sm.shape[-1]
    tm = _pick(S, (256, 128))
    half = N_DEV // 2
    ops = [(h, _spec((tm, D), lambda j, i, k: (i, 0))),
           (w_sm, _spec((None, D, c), lambda j, i, k: (j, 0, 0))),
           (w_sm, _spec((None, D, c), lambda j, i, k: (j + half, 0, 0)))]
    outs = [(_sd((2, S, half * c), out_dtype), _spec((2, tm, c), lambda j, i, k: (0, i, j)))]
    if act is not None:
        outs.append((_sd((S, half * c), BF), _spec((tm, c), lambda j, i, k: (i, j))))

    def epi(accs, ex, orefs, ids):
        orefs[0][0] = accs[0].astype(out_dtype)
        orefs[0][1] = accs[1].astype(out_dtype)
        if act is not None:
            orefs[1][...] = act(accs[0], accs[1]).astype(BF)

    return _mm(name, (half, S // tm, 1), ops, [(0, 1, False, False, 0), (0, 2, False, False, 1)], [], outs, epi,
               [(tm, c), (tm, c)])


def _two_slabs(ref):
    return jnp.concatenate([ref[0], ref[1]], axis=0)


def _rows_out(name, a, w_sm, res, scale):
    S = a.shape[0]
    r, D = w_sm.shape[-2], w_sm.shape[-1]
    tm = _pick(S, (1024, 512, 256, 128))
    tn = _pick(D, (1024, 512, 256, 128))
    ops = [(a, _spec((tm, 2 * r), lambda i, j, k: (i, k))),
           (w_sm, _spec((2, r, tn), lambda i, j, k: (k, 0, j)), _two_slabs)]
    extras = [(res, _spec((tm, tn), lambda i, j, k: (i, j)))]
    outs = [(_sd((S, D), F32), _spec((tm, tn), lambda i, j, k: (i, j)))]
    return _mm(name, (S // tm, D // tn, N_DEV // 2), ops, [(0, 1, False, False, 0)], extras, outs,
               _store_residual(scale), [(tm, tn)])[0]


def _rows_dact(name, d_bf, w_sm, extras_arrays, out_shapes, epi):
    S, D = d_bf.shape
    r = w_sm.shape[-2]
    tm = _pick(S, (1024, 512, 256, 128))
    ops = [(d_bf, _spec((tm, D), lambda j, i, k: (i, 0))),
           (w_sm, _spec((2, r, D), lambda j, i, k: (j, 0, 0)), _two_slabs)]
    extras = []
    for arr in extras_arrays:
        if arr.ndim == 3:
            extras.append((arr, _spec((arr.shape[0], tm, 2 * r), lambda j, i, k: (0, i, j))))
        else:
            extras.append((arr, _spec((tm, 2 * r), lambda j, i, k: (i, j))))
    outs = []
    for sd in out_shapes:
        if len(sd.shape) == 3:
            outs.append((sd, _spec((sd.shape[0], tm, 2 * r), lambda j, i, k: (0, i, j))))
        else:
            outs.append((sd, _spec((tm, 2 * r), lambda j, i, k: (i, j))))
    return _mm(name, (N_DEV // 2, S // tm, 1), ops, [(0, 1, False, True, 0)], extras, outs, epi, [(tm, 2 * r)])


def _rows_wgrad(name, a, d_bf, scale):
    S, D = d_bf.shape
    r = a.shape[1] // N_DEV
    tn = _pick(D, (2048, 1024, 512, 256, 128))
    tk = _pick(S, (1024, 512, 256, 128))
    ops = [(a, _spec((tk, 2 * r), lambda s, j, k: (k, s))),
           (d_bf, _spec((tk, tn), lambda s, j, k: (k, j)))]
    outs = [(_sd((N_DEV, r, D), BF), _spec((2, r, tn), lambda s, j, k: (s, 0, j)))]

    def epi(accs, ex, orefs, ids):
        v = accs[0] if scale is None else accs[0] * scale
        orefs[0][0] = v[:r].astype(BF)
        orefs[0][1] = v[r:].astype(BF)

    return _mm(name, (N_DEV // 2, D // tn, S // tk), ops, [(0, 1, True, False, 0)], [], outs, epi, [(2 * r, tn)])[0]


def _cols_dh(name, dpair, w_sm, deps=()):
    _, S, _ = dpair.shape
    D, c = w_sm.shape[-2], w_sm.shape[-1]
    half = N_DEV // 2
    tm = _pick(S, (1024, 512, 256, 128))
    tn = _pick(D, (1024, 512, 256, 128))
    ops = [(dpair, _spec((None, tm, c), lambda i, j, k: (k // half, i, k % half))),
           (w_sm, _spec((None, tn, c), lambda i, j, k: (k, j, 0)))]
    outs = [(_sd((S, D), F32), _spec((tm, tn), lambda i, j, k: (i, j)))]
    return _mm(name, (S // tm, D // tn, N_DEV), ops, [(0, 1, False, True, 0)], [], outs, _store(), [(tm, tn)],
               deps=deps)[0]


def _cols_wgrad(name, h, dpair):
    S, D = h.shape
    half = N_DEV // 2
    c = dpair.shape[2] // half
    tm = _pick(D, (1024, 512, 256, 128))
    tk = _pick(S, (1024, 512, 256, 128))
    ops = [(h, _spec((tk, tm), lambda s, i, k: (k, i))),
           (dpair, _spec((None, tk, c), lambda s, i, k: (s // half, k, s % half)))]
    outs = [(_sd((N_DEV, D, c), BF), _spec((None, tm, c), lambda s, i, k: (s, i, 0)))]
    return _mm(name, (N_DEV, D // tm, S // tk), ops, [(0, 1, True, False, 0)], [], outs, _store(), [(tm, c)])[0]


def _mm2(name, a, b, ta, tb, out_dtype, epi=None, extras=(), res=None, tn_cands=(512, 384, 256, 128), deps=()):
    M = a.shape[1] if ta else a.shape[0]
    K = a.shape[0] if ta else a.shape[1]
    N = b.shape[0] if tb else b.shape[1]
    tm = _pick(M, (1024, 512, 256, 128))
    tn = _pick(N, tn_cands)
    tk = _pick(K, (2048, 1152, 1024, 512, 256, 128))
    a_spec = _spec((tk, tm), lambda i, j, k: (k, i)) if ta else _spec((tm, tk), lambda i, j, k: (i, k))
    b_spec = _spec((tn, tk), lambda i, j, k: (j, k)) if tb else _spec((tk, tn), lambda i, j, k: (k, j))
    ex = [(e, _spec((tm, e.shape[1]), lambda i, j, k: (i, 0))) for e in extras]
    if res is not None:
        ex = [(res, _spec((tm, tn), lambda i, j, k: (i, j)))]
        epi = _store_residual(1.0)
    outs = [(_sd((M, N), out_dtype), _spec((tm, tn), lambda i, j, k: (i, j)))]
    return _mm(name, (M // tm, N // tn, K // tk), [(a, a_spec), (b, b_spec)], [(0, 1, ta, tb, 0)], ex, outs,
               epi or _store(), [(tm, tn)], deps=deps)[0]


def _rms_fwd(name, x, g, deps=()):
    S, D = x.shape
    ts = _pick(S, (512, 256, 128))

    def body(x_ref, g_ref, *rest):
        h_ref = rest[-1]
        xv = x_ref[...]
        r = lax.rsqrt(jnp.mean(xv * xv, axis=-1, keepdims=True) + EPS)
        h_ref[...] = (xv * r * g_ref[...]).astype(BF)

    return pl.pallas_call(
        body, name=name, grid=(S // ts,),
        in_specs=[_spec((ts, D), lambda i: (i, 0)), _spec((1, D), lambda i: (0, 0))] + [_ANY] * len(deps),
        out_specs=_spec((ts, D), lambda i: (i, 0)), out_shape=_sd((S, D), BF),
        compiler_params=_params(("parallel",), 12 * ts * D * 4))(x, g, *deps)


def _rms_bwd(name, dh, x, g, dres, deps=()):
    S, D = x.shape
    ts = _pick(S, (256, 128))

    def body(dh_ref, x_ref, g_ref, dres_ref, *rest):
        dx_ref, dxb_ref, dg_ref = rest[len(deps):]
        xv = x_ref[...]
        dhv = dh_ref[...]
        r = lax.rsqrt(jnp.mean(xv * xv, axis=-1, keepdims=True) + EPS)
        xhat = xv * r
        dxh = dhv * g_ref[...]
        cm = jnp.mean(dxh * xhat, axis=-1, keepdims=True)
        dx = r * (dxh - xhat * cm) + dres_ref[...]
        dx_ref[...] = dx
        dxb_ref[...] = dx.astype(BF)

        @pl.when(pl.program_id(0) == 0)
        def _():
            dg_ref[...] = jnp.zeros_like(dg_ref)

        dg_ref[...] += jnp.sum(dhv * xhat, axis=0, keepdims=True)

    row = _spec((ts, D), lambda i: (i, 0))
    vec = _spec((1, D), lambda i: (0, 0))
    return pl.pallas_call(
        body, name=name, grid=(S // ts,),
        in_specs=[row, row, vec, row] + [_ANY] * len(deps), out_specs=[row, row, vec],
        out_shape=[_sd((S, D), F32), _sd((S, D), BF), _sd((1, D), F32)],
        compiler_params=_params(("arbitrary",), 20 * ts * D * 4))(dh, x, g, dres, *deps)


def _final_loss(name, x, g, target):
    S, D = x.shape
    ts = _pick(S, (256, 128))

    def body(x_ref, g_ref, t_ref, loss_ref, dx_ref, dxb_ref, dg_ref):
        xv = x_ref[...]
        gv = g_ref[...]
        r = lax.rsqrt(jnp.mean(xv * xv, axis=-1, keepdims=True) + EPS)
        xhat = xv * r
        err = xhat * gv - t_ref[...]
        part = 0.5 * jnp.sum(jnp.mean(err * err, axis=-1, keepdims=True), axis=0, keepdims=True)
        dy = err * (1.0 / D)
        dxh = dy * gv
        cm = jnp.mean(dxh * xhat, axis=-1, keepdims=True)
        dx = r * (dxh - xhat * cm)
        dx_ref[...] = dx
        dxb_ref[...] = dx.astype(BF)

        @pl.when(pl.program_id(0) == 0)
        def _():
            dg_ref[...] = jnp.zeros_like(dg_ref)
            loss_ref[...] = jnp.zeros_like(loss_ref)

        dg_ref[...] += jnp.sum(dy * xhat, axis=0, keepdims=True)
        loss_ref[...] += jnp.broadcast_to(part, loss_ref.shape)

    row = _spec((ts, D), lambda i: (i, 0))
    vec = _spec((1, D), lambda i: (0, 0))
    return pl.pallas_call(
        body, name=name, grid=(S // ts,),
        in_specs=[row, vec, row], out_specs=[_spec((1, LANES), lambda i: (0, 0)), row, row, vec],
        out_shape=[_sd((1, LANES), F32), _sd((S, D), F32), _sd((S, D), BF), _sd((1, D), F32)],
        compiler_params=_params(("arbitrary",), 20 * ts * D * 4))(x, g, target)


def _swiglu(gate, up):
    return gate * jax.nn.sigmoid(gate) * up


def _swiglu_bwd_epi(accs, ex, orefs, ids):
    da = 0.5 * accs[0]
    gate = ex[0][0].astype(F32)
    up = ex[0][1].astype(F32)
    sg = jax.nn.sigmoid(gate)
    orefs[0][0] = (da * up * (sg * (1.0 + gate * (1.0 - sg)))).astype(BF)
    orefs[0][1] = (da * gate * sg).astype(BF)


_GELU_C = math.sqrt(2.0 / math.pi)


def _gelu(x):
    return x * (0.5 * (1.0 + jnp.tanh(_GELU_C * (x + 0.044715 * (x * x * x)))))


def _gelu_grad(x):
    t = jnp.tanh(_GELU_C * (x + 0.044715 * (x * x * x)))
    return 0.5 * (1.0 + t) + x * (0.5 * (1.0 - t * t) * _GELU_C * (1.0 + 3.0 * 0.044715 * (x * x)))


def _causal_block_mask():
    row = lax.broadcasted_iota(jnp.int32, (SGU_BLOCK, SGU_BLOCK), 0) // CHUNK
    col = lax.broadcasted_iota(jnp.int32, (SGU_BLOCK, SGU_BLOCK), 1) // CHUNK
    return row >= col


def _sgu_mid_fwd(name, puv, gain, bias, w_sp, b_sp):
    _, S, W = puv.shape
    G = SGU_GROUPS
    C = W // G
    T = SGU_BLOCK

    def body(puv_ref, gain_ref, bias_ref, w_ref, b_ref, out_ref):
        mask = _causal_block_mask()
        v = _gelu(puv_ref[1])
        mu = jnp.mean(v, axis=-1, keepdims=True)
        vc = v - mu
        rs = lax.rsqrt(jnp.mean(vc * vc, axis=-1, keepdims=True) + EPS)
        vln = (vc * rs * gain_ref[...] + bias_ref[...]).astype(BF)
        for g in range(G):
            wg = jnp.where(mask, w_ref[g], 0.0).astype(BF)
            mixed = jnp.dot(wg, vln[:, g * C:(g + 1) * C], preferred_element_type=F32) + b_ref[g]
            out_ref[:, g * C:(g + 1) * C] = (_gelu(puv_ref[0, :, g * C:(g + 1) * C]) * mixed).astype(BF)

    return pl.pallas_call(
        body, name=name, grid=(S // T,),
        in_specs=[_spec((2, T, W), lambda i: (0, i, 0)), _spec((1, W), lambda i: (0, 0)), _spec((1, W), lambda i: (0, 0)),
                  _spec((G, T, T), lambda i: (0, 0, 0)), _spec((G, T, 1), lambda i: (0, 0, 0))],
        out_specs=_spec((T, W), lambda i: (i, 0)), out_shape=_sd((S, W), BF),
        compiler_params=_params(("parallel",), 16 * T * W * 4))(puv, gain, bias, w_sp, b_sp)


def _sgu_mid_bwd(name, puv, dgated, gain, bias, w_sp, b_sp):
    _, S, W = puv.shape
    G = SGU_GROUPS
    C = W // G
    T = SGU_BLOCK

    def body(puv_ref, dg_ref, gain_ref, bias_ref, w_ref, b_ref, dpuv_ref, dgain_ref, dbias_ref, dw_ref, db_ref, dvln_ref):
        @pl.when(pl.program_id(0) == 0)
        def _():
            dgain_ref[...] = jnp.zeros_like(dgain_ref)
            dbias_ref[...] = jnp.zeros_like(dbias_ref)
            dw_ref[...] = jnp.zeros_like(dw_ref)
            db_ref[...] = jnp.zeros_like(db_ref)

        mask = _causal_block_mask()
        pv = puv_ref[1]
        v = _gelu(pv)
        mu = jnp.mean(v, axis=-1, keepdims=True)
        vc = v - mu
        rs = lax.rsqrt(jnp.mean(vc * vc, axis=-1, keepdims=True) + EPS)
        vhat = vc * rs
        gain_v = gain_ref[...]
        vln = (vhat * gain_v + bias_ref[...]).astype(BF)
        for g in range(G):
            sl = slice(g * C, (g + 1) * C)
            wg = jnp.where(mask, w_ref[g], 0.0).astype(BF)
            vg = vln[:, sl]
            mixed = jnp.dot(wg, vg, preferred_element_type=F32) + b_ref[g]
            pu = puv_ref[0, :, sl]
            dgt = dg_ref[:, sl].astype(F32)
            dpuv_ref[0, :, sl] = (dgt * mixed * _gelu_grad(pu)).astype(BF)
            dmix = dgt * _gelu(pu)
            db_ref[g] += jnp.sum(dmix, axis=-1, keepdims=True)
            dmb = dmix.astype(BF)
            dwg = lax.dot_general(dmb, vg, (((1,), (1,)), ((), ())), preferred_element_type=F32)
            dw_ref[g] += jnp.where(mask, dwg, 0.0)
            dvln_ref[:, sl] = lax.dot_general(wg, dmb, (((0,), (0,)), ((), ())), preferred_element_type=F32)
        dvln = dvln_ref[...]
        dgain_ref[...] += jnp.sum(dvln * vhat, axis=0, keepdims=True)
        dbias_ref[...] += jnp.sum(dvln, axis=0, keepdims=True)
        dvh = dvln * gain_v
        m1 = jnp.mean(dvh, axis=-1, keepdims=True)
        m2 = jnp.mean(dvh * vhat, axis=-1, keepdims=True)
        dv = rs * (dvh - m1 - vhat * m2)
        dpuv_ref[1] = (dv * _gelu_grad(pv)).astype(BF)

    vec = _spec((1, W), lambda i: (0, 0))
    wsp = _spec((G, T, T), lambda i: (0, 0, 0))
    bsp = _spec((G, T, 1), lambda i: (0, 0, 0))
    return pl.pallas_call(
        body, name=name, grid=(S // T,),
        in_specs=[_spec((2, T, W), lambda i: (0, i, 0)), _spec((T, W), lambda i: (i, 0)), vec, vec, wsp, bsp],
        out_specs=[_spec((2, T, W), lambda i: (0, i, 0)), vec, vec, wsp, bsp],
        out_shape=[_sd((2, S, W), BF), _sd((1, W), F32), _sd((1, W), F32), _sd((G, T, T), F32), _sd((G, T, 1), F32)],
        scratch_shapes=[pltpu.VMEM((T, W), F32)],
        compiler_params=_params(("arbitrary",), 24 * T * W * 4))(puv, dgated, gain, bias, w_sp, b_sp)


def _rope_tables(positions):
    half = QK_ROPE // 2
    inv_freq = 1.0 / (ROPE_THETA ** (jnp.arange(half, dtype=F32) / half))
    ang = positions.astype(F32)[:, None] * inv_freq[None, :]
    cos, sin = jnp.cos(ang), jnp.sin(ang)
    z = jnp.zeros_like(cos)
    return (jnp.concatenate([cos, cos, z, z], axis=1), jnp.concatenate([-sin, z, z, z], axis=1),
            jnp.concatenate([z, sin, z, z], axis=1))


def _rope(x, cos, sa, sb):
    return x * cos + pltpu.roll(x, HEAD_PAD - QK_ROPE // 2, 1) * sa + pltpu.roll(x, QK_ROPE // 2, 1) * sb


def _rope_t(dy, cos, sa, sb):
    return dy * cos + pltpu.roll(dy * sa, QK_ROPE // 2, 1) + pltpu.roll(dy * sb, HEAD_PAD - QK_ROPE // 2, 1)


def _rms_rows(x, g):
    r = lax.rsqrt(jnp.mean(x * x, axis=-1, keepdims=True) + EPS)
    return x * r * g


def _rms_rows_bwd(dy, x, g):
    r = lax.rsqrt(jnp.mean(x * x, axis=-1, keepdims=True) + EPS)
    xhat = x * r
    dxh = dy * g
    cm = jnp.mean(dxh * xhat, axis=-1, keepdims=True)
    return r * (dxh - xhat * cm), jnp.sum(dy * xhat, axis=0, keepdims=True)


def _mla_mid_fwd(name, proj, qg, kvg, cos, sa, sb):
    S, P = proj.shape
    ts = _pick(S, (512, 256, 128))

    def body(p_ref, qg_ref, kvg_ref, cos_ref, sa_ref, sb_ref, lat_ref, kr_ref):
        lat_ref[0] = _rms_rows(p_ref[:, :Q_LORA], qg_ref[...]).astype(BF)
        lat_ref[1] = _rms_rows(p_ref[:, Q_LORA:Q_LORA + KV_LORA], kvg_ref[...]).astype(BF)
        kr_ref[...] = _rope(p_ref[:, Q_LORA + KV_LORA:], cos_ref[...], sa_ref[...], sb_ref[...]).astype(BF)

    tab = _spec((ts, HEAD_PAD), lambda i: (i, 0))
    return pl.pallas_call(
        body, name=name, grid=(S // ts,),
        in_specs=[_spec((ts, P), lambda i: (i, 0)), _spec((1, Q_LORA), lambda i: (0, 0)),
                  _spec((1, KV_LORA), lambda i: (0, 0)), tab, tab, tab],
        out_specs=[_spec((2, ts, Q_LORA), lambda i: (0, i, 0)), tab],
        out_shape=[_sd((2, S, Q_LORA), BF), _sd((S, HEAD_PAD), BF)],
        compiler_params=_params(("parallel",), 16 * ts * P * 4))(proj, qg, kvg, cos, sa, sb)


def _mla_mid_bwd(name, proj, dqn, dkvn, dkr_heads, qg, kvg, cos, sa, sb):
    S, P = proj.shape
    H = dkr_heads.shape[0]
    ts = _pick(S, (256, 128))

    def body(p_ref, dqn_ref, dkvn_ref, dkr_ref, qg_ref, kvg_ref, cos_ref, sa_ref, sb_ref, dp_ref, dqg_ref, dkvg_ref):
        @pl.when(pl.program_id(0) == 0)
        def _():
            dqg_ref[...] = jnp.zeros_like(dqg_ref)
            dkvg_ref[...] = jnp.zeros_like(dkvg_ref)

        dq, dqg = _rms_rows_bwd(dqn_ref[...], p_ref[:, :Q_LORA], qg_ref[...])
        dkv, dkvg = _rms_rows_bwd(dkvn_ref[...], p_ref[:, Q_LORA:Q_LORA + KV_LORA], kvg_ref[...])
        dqg_ref[...] += dqg
        dkvg_ref[...] += dkvg
        dkr = dkr_ref[0]
        for h in range(1, H):
            dkr = dkr + dkr_ref[h]
        dp_ref[:, :Q_LORA] = dq.astype(BF)
        dp_ref[:, Q_LORA:Q_LORA + KV_LORA] = dkv.astype(BF)
        dp_ref[:, Q_LORA + KV_LORA:] = _rope_t(dkr, cos_ref[...], sa_ref[...], sb_ref[...]).astype(BF)

    tab = _spec((ts, HEAD_PAD), lambda i: (i, 0))
    lat = _spec((ts, Q_LORA), lambda i: (i, 0))
    gq = _spec((1, Q_LORA), lambda i: (0, 0))
    return pl.pallas_call(
        body, name=name, grid=(S // ts,),
        in_specs=[_spec((ts, P), lambda i: (i, 0)), lat, lat, _spec((H, ts, HEAD_PAD), lambda i: (0, i, 0)),
                  gq, gq, tab, tab, tab],
        out_specs=[_spec((ts, P), lambda i: (i, 0)), gq, gq],
        out_shape=[_sd((S, P), BF), _sd((1, Q_LORA), F32), _sd((1, KV_LORA), F32)],
        compiler_params=_params(("arbitrary",), 24 * ts * P * 4))(proj, dqn, dkvn, dkr_heads, qg, kvg, cos, sa, sb)


def _attn_tile(S):
    return _pick(S, (512,)) if S >= 2048 else _pick(S, (128,))


def _diag_mask(t, transposed):
    q = lax.broadcasted_iota(jnp.int32, (t, t), 1 if transposed else 0) // CHUNK
    k = lax.broadcasted_iota(jnp.int32, (t, t), 0 if transposed else 1) // CHUNK
    return k <= q


_NT = (((1,), (1,)), ((), ()))


def _attn_fwd(name, q_all, kv_all, kr):
    _, S, HP = q_all.shape
    H = HP // HEAD_PAD
    t = _attn_tile(S)
    nq = S // t
    ng = ATTN_GROUPS
    tg = t // ng

    def body(q_ref, kv_ref, kr_ref, o_ref, lse_ref, kcat_ref):
        i = pl.program_id(1)

        @pl.when(i == 0)
        def _():
            kcat_ref[:, :HEAD_PAD] = kv_ref[0]
            kcat_ref[:, HEAD_PAD:] = kr_ref[...]

        qs = [jnp.concatenate([q_ref[0, g * tg:(g + 1) * tg], q_ref[1, g * tg:(g + 1) * tg]], axis=1) for g in range(ng)]

        def step(j, carry, masked):
            off = pl.multiple_of(j * t, t)
            kj = kcat_ref[pl.ds(off, t), :]
            vj = kv_ref[1, pl.ds(off, t), :]
            out = []
            for g in range(ng):
                m, l, acc = carry[g]
                s = lax.dot_general(qs[g], kj, _NT, preferred_element_type=F32)
                if masked:
                    s = jnp.where(_diag_mask(t, False)[g * tg:(g + 1) * tg], s, NEG)
                m2 = jnp.maximum(m, jnp.max(s, axis=-1, keepdims=True))
                al = jnp.exp(m - m2)
                p = jnp.exp(s - m2)
                l2 = al * l + jnp.sum(p, axis=-1, keepdims=True)
                acc2 = al * acc + jnp.dot(p.astype(BF), vj, preferred_element_type=F32)
                out.append((m2, l2, acc2))
            return tuple(out)

        init = tuple((jnp.full((tg, 1), NEG, F32), jnp.zeros((tg, 1), F32), jnp.zeros((tg, V_DIM), F32))
                     for _ in range(ng))
        carry = lax.fori_loop(0, i, lambda j, c: step(j, c, False), init)
        carry = step(i, carry, True)
        for g in range(ng):
            m, l, acc = carry[g]
            o_ref[g * tg:(g + 1) * tg, :] = acc / l
            lse_ref[g * tg:(g + 1) * tg, :] = jnp.broadcast_to(m + jnp.log(l), (tg, LANES))

    return pl.pallas_call(
        body, name=name, grid=(H, nq),
        in_specs=[_spec((2, t, HEAD_PAD), lambda h, i: (0, i, h)), _spec((2, S, HEAD_PAD), lambda h, i: (0, 0, h)),
                  _spec((S, HEAD_PAD), lambda h, i: (0, 0))],
        out_specs=[_spec((t, HEAD_PAD), lambda h, i: (i, h)), _spec((None, t, LANES), lambda h, i: (h, i, 0))],
        out_shape=[_sd((S, HP), F32), _sd((H, S, LANES), F32)],
        scratch_shapes=[pltpu.VMEM((S, 2 * HEAD_PAD), BF)],
        compiler_params=_params(("parallel", "arbitrary"), 8 * S * HEAD_PAD * 2 + 24 * t * t * 4))(q_all, kv_all, kr)


def _attn_delta(name, do, o):
    S, HP = o.shape
    H = HP // HEAD_PAD
    ts = _pick(S, (512, 256, 128))

    def body(do_ref, o_ref, d_ref):
        d_ref[...] = jnp.broadcast_to(jnp.sum(do_ref[...] * o_ref[...], axis=-1, keepdims=True), (ts, LANES))

    tile = _spec((ts, HEAD_PAD), lambda h, i: (i, h))
    return pl.pallas_call(
        body, name=name, grid=(H, S // ts), in_specs=[tile, tile],
        out_specs=_spec((None, ts, LANES), lambda h, i: (h, i, 0)), out_shape=_sd((H, S, LANES), F32),
        compiler_params=_params(("parallel", "parallel"), VMEM_FLOOR))(do, o)


def _attn_dq(name, q_all, kv_all, kr, do, lse, delta, cos, sa, sb, scale):
    _, S, HP = q_all.shape
    H = HP // HEAD_PAD
    t = _attn_tile(S)
    nq = S // t

    def body(q_ref, kv_ref, kr_ref, do_ref, lse_ref, dl_ref, cos_ref, sa_ref, sb_ref, dq_ref, kcat_ref):
        i = pl.program_id(1)

        @pl.when(i == 0)
        def _():
            kcat_ref[:, :HEAD_PAD] = kv_ref[0]
            kcat_ref[:, HEAD_PAD:] = kr_ref[...]

        ng = ATTN_GROUPS
        tg = t // ng
        rows = [slice(g * tg, (g + 1) * tg) for g in range(ng)]
        qs = [jnp.concatenate([q_ref[0, r], q_ref[1, r]], axis=1) for r in rows]
        dobs = [do_ref[r, :].astype(BF) for r in rows]
        lses = [lse_ref[r, 0:1] for r in rows]
        dls = [dl_ref[r, 0:1] for r in rows]

        def step(j, dqs, masked):
            off = pl.multiple_of(j * t, t)
            kj = kcat_ref[pl.ds(off, t), :]
            vj = kv_ref[1, pl.ds(off, t), :]
            out = []
            for g in range(ng):
                s = lax.dot_general(qs[g], kj, _NT, preferred_element_type=F32)
                if masked:
                    s = jnp.where(_diag_mask(t, False)[rows[g]], s, NEG)
                p = jnp.exp(s - lses[g])
                dp = lax.dot_general(dobs[g], vj, _NT, preferred_element_type=F32)
                ds = (p * (dp - dls[g])).astype(BF)
                out.append(dqs[g] + jnp.dot(ds, kj, preferred_element_type=F32))
            return tuple(out)

        init = tuple(jnp.zeros((tg, 2 * HEAD_PAD), F32) for _ in range(ng))
        dqs = lax.fori_loop(0, i, lambda j, c: step(j, c, False), init)
        dqs = step(i, dqs, True)
        for g in range(ng):
            dq_ref[0, rows[g]] = (dqs[g][:, :HEAD_PAD] * scale).astype(BF)
            dq_ref[1, rows[g]] = (_rope_t(dqs[g][:, HEAD_PAD:], cos_ref[rows[g], :], sa_ref[rows[g], :],
                                          sb_ref[rows[g], :]) * scale).astype(BF)

    tab = _spec((t, HEAD_PAD), lambda h, i: (i, 0))
    stat = _spec((None, t, LANES), lambda h, i: (h, i, 0))
    return pl.pallas_call(
        body, name=name, grid=(H, nq),
        in_specs=[_spec((2, t, HEAD_PAD), lambda h, i: (0, i, h)), _spec((2, S, HEAD_PAD), lambda h, i: (0, 0, h)),
                  _spec((S, HEAD_PAD), lambda h, i: (0, 0)), _spec((t, HEAD_PAD), lambda h, i: (i, h)), stat, stat,
                  tab, tab, tab],
        out_specs=_spec((2, t, HEAD_PAD), lambda h, i: (0, i, h)), out_shape=_sd((2, S, HP), BF),
        scratch_shapes=[pltpu.VMEM((S, 2 * HEAD_PAD), BF)],
        compiler_params=_params(("parallel", "arbitrary"), 8 * S * HEAD_PAD * 2 + 32 * t * t * 4))(
            q_all, kv_all, kr, do, lse, delta, cos, sa, sb)


def _attn_dkv(name, q_all, kv_all, kr, do, lse_row, delta_row):
    _, S, HP = q_all.shape
    H = HP // HEAD_PAD
    t = _attn_tile(S)
    nq = S // t

    def body(q_ref, kv_ref, kr_ref, do_ref, lse_ref, dl_ref, dkv_ref, dkr_ref, qcat_ref):
        j = pl.program_id(1)

        @pl.when(j == 0)
        def _():
            qcat_ref[:, :HEAD_PAD] = q_ref[0]
            qcat_ref[:, HEAD_PAD:] = q_ref[1]

        ng = ATTN_GROUPS
        tg = t // ng
        rows = [slice(g * tg, (g + 1) * tg) for g in range(ng)]
        kjs = [jnp.concatenate([kv_ref[0, r], kr_ref[r, :]], axis=1) for r in rows]
        vjs = [kv_ref[1, r] for r in rows]

        def step(i, carry, masked):
            off = pl.multiple_of(i * t, t)
            qi = qcat_ref[pl.ds(off, t), :]
            doi = do_ref[pl.ds(off, t), :].astype(BF)
            lse_i = lse_ref[i]
            dl_i = dl_ref[i]
            out = []
            for g in range(ng):
                dk, dv = carry[g]
                st = lax.dot_general(kjs[g], qi, _NT, preferred_element_type=F32)
                if masked:
                    st = jnp.where(_diag_mask(t, True)[rows[g]], st, NEG)
                pt = jnp.exp(st - lse_i)
                dv2 = dv + jnp.dot(pt.astype(BF), doi, preferred_element_type=F32)
                dpt = lax.dot_general(vjs[g], doi, _NT, preferred_element_type=F32)
                dst = (pt * (dpt - dl_i)).astype(BF)
                out.append((dk + jnp.dot(dst, qi, preferred_element_type=F32), dv2))
            return tuple(out)

        init = tuple((jnp.zeros((tg, 2 * HEAD_PAD), F32), jnp.zeros((tg, V_DIM), F32)) for _ in range(ng))
        carry = step(j, init, True)
        carry = lax.fori_loop(j + 1, nq, lambda i, c: step(i, c, False), carry)
        for g in range(ng):
            dk, dv = carry[g]
            dkv_ref[0, rows[g]] = dk[:, :HEAD_PAD].astype(BF)
            dkv_ref[1, rows[g]] = dv.astype(BF)
            dkr_ref[rows[g], :] = dk[:, HEAD_PAD:]

    stat = _spec((None, nq, 1, t), lambda h, j: (h, 0, 0, 0))
    return pl.pallas_call(
        body, name=name, grid=(H, nq),
        in_specs=[_spec((2, S, HEAD_PAD), lambda h, j: (0, 0, h)), _spec((2, t, HEAD_PAD), lambda h, j: (0, j, h)),
                  _spec((t, HEAD_PAD), lambda h, j: (j, 0)), _spec((S, HEAD_PAD), lambda h, j: (0, h)), stat, stat],
        out_specs=[_spec((2, t, HEAD_PAD), lambda h, j: (0, j, h)), _spec((None, t, HEAD_PAD), lambda h, j: (h, j, 0))],
        out_shape=[_sd((2, S, HP), BF), _sd((H, S, HEAD_PAD), F32)],
        scratch_shapes=[pltpu.VMEM((S, 2 * HEAD_PAD), BF)],
        compiler_params=_params(("parallel", "arbitrary"), 8 * S * HEAD_PAD * 4 + 32 * t * t * 4))(
            q_all, kv_all, kr, do, lse_row, delta_row)


def _place():
    x, y, c = lax.axis_index("x"), lax.axis_index("y"), lax.axis_index("c")
    return x, y, c


def _all_gather(blocks):
    n = len(blocks)

    def body(*refs):
        ins, outs = refs[:n], refs[n:2 * n]
        send_sems, recv_sems, local_sems = refs[2 * n:]
        x, y, c = _place()
        me = 4 * x + 2 * y + c
        sibling = (x, y, 1 - c)
        chips = [(1 - x, y), (x, 1 - y), (1 - x, 1 - y)]

        def slab(a, px, py, pc):
            return outs[a].at[4 * px + 2 * py + pc]

        def copy(a, k, src, dst, to):
            return pltpu.make_async_remote_copy(src_ref=src, dst_ref=dst, send_sem=send_sems.at[a, k],
                                                recv_sem=recv_sems.at[a, k], device_id=to, device_id_type=MESH)

        local = [pltpu.make_async_copy(ins[a], outs[a].at[me], local_sems.at[a]) for a in range(n)]
        for cp in local:
            cp.start()
        sends = []
        for a in range(n):
            mine = slab(a, x, y, c)
            sends.append(copy(a, 0, ins[a], mine, sibling))
            for j, chip in enumerate(chips):
                sends.append(copy(a, 1 + j, ins[a], mine, (*chip, c)))
        for cp in sends:
            cp.start()
        for j, chip in enumerate(chips):
            for a in range(n):
                got = slab(a, *chip, c)
                copy(a, 1 + j, got, got, (x, y, c)).wait_recv()
                fwd = copy(a, 4 + j, got, got, sibling)
                fwd.start()
                sends.append(fwd)
        for a in range(n):
            got = slab(a, x, y, 1 - c)
            copy(a, 0, got, got, (x, y, c)).wait_recv()
            for j, chip in enumerate(chips):
                got = slab(a, *chip, 1 - c)
                copy(a, 4 + j, got, got, (x, y, c)).wait_recv()
        for cp in sends:
            cp.wait_send()
        for cp in local:
            cp.wait()

    return pl.pallas_call(
        body, name="weights_all_gather", in_specs=[_ANY] * n, out_specs=[_ANY] * n,
        out_shape=[_sd((N_DEV,) + b.shape, b.dtype) for b in blocks],
        scratch_shapes=[pltpu.SemaphoreType.DMA((n, 7)), pltpu.SemaphoreType.DMA((n, 7)), pltpu.SemaphoreType.DMA((n,))],
    )(*blocks)


_HBM = pl.BlockSpec(memory_space=pltpu.HBM)
_SEM = pl.BlockSpec(memory_space=pltpu.SEMAPHORE)
_EFFECT = pltpu.SideEffectType.DATAFLOW_SIDE_EFFECTING


def _peers():
    x, y, c = _place()
    out = []
    for m in range(1, N_DEV):
        px, py, pc = x ^ (m >> 2), y ^ ((m >> 1) & 1), c ^ (m & 1)
        out.append((m, (px, py, pc), 4 * px + 2 * py + pc))
    return 4 * x + 2 * y + c, out


def _exchange_copies(src, land, send_sem, recv_sem, gather):
    me, peers = _peers()
    cps = []
    for a in range(len(src)):
        for m, pos, idx in peers:
            s_ref, d_ref = (src[a], land[a].at[me]) if gather else (src[a].at[idx], land[a].at[m - 1])
            k = a * (N_DEV - 1) + m - 1
            cps.append(pltpu.make_async_remote_copy(src_ref=s_ref, dst_ref=d_ref, send_sem=send_sem.at[k],
                                                    recv_sem=recv_sem.at[k], device_id=pos, device_id_type=MESH))
    return cps


def _xstart(name, srcs, gather):
    n = len(srcs)
    if gather:
        land_shapes = [(N_DEV,) + s.shape for s in srcs]
    else:
        land_shapes = [(N_DEV - 1,) + s.shape[1:] for s in srcs]

    def body(*refs):
        src, land = refs[:n], refs[n:2 * n]
        send_sem, recv_sem = refs[2 * n], refs[2 * n + 1]
        token = refs[-1]
        for cp in _exchange_copies(src, land, send_sem, recv_sem, gather):
            cp.start()
        token[...] = jnp.zeros_like(token)

    sem = pltpu.SemaphoreType.DMA((n * (N_DEV - 1),))
    out_shape = ([sem, sem] + [pltpu.HBM(s.shape, s.dtype) for s in srcs]
                 + [pltpu.HBM(sh, s.dtype) for sh, s in zip(land_shapes, srcs)] + [_sd((8, LANES), F32)])
    args = [pltpu.with_memory_space_constraint(s, pltpu.HBM) for s in srcs]
    args += [pltpu.with_memory_space_constraint(lax.empty(sh, s.dtype), pltpu.HBM) for sh, s in zip(land_shapes, srcs)]
    res = pl.pallas_call(
        body, name=name, out_shape=out_shape, in_specs=[_HBM] * (2 * n),
        out_specs=[_SEM, _SEM] + [_HBM] * (2 * n) + [pl.BlockSpec(memory_space=pltpu.VMEM)],
        input_output_aliases={i: 2 + i for i in range(2 * n)},
        compiler_params=pltpu.CompilerParams(has_side_effects=_EFFECT))(*args)
    return dict(send=res[0], recv=res[1], srcs=list(res[2:2 + n]), lands=list(res[2 + n:2 + 2 * n]), token=res[-1])


def _xwait(name, st, after, gather):
    n = len(st["srcs"])

    def body(*refs):
        src, land = refs[:n], refs[n:2 * n]
        send_sem, recv_sem = refs[2 * n], refs[2 * n + 1]
        for cp in _exchange_copies(src, land, send_sem, recv_sem, gather):
            cp.wait_send()
            cp.wait_recv()

    arrays = st["srcs"] + st["lands"]
    res = pl.pallas_call(
        body, name=name, out_shape=[pltpu.HBM(a.shape, a.dtype) for a in arrays],
        in_specs=[_HBM] * (2 * n) + [_SEM, _SEM, _ANY], out_specs=[_HBM] * (2 * n),
        input_output_aliases={i: i for i in range(2 * n)},
        compiler_params=pltpu.CompilerParams(has_side_effects=_EFFECT))(*arrays, st["send"], st["recv"], after)
    return list(res[:n]), list(res[n:])


def _put_own(name, land, block, me_arr):
    R, C = block.shape
    tr = _pick(R, (512, 256, 128, 64, 32, 16))

    def body(me_ref, b_ref, land_ref, o_ref):
        o_ref[...] = b_ref[...]

    return pl.pallas_call(
        body, name=name, out_shape=_sd(land.shape, land.dtype),
        grid_spec=pltpu.PrefetchScalarGridSpec(
            num_scalar_prefetch=1, grid=(R // tr,),
            in_specs=[_spec((tr, C), lambda i, me_ref: (i, 0)), _ANY],
            out_specs=_spec((None, tr, C), lambda i, me_ref: (me_ref[0], i, 0))),
        input_output_aliases={2: 0},
        compiler_params=_params(("parallel",), 8 * tr * C * 2))(me_arr, block, land)


def _all_reduce_small(name, part):
    R = part.shape[0]

    def body(p_ref, out_ref, gath_ref, send_sems, recv_sems):
        x, y, c = _place()
        me = 4 * x + 2 * y + c
        gath_ref[me] = p_ref[...]
        cps = []
        for m in range(1, N_DEV):
            to = (x ^ (m >> 2), y ^ ((m >> 1) & 1), c ^ (m & 1))
            cps.append(pltpu.make_async_remote_copy(
                src_ref=p_ref, dst_ref=gath_ref.at[me], send_sem=send_sems.at[m - 1], recv_sem=recv_sems.at[m - 1],
                device_id=to, device_id_type=MESH))
        for cp in cps:
            cp.start()
        for m in range(1, N_DEV):
            frm = 4 * (x ^ (m >> 2)) + 2 * (y ^ ((m >> 1) & 1)) + (c ^ (m & 1))
            pltpu.make_async_remote_copy(
                src_ref=p_ref, dst_ref=gath_ref.at[frm], send_sem=send_sems.at[m - 1], recv_sem=recv_sems.at[m - 1],
                device_id=(x, y, c), device_id_type=MESH).wait_recv()
        for cp in cps:
            cp.wait_send()
        tot = gath_ref[0]
        for k in range(1, N_DEV):
            tot = tot + gath_ref[k]
        out_ref[...] = tot

    vm = pl.BlockSpec(memory_space=pltpu.VMEM)
    return pl.pallas_call(
        body, name=name, in_specs=[vm], out_specs=vm, out_shape=_sd((R, LANES), F32),
        scratch_shapes=[pltpu.VMEM((N_DEV, R, LANES), F32), pltpu.SemaphoreType.DMA((N_DEV - 1,)),
                        pltpu.SemaphoreType.DMA((N_DEV - 1,))],
        compiler_params=pltpu.CompilerParams(vmem_limit_bytes=VMEM_FLOOR),
    )(part)


def _adam_math(w, g, m, v):
    m2 = ADAM_B1 * m + (1.0 - ADAM_B1) * g
    v2 = ADAM_B2 * v + (1.0 - ADAM_B2) * (g * g)
    m_hat = m2 / (1.0 - ADAM_B1 ** ADAM_STEP)
    v_hat = v2 / (1.0 - ADAM_B2 ** ADAM_STEP)
    delta = -ADAM_LR * (m_hat / (jnp.sqrt(v_hat) + ADAM_EPS) + ADAM_WD * w)
    return delta, m2, v2


def _adamw_sharded(name, lands, fulls, me_arr, w, m, v):
    n_l = len(lands)
    R, C = lands[0].shape[1], lands[0].shape[2]
    tr = _pick(R, (128, 64, 32, 16, 8))
    nr = R // tr

    def body(me_ref, *refs):
        land_refs, own_refs = refs[:n_l], refs[n_l:2 * n_l]
        w_ref, m_ref, v_ref, g_ref, d_ref, m2_ref, v2_ref = refs[2 * n_l:]
        layer = pl.program_id(0)
        for ll in range(n_l):
            @pl.when(layer == ll)
            def _(ll=ll):
                g = own_refs[ll][...].astype(F32)
                for j in range(N_DEV - 1):
                    g = g + land_refs[ll][j].astype(F32)
                delta, m2, v2 = _adam_math(w_ref[...], g, m_ref[...], v_ref[...])
                g_ref[...] = g
                d_ref[...] = delta
                m2_ref[...] = m2
                v2_ref[...] = v2

    def row_of(ll):
        return lambda l, i, me_ref: jnp.where(l == ll, i, 0)

    in_specs = [_spec((N_DEV - 1, tr, C), lambda l, i, me_ref, f=row_of(ll): (0, f(l, i, me_ref), 0)) for ll in range(n_l)]
    in_specs += [_spec((None, tr, C), lambda l, i, me_ref, f=row_of(ll): (me_ref[0], f(l, i, me_ref), 0))
                 for ll in range(n_l)]
    blk = _spec((tr, C), lambda l, i, me_ref: (l * nr + i, 0))
    return pl.pallas_call(
        body, name=name, out_shape=[_sd(w.shape, F32)] * 4,
        grid_spec=pltpu.PrefetchScalarGridSpec(num_scalar_prefetch=1, grid=(n_l, nr), in_specs=in_specs + [blk] * 3,
                                               out_specs=[blk] * 4),
        compiler_params=_params(("parallel", "parallel"), (4 * n_l * N_DEV + 40) * tr * C * 4))(
            me_arr, *lands, *fulls, w, m, v)


def _adamw_packed(name, w, g, m, v):
    R = w.shape[0]

    def body(w_ref, g_ref, m_ref, v_ref, d_ref, m2_ref, v2_ref):
        delta, m2, v2 = _adam_math(w_ref[...], g_ref[...], m_ref[...], v_ref[...])
        d_ref[...] = delta
        m2_ref[...] = m2
        v2_ref[...] = v2

    vm = pl.BlockSpec(memory_space=pltpu.VMEM)
    return pl.pallas_call(
        body, name=name, in_specs=[vm] * 4, out_specs=[vm] * 3, out_shape=[_sd((R, LANES), F32)] * 3,
        compiler_params=pltpu.CompilerParams(vmem_limit_bytes=VMEM_FLOOR))(w, g, m, v)


def _pack(arrays):
    flat = jnp.concatenate([a.reshape(-1).astype(F32) for a in arrays])
    pad = (-flat.shape[0]) % (8 * LANES)
    return jnp.pad(flat, (0, pad)).reshape(-1, LANES)


def _unpack(packed, like):
    flat = packed.reshape(-1)
    out, pos = [], 0
    for a in like:
        n = math.prod(a.shape)
        out.append(flat[pos:pos + n].reshape(a.shape))
        pos += n
    return out


def _ffn_fwd(tag, x, gain, w_in_sm, w_out_sm, deps=()):
    h = _rms_fwd(tag + "_norm", x, gain, deps)
    gu, act = _pair_in(tag + "_in", h, w_in_sm, BF, _swiglu)
    x_new = _rows_out(tag + "_out", act, w_out_sm, x, 0.5)
    return x_new, (x, h, gu, act)


def _ffn_bwd(tag, d, d_bf, saved, gain, w_in_sm, w_out_sm, send_grads):
    x, h, gu, act = saved
    dgu = _rows_dact(tag + "_dact", d_bf, w_out_sm, [gu], [_sd(gu.shape, BF)], _swiglu_bwd_epi)[0]
    g_out = _rows_wgrad(tag + "_wgrad_out", act, d_bf, 0.5)
    g_in = _cols_wgrad(tag + "_wgrad_in", h, dgu)
    token = send_grads(g_in, g_out)
    dh = _cols_dh(tag + "_dh", dgu, w_in_sm, deps=(token,))
    dx, dx_bf, dgain = _rms_bwd(tag + "_dnorm", dh, x, gain, d)
    return dx, dx_bf, dgain


def kernel(x, positions, ln_ffn1, ffn1_w_in, ffn1_w_out, ln_mix, ln_ffn2, ffn2_w_in, ffn2_w_out, sgu_w_in, sgu_v_gain, sgu_v_bias, sgu_w_spatial, sgu_b_spatial, sgu_w_out, mla_w_in, mla_q_norm, mla_w_q_up, mla_kv_norm, mla_w_kv_up, mla_w_out, ln_final, loss_target, m_ln_ffn1, m_ffn1_w_in, m_ffn1_w_out, m_ln_mix, m_ln_ffn2, m_ffn2_w_in, m_ffn2_w_out, m_sgu_w_in, m_sgu_v_gain, m_sgu_v_bias, m_sgu_w_spatial, m_sgu_b_spatial, m_sgu_w_out, m_mla_w_in, m_mla_q_norm, m_mla_w_q_up, m_mla_kv_norm, m_mla_w_kv_up, m_mla_w_out, m_ln_final, v_ln_ffn1, v_ffn1_w_in, v_ffn1_w_out, v_ln_mix, v_ln_ffn2, v_ffn2_w_in, v_ffn2_w_out, v_sgu_w_in, v_sgu_v_gain, v_sgu_v_bias, v_sgu_w_spatial, v_sgu_b_spatial, v_sgu_w_out, v_mla_w_in, v_mla_q_norm, v_mla_w_q_up, v_mla_kv_norm, v_mla_w_kv_up, v_mla_w_out, v_ln_final):
    S, D = x.shape[1], x.shape[2]
    L = ln_ffn1.shape[0]
    H = mla_w_q_up.shape[-1] * N_DEV // (QK_NOPE + QK_ROPE)
    xi, yi, ci = _place()
    me = 4 * xi + 2 * yi + ci
    me_arr = jnp.reshape(me, (1,)).astype(jnp.int32)
    big = dict(ffn1_w_in=ffn1_w_in, ffn1_w_out=ffn1_w_out, ffn2_w_in=ffn2_w_in, ffn2_w_out=ffn2_w_out,
               sgu_w_in=sgu_w_in, sgu_w_out=sgu_w_out, mla_w_in=mla_w_in, mla_w_q_up=mla_w_q_up,
               mla_w_kv_up=mla_w_kv_up, mla_w_out=mla_w_out)
    big_m = dict(ffn1_w_in=m_ffn1_w_in, ffn1_w_out=m_ffn1_w_out, ffn2_w_in=m_ffn2_w_in, ffn2_w_out=m_ffn2_w_out,
                 sgu_w_in=m_sgu_w_in, sgu_w_out=m_sgu_w_out, mla_w_in=m_mla_w_in, mla_w_q_up=m_mla_w_q_up,
                 mla_w_kv_up=m_mla_w_kv_up, mla_w_out=m_mla_w_out)
    big_v = dict(ffn1_w_in=v_ffn1_w_in, ffn1_w_out=v_ffn1_w_out, ffn2_w_in=v_ffn2_w_in, ffn2_w_out=v_ffn2_w_out,
                 sgu_w_in=v_sgu_w_in, sgu_w_out=v_sgu_w_out, mla_w_in=v_mla_w_in, mla_w_q_up=v_mla_w_q_up,
                 mla_w_kv_up=v_mla_w_kv_up, mla_w_out=v_mla_w_out)
    names = list(big)
    mla_names = ["mla_w_in", "mla_w_q_up", "mla_w_kv_up", "mla_w_out"]

    blocks = {(k, l): big[k][l].astype(BF) for k in names for l in range(big[k].shape[0])}
    groups = []
    for i in range(L):
        groups.append([("ffn1_w_in", i), ("ffn1_w_out", i)])
        groups.append([("sgu_w_in", i // 2), ("sgu_w_out", i // 2)] if i % 2 == 0 else [(k, i // 2) for k in mla_names])
        groups.append([("ffn2_w_in", i), ("ffn2_w_out", i)])
    gathered = dict(zip(groups[0], _all_gather([blocks[k] for k in groups[0]])))
    started = [_xstart(f"gather_start_{gi}", [blocks[k] for k in grp], True) for gi, grp in enumerate(groups[1:], 1)]

    def fetch(gi, after):
        own, lands = _xwait(f"gather_wait_{gi}", started[gi - 1], after, True)
        for k, blk, land in zip(groups[gi], own, lands):
            gathered[k] = _put_own(f"gather_own_{k[0]}_{k[1]}", land, blk, me_arr)

    norm_rows = jnp.zeros((N_DEV, LANES), F32)
    mine = jnp.concatenate([mla_q_norm[0], mla_kv_norm[0]])
    norm_rows = lax.dynamic_update_slice(norm_rows, mine[None, :], (me, 0))
    norm_all = _all_reduce_small("norm_gains_gather", norm_rows)
    nq_sh = mla_q_norm.shape[1]
    q_gain = norm_all[:, :nq_sh].reshape(1, Q_LORA)
    kv_gain = norm_all[:, nq_sh:2 * nq_sh].reshape(1, KV_LORA)
    cos, sa, sb = _rope_tables(positions[0])
    scale = float((QK_NOPE + QK_ROPE) ** -0.5)

    xs = x[0]
    w_sp = sgu_w_spatial[0]
    b_sp = sgu_b_spatial[0][:, :, None]
    saved = []
    mla_w = {}
    deps = tuple(st["token"] for st in started)
    for i in range(L):
        if i > 0:
            fetch(3 * i, xs)
        xs, s1 = _ffn_fwd(f"l{i}_ffn1", xs, ln_ffn1[i:i + 1], gathered[("ffn1_w_in", i)], gathered[("ffn1_w_out", i)], deps)
        deps = ()
        fetch(3 * i + 1, xs)
        x_mix = xs
        h = _rms_fwd(f"l{i}_mix_norm", xs, ln_mix[i:i + 1])
        j = i // 2
        if i % 2 == 0:
            puv = _pair_in(f"l{i}_sgu_in", h, gathered[("sgu_w_in", j)], F32, None)[0]
            gated = _sgu_mid_fwd(f"l{i}_sgu_mid", puv, sgu_v_gain, sgu_v_bias, w_sp, b_sp)
            xs = _rows_out(f"l{i}_sgu_out", gated, gathered[("sgu_w_out", j)], xs, 1.0)
            sm = (x_mix, h, puv, gated)
        else:
            w_in_nat = gathered[("mla_w_in", j)].reshape(D, Q_LORA + KV_LORA + QK_ROPE)
            w_in_pad = jnp.pad(w_in_nat, ((0, 0), (0, HEAD_PAD - QK_ROPE)))
            wq_nat = jnp.transpose(gathered[("mla_w_q_up", j)], (1, 0, 2)).reshape(Q_LORA, H, QK_NOPE + QK_ROPE)
            wq_t = jnp.stack([wq_nat[:, :, :QK_NOPE].reshape(Q_LORA, H * HEAD_PAD),
                              jnp.pad(wq_nat[:, :, QK_NOPE:], ((0, 0), (0, 0), (0, HEAD_PAD - QK_ROPE))).reshape(
                                  Q_LORA, H * HEAD_PAD)])
            wkv_nat = jnp.transpose(gathered[("mla_w_kv_up", j)], (1, 0, 2)).reshape(KV_LORA, H, QK_NOPE + V_DIM)
            wkv_t = jnp.stack([wkv_nat[:, :, :QK_NOPE].reshape(KV_LORA, H * HEAD_PAD),
                               wkv_nat[:, :, QK_NOPE:].reshape(KV_LORA, H * HEAD_PAD)])
            w_o_nat = gathered[("mla_w_out", j)].reshape(H * V_DIM, D)
            mla_w[i] = (w_in_pad, wq_t, wkv_t, w_o_nat)
            proj = _mm2(f"l{i}_mla_in", h, w_in_pad, False, False, F32, tn_cands=(384, 128))
            lat, kr = _mla_mid_fwd(f"l{i}_mla_mid", proj, q_gain, kv_gain, cos, sa, sb)

            def q_epi(accs, ex, orefs, ids):
                orefs[0][...] = (accs[0] * scale).astype(BF)

            def qr_epi(accs, ex, orefs, ids):
                for hh in range(accs[0].shape[1] // HEAD_PAD):
                    sl = slice(hh * HEAD_PAD, (hh + 1) * HEAD_PAD)
                    orefs[0][:, sl] = (_rope(accs[0][:, sl], *ex) * scale).astype(BF)

            q_nope = _mm2(f"l{i}_mla_q", lat[0], wq_t[0], False, False, BF, epi=q_epi)
            q_rope = _mm2(f"l{i}_mla_qr", lat[0], wq_t[1], False, False, BF, epi=qr_epi, extras=(cos, sa, sb))
            q_all = jnp.stack([q_nope, q_rope])
            kv_all = jnp.stack([_mm2(f"l{i}_mla_k", lat[1], wkv_t[0], False, False, BF),
                                _mm2(f"l{i}_mla_v", lat[1], wkv_t[1], False, False, BF)])
            o, lse = _attn_fwd(f"l{i}_attn", q_all, kv_all, kr)
            xs = _mm2(f"l{i}_mla_out", o, w_o_nat, False, False, F32, res=xs)
            sm = (x_mix, h, proj, lat, kr, q_all, kv_all, o, lse)
        fetch(3 * i + 2, xs)
        xs, s2 = _ffn_fwd(f"l{i}_ffn2", xs, ln_ffn2[i:i + 1], gathered[("ffn2_w_in", i)], gathered[("ffn2_w_out", i)])
        saved.append((s1, sm, s2))

    loss_row, d, d_bf, g_ln_final = _final_loss("final_loss", xs, ln_final[None, :], loss_target[0])
    loss = lax.psum(loss_row[0, 0], ("x", "y", "c"))

    sent = []

    def send(tag, keys, grads):
        st = _xstart(f"scatter_start_{tag}", grads, False)
        sent.append((tag, keys, st))
        return st["token"]

    g_ln1, g_ln2, g_lnm = [None] * L, [None] * L, [None] * L
    small_g = {}
    for i in reversed(range(L)):
        s1, sm, s2 = saved[i]
        d, d_bf, g_ln2[i] = _ffn_bwd(
            f"l{i}_ffn2", d, d_bf, s2, ln_ffn2[i:i + 1], gathered[("ffn2_w_in", i)], gathered[("ffn2_w_out", i)],
            lambda g_in, g_out, i=i: send(f"l{i}_ffn2", [("ffn2_w_in", i), ("ffn2_w_out", i)], [g_in, g_out]))
        j = i // 2
        if i % 2 == 0:
            x_mix, h, puv, gated = sm
            dgated = _rows_dact(f"l{i}_sgu_dgated", d_bf, gathered[("sgu_w_out", j)], [], [_sd(gated.shape, BF)], _store())[0]
            g_so = _rows_wgrad(f"l{i}_sgu_wgrad_out", gated, d_bf, None)
            dpuv, dgain, dbias, dwsp, dbsp = _sgu_mid_bwd(f"l{i}_sgu_mid_bwd", puv, dgated, sgu_v_gain, sgu_v_bias,
                                                          w_sp, b_sp)
            small_g.update(sgu_v_gain=dgain, sgu_v_bias=dbias, sgu_w_spatial=dwsp[None], sgu_b_spatial=dbsp[None, :, :, 0])
            g_si = _cols_wgrad(f"l{i}_sgu_wgrad_in", h, dpuv)
            token = send(f"l{i}_sgu", [("sgu_w_in", j), ("sgu_w_out", j)], [g_si, g_so])
            dh = _cols_dh(f"l{i}_sgu_dh", dpuv, gathered[("sgu_w_in", j)], deps=(token,))
        else:
            x_mix, h, proj, lat, kr, q_all, kv_all, o, lse = sm
            w_in_pad, wq_t, wkv_t, w_o_nat = mla_w[i]
            t = _attn_tile(S)
            do = _mm2(f"l{i}_mla_do", d_bf, w_o_nat, False, True, F32)
            g_wo = _mm2(f"l{i}_mla_wgrad_out", o, d_bf, True, False, BF)
            delta = _attn_delta(f"l{i}_attn_delta", do, o)
            dq_all = _attn_dq(f"l{i}_attn_dq", q_all, kv_all, kr, do, lse, delta, cos, sa, sb, scale)
            lse_row = lse[:, :, 0].reshape(H, S // t, 1, t)
            delta_row = delta[:, :, 0].reshape(H, S // t, 1, t)
            dkv_all, dkr_heads = _attn_dkv(f"l{i}_attn_dkv", q_all, kv_all, kr, do, lse_row, delta_row)
            dqn = _mm2(f"l{i}_mla_dqn", dq_all[0], wq_t[0], False, True, F32)
            dqn = _mm2(f"l{i}_mla_dqn2", dq_all[1], wq_t[1], False, True, F32, res=dqn)
            dkvn = _mm2(f"l{i}_mla_dkvn", dkv_all[0], wkv_t[0], False, True, F32)
            dkvn = _mm2(f"l{i}_mla_dkvn2", dkv_all[1], wkv_t[1], False, True, F32, res=dkvn)
            g_wq = [_mm2(f"l{i}_mla_wgrad_q{t2}", lat[0], dq_all[t2], True, False, BF) for t2 in range(2)]
            g_wkv = [_mm2(f"l{i}_mla_wgrad_kv{t2}", lat[1], dkv_all[t2], True, False, BF) for t2 in range(2)]
            dproj, g_qn, g_kvn = _mla_mid_bwd(f"l{i}_mla_mid_bwd", proj, dqn, dkvn, dkr_heads, q_gain, kv_gain, cos, sa, sb)
            g_win = _mm2(f"l{i}_mla_wgrad_in", h, dproj, True, False, BF, tn_cands=(384, 128))
            n_in = Q_LORA + KV_LORA + QK_ROPE
            gq_nat = jnp.concatenate([g_wq[0].reshape(Q_LORA, H, HEAD_PAD),
                                      g_wq[1].reshape(Q_LORA, H, HEAD_PAD)[:, :, :QK_ROPE]], axis=2)
            gkv_nat = jnp.concatenate([g_wkv[0].reshape(KV_LORA, H, HEAD_PAD), g_wkv[1].reshape(KV_LORA, H, HEAD_PAD)], axis=2)
            token = send(f"l{i}_mla", [(k, j) for k in mla_names],
                         [g_win[:, :n_in].reshape(N_DEV, D // N_DEV, n_in),
                          jnp.transpose(gq_nat.reshape(Q_LORA, N_DEV, -1), (1, 0, 2)),
                          jnp.transpose(gkv_nat.reshape(KV_LORA, N_DEV, -1), (1, 0, 2)),
                          g_wo.reshape(N_DEV, H * V_DIM // N_DEV, D)])
            small_g.update(mla_q_norm=g_qn, mla_kv_norm=g_kvn)
            dh = _mm2(f"l{i}_mla_dh", dproj, w_in_pad, False, True, F32, tn_cands=(512, 256, 128), deps=(token,))
        d, d_bf, g_lnm[i] = _rms_bwd(f"l{i}_mix_dnorm", dh, x_mix, ln_mix[i:i + 1], d)
        d, d_bf, g_ln1[i] = _ffn_bwd(
            f"l{i}_ffn1", d, d_bf, s1, ln_ffn1[i:i + 1], gathered[("ffn1_w_in", i)], gathered[("ffn1_w_out", i)],
            lambda g_in, g_out, i=i: send(f"l{i}_ffn1", [("ffn1_w_in", i), ("ffn1_w_out", i)], [g_in, g_out]))
    grad_x = d[None]

    landed, partial = {}, {}
    for tag, keys, st in sent:
        fulls, lands = _xwait(f"scatter_wait_{tag}", st, d, False)
        for k, full, land in zip(keys, fulls, lands):
            partial[k], landed[k] = full, land
    big_out = {}
    for k in names:
        w = big[k]
        rc = (math.prod(w.shape[1:-1]), w.shape[-1])
        flat = (w.shape[0] * rc[0], rc[1])
        lands = [landed[(k, l)].reshape((N_DEV - 1,) + rc) for l in range(w.shape[0])]
        fulls = [partial[(k, l)].reshape((N_DEV,) + rc) for l in range(w.shape[0])]
        res = _adamw_sharded(f"adamw_{k}", lands, fulls, me_arr, w.reshape(flat), big_m[k].reshape(flat),
                             big_v[k].reshape(flat))
        big_out[k] = [r.reshape(w.shape) for r in res]

    small_g.update(ln_ffn1=jnp.concatenate(g_ln1), ln_mix=jnp.concatenate(g_lnm), ln_ffn2=jnp.concatenate(g_ln2),
                   ln_final=g_ln_final[0])
    small_names = ["ln_ffn1", "ln_mix", "ln_ffn2", "sgu_v_gain", "sgu_v_bias", "sgu_w_spatial", "sgu_b_spatial",
                   "ln_final", "mla_q_norm", "mla_kv_norm"]
    summed = _unpack(_all_reduce_small("small_grads_all_reduce", _pack([small_g[k] for k in small_names])),
                     [small_g[k] for k in small_names])
    small_grad = dict(zip(small_names, summed))
    for k in ("mla_q_norm", "mla_kv_norm"):
        small_grad[k] = lax.dynamic_slice(small_grad[k], (0, me * nq_sh), (1, nq_sh))
    small_w = dict(ln_ffn1=ln_ffn1, ln_mix=ln_mix, ln_ffn2=ln_ffn2, sgu_v_gain=sgu_v_gain, sgu_v_bias=sgu_v_bias,
                   sgu_w_spatial=sgu_w_spatial, sgu_b_spatial=sgu_b_spatial, ln_final=ln_final, mla_q_norm=mla_q_norm,
                   mla_kv_norm=mla_kv_norm)
    small_m = dict(ln_ffn1=m_ln_ffn1, ln_mix=m_ln_mix, ln_ffn2=m_ln_ffn2, sgu_v_gain=m_sgu_v_gain, sgu_v_bias=m_sgu_v_bias,
                   sgu_w_spatial=m_sgu_w_spatial, sgu_b_spatial=m_sgu_b_spatial, ln_final=m_ln_final,
                   mla_q_norm=m_mla_q_norm, mla_kv_norm=m_mla_kv_norm)
    small_v = dict(ln_ffn1=v_ln_ffn1, ln_mix=v_ln_mix, ln_ffn2=v_ln_ffn2, sgu_v_gain=v_sgu_v_gain, sgu_v_bias=v_sgu_v_bias,
                   sgu_w_spatial=v_sgu_w_spatial, sgu_b_spatial=v_sgu_b_spatial, ln_final=v_ln_final,
                   mla_q_norm=v_mla_q_norm, mla_kv_norm=v_mla_kv_norm)
    like = [small_w[k] for k in small_names]
    packed = _adamw_packed("adamw_small", _pack(like), _pack([small_grad[k] for k in small_names]),
                           _pack([small_m[k] for k in small_names]), _pack([small_v[k] for k in small_names]))
    small_out = {}
    unpacked = [_unpack(p, like) for p in packed]
    for idx, k in enumerate(small_names):
        small_out[k] = [small_grad[k].reshape(small_w[k].shape)] + [u[idx] for u in unpacked]

    order = ["ln_ffn1", "ffn1_w_in", "ffn1_w_out", "ln_mix", "ln_ffn2", "ffn2_w_in", "ffn2_w_out", "sgu_w_in",
             "sgu_v_gain", "sgu_v_bias", "sgu_w_spatial", "sgu_b_spatial", "sgu_w_out", "mla_w_in", "mla_q_norm",
             "mla_w_q_up", "mla_kv_norm", "mla_w_kv_up", "mla_w_out", "ln_final"]
    res = {k: (big_out[k] if k in big_out else small_out[k]) for k in order}
    outs = [loss, grad_x]
    for part in range(4):
        outs.extend(res[k][part] for k in order)
    return tuple(outs)
```

```python
import math

import jax
import jax.numpy as jnp
from jax import lax
from jax.experimental import pallas as pl
from jax.experimental.pallas import tpu as pltpu

F32 = jnp.float32
BF = jnp.bfloat16
MESH = pl.DeviceIdType.MESH

N_DEV = 8
EPS = 1e-6
CHUNK = 64
SGU_BLOCK = 128
SGU_GROUPS = 8
Q_LORA = 512
KV_LORA = 512
QK_NOPE = 128
QK_ROPE = 64
V_DIM = 128
ROPE_THETA = 10000.0
HEAD_PAD = 128
LANES = 128
ADAM_LR = 0.001
ADAM_B1 = 0.9
ADAM_B2 = 0.999
ADAM_EPS = 1e-08
ADAM_WD = 0.01
ADAM_STEP = 10
V7X_VMEM_BYTES = 64 * 1024 * 1024
VMEM_CAP = V7X_VMEM_BYTES - 6 * 1024 * 1024
VMEM_FLOOR = 32 * 1024 * 1024
NEG = -1e30
ATTN_GROUPS = 1


def _pick(n, cands):
    for c in cands:
        if n % c == 0:
            return c
    return n


def _nbytes(shape, dtype):
    return math.prod(int(s) for s in shape if s is not None) * jnp.dtype(dtype).itemsize


def _params(sem, block_bytes):
    limit = int(min(VMEM_CAP, max(VMEM_FLOOR, block_bytes)))
    return pltpu.CompilerParams(dimension_semantics=sem, vmem_limit_bytes=limit)


def _spec(shape, fn):
    return pl.BlockSpec(shape, fn)


_ANY = pl.BlockSpec(memory_space=pl.ANY)


def _mm(name, grid, ops, pairs, extras, outs, epilogue, acc_shapes, deps=()):
    nk = grid[2]
    n_ops, n_ex, n_out = len(ops), len(extras), len(outs)

    def load(refs, idx):
        loader = ops[idx][2] if len(ops[idx]) > 2 else None
        return (refs[idx][...] if loader is None else loader(refs[idx])).astype(BF)

    def prod(refs, p):
        ia, ib, ta, tb, _ = p
        a = load(refs, ia)
        b = load(refs, ib)
        dims = (((0 if ta else 1,), (1 if tb else 0,)), ((), ()))
        return lax.dot_general(a, b, dims, preferred_element_type=F32)

    def body(*refs):
        op_refs = refs[:n_ops]
        ex_refs = refs[n_ops:n_ops + n_ex]
        n_in = n_ops + n_ex + len(deps)
        out_refs = refs[n_in:n_in + n_out]
        acc_refs = refs[n_in + n_out:]
        ids = (pl.program_id(0), pl.program_id(1))

        def finish(vals):
            epilogue(vals, [e[...] for e in ex_refs], out_refs, ids)

        if nk == 1:
            vals = [None] * len(acc_shapes)
            for p in pairs:
                r = prod(op_refs, p)
                vals[p[4]] = r if vals[p[4]] is None else vals[p[4]] + r
            finish(vals)
        else:
            k = pl.program_id(2)

            @pl.when(k == 0)
            def _():
                for a in acc_refs:
                    a[...] = jnp.zeros_like(a)

            for p in pairs:
                acc_refs[p[4]][...] += prod(op_refs, p)

            @pl.when(k == nk - 1)
            def _():
                finish([a[...] for a in acc_refs])

    in_arrays = [o[0] for o in ops] + [e[0] for e in extras]
    in_specs = [o[1] for o in ops] + [e[1] for e in extras]
    in_arrays += list(deps)
    in_specs += [_ANY] * len(deps)
    blk = 0
    for entry in ops + extras:
        blk += 2 * _nbytes(entry[1].block_shape, entry[0].dtype)
    for sd, sp in outs:
        blk += 2 * _nbytes(sp.block_shape, sd.dtype)
    acc_b = sum(_nbytes(s, F32) for s in acc_shapes)
    blk += 6 * acc_b
    scratch = [pltpu.VMEM(s, F32) for s in acc_shapes] if nk > 1 else []
    res = pl.pallas_call(
        body, name=name, grid=grid, in_specs=in_specs,
        out_specs=[o[1] for o in outs], out_shape=[o[0] for o in outs],
        scratch_shapes=scratch,
        compiler_params=_params(("parallel", "parallel", "arbitrary"), blk))(*in_arrays)
    return res


def _store(scale=None):
    def epi(accs, ex, outs, ids):
        v = accs[0]
        if scale is not None:
            v = v * scale
        outs[0][...] = v.astype(outs[0].dtype)
    return epi


def _store_residual(scale):
    def epi(accs, ex, outs, ids):
        outs[0][...] = ex[0] + scale * accs[0]
    return epi


def _sd(shape, dtype):
    return jax.ShapeDtypeStruct(tuple(shape), dtype)


def _pair_in(name, h, w_sm, out_dtype, act):
    S, D = h.shape
    c = w_sm.shape[-1]
    tm = _pick(S, (256, 128))
    half = N_DEV // 2
    ops = [(h, _spec((tm, D), lambda j, i, k: (i, 0))),
           (w_sm, _spec((None, D, c), lambda j, i, k: (j, 0, 0))),
           (w_sm, _spec((None, D, c), lambda j, i, k: (j + half, 0, 0)))]
    outs = [(_sd((2, S, half * c), out_dtype), _spec((2, tm, c), lambda j, i, k: (0, i, j)))]
    if act is not None:
        outs.append((_sd((S, half * c), BF), _spec((tm, c), lambda j, i, k: (i, j))))

    def epi(accs, ex, orefs, ids):
        orefs[0][0] = accs[0].astype(out_dtype)
        orefs[0][1] = accs[1].astype(out_dtype)
        if act is not None:
            orefs[1][...] = act(accs[0], accs[1]).astype(BF)

    return _mm(name, (half, S // tm, 1), ops, [(0, 1, False, False, 0), (0, 2, False, False, 1)], [], outs, epi,
               [(tm, c), (tm, c)])


def _two_slabs(ref):
    return jnp.concatenate([ref[0], ref[1]], axis=0)


def _rows_out(name, a, w_sm, res, scale):
    S = a.shape[0]
    r, D = w_sm.shape[-2], w_sm.shape[-1]
    tm = _pick(S, (1024, 512, 256, 128))
    tn = _pick(D, (1024, 512, 256, 128))
    ops = [(a, _spec((tm, 2 * r), lambda i, j, k: (i, k))),
           (w_sm, _spec((2, r, tn), lambda i, j, k: (k, 0, j)), _two_slabs)]
    extras = [(res, _spec((tm, tn), lambda i, j, k: (i, j)))]
    outs = [(_sd((S, D), F32), _spec((tm, tn), lambda i, j, k: (i, j)))]
    return _mm(name, (S // tm, D // tn, N_DEV // 2), ops, [(0, 1, False, False, 0)], extras, outs,
               _store_residual(scale), [(tm, tn)])[0]


def _rows_dact(name, d_bf, w_sm, extras_arrays, out_shapes, epi):
    S, D = d_bf.shape
    r = w_sm.shape[-2]
    tm = _pick(S, (1024, 512, 256, 128))
    ops = [(d_bf, _spec((tm, D), lambda j, i, k: (i, 0))),
           (w_sm, _spec((2, r, D), lambda j, i, k: (j, 0, 0)), _two_slabs)]
    extras = []
    for arr in extras_arrays:
        if arr.ndim == 3:
            extras.append((arr, _spec((arr.shape[0], tm, 2 * r), lambda j, i, k: (0, i, j))))
        else:
            extras.append((arr, _spec((tm, 2 * r), lambda j, i, k: (i, j))))
    outs = []
    for sd in out_shapes:
        if len(sd.shape) == 3:
            outs.append((sd, _spec((sd.shape[0], tm, 2 * r), lambda j, i, k: (0, i, j))))
        else:
            outs.append((sd, _spec((tm, 2 * r), lambda j, i, k: (i, j))))
    return _mm(name, (N_DEV // 2, S // tm, 1), ops, [(0, 1, False, True, 0)], extras, outs, epi, [(tm, 2 * r)])


def _rows_wgrad(name, a, d_bf, scale):
    S, D = d_bf.shape
    r = a.shape[1] // N_DEV
    tn = _pick(D, (2048, 1024, 512, 256, 128))
    tk = _pick(S, (1024, 512, 256, 128))
    ops = [(a, _spec((tk, 2 * r), lambda s, j, k: (k, s))),
           (d_bf, _spec((tk, tn), lambda s, j, k: (k, j)))]
    outs = [(_sd((N_DEV, r, D), BF), _spec((2, r, tn), lambda s, j, k: (s, 0, j)))]

    def epi(accs, ex, orefs, ids):
        v = accs[0] if scale is None else accs[0] * scale
        orefs[0][0] = v[:r].astype(BF)
        orefs[0][1] = v[r:].astype(BF)

    return _mm(name, (N_DEV // 2, D // tn, S // tk), ops, [(0, 1, True, False, 0)], [], outs, epi, [(2 * r, tn)])[0]


def _cols_dh(name, dpair, w_sm, deps=()):
    _, S, _ = dpair.shape
    D, c = w_sm.shape[-2], w_sm.shape[-1]
    half = N_DEV // 2
    tm = _pick(S, (1024, 512, 256, 128))
    tn = _pick(D, (1024, 512, 256, 128))
    ops = [(dpair, _spec((None, tm, c), lambda i, j, k: (k // half, i, k % half))),
           (w_sm, _spec((None, tn, c), lambda i, j, k: (k, j, 0)))]
    outs = [(_sd((S, D), F32), _spec((tm, tn), lambda i, j, k: (i, j)))]
    return _mm(name, (S // tm, D // tn, N_DEV), ops, [(0, 1, False, True, 0)], [], outs, _store(), [(tm, tn)],
               deps=deps)[0]


def _cols_wgrad(name, h, dpair):
    S, D = h.shape
    half = N_DEV // 2
    c = dpair.shape[2] // half
    tm = _pick(D, (1024, 512, 256, 128))
    tk = _pick(S, (1024, 512, 256, 128))
    ops = [(h, _spec((tk, tm), lambda s, i, k: (k, i))),
           (dpair, _spec((None, tk, c), lambda s, i, k: (s // half, k, s % half)))]
    outs = [(_sd((N_DEV, D, c), BF), _spec((None, tm, c), lambda s, i, k: (s, i, 0)))]
    return _mm(name, (N_DEV, D // tm, S // tk), ops, [(0, 1, True, False, 0)], [], outs, _store(), [(tm, c)])[0]


def _mm2(name, a, b, ta, tb, out_dtype, epi=None, extras=(), res=None, tn_cands=(512, 384, 256, 128), deps=()):
    M = a.shape[1] if ta else a.shape[0]
    K = a.shape[0] if ta else a.shape[1]
    N = b.shape[0] if tb else b.shape[1]
    tm = _pick(M, (1024, 512, 256, 128))
    tn = _pick(N, tn_cands)
    tk = _pick(K, (2048, 1152, 1024, 512, 256, 128))
    a_spec = _spec((tk, tm), lambda i, j, k: (k, i)) if ta else _spec((tm, tk), lambda i, j, k: (i, k))
    b_spec = _spec((tn, tk), lambda i, j, k: (j, k)) if tb else _spec((tk, tn), lambda i, j, k: (k, j))
    ex = [(e, _spec((tm, e.shape[1]), lambda i, j, k: (i, 0))) for e in extras]
    if res is not None:
        ex = [(res, _spec((tm, tn), lambda i, j, k: (i, j)))]
        epi = _store_residual(1.0)
    outs = [(_sd((M, N), out_dtype), _spec((tm, tn), lambda i, j, k: (i, j)))]
    return _mm(name, (M // tm, N // tn, K // tk), [(a, a_spec), (b, b_spec)], [(0, 1, ta, tb, 0)], ex, outs,
               epi or _store(), [(tm, tn)], deps=deps)[0]


def _rms_fwd(name, x, g, deps=()):
    S, D = x.shape
    ts = _pick(S, (512, 256, 128))

    def body(x_ref, g_ref, *rest):
        h_ref = rest[-1]
        xv = x_ref[...]
        r = lax.rsqrt(jnp.mean(xv * xv, axis=-1, keepdims=True) + EPS)
        h_ref[...] = (xv * r * g_ref[...]).astype(BF)

    return pl.pallas_call(
        body, name=name, grid=(S // ts,),
        in_specs=[_spec((ts, D), lambda i: (i, 0)), _spec((1, D), lambda i: (0, 0))] + [_ANY] * len(deps),
        out_specs=_spec((ts, D), lambda i: (i, 0)), out_shape=_sd((S, D), BF),
        compiler_params=_params(("parallel",), 12 * ts * D * 4))(x, g, *deps)


def _rms_bwd(name, dh, x, g, dres, deps=()):
    S, D = x.shape
    ts = _pick(S, (256, 128))

    def body(dh_ref, x_ref, g_ref, dres_ref, *rest):
        dx_ref, dxb_ref, dg_ref = rest[len(deps):]
        xv = x_ref[...]
        dhv = dh_ref[...]
        r = lax.rsqrt(jnp.mean(xv * xv, axis=-1, keepdims=True) + EPS)
        xhat = xv * r
        dxh = dhv * g_ref[...]
        cm = jnp.mean(dxh * xhat, axis=-1, keepdims=True)
        dx = r * (dxh - xhat * cm) + dres_ref[...]
        dx_ref[...] = dx
        dxb_ref[...] = dx.astype(BF)

        @pl.when(pl.program_id(0) == 0)
        def _():
            dg_ref[...] = jnp.zeros_like(dg_ref)

        dg_ref[...] += jnp.sum(dhv * xhat, axis=0, keepdims=True)

    row = _spec((ts, D), lambda i: (i, 0))
    vec = _spec((1, D), lambda i: (0, 0))
    return pl.pallas_call(
        body, name=name, grid=(S // ts,),
        in_specs=[row, row, vec, row] + [_ANY] * len(deps), out_specs=[row, row, vec],
        out_shape=[_sd((S, D), F32), _sd((S, D), BF), _sd((1, D), F32)],
        compiler_params=_params(("arbitrary",), 20 * ts * D * 4))(dh, x, g, dres, *deps)


def _final_loss(name, x, g, target):
    S, D = x.shape
    ts = _pick(S, (256, 128))

    def body(x_ref, g_ref, t_ref, loss_ref, dx_ref, dxb_ref, dg_ref):
        xv = x_ref[...]
        gv = g_ref[...]
        r = lax.rsqrt(jnp.mean(xv * xv, axis=-1, keepdims=True) + EPS)
        xhat = xv * r
        err = xhat * gv - t_ref[...]
        part = 0.5 * jnp.sum(jnp.mean(err * err, axis=-1, keepdims=True), axis=0, keepdims=True)
        dy = err * (1.0 / D)
        dxh = dy * gv
        cm = jnp.mean(dxh * xhat, axis=-1, keepdims=True)
        dx = r * (dxh - xhat * cm)
        dx_ref[...] = dx
        dxb_ref[...] = dx.astype(BF)

        @pl.when(pl.program_id(0) == 0)
        def _():
            dg_ref[...] = jnp.zeros_like(dg_ref)
            loss_ref[...] = jnp.zeros_like(loss_ref)

        dg_ref[...] += jnp.sum(dy * xhat, axis=0, keepdims=True)
        loss_ref[...] += jnp.broadcast_to(part, loss_ref.shape)

    row = _spec((ts, D), lambda i: (i, 0))
    vec = _spec((1, D), lambda i: (0, 0))
    return pl.pallas_call(
        body, name=name, grid=(S // ts,),
        in_specs=[row, vec, row], out_specs=[_spec((1, LANES), lambda i: (0, 0)), row, row, vec],
        out_shape=[_sd((1, LANES), F32), _sd((S, D), F32), _sd((S, D), BF), _sd((1, D), F32)],
        compiler_params=_params(("arbitrary",), 20 * ts * D * 4))(x, g, target)


def _swiglu(gate, up):
    return gate * jax.nn.sigmoid(gate) * up


def _swiglu_bwd_epi(accs, ex, orefs, ids):
    da = 0.5 * accs[0]
    gate = ex[0][0].astype(F32)
    up = ex[0][1].astype(F32)
    sg = jax.nn.sigmoid(gate)
    orefs[0][0] = (da * up * (sg * (1.0 + gate * (1.0 - sg)))).astype(BF)
    orefs[0][1] = (da * gate * sg).astype(BF)


_GELU_C = math.sqrt(2.0 / math.pi)


def _gelu(x):
    return x * (0.5 * (1.0 + jnp.tanh(_GELU_C * (x + 0.044715 * (x * x * x)))))


def _gelu_grad(x):
    t = jnp.tanh(_GELU_C * (x + 0.044715 * (x * x * x)))
    return 0.5 * (1.0 + t) + x * (0.5 * (1.0 - t * t) * _GELU_C * (1.0 + 3.0 * 0.044715 * (x * x)))


def _causal_block_mask():
    row = lax.broadcasted_iota(jnp.int32, (SGU_BLOCK, SGU_BLOCK), 0) // CHUNK
    col = lax.broadcasted_iota(jnp.int32, (SGU_BLOCK, SGU_BLOCK), 1) // CHUNK
    return row >= col


def _sgu_mid_fwd(name, puv, gain, bias, w_sp, b_sp):
    _, S, W = puv.shape
    G = SGU_GROUPS
    C = W // G
    T = SGU_BLOCK

    def body(puv_ref, gain_ref, bias_ref, w_ref, b_ref, out_ref):
        mask = _causal_block_mask()
        v = _gelu(puv_ref[1])
        mu = jnp.mean(v, axis=-1, keepdims=True)
        vc = v - mu
        rs = lax.rsqrt(jnp.mean(vc * vc, axis=-1, keepdims=True) + EPS)
        vln = (vc * rs * gain_ref[...] + bias_ref[...]).astype(BF)
        for g in range(G):
            wg = jnp.where(mask, w_ref[g], 0.0).astype(BF)
            mixed = jnp.dot(wg, vln[:, g * C:(g + 1) * C], preferred_element_type=F32) + b_ref[g]
            out_ref[:, g * C:(g + 1) * C] = (_gelu(puv_ref[0, :, g * C:(g + 1) * C]) * mixed).astype(BF)

    return pl.pallas_call(
        body, name=name, grid=(S // T,),
        in_specs=[_spec((2, T, W), lambda i: (0, i, 0)), _spec((1, W), lambda i: (0, 0)), _spec((1, W), lambda i: (0, 0)),
                  _spec((G, T, T), lambda i: (0, 0, 0)), _spec((G, T, 1), lambda i: (0, 0, 0))],
        out_specs=_spec((T, W), lambda i: (i, 0)), out_shape=_sd((S, W), BF),
        compiler_params=_params(("parallel",), 16 * T * W * 4))(puv, gain, bias, w_sp, b_sp)


def _sgu_mid_bwd(name, puv, dgated, gain, bias, w_sp, b_sp):
    _, S, W = puv.shape
    G = SGU_GROUPS
    C = W // G
    T = SGU_BLOCK

    def body(puv_ref, dg_ref, gain_ref, bias_ref, w_ref, b_ref, dpuv_ref, dgain_ref, dbias_ref, dw_ref, db_ref, dvln_ref):
        @pl.when(pl.program_id(0) == 0)
        def _():
            dgain_ref[...] = jnp.zeros_like(dgain_ref)
            dbias_ref[...] = jnp.zeros_like(dbias_ref)
            dw_ref[...] = jnp.zeros_like(dw_ref)
            db_ref[...] = jnp.zeros_like(db_ref)

        mask = _causal_block_mask()
        pv = puv_ref[1]
        v = _gelu(pv)
        mu = jnp.mean(v, axis=-1, keepdims=True)
        vc = v - mu
        rs = lax.rsqrt(jnp.mean(vc * vc, axis=-1, keepdims=True) + EPS)
        vhat = vc * rs
        gain_v = gain_ref[...]
        vln = (vhat * gain_v + bias_ref[...]).astype(BF)
        for g in range(G):
            sl = slice(g * C, (g + 1) * C)
            wg = jnp.where(mask, w_ref[g], 0.0).astype(BF)
            vg = vln[:, sl]
            mixed = jnp.dot(wg, vg, preferred_element_type=F32) + b_ref[g]
            pu = puv_ref[0, :, sl]
            dgt = dg_ref[:, sl].astype(F32)
            dpuv_ref[0, :, sl] = (dgt * mixed * _gelu_grad(pu)).astype(BF)
            dmix = dgt * _gelu(pu)
            db_ref[g] += jnp.sum(dmix, axis=-1, keepdims=True)
            dmb = dmix.astype(BF)
            dwg = lax.dot_general(dmb, vg, (((1,), (1,)), ((), ())), preferred_element_type=F32)
            dw_ref[g] += jnp.where(mask, dwg, 0.0)
            dvln_ref[:, sl] = lax.dot_general(wg, dmb, (((0,), (0,)), ((), ())), preferred_element_type=F32)
        dvln = dvln_ref[...]
        dgain_ref[...] += jnp.sum(dvln * vhat, axis=0, keepdims=True)
        dbias_ref[...] += jnp.sum(dvln, axis=0, keepdims=True)
        dvh = dvln * gain_v
        m1 = jnp.mean(dvh, axis=-1, keepdims=True)
        m2 = jnp.mean(dvh * vhat, axis=-1, keepdims=True)
        dv = rs * (dvh - m1 - vhat * m2)
        dpuv_ref[1] = (dv * _gelu_grad(pv)).astype(BF)

    vec = _spec((1, W), lambda i: (0, 0))
    wsp = _spec((G, T, T), lambda i: (0, 0, 0))
    bsp = _spec((G, T, 1), lambda i: (0, 0, 0))
    return pl.pallas_call(
        body, name=name, grid=(S // T,),
        in_specs=[_spec((2, T, W), lambda i: (0, i, 0)), _spec((T, W), lambda i: (i, 0)), vec, vec, wsp, bsp],
        out_specs=[_spec((2, T, W), lambda i: (0, i, 0)), vec, vec, wsp, bsp],
        out_shape=[_sd((2, S, W), BF), _sd((1, W), F32), _sd((1, W), F32), _sd((G, T, T), F32), _sd((G, T, 1), F32)],
        scratch_shapes=[pltpu.VMEM((T, W), F32)],
        compiler_params=_params(("arbitrary",), 24 * T * W * 4))(puv, dgated, gain, bias, w_sp, b_sp)


def _rope_tables(positions):
    half = QK_ROPE // 2
    inv_freq = 1.0 / (ROPE_THETA ** (jnp.arange(half, dtype=F32) / half))
    ang = positions.astype(F32)[:, None] * inv_freq[None, :]
    cos, sin = jnp.cos(ang), jnp.sin(ang)
    z = jnp.zeros_like(cos)
    return (jnp.concatenate([cos, cos, z, z], axis=1), jnp.concatenate([-sin, z, z, z], axis=1),
            jnp.concatenate([z, sin, z, z], axis=1))


def _rope(x, cos, sa, sb):
    return x * cos + pltpu.roll(x, HEAD_PAD - QK_ROPE // 2, 1) * sa + pltpu.roll(x, QK_ROPE // 2, 1) * sb


def _rope_t(dy, cos, sa, sb):
    return dy * cos + pltpu.roll(dy * sa, QK_ROPE // 2, 1) + pltpu.roll(dy * sb, HEAD_PAD - QK_ROPE // 2, 1)


def _rms_rows(x, g):
    r = lax.rsqrt(jnp.mean(x * x, axis=-1, keepdims=True) + EPS)
    return x * r * g


def _rms_rows_bwd(dy, x, g):
    r = lax.rsqrt(jnp.mean(x * x, axis=-1, keepdims=True) + EPS)
    xhat = x * r
    dxh = dy * g
    cm = jnp.mean(dxh * xhat, axis=-1, keepdims=True)
    return r * (dxh - xhat * cm), jnp.sum(dy * xhat, axis=0, keepdims=True)


def _mla_mid_fwd(name, proj, qg, kvg, cos, sa, sb):
    S, P = proj.shape
    ts = _pick(S, (512, 256, 128))

    def body(p_ref, qg_ref, kvg_ref, cos_ref, sa_ref, sb_ref, lat_ref, kr_ref):
        lat_ref[0] = _rms_rows(p_ref[:, :Q_LORA], qg_ref[...]).astype(BF)
        lat_ref[1] = _rms_rows(p_ref[:, Q_LORA:Q_LORA + KV_LORA], kvg_ref[...]).astype(BF)
        kr_ref[...] = _rope(p_ref[:, Q_LORA + KV_LORA:], cos_ref[...], sa_ref[...], sb_ref[...]).astype(BF)

    tab = _spec((ts, HEAD_PAD), lambda i: (i, 0))
    return pl.pallas_call(
        body, name=name, grid=(S // ts,),
        in_specs=[_spec((ts, P), lambda i: (i, 0)), _spec((1, Q_LORA), lambda i: (0, 0)),
                  _spec((1, KV_LORA), lambda i: (0, 0)), tab, tab, tab],
        out_specs=[_spec((2, ts, Q_LORA), lambda i: (0, i, 0)), tab],
        out_shape=[_sd((2, S, Q_LORA), BF), _sd((S, HEAD_PAD), BF)],
        compiler_params=_params(("parallel",), 16 * ts * P * 4))(proj, qg, kvg, cos, sa, sb)


def _mla_mid_bwd(name, proj, dqn, dkvn, dkr_heads, qg, kvg, cos, sa, sb):
    S, P = proj.shape
    H = dkr_heads.shape[0]
    ts = _pick(S, (256, 128))

    def body(p_ref, dqn_ref, dkvn_ref, dkr_ref, qg_ref, kvg_ref, cos_ref, sa_ref, sb_ref, dp_ref, dqg_ref, dkvg_ref):
        @pl.when(pl.program_id(0) == 0)
        def _():
            dqg_ref[...] = jnp.zeros_like(dqg_ref)
            dkvg_ref[...] = jnp.zeros_like(dkvg_ref)

        dq, dqg = _rms_rows_bwd(dqn_ref[...], p_ref[:, :Q_LORA], qg_ref[...])
        dkv, dkvg = _rms_rows_bwd(dkvn_ref[...], p_ref[:, Q_LORA:Q_LORA + KV_LORA], kvg_ref[...])
        dqg_ref[...] += dqg
        dkvg_ref[...] += dkvg
        dkr = dkr_ref[0]
        for h in range(1, H):
            dkr = dkr + dkr_ref[h]
        dp_ref[:, :Q_LORA] = dq.astype(BF)
        dp_ref[:, Q_LORA:Q_LORA + KV_LORA] = dkv.astype(BF)
        dp_ref[:, Q_LORA + KV_LORA:] = _rope_t(dkr, cos_ref[...], sa_ref[...], sb_ref[...]).astype(BF)

    tab = _spec((ts, HEAD_PAD), lambda i: (i, 0))
    lat = _spec((ts, Q_LORA), lambda i: (i, 0))
    gq = _spec((1, Q_LORA), lambda i: (0, 0))
    return pl.pallas_call(
        body, name=name, grid=(S // ts,),
        in_specs=[_spec((ts, P), lambda i: (i, 0)), lat, lat, _spec((H, ts, HEAD_PAD), lambda i: (0, i, 0)),
                  gq, gq, tab, tab, tab],
        out_specs=[_spec((ts, P), lambda i: (i, 0)), gq, gq],
        out_shape=[_sd((S, P), BF), _sd((1, Q_LORA), F32), _sd((1, KV_LORA), F32)],
        compiler_params=_params(("arbitrary",), 24 * ts * P * 4))(proj, dqn, dkvn, dkr_heads, qg, kvg, cos, sa, sb)


def _attn_tile(S):
    return _pick(S, (512,)) if S >= 2048 else _pick(S, (128,))


def _diag_mask(t, transposed):
    q = lax.broadcasted_iota(jnp.int32, (t, t), 1 if transposed else 0) // CHUNK
    k = lax.broadcasted_iota(jnp.int32, (t, t), 0 if transposed else 1) // CHUNK
    return k <= q


_NT = (((1,), (1,)), ((), ()))


def _attn_fwd(name, q_all, kv_all, kr):
    _, S, HP = q_all.shape
    H = HP // HEAD_PAD
    t = _attn_tile(S)
    nq = S // t
    ng = ATTN_GROUPS
    tg = t // ng

    def body(q_ref, kv_ref, kr_ref, o_ref, lse_ref, kcat_ref):
        i = pl.program_id(1)

        @pl.when(i == 0)
        def _():
            kcat_ref[:, :HEAD_PAD] = kv_ref[0]
            kcat_ref[:, HEAD_PAD:] = kr_ref[...]

        qs = [jnp.concatenate([q_ref[0, g * tg:(g + 1) * tg], q_ref[1, g * tg:(g + 1) * tg]], axis=1) for g in range(ng)]

        def step(j, carry, masked):
            off = pl.multiple_of(j * t, t)
            kj = kcat_ref[pl.ds(off, t), :]
            vj = kv_ref[1, pl.ds(off, t), :]
            out = []
            for g in range(ng):
                m, l, acc = carry[g]
                s = lax.dot_general(qs[g], kj, _NT, preferred_element_type=F32)
                if masked:
                    s = jnp.where(_diag_mask(t, False)[g * tg:(g + 1) * tg], s, NEG)
                m2 = jnp.maximum(m, jnp.max(s, axis=-1, keepdims=True))
                al = jnp.exp(m - m2)
                p = jnp.exp(s - m2)
                l2 = al * l + jnp.sum(p, axis=-1, keepdims=True)
                acc2 = al * acc + jnp.dot(p.astype(BF), vj, preferred_element_type=F32)
                out.append((m2, l2, acc2))
            return tuple(out)

        init = tuple((jnp.full((tg, 1), NEG, F32), jnp.zeros((tg, 1), F32), jnp.zeros((tg, V_DIM), F32))
                     for _ in range(ng))
        carry = lax.fori_loop(0, i, lambda j, c: step(j, c, False), init)
        carry = step(i, carry, True)
        for g in range(ng):
            m, l, acc = carry[g]
            o_ref[g * tg:(g + 1) * tg, :] = acc / l
            lse_ref[g * tg:(g + 1) * tg, :] = jnp.broadcast_to(m + jnp.log(l), (tg, LANES))

    return pl.pallas_call(
        body, name=name, grid=(H, nq),
        in_specs=[_spec((2, t, HEAD_PAD), lambda h, i: (0, i, h)), _spec((2, S, HEAD_PAD), lambda h, i: (0, 0, h)),
                  _spec((S, HEAD_PAD), lambda h, i: (0, 0))],
        out_specs=[_spec((t, HEAD_PAD), lambda h, i: (i, h)), _spec((None, t, LANES), lambda h, i: (h, i, 0))],
        out_shape=[_sd((S, HP), F32), _sd((H, S, LANES), F32)],
        scratch_shapes=[pltpu.VMEM((S, 2 * HEAD_PAD), BF)],
        compiler_params=_params(("parallel", "arbitrary"), 8 * S * HEAD_PAD * 2 + 24 * t * t * 4))(q_all, kv_all, kr)


def _attn_delta(name, do, o):
    S, HP = o.shape
    H = HP // HEAD_PAD
    ts = _pick(S, (512, 256, 128))

    def body(do_ref, o_ref, d_ref):
        d_ref[...] = jnp.broadcast_to(jnp.sum(do_ref[...] * o_ref[...], axis=-1, keepdims=True), (ts, LANES))

    tile = _spec((ts, HEAD_PAD), lambda h, i: (i, h))
    return pl.pallas_call(
        body, name=name, grid=(H, S // ts), in_specs=[tile, tile],
        out_specs=_spec((None, ts, LANES), lambda h, i: (h, i, 0)), out_shape=_sd((H, S, LANES), F32),
        compiler_params=_params(("parallel", "parallel"), VMEM_FLOOR))(do, o)


def _attn_dq(name, q_all, kv_all, kr, do, lse, delta, cos, sa, sb, scale):
    _, S, HP = q_all.shape
    H = HP // HEAD_PAD
    t = _attn_tile(S)
    nq = S // t

    def body(q_ref, kv_ref, kr_ref, do_ref, lse_ref, dl_ref, cos_ref, sa_ref, sb_ref, dq_ref, kcat_ref):
        i = pl.program_id(1)

        @pl.when(i == 0)
        def _():
            kcat_ref[:, :HEAD_PAD] = kv_ref[0]
            kcat_ref[:, HEAD_PAD:] = kr_ref[...]

        ng = ATTN_GROUPS
        tg = t // ng
        rows = [slice(g * tg, (g + 1) * tg) for g in range(ng)]
        qs = [jnp.concatenate([q_ref[0, r], q_ref[1, r]], axis=1) for r in rows]
        dobs = [do_ref[r, :].astype(BF) for r in rows]
        lses = [lse_ref[r, 0:1] for r in rows]
        dls = [dl_ref[r, 0:1] for r in rows]

        def step(j, dqs, masked):
            off = pl.multiple_of(j * t, t)
            kj = kcat_ref[pl.ds(off, t), :]
            vj = kv_ref[1, pl.ds(off, t), :]
            out = []
            for g in range(ng):
                s = lax.dot_general(qs[g], kj, _NT, preferred_element_type=F32)
                if masked:
                    s = jnp.where(_diag_mask(t, False)[rows[g]], s, NEG)
                p = jnp.exp(s - lses[g])
                dp = lax.dot_general(dobs[g], vj, _NT, preferred_element_type=F32)
                ds = (p * (dp - dls[g])).astype(BF)
                out.append(dqs[g] + jnp.dot(ds, kj, preferred_element_type=F32))
            return tuple(out)

        init = tuple(jnp.zeros((tg, 2 * HEAD_PAD), F32) for _ in range(ng))
        dqs = lax.fori_loop(0, i, lambda j, c: step(j, c, False), init)
        dqs = step(i, dqs, True)
        for g in range(ng):
            dq_ref[0, rows[g]] = (dqs[g][:, :HEAD_PAD] * scale).astype(BF)
            dq_ref[1, rows[g]] = (_rope_t(dqs[g][:, HEAD_PAD:], cos_ref[rows[g], :], sa_ref[rows[g], :],
                                          sb_ref[rows[g], :]) * scale).astype(BF)

    tab = _spec((t, HEAD_PAD), lambda h, i: (i, 0))
    stat = _spec((None, t, LANES), lambda h, i: (h, i, 0))
    return pl.pallas_call(
        body, name=name, grid=(H, nq),
        in_specs=[_spec((2, t, HEAD_PAD), lambda h, i: (0, i, h)), _spec((2, S, HEAD_PAD), lambda h, i: (0, 0, h)),
                  _spec((S, HEAD_PAD), lambda h, i: (0, 0)), _spec((t, HEAD_PAD), lambda h, i: (i, h)), stat, stat,
                  tab, tab, tab],
        out_specs=_spec((2, t, HEAD_PAD), lambda h, i: (0, i, h)), out_shape=_sd((2, S, HP), BF),
        scratch_shapes=[pltpu.VMEM((S, 2 * HEAD_PAD), BF)],
        compiler_params=_params(("parallel", "arbitrary"), 8 * S * HEAD_PAD * 2 + 32 * t * t * 4))(
            q_all, kv_all, kr, do, lse, delta, cos, sa, sb)


def _attn_dkv(name, q_all, kv_all, kr, do, lse_row, delta_row):
    _, S, HP = q_all.shape
    H = HP // HEAD_PAD
    t = _attn_tile(S)
    nq = S // t

    def body(q_ref, kv_ref, kr_ref, do_ref, lse_ref, dl_ref, dkv_ref, dkr_ref, qcat_ref):
        j = pl.program_id(1)

        @pl.when(j == 0)
        def _():
            qcat_ref[:, :HEAD_PAD] = q_ref[0]
            qcat_ref[:, HEAD_PAD:] = q_ref[1]

        ng = ATTN_GROUPS
        tg = t // ng
        rows = [slice(g * tg, (g + 1) * tg) for g in range(ng)]
        kjs = [jnp.concatenate([kv_ref[0, r], kr_ref[r, :]], axis=1) for r in rows]
        vjs = [kv_ref[1, r] for r in rows]

        def step(i, carry, masked):
            off = pl.multiple_of(i * t, t)
            qi = qcat_ref[pl.ds(off, t), :]
            doi = do_ref[pl.ds(off, t), :].astype(BF)
            lse_i = lse_ref[i]
            dl_i = dl_ref[i]
            out = []
            for g in range(ng):
                dk, dv = carry[g]
                st = lax.dot_general(kjs[g], qi, _NT, preferred_element_type=F32)
                if masked:
                    st = jnp.where(_diag_mask(t, True)[rows[g]], st, NEG)
                pt = jnp.exp(st - lse_i)
                dv2 = dv + jnp.dot(pt.astype(BF), doi, preferred_element_type=F32)
                dpt = lax.dot_general(vjs[g], doi, _NT, preferred_element_type=F32)
                dst = (pt * (dpt - dl_i)).astype(BF)
                out.append((dk + jnp.dot(dst, qi, preferred_element_type=F32), dv2))
            return tuple(out)

        init = tuple((jnp.zeros((tg, 2 * HEAD_PAD), F32), jnp.zeros((tg, V_DIM), F32)) for _ in range(ng))
        carry = step(j, init, True)
        carry = lax.fori_loop(j + 1, nq, lambda i, c: step(i, c, False), carry)
        for g in range(ng):
            dk, dv = carry[g]
            dkv_ref[0, rows[g]] = dk[:, :HEAD_PAD].astype(BF)
            dkv_ref[1, rows[g]] = dv.astype(BF)
            dkr_ref[rows[g], :] = dk[:, HEAD_PAD:]

    stat = _spec((None, nq, 1, t), lambda h, j: (h, 0, 0, 0))
    return pl.pallas_call(
        body, name=name, grid=(H, nq),
        in_specs=[_spec((2, S, HEAD_PAD), lambda h, j: (0, 0, h)), _spec((2, t, HEAD_PAD), lambda h, j: (0, j, h)),
                  _spec((t, HEAD_PAD), lambda h, j: (j, 0)), _spec((S, HEAD_PAD), lambda h, j: (0, h)), stat, stat],
        out_specs=[_spec((2, t, HEAD_PAD), lambda h, j: (0, j, h)), _spec((None, t, HEAD_PAD), lambda h, j: (h, j, 0))],
        out_shape=[_sd((2, S, HP), BF), _sd((H, S, HEAD_PAD), F32)],
        scratch_shapes=[pltpu.VMEM((S, 2 * HEAD_PAD), BF)],
        compiler_params=_params(("parallel", "arbitrary"), 8 * S * HEAD_PAD * 4 + 32 * t * t * 4))(
            q_all, kv_all, kr, do, lse_row, delta_row)


def _place():
    x, y, c = lax.axis_index("x"), lax.axis_index("y"), lax.axis_index("c")
    return x, y, c


def _all_gather(blocks):
    n = len(blocks)

    def body(*refs):
        ins, outs = refs[:n], refs[n:2 * n]
        send_sems, recv_sems, local_sems = refs[2 * n:]
        x, y, c = _place()
        me = 4 * x + 2 * y + c
        sibling = (x, y, 1 - c)
        chips = [(1 - x, y), (x, 1 - y), (1 - x, 1 - y)]

        def slab(a, px, py, pc):
            return outs[a].at[4 * px + 2 * py + pc]

        def copy(a, k, src, dst, to):
            return pltpu.make_async_remote_copy(src_ref=src, dst_ref=dst, send_sem=send_sems.at[a, k],
                                                recv_sem=recv_sems.at[a, k], device_id=to, device_id_type=MESH)

        local = [pltpu.make_async_copy(ins[a], outs[a].at[me], local_sems.at[a]) for a in range(n)]
        for cp in local:
            cp.start()
        sends = []
        for a in range(n):
            mine = slab(a, x, y, c)
            sends.append(copy(a, 0, ins[a], mine, sibling))
            for j, chip in enumerate(chips):
                sends.append(copy(a, 1 + j, ins[a], mine, (*chip, c)))
        for cp in sends:
            cp.start()
        for j, chip in enumerate(chips):
            for a in range(n):
                got = slab(a, *chip, c)
                copy(a, 1 + j, got, got, (x, y, c)).wait_recv()
                fwd = copy(a, 4 + j, got, got, sibling)
                fwd.start()
                sends.append(fwd)
        for a in range(n):
            got = slab(a, x, y, 1 - c)
            copy(a, 0, got, got, (x, y, c)).wait_recv()
            for j, chip in enumerate(chips):
                got = slab(a, *chip, 1 - c)
                copy(a, 4 + j, got, got, (x, y, c)).wait_recv()
        for cp in sends:
            cp.wait_send()
        for cp in local:
            cp.wait()

    return pl.pallas_call(
        body, name="weights_all_gather", in_specs=[_ANY] * n, out_specs=[_ANY] * n,
        out_shape=[_sd((N_DEV,) + b.shape, b.dtype) for b in blocks],
        scratch_shapes=[pltpu.SemaphoreType.DMA((n, 7)), pltpu.SemaphoreType.DMA((n, 7)), pltpu.SemaphoreType.DMA((n,))],
    )(*blocks)


_HBM = pl.BlockSpec(memory_space=pltpu.HBM)
_SEM = pl.BlockSpec(memory_space=pltpu.SEMAPHORE)
_EFFECT = pltpu.SideEffectType.DATAFLOW_SIDE_EFFECTING


def _peers():
    x, y, c = _place()
    out = []
    for m in range(1, N_DEV):
        px, py, pc = x ^ (m >> 2), y ^ ((m >> 1) & 1), c ^ (m & 1)
        out.append((m, (px, py, pc), 4 * px + 2 * py + pc))
    return 4 * x + 2 * y + c, out


def _exchange_copies(src, land, send_sem, recv_sem, gather):
    me, peers = _peers()
    cps = []
    for a in range(len(src)):
        for m, pos, idx in peers:
            s_ref, d_ref = (src[a], land[a].at[me]) if gather else (src[a].at[idx], land[a].at[m - 1])
            k = a * (N_DEV - 1) + m - 1
            cps.append(pltpu.make_async_remote_copy(src_ref=s_ref, dst_ref=d_ref, send_sem=send_sem.at[k],
                                                    recv_sem=recv_sem.at[k], device_id=pos, device_id_type=MESH))
    return cps


def _xstart(name, srcs, gather, after=()):
    n = len(srcs)
    if gather:
        land_shapes = [(N_DEV,) + s.shape for s in srcs]
    else:
        land_shapes = [(N_DEV - 1,) + s.shape[1:] for s in srcs]

    def body(*refs):
        src, land = refs[:n], refs[n:2 * n]
        send_sem, recv_sem = refs[2 * n + len(after)], refs[2 * n + len(after) + 1]
        token = refs[-1]
        for cp in _exchange_copies(src, land, send_sem, recv_sem, gather):
            cp.start()
        token[...] = jnp.zeros_like(token)

    sem = pltpu.SemaphoreType.DMA((n * (N_DEV - 1),))
    out_shape = ([sem, sem] + [pltpu.HBM(s.shape, s.dtype) for s in srcs]
                 + [pltpu.HBM(sh, s.dtype) for sh, s in zip(land_shapes, srcs)] + [_sd((8, LANES), F32)])
    args = [pltpu.with_memory_space_constraint(s, pltpu.HBM) for s in srcs]
    args += [pltpu.with_memory_space_constraint(lax.empty(sh, s.dtype), pltpu.HBM) for sh, s in zip(land_shapes, srcs)]
    res = pl.pallas_call(
        body, name=name, out_shape=out_shape, in_specs=[_HBM] * (2 * n) + [_ANY] * len(after),
        out_specs=[_SEM, _SEM] + [_HBM] * (2 * n) + [pl.BlockSpec(memory_space=pltpu.VMEM)],
        input_output_aliases={i: 2 + i for i in range(2 * n)},
        compiler_params=pltpu.CompilerParams(has_side_effects=_EFFECT))(*args, *after)
    return dict(send=res[0], recv=res[1], srcs=list(res[2:2 + n]), lands=list(res[2 + n:2 + 2 * n]), token=res[-1])


def _xwait(name, st, after, gather):
    n = len(st["srcs"])

    def body(*refs):
        src, land = refs[:n], refs[n:2 * n]
        send_sem, recv_sem = refs[2 * n], refs[2 * n + 1]
        for cp in _exchange_copies(src, land, send_sem, recv_sem, gather):
            cp.wait_send()
            cp.wait_recv()

    arrays = st["srcs"] + st["lands"]
    res = pl.pallas_call(
        body, name=name, out_shape=[pltpu.HBM(a.shape, a.dtype) for a in arrays],
        in_specs=[_HBM] * (2 * n) + [_SEM, _SEM, _ANY], out_specs=[_HBM] * (2 * n),
        input_output_aliases={i: i for i in range(2 * n)},
        compiler_params=pltpu.CompilerParams(has_side_effects=_EFFECT))(*arrays, st["send"], st["recv"], after)
    return list(res[:n]), list(res[n:])


def _put_own(name, land, block, me_arr):
    R, C = block.shape
    tr = _pick(R, (512, 256, 128, 64, 32, 16))

    def body(me_ref, b_ref, land_ref, o_ref):
        o_ref[...] = b_ref[...]

    return pl.pallas_call(
        body, name=name, out_shape=_sd(land.shape, land.dtype),
        grid_spec=pltpu.PrefetchScalarGridSpec(
            num_scalar_prefetch=1, grid=(R // tr,),
            in_specs=[_spec((tr, C), lambda i, me_ref: (i, 0)), _ANY],
            out_specs=_spec((None, tr, C), lambda i, me_ref: (me_ref[0], i, 0))),
        input_output_aliases={2: 0},
        compiler_params=_params(("parallel",), 8 * tr * C * 2))(me_arr, block, land)


def _all_reduce_small(name, part):
    R = part.shape[0]

    def body(p_ref, out_ref, gath_ref, send_sems, recv_sems):
        x, y, c = _place()
        me = 4 * x + 2 * y + c
        gath_ref[me] = p_ref[...]
        cps = []
        for m in range(1, N_DEV):
            to = (x ^ (m >> 2), y ^ ((m >> 1) & 1), c ^ (m & 1))
            cps.append(pltpu.make_async_remote_copy(
                src_ref=p_ref, dst_ref=gath_ref.at[me], send_sem=send_sems.at[m - 1], recv_sem=recv_sems.at[m - 1],
                device_id=to, device_id_type=MESH))
        for cp in cps:
            cp.start()
        for m in range(1, N_DEV):
            frm = 4 * (x ^ (m >> 2)) + 2 * (y ^ ((m >> 1) & 1)) + (c ^ (m & 1))
            pltpu.make_async_remote_copy(
                src_ref=p_ref, dst_ref=gath_ref.at[frm], send_sem=send_sems.at[m - 1], recv_sem=recv_sems.at[m - 1],
                device_id=(x, y, c), device_id_type=MESH).wait_recv()
        for cp in cps:
            cp.wait_send()
        tot = gath_ref[0]
        for k in range(1, N_DEV):
            tot = tot + gath_ref[k]
        out_ref[...] = tot

    vm = pl.BlockSpec(memory_space=pltpu.VMEM)
    return pl.pallas_call(
        body, name=name, in_specs=[vm], out_specs=vm, out_shape=_sd((R, LANES), F32),
        scratch_shapes=[pltpu.VMEM((N_DEV, R, LANES), F32), pltpu.SemaphoreType.DMA((N_DEV - 1,)),
                        pltpu.SemaphoreType.DMA((N_DEV - 1,))],
        compiler_params=pltpu.CompilerParams(vmem_limit_bytes=VMEM_FLOOR),
    )(part)


def _adam_math(w, g, m, v):
    m2 = ADAM_B1 * m + (1.0 - ADAM_B1) * g
    v2 = ADAM_B2 * v + (1.0 - ADAM_B2) * (g * g)
    m_hat = m2 / (1.0 - ADAM_B1 ** ADAM_STEP)
    v_hat = v2 / (1.0 - ADAM_B2 ** ADAM_STEP)
    delta = -ADAM_LR * (m_hat / (jnp.sqrt(v_hat) + ADAM_EPS) + ADAM_WD * w)
    return delta, m2, v2


def _adamw_sharded(name, lands, fulls, me_arr, w, m, v):
    n_l = len(lands)
    R, C = lands[0].shape[1], lands[0].shape[2]
    tr = _pick(R, (128, 64, 32, 16, 8))
    nr = R // tr

    def body(me_ref, *refs):
        land_refs, own_refs = refs[:n_l], refs[n_l:2 * n_l]
        w_ref, m_ref, v_ref, g_ref, d_ref, m2_ref, v2_ref = refs[2 * n_l:]
        layer = pl.program_id(0)
        for ll in range(n_l):
            @pl.when(layer == ll)
            def _(ll=ll):
                g = own_refs[ll][...].astype(F32)
                for j in range(N_DEV - 1):
                    g = g + land_refs[ll][j].astype(F32)
                delta, m2, v2 = _adam_math(w_ref[...], g, m_ref[...], v_ref[...])
                g_ref[...] = g
                d_ref[...] = delta
                m2_ref[...] = m2
                v2_ref[...] = v2

    def row_of(ll):
        return lambda l, i, me_ref: jnp.where(l == ll, i, 0)

    in_specs = [_spec((N_DEV - 1, tr, C), lambda l, i, me_ref, f=row_of(ll): (0, f(l, i, me_ref), 0)) for ll in range(n_l)]
    in_specs += [_spec((None, tr, C), lambda l, i, me_ref, f=row_of(ll): (me_ref[0], f(l, i, me_ref), 0))
                 for ll in range(n_l)]
    blk = _spec((tr, C), lambda l, i, me_ref: (l * nr + i, 0))
    return pl.pallas_call(
        body, name=name, out_shape=[_sd(w.shape, F32)] * 4,
        grid_spec=pltpu.PrefetchScalarGridSpec(num_scalar_prefetch=1, grid=(n_l, nr), in_specs=in_specs + [blk] * 3,
                                               out_specs=[blk] * 4),
        compiler_params=_params(("parallel", "parallel"), (4 * n_l * N_DEV + 40) * tr * C * 4))(
            me_arr, *lands, *fulls, w, m, v)


def _adamw_packed(name, w, g, m, v):
    R = w.shape[0]

    def body(w_ref, g_ref, m_ref, v_ref, d_ref, m2_ref, v2_ref):
        delta, m2, v2 = _adam_math(w_ref[...], g_ref[...], m_ref[...], v_ref[...])
        d_ref[...] = delta
        m2_ref[...] = m2
        v2_ref[...] = v2

    vm = pl.BlockSpec(memory_space=pltpu.VMEM)
    return pl.pallas_call(
        body, name=name, in_specs=[vm] * 4, out_specs=[vm] * 3, out_shape=[_sd((R, LANES), F32)] * 3,
        compiler_params=pltpu.CompilerParams(vmem_limit_bytes=VMEM_FLOOR))(w, g, m, v)


def _pack(arrays):
    flat = jnp.concatenate([a.reshape(-1).astype(F32) for a in arrays])
    pad = (-flat.shape[0]) % (8 * LANES)
    return jnp.pad(flat, (0, pad)).reshape(-1, LANES)


def _unpack(packed, like):
    flat = packed.reshape(-1)
    out, pos = [], 0
    for a in like:
        n = math.prod(a.shape)
        out.append(flat[pos:pos + n].reshape(a.shape))
        pos += n
    return out


def _ffn_fwd(tag, x, gain, w_in_sm, w_out_of, deps=()):
    h = _rms_fwd(tag + "_norm", x, gain, deps)
    gu, act = _pair_in(tag + "_in", h, w_in_sm, BF, _swiglu)
    x_new = _rows_out(tag + "_out", act, w_out_of(act), x, 0.5)
    return x_new, (x, h, gu, act)


def _ffn_bwd(tag, d, d_bf, saved, gain, w_in_sm, w_out_sm, send_grads):
    x, h, gu, act = saved
    dgu = _rows_dact(tag + "_dact", d_bf, w_out_sm, [gu], [_sd(gu.shape, BF)], _swiglu_bwd_epi)[0]
    g_out = _rows_wgrad(tag + "_wgrad_out", act, d_bf, 0.5)
    g_in = _cols_wgrad(tag + "_wgrad_in", h, dgu)
    token = send_grads(g_in, g_out)
    dh = _cols_dh(tag + "_dh", dgu, w_in_sm, deps=(token,))
    dx, dx_bf, dgain = _rms_bwd(tag + "_dnorm", dh, x, gain, d)
    return dx, dx_bf, dgain


def kernel(x, positions, ln_ffn1, ffn1_w_in, ffn1_w_out, ln_mix, ln_ffn2, ffn2_w_in, ffn2_w_out, sgu_w_in, sgu_v_gain, sgu_v_bias, sgu_w_spatial, sgu_b_spatial, sgu_w_out, mla_w_in, mla_q_norm, mla_w_q_up, mla_kv_norm, mla_w_kv_up, mla_w_out, ln_final, loss_target, m_ln_ffn1, m_ffn1_w_in, m_ffn1_w_out, m_ln_mix, m_ln_ffn2, m_ffn2_w_in, m_ffn2_w_out, m_sgu_w_in, m_sgu_v_gain, m_sgu_v_bias, m_sgu_w_spatial, m_sgu_b_spatial, m_sgu_w_out, m_mla_w_in, m_mla_q_norm, m_mla_w_q_up, m_mla_kv_norm, m_mla_w_kv_up, m_mla_w_out, m_ln_final, v_ln_ffn1, v_ffn1_w_in, v_ffn1_w_out, v_ln_mix, v_ln_ffn2, v_ffn2_w_in, v_ffn2_w_out, v_sgu_w_in, v_sgu_v_gain, v_sgu_v_bias, v_sgu_w_spatial, v_sgu_b_spatial, v_sgu_w_out, v_mla_w_in, v_mla_q_norm, v_mla_w_q_up, v_mla_kv_norm, v_mla_w_kv_up, v_mla_w_out, v_ln_final):
    S, D = x.shape[1], x.shape[2]
    L = ln_ffn1.shape[0]
    H = mla_w_q_up.shape[-1] * N_DEV // (QK_NOPE + QK_ROPE)
    xi, yi, ci = _place()
    me = 4 * xi + 2 * yi + ci
    me_arr = jnp.reshape(me, (1,)).astype(jnp.int32)
    big = dict(ffn1_w_in=ffn1_w_in, ffn1_w_out=ffn1_w_out, ffn2_w_in=ffn2_w_in, ffn2_w_out=ffn2_w_out,
               sgu_w_in=sgu_w_in, sgu_w_out=sgu_w_out, mla_w_in=mla_w_in, mla_w_q_up=mla_w_q_up,
               mla_w_kv_up=mla_w_kv_up, mla_w_out=mla_w_out)
    big_m = dict(ffn1_w_in=m_ffn1_w_in, ffn1_w_out=m_ffn1_w_out, ffn2_w_in=m_ffn2_w_in, ffn2_w_out=m_ffn2_w_out,
                 sgu_w_in=m_sgu_w_in, sgu_w_out=m_sgu_w_out, mla_w_in=m_mla_w_in, mla_w_q_up=m_mla_w_q_up,
                 mla_w_kv_up=m_mla_w_kv_up, mla_w_out=m_mla_w_out)
    big_v = dict(ffn1_w_in=v_ffn1_w_in, ffn1_w_out=v_ffn1_w_out, ffn2_w_in=v_ffn2_w_in, ffn2_w_out=v_ffn2_w_out,
                 sgu_w_in=v_sgu_w_in, sgu_w_out=v_sgu_w_out, mla_w_in=v_mla_w_in, mla_w_q_up=v_mla_w_q_up,
                 mla_w_kv_up=v_mla_w_kv_up, mla_w_out=v_mla_w_out)
    names = list(big)
    mla_names = ["mla_w_in", "mla_w_q_up", "mla_w_kv_up", "mla_w_out"]

    blocks = {(k, l): big[k][l].astype(BF) for k in names for l in range(big[k].shape[0])}
    first = ("ffn1_w_in", 0)
    groups = {}
    for i in range(L):
        if i > 0:
            groups[f"ffn1_in_{i}"] = [("ffn1_w_in", i)]
        groups[f"ffn1_out_{i}"] = [("ffn1_w_out", i)]
        groups[f"mix_{i}"] = [("sgu_w_in", i // 2), ("sgu_w_out", i // 2)] if i % 2 == 0 else [(k, i // 2) for k in mla_names]
        groups[f"ffn2_{i}"] = [("ffn2_w_in", i), ("ffn2_w_out", i)]
    gathered = {first: _all_gather([blocks[first]])[0]}
    started, order_after = {}, (gathered[first],)
    for tag, grp in groups.items():
        started[tag] = _xstart(f"gather_start_{tag}", [blocks[k] for k in grp], True, order_after)
        order_after = (started[tag]["token"],)

    def fetch(tag, after):
        own, lands = _xwait(f"gather_wait_{tag}", started[tag], after, True)
        for k, blk, land in zip(groups[tag], own, lands):
            gathered[k] = _put_own(f"gather_own_{k[0]}_{k[1]}", land, blk, me_arr)

    def w_out_of(name, tag):
        def get(after):
            fetch(tag, after)
            return gathered[name]
        return get

    norm_rows = jnp.zeros((N_DEV, LANES), F32)
    mine = jnp.concatenate([mla_q_norm[0], mla_kv_norm[0]])
    norm_rows = lax.dynamic_update_slice(norm_rows, mine[None, :], (me, 0))
    norm_all = _all_reduce_small("norm_gains_gather", norm_rows)
    nq_sh = mla_q_norm.shape[1]
    q_gain = norm_all[:, :nq_sh].reshape(1, Q_LORA)
    kv_gain = norm_all[:, nq_sh:2 * nq_sh].reshape(1, KV_LORA)
    cos, sa, sb = _rope_tables(positions[0])
    scale = float((QK_NOPE + QK_ROPE) ** -0.5)

    xs = x[0]
    w_sp = sgu_w_spatial[0]
    b_sp = sgu_b_spatial[0][:, :, None]
    saved = []
    mla_w = {}
    deps = order_after
    for i in range(L):
        if i > 0:
            fetch(f"ffn1_in_{i}", xs)
        xs, s1 = _ffn_fwd(f"l{i}_ffn1", xs, ln_ffn1[i:i + 1], gathered[("ffn1_w_in", i)],
                          w_out_of(("ffn1_w_out", i), f"ffn1_out_{i}"), deps)
        deps = ()
        fetch(f"mix_{i}", xs)
        x_mix = xs
        h = _rms_fwd(f"l{i}_mix_norm", xs, ln_mix[i:i + 1])
        j = i // 2
        if i % 2 == 0:
            puv = _pair_in(f"l{i}_sgu_in", h, gathered[("sgu_w_in", j)], F32, None)[0]
            gated = _sgu_mid_fwd(f"l{i}_sgu_mid", puv, sgu_v_gain, sgu_v_bias, w_sp, b_sp)
            xs = _rows_out(f"l{i}_sgu_out", gated, gathered[("sgu_w_out", j)], xs, 1.0)
            sm = (x_mix, h, puv, gated)
        else:
            w_in_nat = gathered[("mla_w_in", j)].reshape(D, Q_LORA + KV_LORA + QK_ROPE)
            w_in_pad = jnp.pad(w_in_nat, ((0, 0), (0, HEAD_PAD - QK_ROPE)))
            wq_nat = jnp.transpose(gathered[("mla_w_q_up", j)], (1, 0, 2)).reshape(Q_LORA, H, QK_NOPE + QK_ROPE)
            wq_t = jnp.stack([wq_nat[:, :, :QK_NOPE].reshape(Q_LORA, H * HEAD_PAD),
                              jnp.pad(wq_nat[:, :, QK_NOPE:], ((0, 0), (0, 0), (0, HEAD_PAD - QK_ROPE))).reshape(
                                  Q_LORA, H * HEAD_PAD)])
            wkv_nat = jnp.transpose(gathered[("mla_w_kv_up", j)], (1, 0, 2)).reshape(KV_LORA, H, QK_NOPE + V_DIM)
            wkv_t = jnp.stack([wkv_nat[:, :, :QK_NOPE].reshape(KV_LORA, H * HEAD_PAD),
                               wkv_nat[:, :, QK_NOPE:].reshape(KV_LORA, H * HEAD_PAD)])
            w_o_nat = gathered[("mla_w_out", j)].reshape(H * V_DIM, D)
            mla_w[i] = (w_in_pad, wq_t, wkv_t, w_o_nat)
            proj = _mm2(f"l{i}_mla_in", h, w_in_pad, False, False, F32, tn_cands=(384, 128))
            lat, kr = _mla_mid_fwd(f"l{i}_mla_mid", proj, q_gain, kv_gain, cos, sa, sb)

            def q_epi(accs, ex, orefs, ids):
                orefs[0][...] = (accs[0] * scale).astype(BF)

            def qr_epi(accs, ex, orefs, ids):
                for hh in range(accs[0].shape[1] // HEAD_PAD):
                    sl = slice(hh * HEAD_PAD, (hh + 1) * HEAD_PAD)
                    orefs[0][:, sl] = (_rope(accs[0][:, sl], *ex) * scale).astype(BF)

            q_nope = _mm2(f"l{i}_mla_q", lat[0], wq_t[0], False, False, BF, epi=q_epi)
            q_rope = _mm2(f"l{i}_mla_qr", lat[0], wq_t[1], False, False, BF, epi=qr_epi, extras=(cos, sa, sb))
            q_all = jnp.stack([q_nope, q_rope])
            kv_all = jnp.stack([_mm2(f"l{i}_mla_k", lat[1], wkv_t[0], False, False, BF),
                                _mm2(f"l{i}_mla_v", lat[1], wkv_t[1], False, False, BF)])
            o, lse = _attn_fwd(f"l{i}_attn", q_all, kv_all, kr)
            xs = _mm2(f"l{i}_mla_out", o, w_o_nat, False, False, F32, res=xs)
            sm = (x_mix, h, proj, lat, kr, q_all, kv_all, o, lse)
        fetch(f"ffn2_{i}", xs)
        xs, s2 = _ffn_fwd(f"l{i}_ffn2", xs, ln_ffn2[i:i + 1], gathered[("ffn2_w_in", i)],
                          lambda after, i=i: gathered[("ffn2_w_out", i)])
        saved.append((s1, sm, s2))

    loss_row, d, d_bf, g_ln_final = _final_loss("final_loss", xs, ln_final[None, :], loss_target[0])
    loss = lax.psum(loss_row[0, 0], ("x", "y", "c"))

    sent = []

    def send(tag, keys, grads):
        st = _xstart(f"scatter_start_{tag}", grads, False)
        sent.append((tag, keys, st))
        return st["token"]

    g_ln1, g_ln2, g_lnm = [None] * L, [None] * L, [None] * L
    small_g = {}
    for i in reversed(range(L)):
        s1, sm, s2 = saved[i]
        d, d_bf, g_ln2[i] = _ffn_bwd(
            f"l{i}_ffn2", d, d_bf, s2, ln_ffn2[i:i + 1], gathered[("ffn2_w_in", i)], gathered[("ffn2_w_out", i)],
            lambda g_in, g_out, i=i: send(f"l{i}_ffn2", [("ffn2_w_in", i), ("ffn2_w_out", i)], [g_in, g_out]))
        j = i // 2
        if i % 2 == 0:
            x_mix, h, puv, gated = sm
            dgated = _rows_dact(f"l{i}_sgu_dgated", d_bf, gathered[("sgu_w_out", j)], [], [_sd(gated.shape, BF)], _store())[0]
            g_so = _rows_wgrad(f"l{i}_sgu_wgrad_out", gated, d_bf, None)
            dpuv, dgain, dbias, dwsp, dbsp = _sgu_mid_bwd(f"l{i}_sgu_mid_bwd", puv, dgated, sgu_v_gain, sgu_v_bias,
                                                          w_sp, b_sp)
            small_g.update(sgu_v_gain=dgain, sgu_v_bias=dbias, sgu_w_spatial=dwsp[None], sgu_b_spatial=dbsp[None, :, :, 0])
            g_si = _cols_wgrad(f"l{i}_sgu_wgrad_in", h, dpuv)
            token = send(f"l{i}_sgu", [("sgu_w_in", j), ("sgu_w_out", j)], [g_si, g_so])
            dh = _cols_dh(f"l{i}_sgu_dh", dpuv, gathered[("sgu_w_in", j)], deps=(token,))
        else:
            x_mix, h, proj, lat, kr, q_all, kv_all, o, lse = sm
            w_in_pad, wq_t, wkv_t, w_o_nat = mla_w[i]
            t = _attn_tile(S)
            do = _mm2(f"l{i}_mla_do", d_bf, w_o_nat, False, True, F32)
            g_wo = _mm2(f"l{i}_mla_wgrad_out", o, d_bf, True, False, BF)
            delta = _attn_delta(f"l{i}_attn_delta", do, o)
            dq_all = _attn_dq(f"l{i}_attn_dq", q_all, kv_all, kr, do, lse, delta, cos, sa, sb, scale)
            lse_row = lse[:, :, 0].reshape(H, S // t, 1, t)
            delta_row = delta[:, :, 0].reshape(H, S // t, 1, t)
            dkv_all, dkr_heads = _attn_dkv(f"l{i}_attn_dkv", q_all, kv_all, kr, do, lse_row, delta_row)
            dqn = _mm2(f"l{i}_mla_dqn", dq_all[0], wq_t[0], False, True, F32)
            dqn = _mm2(f"l{i}_mla_dqn2", dq_all[1], wq_t[1], False, True, F32, res=dqn)
            dkvn = _mm2(f"l{i}_mla_dkvn", dkv_all[0], wkv_t[0], False, True, F32)
            dkvn = _mm2(f"l{i}_mla_dkvn2", dkv_all[1], wkv_t[1], False, True, F32, res=dkvn)
            g_wq = [_mm2(f"l{i}_mla_wgrad_q{t2}", lat[0], dq_all[t2], True, False, BF) for t2 in range(2)]
            g_wkv = [_mm2(f"l{i}_mla_wgrad_kv{t2}", lat[1], dkv_all[t2], True, False, BF) for t2 in range(2)]
            dproj, g_qn, g_kvn = _mla_mid_bwd(f"l{i}_mla_mid_bwd", proj, dqn, dkvn, dkr_heads, q_gain, kv_gain, cos, sa, sb)
            g_win = _mm2(f"l{i}_mla_wgrad_in", h, dproj, True, False, BF, tn_cands=(384, 128))
            n_in = Q_LORA + KV_LORA + QK_ROPE
            gq_nat = jnp.concatenate([g_wq[0].reshape(Q_LORA, H, HEAD_PAD),
                                      g_wq[1].reshape(Q_LORA, H, HEAD_PAD)[:, :, :QK_ROPE]], axis=2)
            gkv_nat = jnp.concatenate([g_wkv[0].reshape(KV_LORA, H, HEAD_PAD), g_wkv[1].reshape(KV_LORA, H, HEAD_PAD)], axis=2)
            token = send(f"l{i}_mla", [(k, j) for k in mla_names],
                         [g_win[:, :n_in].reshape(N_DEV, D // N_DEV, n_in),
                          jnp.transpose(gq_nat.reshape(Q_LORA, N_DEV, -1), (1, 0, 2)),
                          jnp.transpose(gkv_nat.reshape(KV_LORA, N_DEV, -1), (1, 0, 2)),
                          g_wo.reshape(N_DEV, H * V_DIM // N_DEV, D)])
            small_g.update(mla_q_norm=g_qn, mla_kv_norm=g_kvn)
            dh = _mm2(f"l{i}_mla_dh", dproj, w_in_pad, False, True, F32, tn_cands=(512, 256, 128), deps=(token,))
        d, d_bf, g_lnm[i] = _rms_bwd(f"l{i}_mix_dnorm", dh, x_mix, ln_mix[i:i + 1], d)
        d, d_bf, g_ln1[i] = _ffn_bwd(
            f"l{i}_ffn1", d, d_bf, s1, ln_ffn1[i:i + 1], gathered[("ffn1_w_in", i)], gathered[("ffn1_w_out", i)],
            lambda g_in, g_out, i=i: send(f"l{i}_ffn1", [("ffn1_w_in", i), ("ffn1_w_out", i)], [g_in, g_out]))
    grad_x = d[None]

    landed, partial = {}, {}
    for tag, keys, st in sent:
        fulls, lands = _xwait(f"scatter_wait_{tag}", st, d, False)
        for k, full, land in zip(keys, fulls, lands):
            partial[k], landed[k] = full, land
    big_out = {}
    for k in names:
        w = big[k]
        rc = (math.prod(w.shape[1:-1]), w.shape[-1])
        flat = (w.shape[0] * rc[0], rc[1])
        lands = [landed[(k, l)].reshape((N_DEV - 1,) + rc) for l in range(w.shape[0])]
        fulls = [partial[(k, l)].reshape((N_DEV,) + rc) for l in range(w.shape[0])]
        res = _adamw_sharded(f"adamw_{k}", lands, fulls, me_arr, w.reshape(flat), big_m[k].reshape(flat),
                             big_v[k].reshape(flat))
        big_out[k] = [r.reshape(w.shape) for r in res]

    small_g.update(ln_ffn1=jnp.concatenate(g_ln1), ln_mix=jnp.concatenate(g_lnm), ln_ffn2=jnp.concatenate(g_ln2),
                   ln_final=g_ln_final[0])
    small_names = ["ln_ffn1", "ln_mix", "ln_ffn2", "sgu_v_gain", "sgu_v_bias", "sgu_w_spatial", "sgu_b_spatial",
                   "ln_final", "mla_q_norm", "mla_kv_norm"]
    summed = _unpack(_all_reduce_small("small_grads_all_reduce", _pack([small_g[k] for k in small_names])),
                     [small_g[k] for k in small_names])
    small_grad = dict(zip(small_names, summed))
    for k in ("mla_q_norm", "mla_kv_norm"):
        small_grad[k] = lax.dynamic_slice(small_grad[k], (0, me * nq_sh), (1, nq_sh))
    small_w = dict(ln_ffn1=ln_ffn1, ln_mix=ln_mix, ln_ffn2=ln_ffn2, sgu_v_gain=sgu_v_gain, sgu_v_bias=sgu_v_bias,
                   sgu_w_spatial=sgu_w_spatial, sgu_b_spatial=sgu_b_spatial, ln_final=ln_final, mla_q_norm=mla_q_norm,
                   mla_kv_norm=mla_kv_norm)
    small_m = dict(ln_ffn1=m_ln_ffn1, ln_mix=m_ln_mix, ln_ffn2=m_ln_ffn2, sgu_v_gain=m_sgu_v_gain, sgu_v_bias=m_sgu_v_bias,
                   sgu_w_spatial=m_sgu_w_spatial, sgu_b_spatial=m_sgu_b_spatial, ln_final=m_ln_final,
                   mla_q_norm=m_mla_q_norm, mla_kv_norm=m_mla_kv_norm)
    small_v = dict(ln_ffn1=v_ln_ffn1, ln_mix=v_ln_mix, ln_ffn2=v_ln_ffn2, sgu_v_gain=v_sgu_v_gain, sgu_v_bias=v_sgu_v_bias,
                   sgu_w_spatial=v_sgu_w_spatial, sgu_b_spatial=v_sgu_b_spatial, ln_final=v_ln_final,
                   mla_q_norm=v_mla_q_norm, mla_kv_norm=v_mla_kv_norm)
    like = [small_w[k] for k in small_names]
    packed = _adamw_packed("adamw_small", _pack(like), _pack([small_grad[k] for k in small_names]),
                           _pack([small_m[k] for k in small_names]), _pack([small_v[k] for k in small_names]))
    small_out = {}
    unpacked = [_unpack(p, like) for p in packed]
    for idx, k in enumerate(small_names):
        small_out[k] = [small_grad[k].reshape(small_w[k].shape)] + [u[idx] for u in unpacked]

    order = ["ln_ffn1", "ffn1_w_in", "ffn1_w_out", "ln_mix", "ln_ffn2", "ffn2_w_in", "ffn2_w_out", "sgu_w_in",
             "sgu_v_gain", "sgu_v_bias", "sgu_w_spatial", "sgu_b_spatial", "sgu_w_out", "mla_w_in", "mla_q_norm",
             "mla_w_q_up", "mla_kv_norm", "mla_w_kv_up", "mla_w_out", "ln_final"]
    res = {k: (big_out[k] if k in big_out else small_out[k]) for k in order}
    outs = [loss, grad_x]
    for part in range(4):
        outs.extend(res[k][part] for k in order)
    return tuple(outs)
```

```python
import math

import jax
import jax.numpy as jnp
from jax import lax
from jax.experimental import pallas as pl
from jax.experimental.pallas import tpu as pltpu

F32 = jnp.float32
BF = jnp.bfloat16
MESH = pl.DeviceIdType.MESH

N_DEV = 8
EPS = 1e-6
CHUNK = 64
SGU_BLOCK = 128
SGU_GROUPS = 8
Q_LORA = 512
KV_LORA = 512
QK_NOPE = 128
QK_ROPE = 64
V_DIM = 128
ROPE_THETA = 10000.0
HEAD_PAD = 128
LANES = 128
ADAM_LR = 0.001
ADAM_B1 = 0.9
ADAM_B2 = 0.999
ADAM_EPS = 1e-08
ADAM_WD = 0.01
ADAM_STEP = 10
V7X_VMEM_BYTES = 64 * 1024 * 1024
VMEM_CAP = V7X_VMEM_BYTES - 6 * 1024 * 1024
VMEM_FLOOR = 32 * 1024 * 1024
NEG = -1e30
ATTN_GROUPS = 1
ATTN_UNROLL = 4


def _pick(n, cands):
    for c in cands:
        if n % c == 0:
            return c
    return n


def _nbytes(shape, dtype):
    return math.prod(int(s) for s in shape if s is not None) * jnp.dtype(dtype).itemsize


def _params(sem, block_bytes):
    limit = int(min(VMEM_CAP, max(VMEM_FLOOR, block_bytes)))
    return pltpu.CompilerParams(dimension_semantics=sem, vmem_limit_bytes=limit)


def _spec(shape, fn):
    return pl.BlockSpec(shape, fn)


_ANY = pl.BlockSpec(memory_space=pl.ANY)


def _mm(name, grid, ops, pairs, extras, outs, epilogue, acc_shapes, deps=()):
    nk = grid[2]
    n_ops, n_ex, n_out = len(ops), len(extras), len(outs)

    def load(refs, idx):
        loader = ops[idx][2] if len(ops[idx]) > 2 else None
        return (refs[idx][...] if loader is None else loader(refs[idx])).astype(BF)

    def prod(refs, p):
        ia, ib, ta, tb, _ = p
        a = load(refs, ia)
        b = load(refs, ib)
        dims = (((0 if ta else 1,), (1 if tb else 0,)), ((), ()))
        return lax.dot_general(a, b, dims, preferred_element_type=F32)

    def body(*refs):
        op_refs = refs[:n_ops]
        ex_refs = refs[n_ops:n_ops + n_ex]
        n_in = n_ops + n_ex + len(deps)
        out_refs = refs[n_in:n_in + n_out]
        acc_refs = refs[n_in + n_out:]
        ids = (pl.program_id(0), pl.program_id(1))

        def finish(vals):
            epilogue(vals, [e[...] for e in ex_refs], out_refs, ids)

        if nk == 1:
            vals = [None] * len(acc_shapes)
            for p in pairs:
                r = prod(op_refs, p)
                vals[p[4]] = r if vals[p[4]] is None else vals[p[4]] + r
            finish(vals)
        else:
            k = pl.program_id(2)

            @pl.when(k == 0)
            def _():
                for a in acc_refs:
                    a[...] = jnp.zeros_like(a)

            for p in pairs:
                acc_refs[p[4]][...] += prod(op_refs, p)

            @pl.when(k == nk - 1)
            def _():
                finish([a[...] for a in acc_refs])

    in_arrays = [o[0] for o in ops] + [e[0] for e in extras]
    in_specs = [o[1] for o in ops] + [e[1] for e in extras]
    in_arrays += list(deps)
    in_specs += [_ANY] * len(deps)
    blk = 0
    for entry in ops + extras:
        blk += 2 * _nbytes(entry[1].block_shape, entry[0].dtype)
    for sd, sp in outs:
        blk += 2 * _nbytes(sp.block_shape, sd.dtype)
    acc_b = sum(_nbytes(s, F32) for s in acc_shapes)
    blk += 6 * acc_b
    scratch = [pltpu.VMEM(s, F32) for s in acc_shapes] if nk > 1 else []
    res = pl.pallas_call(
        body, name=name, grid=grid, in_specs=in_specs,
        out_specs=[o[1] for o in outs], out_shape=[o[0] for o in outs],
        scratch_shapes=scratch,
        compiler_params=_params(("parallel", "parallel", "arbitrary"), blk))(*in_arrays)
    return res


def _store(scale=None):
    def epi(accs, ex, outs, ids):
        v = accs[0]
        if scale is not None:
            v = v * scale
        outs[0][...] = v.astype(outs[0].dtype)
    return epi


def _store_residual(scale):
    def epi(accs, ex, outs, ids):
        outs[0][...] = ex[0] + scale * accs[0]
    return epi


def _sd(shape, dtype):
    return jax.ShapeDtypeStruct(tuple(shape), dtype)


def _pair_in(name, h, w_sm, out_dtype, act):
    S, D = h.shape
    c = w_sm.shape[-1]
    tm = _pick(S, (256, 128))
    half = N_DEV // 2
    ops = [(h, _spec((tm, D), lambda j, i, k: (i, 0))),
           (w_sm, _spec((None, D, c), lambda j, i, k: (j, 0, 0))),
           (w_sm, _spec((None, D, c), lambda j, i, k: (j + half, 0, 0)))]
    outs = [(_sd((2, S, half * c), out_dtype), _spec((2, tm, c), lambda j, i, k: (0, i, j)))]
    if act is not None:
        outs.append((_sd((S, half * c), BF), _spec((tm, c), lambda j, i, k: (i, j))))

    def epi(accs, ex, orefs, ids):
        orefs[0][0] = accs[0].astype(out_dtype)
        orefs[0][1] = accs[1].astype(out_dtype)
        if act is not None:
            orefs[1][...] = act(accs[0], accs[1]).astype(BF)

    return _mm(name, (half, S // tm, 1), ops, [(0, 1, False, False, 0), (0, 2, False, False, 1)], [], outs, epi,
               [(tm, c), (tm, c)])


def _two_slabs(ref):
    return jnp.concatenate([ref[0], ref[1]], axis=0)


def _rows_out(name, a, w_sm, res, scale):
    S = a.shape[0]
    r, D = w_sm.shape[-2], w_sm.shape[-1]
    tm = _pick(S, (1024, 512, 256, 128))
    tn = _pick(D, (1024, 512, 256, 128))
    ops = [(a, _spec((tm, 2 * r), lambda i, j, k: (i, k))),
           (w_sm, _spec((2, r, tn), lambda i, j, k: (k, 0, j)), _two_slabs)]
    extras = [(res, _spec((tm, tn), lambda i, j, k: (i, j)))]
    outs = [(_sd((S, D), F32), _spec((tm, tn), lambda i, j, k: (i, j)))]
    return _mm(name, (S // tm, D // tn, N_DEV // 2), ops, [(0, 1, False, False, 0)], extras, outs,
               _store_residual(scale), [(tm, tn)])[0]


def _rows_dact(name, d_bf, w_sm, extras_arrays, out_shapes, epi):
    S, D = d_bf.shape
    r = w_sm.shape[-2]
    tm = _pick(S, (1024, 512, 256, 128))
    ops = [(d_bf, _spec((tm, D), lambda j, i, k: (i, 0))),
           (w_sm, _spec((2, r, D), lambda j, i, k: (j, 0, 0)), _two_slabs)]
    extras = []
    for arr in extras_arrays:
        if arr.ndim == 3:
            extras.append((arr, _spec((arr.shape[0], tm, 2 * r), lambda j, i, k: (0, i, j))))
        else:
            extras.append((arr, _spec((tm, 2 * r), lambda j, i, k: (i, j))))
    outs = []
    for sd in out_shapes:
        if len(sd.shape) == 3:
            outs.append((sd, _spec((sd.shape[0], tm, 2 * r), lambda j, i, k: (0, i, j))))
        else:
            outs.append((sd, _spec((tm, 2 * r), lambda j, i, k: (i, j))))
    return _mm(name, (N_DEV // 2, S // tm, 1), ops, [(0, 1, False, True, 0)], extras, outs, epi, [(tm, 2 * r)])


def _rows_wgrad(name, a, d_bf, scale):
    S, D = d_bf.shape
    r = a.shape[1] // N_DEV
    tn = _pick(D, (2048, 1024, 512, 256, 128))
    tk = _pick(S, (1024, 512, 256, 128))
    ops = [(a, _spec((tk, 2 * r), lambda s, j, k: (k, s))),
           (d_bf, _spec((tk, tn), lambda s, j, k: (k, j)))]
    outs = [(_sd((N_DEV, r, D), BF), _spec((2, r, tn), lambda s, j, k: (s, 0, j)))]

    def epi(accs, ex, orefs, ids):
        v = accs[0] if scale is None else accs[0] * scale
        orefs[0][0] = v[:r].astype(BF)
        orefs[0][1] = v[r:].astype(BF)

    return _mm(name, (N_DEV // 2, D // tn, S // tk), ops, [(0, 1, True, False, 0)], [], outs, epi, [(2 * r, tn)])[0]


def _cols_dh(name, dpair, w_sm, deps=()):
    _, S, _ = dpair.shape
    D, c = w_sm.shape[-2], w_sm.shape[-1]
    half = N_DEV // 2
    tm = _pick(S, (1024, 512, 256, 128))
    tn = _pick(D, (1024, 512, 256, 128))
    ops = [(dpair, _spec((None, tm, c), lambda i, j, k: (k // half, i, k % half))),
           (w_sm, _spec((None, tn, c), lambda i, j, k: (k, j, 0)))]
    outs = [(_sd((S, D), F32), _spec((tm, tn), lambda i, j, k: (i, j)))]
    return _mm(name, (S // tm, D // tn, N_DEV), ops, [(0, 1, False, True, 0)], [], outs, _store(), [(tm, tn)],
               deps=deps)[0]


def _cols_wgrad(name, h, dpair):
    S, D = h.shape
    half = N_DEV // 2
    c = dpair.shape[2] // half
    tm = _pick(D, (1024, 512, 256, 128))
    tk = _pick(S, (1024, 512, 256, 128))
    ops = [(h, _spec((tk, tm), lambda s, i, k: (k, i))),
           (dpair, _spec((None, tk, c), lambda s, i, k: (s // half, k, s % half)))]
    outs = [(_sd((N_DEV, D, c), BF), _spec((None, tm, c), lambda s, i, k: (s, i, 0)))]
    return _mm(name, (N_DEV, D // tm, S // tk), ops, [(0, 1, True, False, 0)], [], outs, _store(), [(tm, c)])[0]


def _mm2(name, a, b, ta, tb, out_dtype, epi=None, extras=(), res=None, tn_cands=(512, 384, 256, 128), deps=()):
    M = a.shape[1] if ta else a.shape[0]
    K = a.shape[0] if ta else a.shape[1]
    N = b.shape[0] if tb else b.shape[1]
    tm = _pick(M, (1024, 512, 256, 128))
    tn = _pick(N, tn_cands)
    tk = _pick(K, (2048, 1152, 1024, 512, 256, 128))
    a_spec = _spec((tk, tm), lambda i, j, k: (k, i)) if ta else _spec((tm, tk), lambda i, j, k: (i, k))
    b_spec = _spec((tn, tk), lambda i, j, k: (j, k)) if tb else _spec((tk, tn), lambda i, j, k: (k, j))
    ex = [(e, _spec((tm, e.shape[1]), lambda i, j, k: (i, 0))) for e in extras]
    if res is not None:
        ex = [(res, _spec((tm, tn), lambda i, j, k: (i, j)))]
        epi = _store_residual(1.0)
    outs = [(_sd((M, N), out_dtype), _spec((tm, tn), lambda i, j, k: (i, j)))]
    return _mm(name, (M // tm, N // tn, K // tk), [(a, a_spec), (b, b_spec)], [(0, 1, ta, tb, 0)], ex, outs,
               epi or _store(), [(tm, tn)], deps=deps)[0]


def _rms_fwd(name, x, g, deps=()):
    S, D = x.shape
    ts = _pick(S, (512, 256, 128))

    def body(x_ref, g_ref, *rest):
        h_ref = rest[-1]
        xv = x_ref[...]
        r = lax.rsqrt(jnp.mean(xv * xv, axis=-1, keepdims=True) + EPS)
        h_ref[...] = (xv * r * g_ref[...]).astype(BF)

    return pl.pallas_call(
        body, name=name, grid=(S // ts,),
        in_specs=[_spec((ts, D), lambda i: (i, 0)), _spec((1, D), lambda i: (0, 0))] + [_ANY] * len(deps),
        out_specs=_spec((ts, D), lambda i: (i, 0)), out_shape=_sd((S, D), BF),
        compiler_params=_params(("parallel",), 12 * ts * D * 4))(x, g, *deps)


def _rms_bwd(name, dh, x, g, dres, deps=()):
    S, D = x.shape
    ts = _pick(S, (256, 128))

    def body(dh_ref, x_ref, g_ref, dres_ref, *rest):
        dx_ref, dxb_ref, dg_ref = rest[len(deps):]
        xv = x_ref[...]
        dhv = dh_ref[...]
        r = lax.rsqrt(jnp.mean(xv * xv, axis=-1, keepdims=True) + EPS)
        xhat = xv * r
        dxh = dhv * g_ref[...]
        cm = jnp.mean(dxh * xhat, axis=-1, keepdims=True)
        dx = r * (dxh - xhat * cm) + dres_ref[...]
        dx_ref[...] = dx
        dxb_ref[...] = dx.astype(BF)

        @pl.when(pl.program_id(0) == 0)
        def _():
            dg_ref[...] = jnp.zeros_like(dg_ref)

        dg_ref[...] += jnp.sum(dhv * xhat, axis=0, keepdims=True)

    row = _spec((ts, D), lambda i: (i, 0))
    vec = _spec((1, D), lambda i: (0, 0))
    return pl.pallas_call(
        body, name=name, grid=(S // ts,),
        in_specs=[row, row, vec, row] + [_ANY] * len(deps), out_specs=[row, row, vec],
        out_shape=[_sd((S, D), F32), _sd((S, D), BF), _sd((1, D), F32)],
        compiler_params=_params(("arbitrary",), 20 * ts * D * 4))(dh, x, g, dres, *deps)


def _final_loss(name, x, g, target):
    S, D = x.shape
    ts = _pick(S, (256, 128))

    def body(x_ref, g_ref, t_ref, loss_ref, dx_ref, dxb_ref, dg_ref):
        xv = x_ref[...]
        gv = g_ref[...]
        r = lax.rsqrt(jnp.mean(xv * xv, axis=-1, keepdims=True) + EPS)
        xhat = xv * r
        err = xhat * gv - t_ref[...]
        part = 0.5 * jnp.sum(jnp.mean(err * err, axis=-1, keepdims=True), axis=0, keepdims=True)
        dy = err * (1.0 / D)
        dxh = dy * gv
        cm = jnp.mean(dxh * xhat, axis=-1, keepdims=True)
        dx = r * (dxh - xhat * cm)
        dx_ref[...] = dx
        dxb_ref[...] = dx.astype(BF)

        @pl.when(pl.program_id(0) == 0)
        def _():
            dg_ref[...] = jnp.zeros_like(dg_ref)
            loss_ref[...] = jnp.zeros_like(loss_ref)

        dg_ref[...] += jnp.sum(dy * xhat, axis=0, keepdims=True)
        loss_ref[...] += jnp.broadcast_to(part, loss_ref.shape)

    row = _spec((ts, D), lambda i: (i, 0))
    vec = _spec((1, D), lambda i: (0, 0))
    return pl.pallas_call(
        body, name=name, grid=(S // ts,),
        in_specs=[row, vec, row], out_specs=[_spec((1, LANES), lambda i: (0, 0)), row, row, vec],
        out_shape=[_sd((1, LANES), F32), _sd((S, D), F32), _sd((S, D), BF), _sd((1, D), F32)],
        compiler_params=_params(("arbitrary",), 20 * ts * D * 4))(x, g, target)


def _swiglu(gate, up):
    return gate * jax.nn.sigmoid(gate) * up


def _swiglu_bwd_epi(accs, ex, orefs, ids):
    da = 0.5 * accs[0]
    gate = ex[0][0].astype(F32)
    up = ex[0][1].astype(F32)
    sg = jax.nn.sigmoid(gate)
    orefs[0][0] = (da * up * (sg * (1.0 + gate * (1.0 - sg)))).astype(BF)
    orefs[0][1] = (da * gate * sg).astype(BF)


_GELU_C = math.sqrt(2.0 / math.pi)


def _gelu(x):
    return x * (0.5 * (1.0 + jnp.tanh(_GELU_C * (x + 0.044715 * (x * x * x)))))


def _gelu_grad(x):
    t = jnp.tanh(_GELU_C * (x + 0.044715 * (x * x * x)))
    return 0.5 * (1.0 + t) + x * (0.5 * (1.0 - t * t) * _GELU_C * (1.0 + 3.0 * 0.044715 * (x * x)))


def _causal_block_mask():
    row = lax.broadcasted_iota(jnp.int32, (SGU_BLOCK, SGU_BLOCK), 0) // CHUNK
    col = lax.broadcasted_iota(jnp.int32, (SGU_BLOCK, SGU_BLOCK), 1) // CHUNK
    return row >= col


def _sgu_mid_fwd(name, puv, gain, bias, w_sp, b_sp):
    _, S, W = puv.shape
    G = SGU_GROUPS
    C = W // G
    T = SGU_BLOCK

    def body(puv_ref, gain_ref, bias_ref, w_ref, b_ref, out_ref):
        mask = _causal_block_mask()
        v = _gelu(puv_ref[1])
        mu = jnp.mean(v, axis=-1, keepdims=True)
        vc = v - mu
        rs = lax.rsqrt(jnp.mean(vc * vc, axis=-1, keepdims=True) + EPS)
        vln = (vc * rs * gain_ref[...] + bias_ref[...]).astype(BF)
        for g in range(G):
            wg = jnp.where(mask, w_ref[g], 0.0).astype(BF)
            mixed = jnp.dot(wg, vln[:, g * C:(g + 1) * C], preferred_element_type=F32) + b_ref[g]
            out_ref[:, g * C:(g + 1) * C] = (_gelu(puv_ref[0, :, g * C:(g + 1) * C]) * mixed).astype(BF)

    return pl.pallas_call(
        body, name=name, grid=(S // T,),
        in_specs=[_spec((2, T, W), lambda i: (0, i, 0)), _spec((1, W), lambda i: (0, 0)), _spec((1, W), lambda i: (0, 0)),
                  _spec((G, T, T), lambda i: (0, 0, 0)), _spec((G, T, 1), lambda i: (0, 0, 0))],
        out_specs=_spec((T, W), lambda i: (i, 0)), out_shape=_sd((S, W), BF),
        compiler_params=_params(("parallel",), 16 * T * W * 4))(puv, gain, bias, w_sp, b_sp)


def _sgu_mid_bwd(name, puv, dgated, gain, bias, w_sp, b_sp):
    _, S, W = puv.shape
    G = SGU_GROUPS
    C = W // G
    T = SGU_BLOCK

    def body(puv_ref, dg_ref, gain_ref, bias_ref, w_ref, b_ref, dpuv_ref, dgain_ref, dbias_ref, dw_ref, db_ref, dvln_ref):
        @pl.when(pl.program_id(0) == 0)
        def _():
            dgain_ref[...] = jnp.zeros_like(dgain_ref)
            dbias_ref[...] = jnp.zeros_like(dbias_ref)
            dw_ref[...] = jnp.zeros_like(dw_ref)
            db_ref[...] = jnp.zeros_like(db_ref)

        mask = _causal_block_mask()
        pv = puv_ref[1]
        v = _gelu(pv)
        mu = jnp.mean(v, axis=-1, keepdims=True)
        vc = v - mu
        rs = lax.rsqrt(jnp.mean(vc * vc, axis=-1, keepdims=True) + EPS)
        vhat = vc * rs
        gain_v = gain_ref[...]
        vln = (vhat * gain_v + bias_ref[...]).astype(BF)
        for g in range(G):
            sl = slice(g * C, (g + 1) * C)
            wg = jnp.where(mask, w_ref[g], 0.0).astype(BF)
            vg = vln[:, sl]
            mixed = jnp.dot(wg, vg, preferred_element_type=F32) + b_ref[g]
            pu = puv_ref[0, :, sl]
            dgt = dg_ref[:, sl].astype(F32)
            dpuv_ref[0, :, sl] = (dgt * mixed * _gelu_grad(pu)).astype(BF)
            dmix = dgt * _gelu(pu)
            db_ref[g] += jnp.sum(dmix, axis=-1, keepdims=True)
            dmb = dmix.astype(BF)
            dwg = lax.dot_general(dmb, vg, (((1,), (1,)), ((), ())), preferred_element_type=F32)
            dw_ref[g] += jnp.where(mask, dwg, 0.0)
            dvln_ref[:, sl] = lax.dot_general(wg, dmb, (((0,), (0,)), ((), ())), preferred_element_type=F32)
        dvln = dvln_ref[...]
        dgain_ref[...] += jnp.sum(dvln * vhat, axis=0, keepdims=True)
        dbias_ref[...] += jnp.sum(dvln, axis=0, keepdims=True)
        dvh = dvln * gain_v
        m1 = jnp.mean(dvh, axis=-1, keepdims=True)
        m2 = jnp.mean(dvh * vhat, axis=-1, keepdims=True)
        dv = rs * (dvh - m1 - vhat * m2)
        dpuv_ref[1] = (dv * _gelu_grad(pv)).astype(BF)

    vec = _spec((1, W), lambda i: (0, 0))
    wsp = _spec((G, T, T), lambda i: (0, 0, 0))
    bsp = _spec((G, T, 1), lambda i: (0, 0, 0))
    return pl.pallas_call(
        body, name=name, grid=(S // T,),
        in_specs=[_spec((2, T, W), lambda i: (0, i, 0)), _spec((T, W), lambda i: (i, 0)), vec, vec, wsp, bsp],
        out_specs=[_spec((2, T, W), lambda i: (0, i, 0)), vec, vec, wsp, bsp],
        out_shape=[_sd((2, S, W), BF), _sd((1, W), F32), _sd((1, W), F32), _sd((G, T, T), F32), _sd((G, T, 1), F32)],
        scratch_shapes=[pltpu.VMEM((T, W), F32)],
        compiler_params=_params(("arbitrary",), 24 * T * W * 4))(puv, dgated, gain, bias, w_sp, b_sp)


def _rope_tables(positions):
    half = QK_ROPE // 2
    inv_freq = 1.0 / (ROPE_THETA ** (jnp.arange(half, dtype=F32) / half))
    ang = positions.astype(F32)[:, None] * inv_freq[None, :]
    cos, sin = jnp.cos(ang), jnp.sin(ang)
    z = jnp.zeros_like(cos)
    return (jnp.concatenate([cos, cos, z, z], axis=1), jnp.concatenate([-sin, z, z, z], axis=1),
            jnp.concatenate([z, sin, z, z], axis=1))


def _rope(x, cos, sa, sb):
    return x * cos + pltpu.roll(x, HEAD_PAD - QK_ROPE // 2, 1) * sa + pltpu.roll(x, QK_ROPE // 2, 1) * sb


def _rope_t(dy, cos, sa, sb):
    return dy * cos + pltpu.roll(dy * sa, QK_ROPE // 2, 1) + pltpu.roll(dy * sb, HEAD_PAD - QK_ROPE // 2, 1)


def _rms_rows(x, g):
    r = lax.rsqrt(jnp.mean(x * x, axis=-1, keepdims=True) + EPS)
    return x * r * g


def _rms_rows_bwd(dy, x, g):
    r = lax.rsqrt(jnp.mean(x * x, axis=-1, keepdims=True) + EPS)
    xhat = x * r
    dxh = dy * g
    cm = jnp.mean(dxh * xhat, axis=-1, keepdims=True)
    return r * (dxh - xhat * cm), jnp.sum(dy * xhat, axis=0, keepdims=True)


def _mla_mid_fwd(name, proj, qg, kvg, cos, sa, sb):
    S, P = proj.shape
    ts = _pick(S, (512, 256, 128))

    def body(p_ref, qg_ref, kvg_ref, cos_ref, sa_ref, sb_ref, lat_ref, kr_ref):
        lat_ref[0] = _rms_rows(p_ref[:, :Q_LORA], qg_ref[...]).astype(BF)
        lat_ref[1] = _rms_rows(p_ref[:, Q_LORA:Q_LORA + KV_LORA], kvg_ref[...]).astype(BF)
        kr_ref[...] = _rope(p_ref[:, Q_LORA + KV_LORA:], cos_ref[...], sa_ref[...], sb_ref[...]).astype(BF)

    tab = _spec((ts, HEAD_PAD), lambda i: (i, 0))
    return pl.pallas_call(
        body, name=name, grid=(S // ts,),
        in_specs=[_spec((ts, P), lambda i: (i, 0)), _spec((1, Q_LORA), lambda i: (0, 0)),
                  _spec((1, KV_LORA), lambda i: (0, 0)), tab, tab, tab],
        out_specs=[_spec((2, ts, Q_LORA), lambda i: (0, i, 0)), tab],
        out_shape=[_sd((2, S, Q_LORA), BF), _sd((S, HEAD_PAD), BF)],
        compiler_params=_params(("parallel",), 16 * ts * P * 4))(proj, qg, kvg, cos, sa, sb)


def _mla_mid_bwd(name, proj, dqn, dkvn, dkr_heads, qg, kvg, cos, sa, sb):
    S, P = proj.shape
    H = dkr_heads.shape[0]
    ts = _pick(S, (256, 128))

    def body(p_ref, dqn_ref, dkvn_ref, dkr_ref, qg_ref, kvg_ref, cos_ref, sa_ref, sb_ref, dp_ref, dqg_ref, dkvg_ref):
        @pl.when(pl.program_id(0) == 0)
        def _():
            dqg_ref[...] = jnp.zeros_like(dqg_ref)
            dkvg_ref[...] = jnp.zeros_like(dkvg_ref)

        dq, dqg = _rms_rows_bwd(dqn_ref[...], p_ref[:, :Q_LORA], qg_ref[...])
        dkv, dkvg = _rms_rows_bwd(dkvn_ref[...], p_ref[:, Q_LORA:Q_LORA + KV_LORA], kvg_ref[...])
        dqg_ref[...] += dqg
        dkvg_ref[...] += dkvg
        dkr = dkr_ref[0]
        for h in range(1, H):
            dkr = dkr + dkr_ref[h]
        dp_ref[:, :Q_LORA] = dq.astype(BF)
        dp_ref[:, Q_LORA:Q_LORA + KV_LORA] = dkv.astype(BF)
        dp_ref[:, Q_LORA + KV_LORA:] = _rope_t(dkr, cos_ref[...], sa_ref[...], sb_ref[...]).astype(BF)

    tab = _spec((ts, HEAD_PAD), lambda i: (i, 0))
    lat = _spec((ts, Q_LORA), lambda i: (i, 0))
    gq = _spec((1, Q_LORA), lambda i: (0, 0))
    return pl.pallas_call(
        body, name=name, grid=(S // ts,),
        in_specs=[_spec((ts, P), lambda i: (i, 0)), lat, lat, _spec((H, ts, HEAD_PAD), lambda i: (0, i, 0)),
                  gq, gq, tab, tab, tab],
        out_specs=[_spec((ts, P), lambda i: (i, 0)), gq, gq],
        out_shape=[_sd((S, P), BF), _sd((1, Q_LORA), F32), _sd((1, KV_LORA), F32)],
        compiler_params=_params(("arbitrary",), 24 * ts * P * 4))(proj, dqn, dkvn, dkr_heads, qg, kvg, cos, sa, sb)


def _attn_tile(S):
    return _pick(S, (512,)) if S >= 2048 else _pick(S, (128,))


def _diag_mask(t, transposed):
    q = lax.broadcasted_iota(jnp.int32, (t, t), 1 if transposed else 0) // CHUNK
    k = lax.broadcasted_iota(jnp.int32, (t, t), 0 if transposed else 1) // CHUNK
    return k <= q


_NT = (((1,), (1,)), ((), ()))


def _attn_fwd(name, q_all, kv_all, kr):
    _, S, HP = q_all.shape
    H = HP // HEAD_PAD
    t = _attn_tile(S)
    nq = S // t
    ng = ATTN_GROUPS
    tg = t // ng

    def body(q_ref, kv_ref, kr_ref, o_ref, lse_ref, kcat_ref):
        i = pl.program_id(1)

        @pl.when(i == 0)
        def _():
            kcat_ref[:, :HEAD_PAD] = kv_ref[0]
            kcat_ref[:, HEAD_PAD:] = kr_ref[...]

        qs = [jnp.concatenate([q_ref[0, g * tg:(g + 1) * tg], q_ref[1, g * tg:(g + 1) * tg]], axis=1) for g in range(ng)]

        def step(j, carry, masked):
            off = pl.multiple_of(j * t, t)
            kj = kcat_ref[pl.ds(off, t), :]
            vj = kv_ref[1, pl.ds(off, t), :]
            out = []
            for g in range(ng):
                m, l, acc = carry[g]
                s = lax.dot_general(qs[g], kj, _NT, preferred_element_type=F32)
                if masked:
                    s = jnp.where(_diag_mask(t, False)[g * tg:(g + 1) * tg], s, NEG)
                m2 = jnp.maximum(m, jnp.max(s, axis=-1, keepdims=True))
                al = jnp.exp(m - m2)
                p = jnp.exp(s - m2)
                l2 = al * l + jnp.sum(p, axis=-1, keepdims=True)
                acc2 = al * acc + jnp.dot(p.astype(BF), vj, preferred_element_type=F32)
                out.append((m2, l2, acc2))
            return tuple(out)

        init = tuple((jnp.full((tg, 1), NEG, F32), jnp.zeros((tg, 1), F32), jnp.zeros((tg, V_DIM), F32))
                     for _ in range(ng))
        def several(jj, c):
            for u in range(ATTN_UNROLL):
                c = step(jj * ATTN_UNROLL + u, c, False)
            return c

        carry = lax.fori_loop(0, i // ATTN_UNROLL, several, init)
        carry = lax.fori_loop((i // ATTN_UNROLL) * ATTN_UNROLL, i, lambda j, c: step(j, c, False), carry)
        carry = step(i, carry, True)
        for g in range(ng):
            m, l, acc = carry[g]
            o_ref[g * tg:(g + 1) * tg, :] = acc / l
            lse_ref[g * tg:(g + 1) * tg, :] = jnp.broadcast_to(m + jnp.log(l), (tg, LANES))

    return pl.pallas_call(
        body, name=name, grid=(H, nq),
        in_specs=[_spec((2, t, HEAD_PAD), lambda h, i: (0, i, h)), _spec((2, S, HEAD_PAD), lambda h, i: (0, 0, h)),
                  _spec((S, HEAD_PAD), lambda h, i: (0, 0))],
        out_specs=[_spec((t, HEAD_PAD), lambda h, i: (i, h)), _spec((None, t, LANES), lambda h, i: (h, i, 0))],
        out_shape=[_sd((S, HP), F32), _sd((H, S, LANES), F32)],
        scratch_shapes=[pltpu.VMEM((S, 2 * HEAD_PAD), BF)],
        compiler_params=_params(("parallel", "arbitrary"), 8 * S * HEAD_PAD * 2 + 24 * t * t * 4))(q_all, kv_all, kr)


def _attn_delta(name, do, o):
    S, HP = o.shape
    H = HP // HEAD_PAD
    ts = _pick(S, (512, 256, 128))

    def body(do_ref, o_ref, d_ref):
        d_ref[...] = jnp.broadcast_to(jnp.sum(do_ref[...] * o_ref[...], axis=-1, keepdims=True), (ts, LANES))

    tile = _spec((ts, HEAD_PAD), lambda h, i: (i, h))
    return pl.pallas_call(
        body, name=name, grid=(H, S // ts), in_specs=[tile, tile],
        out_specs=_spec((None, ts, LANES), lambda h, i: (h, i, 0)), out_shape=_sd((H, S, LANES), F32),
        compiler_params=_params(("parallel", "parallel"), VMEM_FLOOR))(do, o)


def _attn_dq(name, q_all, kv_all, kr, do, lse, delta, cos, sa, sb, scale):
    _, S, HP = q_all.shape
    H = HP // HEAD_PAD
    t = _attn_tile(S)
    nq = S // t

    def body(q_ref, kv_ref, kr_ref, do_ref, lse_ref, dl_ref, cos_ref, sa_ref, sb_ref, dq_ref, kcat_ref):
        i = pl.program_id(1)

        @pl.when(i == 0)
        def _():
            kcat_ref[:, :HEAD_PAD] = kv_ref[0]
            kcat_ref[:, HEAD_PAD:] = kr_ref[...]

        ng = ATTN_GROUPS
        tg = t // ng
        rows = [slice(g * tg, (g + 1) * tg) for g in range(ng)]
        qs = [jnp.concatenate([q_ref[0, r], q_ref[1, r]], axis=1) for r in rows]
        dobs = [do_ref[r, :].astype(BF) for r in rows]
        lses = [lse_ref[r, 0:1] for r in rows]
        dls = [dl_ref[r, 0:1] for r in rows]

        def step(j, dqs, masked):
            off = pl.multiple_of(j * t, t)
            kj = kcat_ref[pl.ds(off, t), :]
            vj = kv_ref[1, pl.ds(off, t), :]
            out = []
            for g in range(ng):
                s = lax.dot_general(qs[g], kj, _NT, preferred_element_type=F32)
                if masked:
                    s = jnp.where(_diag_mask(t, False)[rows[g]], s, NEG)
                p = jnp.exp(s - lses[g])
                dp = lax.dot_general(dobs[g], vj, _NT, preferred_element_type=F32)
                ds = (p * (dp - dls[g])).astype(BF)
                out.append(dqs[g] + jnp.dot(ds, kj, preferred_element_type=F32))
            return tuple(out)

        init = tuple(jnp.zeros((tg, 2 * HEAD_PAD), F32) for _ in range(ng))
        def several(jj, c):
            for u in range(ATTN_UNROLL):
                c = step(jj * ATTN_UNROLL + u, c, False)
            return c

        dqs = lax.fori_loop(0, i // ATTN_UNROLL, several, init)
        dqs = lax.fori_loop((i // ATTN_UNROLL) * ATTN_UNROLL, i, lambda j, c: step(j, c, False), dqs)
        dqs = step(i, dqs, True)
        for g in range(ng):
            dq_ref[0, rows[g]] = (dqs[g][:, :HEAD_PAD] * scale).astype(BF)
            dq_ref[1, rows[g]] = (_rope_t(dqs[g][:, HEAD_PAD:], cos_ref[rows[g], :], sa_ref[rows[g], :],
                                          sb_ref[rows[g], :]) * scale).astype(BF)

    tab = _spec((t, HEAD_PAD), lambda h, i: (i, 0))
    stat = _spec((None, t, LANES), lambda h, i: (h, i, 0))
    return pl.pallas_call(
        body, name=name, grid=(H, nq),
        in_specs=[_spec((2, t, HEAD_PAD), lambda h, i: (0, i, h)), _spec((2, S, HEAD_PAD), lambda h, i: (0, 0, h)),
                  _spec((S, HEAD_PAD), lambda h, i: (0, 0)), _spec((t, HEAD_PAD), lambda h, i: (i, h)), stat, stat,
                  tab, tab, tab],
        out_specs=_spec((2, t, HEAD_PAD), lambda h, i: (0, i, h)), out_shape=_sd((2, S, HP), BF),
        scratch_shapes=[pltpu.VMEM((S, 2 * HEAD_PAD), BF)],
        compiler_params=_params(("parallel", "arbitrary"), 8 * S * HEAD_PAD * 2 + 32 * t * t * 4))(
            q_all, kv_all, kr, do, lse, delta, cos, sa, sb)


def _attn_dkv(name, q_all, kv_all, kr, do, lse_row, delta_row):
    _, S, HP = q_all.shape
    H = HP // HEAD_PAD
    t = _attn_tile(S)
    nq = S // t

    def body(q_ref, kv_ref, kr_ref, do_ref, lse_ref, dl_ref, dkv_ref, dkr_ref, qcat_ref):
        j = pl.program_id(1)

        @pl.when(j == 0)
        def _():
            qcat_ref[:, :HEAD_PAD] = q_ref[0]
            qcat_ref[:, HEAD_PAD:] = q_ref[1]

        ng = ATTN_GROUPS
        tg = t // ng
        rows = [slice(g * tg, (g + 1) * tg) for g in range(ng)]
        kjs = [jnp.concatenate([kv_ref[0, r], kr_ref[r, :]], axis=1) for r in rows]
        vjs = [kv_ref[1, r] for r in rows]

        def step(i, carry, masked):
            off = pl.multiple_of(i * t, t)
            qi = qcat_ref[pl.ds(off, t), :]
            doi = do_ref[pl.ds(off, t), :].astype(BF)
            lse_i = lse_ref[i]
            dl_i = dl_ref[i]
            out = []
            for g in range(ng):
                dk, dv = carry[g]
                st = lax.dot_general(kjs[g], qi, _NT, preferred_element_type=F32)
                if masked:
                    st = jnp.where(_diag_mask(t, True)[rows[g]], st, NEG)
                pt = jnp.exp(st - lse_i)
                dv2 = dv + jnp.dot(pt.astype(BF), doi, preferred_element_type=F32)
                dpt = lax.dot_general(vjs[g], doi, _NT, preferred_element_type=F32)
                dst = (pt * (dpt - dl_i)).astype(BF)
                out.append((dk + jnp.dot(dst, qi, preferred_element_type=F32), dv2))
            return tuple(out)

        init = tuple((jnp.zeros((tg, 2 * HEAD_PAD), F32), jnp.zeros((tg, V_DIM), F32)) for _ in range(ng))
        def several(ii, c):
            for u in range(ATTN_UNROLL):
                c = step(j + 1 + ii * ATTN_UNROLL + u, c, False)
            return c

        carry = step(j, init, True)
        trips = (nq - 1 - j) // ATTN_UNROLL
        carry = lax.fori_loop(0, trips, several, carry)
        carry = lax.fori_loop(j + 1 + trips * ATTN_UNROLL, nq, lambda i, c: step(i, c, False), carry)
        for g in range(ng):
            dk, dv = carry[g]
            dkv_ref[0, rows[g]] = dk[:, :HEAD_PAD].astype(BF)
            dkv_ref[1, rows[g]] = dv.astype(BF)
            dkr_ref[rows[g], :] = dk[:, HEAD_PAD:]

    stat = _spec((None, nq, 1, t), lambda h, j: (h, 0, 0, 0))
    return pl.pallas_call(
        body, name=name, grid=(H, nq),
        in_specs=[_spec((2, S, HEAD_PAD), lambda h, j: (0, 0, h)), _spec((2, t, HEAD_PAD), lambda h, j: (0, j, h)),
                  _spec((t, HEAD_PAD), lambda h, j: (j, 0)), _spec((S, HEAD_PAD), lambda h, j: (0, h)), stat, stat],
        out_specs=[_spec((2, t, HEAD_PAD), lambda h, j: (0, j, h)), _spec((None, t, HEAD_PAD), lambda h, j: (h, j, 0))],
        out_shape=[_sd((2, S, HP), BF), _sd((H, S, HEAD_PAD), F32)],
        scratch_shapes=[pltpu.VMEM((S, 2 * HEAD_PAD), BF)],
        compiler_params=_params(("parallel", "arbitrary"), 8 * S * HEAD_PAD * 4 + 32 * t * t * 4))(
            q_all, kv_all, kr, do, lse_row, delta_row)


def _place():
    x, y, c = lax.axis_index("x"), lax.axis_index("y"), lax.axis_index("c")
    return x, y, c


def _all_gather(blocks):
    n = len(blocks)

    def body(*refs):
        ins, outs = refs[:n], refs[n:2 * n]
        send_sems, recv_sems, local_sems = refs[2 * n:]
        x, y, c = _place()
        me = 4 * x + 2 * y + c
        sibling = (x, y, 1 - c)
        chips = [(1 - x, y), (x, 1 - y), (1 - x, 1 - y)]

        def slab(a, px, py, pc):
            return outs[a].at[4 * px + 2 * py + pc]

        def copy(a, k, src, dst, to):
            return pltpu.make_async_remote_copy(src_ref=src, dst_ref=dst, send_sem=send_sems.at[a, k],
                                                recv_sem=recv_sems.at[a, k], device_id=to, device_id_type=MESH)

        local = [pltpu.make_async_copy(ins[a], outs[a].at[me], local_sems.at[a]) for a in range(n)]
        for cp in local:
            cp.start()
        sends = []
        for a in range(n):
            mine = slab(a, x, y, c)
            sends.append(copy(a, 0, ins[a], mine, sibling))
            for j, chip in enumerate(chips):
                sends.append(copy(a, 1 + j, ins[a], mine, (*chip, c)))
        for cp in sends:
            cp.start()
        for j, chip in enumerate(chips):
            for a in range(n):
                got = slab(a, *chip, c)
                copy(a, 1 + j, got, got, (x, y, c)).wait_recv()
                fwd = copy(a, 4 + j, got, got, sibling)
                fwd.start()
                sends.append(fwd)
        for a in range(n):
            got = slab(a, x, y, 1 - c)
            copy(a, 0, got, got, (x, y, c)).wait_recv()
            for j, chip in enumerate(chips):
                got = slab(a, *chip, 1 - c)
                copy(a, 4 + j, got, got, (x, y, c)).wait_recv()
        for cp in sends:
            cp.wait_send()
        for cp in local:
            cp.wait()

    return pl.pallas_call(
        body, name="weights_all_gather", in_specs=[_ANY] * n, out_specs=[_ANY] * n,
        out_shape=[_sd((N_DEV,) + b.shape, b.dtype) for b in blocks],
        scratch_shapes=[pltpu.SemaphoreType.DMA((n, 7)), pltpu.SemaphoreType.DMA((n, 7)), pltpu.SemaphoreType.DMA((n,))],
    )(*blocks)


_HBM = pl.BlockSpec(memory_space=pltpu.HBM)
_SEM = pl.BlockSpec(memory_space=pltpu.SEMAPHORE)
_EFFECT = pltpu.SideEffectType.DATAFLOW_SIDE_EFFECTING


def _peers():
    x, y, c = _place()
    out = []
    for m in range(1, N_DEV):
        px, py, pc = x ^ (m >> 2), y ^ ((m >> 1) & 1), c ^ (m & 1)
        out.append((m, (px, py, pc), 4 * px + 2 * py + pc))
    return 4 * x + 2 * y + c, out


def _exchange_copies(src, land, send_sem, recv_sem, gather):
    me, peers = _peers()
    cps = []
    for a in range(len(src)):
        for m, pos, idx in peers:
            s_ref, d_ref = (src[a], land[a].at[me]) if gather else (src[a].at[idx], land[a].at[m - 1])
            k = a * (N_DEV - 1) + m - 1
            cps.append(pltpu.make_async_remote_copy(src_ref=s_ref, dst_ref=d_ref, send_sem=send_sem.at[k],
                                                    recv_sem=recv_sem.at[k], device_id=pos, device_id_type=MESH))
    return cps


def _xstart(name, srcs, gather, after=()):
    n = len(srcs)
    if gather:
        land_shapes = [(N_DEV,) + s.shape for s in srcs]
    else:
        land_shapes = [(N_DEV - 1,) + s.shape[1:] for s in srcs]

    def body(*refs):
        src, land = refs[:n], refs[n:2 * n]
        send_sem, recv_sem = refs[2 * n + len(after)], refs[2 * n + len(after) + 1]
        token = refs[-1]
        for cp in _exchange_copies(src, land, send_sem, recv_sem, gather):
            cp.start()
        token[...] = jnp.zeros_like(token)

    sem = pltpu.SemaphoreType.DMA((n * (N_DEV - 1),))
    out_shape = ([sem, sem] + [pltpu.HBM(s.shape, s.dtype) for s in srcs]
                 + [pltpu.HBM(sh, s.dtype) for sh, s in zip(land_shapes, srcs)] + [_sd((8, LANES), F32)])
    args = [pltpu.with_memory_space_constraint(s, pltpu.HBM) for s in srcs]
    args += [pltpu.with_memory_space_constraint(lax.empty(sh, s.dtype), pltpu.HBM) for sh, s in zip(land_shapes, srcs)]
    res = pl.pallas_call(
        body, name=name, out_shape=out_shape, in_specs=[_HBM] * (2 * n) + [_ANY] * len(after),
        out_specs=[_SEM, _SEM] + [_HBM] * (2 * n) + [pl.BlockSpec(memory_space=pltpu.VMEM)],
        input_output_aliases={i: 2 + i for i in range(2 * n)},
        compiler_params=pltpu.CompilerParams(has_side_effects=_EFFECT))(*args, *after)
    return dict(send=res[0], recv=res[1], srcs=list(res[2:2 + n]), lands=list(res[2 + n:2 + 2 * n]), token=res[-1])


def _xwait(name, st, after, gather):
    n = len(st["srcs"])

    def body(*refs):
        src, land = refs[:n], refs[n:2 * n]
        send_sem, recv_sem = refs[2 * n], refs[2 * n + 1]
        for cp in _exchange_copies(src, land, send_sem, recv_sem, gather):
            cp.wait_send()
            cp.wait_recv()

    arrays = st["srcs"] + st["lands"]
    res = pl.pallas_call(
        body, name=name, out_shape=[pltpu.HBM(a.shape, a.dtype) for a in arrays],
        in_specs=[_HBM] * (2 * n) + [_SEM, _SEM, _ANY], out_specs=[_HBM] * (2 * n),
        input_output_aliases={i: i for i in range(2 * n)},
        compiler_params=pltpu.CompilerParams(has_side_effects=_EFFECT))(*arrays, st["send"], st["recv"], after)
    return list(res[:n]), list(res[n:])


def _put_own(name, land, block, me_arr):
    R, C = block.shape
    tr = _pick(R, (512, 256, 128, 64, 32, 16))

    def body(me_ref, b_ref, land_ref, o_ref):
        o_ref[...] = b_ref[...]

    return pl.pallas_call(
        body, name=name, out_shape=_sd(land.shape, land.dtype),
        grid_spec=pltpu.PrefetchScalarGridSpec(
            num_scalar_prefetch=1, grid=(R // tr,),
            in_specs=[_spec((tr, C), lambda i, me_ref: (i, 0)), _ANY],
            out_specs=_spec((None, tr, C), lambda i, me_ref: (me_ref[0], i, 0))),
        input_output_aliases={2: 0},
        compiler_params=_params(("parallel",), 8 * tr * C * 2))(me_arr, block, land)


def _all_reduce_small(name, part):
    R = part.shape[0]

    def body(p_ref, out_ref, gath_ref, send_sems, recv_sems):
        x, y, c = _place()
        me = 4 * x + 2 * y + c
        gath_ref[me] = p_ref[...]
        cps = []
        for m in range(1, N_DEV):
            to = (x ^ (m >> 2), y ^ ((m >> 1) & 1), c ^ (m & 1))
            cps.append(pltpu.make_async_remote_copy(
                src_ref=p_ref, dst_ref=gath_ref.at[me], send_sem=send_sems.at[m - 1], recv_sem=recv_sems.at[m - 1],
                device_id=to, device_id_type=MESH))
        for cp in cps:
            cp.start()
        for m in range(1, N_DEV):
            frm = 4 * (x ^ (m >> 2)) + 2 * (y ^ ((m >> 1) & 1)) + (c ^ (m & 1))
            pltpu.make_async_remote_copy(
                src_ref=p_ref, dst_ref=gath_ref.at[frm], send_sem=send_sems.at[m - 1], recv_sem=recv_sems.at[m - 1],
                device_id=(x, y, c), device_id_type=MESH).wait_recv()
        for cp in cps:
            cp.wait_send()
        tot = gath_ref[0]
        for k in range(1, N_DEV):
            tot = tot + gath_ref[k]
        out_ref[...] = tot

    vm = pl.BlockSpec(memory_space=pltpu.VMEM)
    return pl.pallas_call(
        body, name=name, in_specs=[vm], out_specs=vm, out_shape=_sd((R, LANES), F32),
        scratch_shapes=[pltpu.VMEM((N_DEV, R, LANES), F32), pltpu.SemaphoreType.DMA((N_DEV - 1,)),
                        pltpu.SemaphoreType.DMA((N_DEV - 1,))],
        compiler_params=pltpu.CompilerParams(vmem_limit_bytes=VMEM_FLOOR),
    )(part)


def _adam_math(w, g, m, v):
    m2 = ADAM_B1 * m + (1.0 - ADAM_B1) * g
    v2 = ADAM_B2 * v + (1.0 - ADAM_B2) * (g * g)
    m_hat = m2 / (1.0 - ADAM_B1 ** ADAM_STEP)
    v_hat = v2 / (1.0 - ADAM_B2 ** ADAM_STEP)
    delta = -ADAM_LR * (m_hat / (jnp.sqrt(v_hat) + ADAM_EPS) + ADAM_WD * w)
    return delta, m2, v2


def _adamw_sharded(name, lands, fulls, me_arr, w, m, v):
    n_l = len(lands)
    R, C = lands[0].shape[1], lands[0].shape[2]
    tr = _pick(R, (128, 64, 32, 16, 8))
    nr = R // tr

    def body(me_ref, *refs):
        land_refs, own_refs = refs[:n_l], refs[n_l:2 * n_l]
        w_ref, m_ref, v_ref, g_ref, d_ref, m2_ref, v2_ref = refs[2 * n_l:]
        layer = pl.program_id(0)
        for ll in range(n_l):
            @pl.when(layer == ll)
            def _(ll=ll):
                g = own_refs[ll][...].astype(F32)
                for j in range(N_DEV - 1):
                    g = g + land_refs[ll][j].astype(F32)
                delta, m2, v2 = _adam_math(w_ref[...], g, m_ref[...], v_ref[...])
                g_ref[...] = g
                d_ref[...] = delta
                m2_ref[...] = m2
                v2_ref[...] = v2

    def row_of(ll):
        return lambda l, i, me_ref: jnp.where(l == ll, i, 0)

    in_specs = [_spec((N_DEV - 1, tr, C), lambda l, i, me_ref, f=row_of(ll): (0, f(l, i, me_ref), 0)) for ll in range(n_l)]
    in_specs += [_spec((None, tr, C), lambda l, i, me_ref, f=row_of(ll): (me_ref[0], f(l, i, me_ref), 0))
                 for ll in range(n_l)]
    blk = _spec((tr, C), lambda l, i, me_ref: (l * nr + i, 0))
    return pl.pallas_call(
        body, name=name, out_shape=[_sd(w.shape, F32)] * 4,
        grid_spec=pltpu.PrefetchScalarGridSpec(num_scalar_prefetch=1, grid=(n_l, nr), in_specs=in_specs + [blk] * 3,
                                               out_specs=[blk] * 4),
        compiler_params=_params(("parallel", "parallel"), (4 * n_l * N_DEV + 40) * tr * C * 4))(
            me_arr, *lands, *fulls, w, m, v)


def _adamw_packed(name, w, g, m, v):
    R = w.shape[0]

    def body(w_ref, g_ref, m_ref, v_ref, d_ref, m2_ref, v2_ref):
        delta, m2, v2 = _adam_math(w_ref[...], g_ref[...], m_ref[...], v_ref[...])
        d_ref[...] = delta
        m2_ref[...] = m2
        v2_ref[...] = v2

    vm = pl.BlockSpec(memory_space=pltpu.VMEM)
    return pl.pallas_call(
        body, name=name, in_specs=[vm] * 4, out_specs=[vm] * 3, out_shape=[_sd((R, LANES), F32)] * 3,
        compiler_params=pltpu.CompilerParams(vmem_limit_bytes=VMEM_FLOOR))(w, g, m, v)


def _pack(arrays):
    flat = jnp.concatenate([a.reshape(-1).astype(F32) for a in arrays])
    pad = (-flat.shape[0]) % (8 * LANES)
    return jnp.pad(flat, (0, pad)).reshape(-1, LANES)


def _unpack(packed, like):
    flat = packed.reshape(-1)
    out, pos = [], 0
    for a in like:
        n = math.prod(a.shape)
        out.append(flat[pos:pos + n].reshape(a.shape))
        pos += n
    return out


def _ffn_fwd(tag, x, gain, w_in_sm, w_out_of, deps=()):
    h = _rms_fwd(tag + "_norm", x, gain, deps)
    gu, act = _pair_in(tag + "_in", h, w_in_sm, BF, _swiglu)
    x_new = _rows_out(tag + "_out", act, w_out_of(act), x, 0.5)
    return x_new, (x, h, gu, act)


def _ffn_bwd(tag, d, d_bf, saved, gain, w_in_sm, w_out_sm, send_grads):
    x, h, gu, act = saved
    dgu = _rows_dact(tag + "_dact", d_bf, w_out_sm, [gu], [_sd(gu.shape, BF)], _swiglu_bwd_epi)[0]
    g_out = _rows_wgrad(tag + "_wgrad_out", act, d_bf, 0.5)
    g_in = _cols_wgrad(tag + "_wgrad_in", h, dgu)
    token = send_grads(g_in, g_out)
    dh = _cols_dh(tag + "_dh", dgu, w_in_sm, deps=(token,))
    dx, dx_bf, dgain = _rms_bwd(tag + "_dnorm", dh, x, gain, d)
    return dx, dx_bf, dgain


def kernel(x, positions, ln_ffn1, ffn1_w_in, ffn1_w_out, ln_mix, ln_ffn2, ffn2_w_in, ffn2_w_out, sgu_w_in, sgu_v_gain, sgu_v_bias, sgu_w_spatial, sgu_b_spatial, sgu_w_out, mla_w_in, mla_q_norm, mla_w_q_up, mla_kv_norm, mla_w_kv_up, mla_w_out, ln_final, loss_target, m_ln_ffn1, m_ffn1_w_in, m_ffn1_w_out, m_ln_mix, m_ln_ffn2, m_ffn2_w_in, m_ffn2_w_out, m_sgu_w_in, m_sgu_v_gain, m_sgu_v_bias, m_sgu_w_spatial, m_sgu_b_spatial, m_sgu_w_out, m_mla_w_in, m_mla_q_norm, m_mla_w_q_up, m_mla_kv_norm, m_mla_w_kv_up, m_mla_w_out, m_ln_final, v_ln_ffn1, v_ffn1_w_in, v_ffn1_w_out, v_ln_mix, v_ln_ffn2, v_ffn2_w_in, v_ffn2_w_out, v_sgu_w_in, v_sgu_v_gain, v_sgu_v_bias, v_sgu_w_spatial, v_sgu_b_spatial, v_sgu_w_out, v_mla_w_in, v_mla_q_norm, v_mla_w_q_up, v_mla_kv_norm, v_mla_w_kv_up, v_mla_w_out, v_ln_final):
    S, D = x.shape[1], x.shape[2]
    L = ln_ffn1.shape[0]
    H = mla_w_q_up.shape[-1] * N_DEV // (QK_NOPE + QK_ROPE)
    xi, yi, ci = _place()
    me = 4 * xi + 2 * yi + ci
    me_arr = jnp.reshape(me, (1,)).astype(jnp.int32)
    big = dict(ffn1_w_in=ffn1_w_in, ffn1_w_out=ffn1_w_out, ffn2_w_in=ffn2_w_in, ffn2_w_out=ffn2_w_out,
               sgu_w_in=sgu_w_in, sgu_w_out=sgu_w_out, mla_w_in=mla_w_in, mla_w_q_up=mla_w_q_up,
               mla_w_kv_up=mla_w_kv_up, mla_w_out=mla_w_out)
    big_m = dict(ffn1_w_in=m_ffn1_w_in, ffn1_w_out=m_ffn1_w_out, ffn2_w_in=m_ffn2_w_in, ffn2_w_out=m_ffn2_w_out,
                 sgu_w_in=m_sgu_w_in, sgu_w_out=m_sgu_w_out, mla_w_in=m_mla_w_in, mla_w_q_up=m_mla_w_q_up,
                 mla_w_kv_up=m_mla_w_kv_up, mla_w_out=m_mla_w_out)
    big_v = dict(ffn1_w_in=v_ffn1_w_in, ffn1_w_out=v_ffn1_w_out, ffn2_w_in=v_ffn2_w_in, ffn2_w_out=v_ffn2_w_out,
                 sgu_w_in=v_sgu_w_in, sgu_w_out=v_sgu_w_out, mla_w_in=v_mla_w_in, mla_w_q_up=v_mla_w_q_up,
                 mla_w_kv_up=v_mla_w_kv_up, mla_w_out=v_mla_w_out)
    names = list(big)
    mla_names = ["mla_w_in", "mla_w_q_up", "mla_w_kv_up", "mla_w_out"]

    blocks = {(k, l): big[k][l].astype(BF) for k in names for l in range(big[k].shape[0])}
    first = ("ffn1_w_in", 0)
    groups = {}
    for i in range(L):
        if i > 0:
            groups[f"ffn1_in_{i}"] = [("ffn1_w_in", i)]
        groups[f"ffn1_out_{i}"] = [("ffn1_w_out", i)]
        groups[f"mix_{i}"] = [("sgu_w_in", i // 2), ("sgu_w_out", i // 2)] if i % 2 == 0 else [(k, i // 2) for k in mla_names]
        groups[f"ffn2_{i}"] = [("ffn2_w_in", i), ("ffn2_w_out", i)]
    gathered = {first: _all_gather([blocks[first]])[0]}
    started, order_after = {}, (gathered[first],)
    for tag, grp in groups.items():
        started[tag] = _xstart(f"gather_start_{tag}", [blocks[k] for k in grp], True, order_after)
        order_after = (started[tag]["token"],)

    def fetch(tag, after):
        own, lands = _xwait(f"gather_wait_{tag}", started[tag], after, True)
        for k, blk, land in zip(groups[tag], own, lands):
            gathered[k] = _put_own(f"gather_own_{k[0]}_{k[1]}", land, blk, me_arr)

    def w_out_of(name, tag):
        def get(after):
            fetch(tag, after)
            return gathered[name]
        return get

    norm_rows = jnp.zeros((N_DEV, LANES), F32)
    mine = jnp.concatenate([mla_q_norm[0], mla_kv_norm[0]])
    norm_rows = lax.dynamic_update_slice(norm_rows, mine[None, :], (me, 0))
    norm_all = _all_reduce_small("norm_gains_gather", norm_rows)
    nq_sh = mla_q_norm.shape[1]
    q_gain = norm_all[:, :nq_sh].reshape(1, Q_LORA)
    kv_gain = norm_all[:, nq_sh:2 * nq_sh].reshape(1, KV_LORA)
    cos, sa, sb = _rope_tables(positions[0])
    scale = float((QK_NOPE + QK_ROPE) ** -0.5)

    xs = x[0]
    w_sp = sgu_w_spatial[0]
    b_sp = sgu_b_spatial[0][:, :, None]
    saved = []
    mla_w = {}
    deps = order_after
    for i in range(L):
        if i > 0:
            fetch(f"ffn1_in_{i}", xs)
        xs, s1 = _ffn_fwd(f"l{i}_ffn1", xs, ln_ffn1[i:i + 1], gathered[("ffn1_w_in", i)],
                          w_out_of(("ffn1_w_out", i), f"ffn1_out_{i}"), deps)
        deps = ()
        fetch(f"mix_{i}", xs)
        x_mix = xs
        h = _rms_fwd(f"l{i}_mix_norm", xs, ln_mix[i:i + 1])
        j = i // 2
        if i % 2 == 0:
            puv = _pair_in(f"l{i}_sgu_in", h, gathered[("sgu_w_in", j)], F32, None)[0]
            gated = _sgu_mid_fwd(f"l{i}_sgu_mid", puv, sgu_v_gain, sgu_v_bias, w_sp, b_sp)
            xs = _rows_out(f"l{i}_sgu_out", gated, gathered[("sgu_w_out", j)], xs, 1.0)
            sm = (x_mix, h, puv, gated)
        else:
            w_in_nat = gathered[("mla_w_in", j)].reshape(D, Q_LORA + KV_LORA + QK_ROPE)
            w_in_pad = jnp.pad(w_in_nat, ((0, 0), (0, HEAD_PAD - QK_ROPE)))
            wq_nat = jnp.transpose(gathered[("mla_w_q_up", j)], (1, 0, 2)).reshape(Q_LORA, H, QK_NOPE + QK_ROPE)
            wq_t = jnp.stack([wq_nat[:, :, :QK_NOPE].reshape(Q_LORA, H * HEAD_PAD),
                              jnp.pad(wq_nat[:, :, QK_NOPE:], ((0, 0), (0, 0), (0, HEAD_PAD - QK_ROPE))).reshape(
                                  Q_LORA, H * HEAD_PAD)])
            wkv_nat = jnp.transpose(gathered[("mla_w_kv_up", j)], (1, 0, 2)).reshape(KV_LORA, H, QK_NOPE + V_DIM)
            wkv_t = jnp.stack([wkv_nat[:, :, :QK_NOPE].reshape(KV_LORA, H * HEAD_PAD),
                               wkv_nat[:, :, QK_NOPE:].reshape(KV_LORA, H * HEAD_PAD)])
            w_o_nat = gathered[("mla_w_out", j)].reshape(H * V_DIM, D)
            mla_w[i] = (w_in_pad, wq_t, wkv_t, w_o_nat)
            proj = _mm2(f"l{i}_mla_in", h, w_in_pad, False, False, F32, tn_cands=(384, 128))
            lat, kr = _mla_mid_fwd(f"l{i}_mla_mid", proj, q_gain, kv_gain, cos, sa, sb)

            def q_epi(accs, ex, orefs, ids):
                orefs[0][...] = (accs[0] * scale).astype(BF)

            def qr_epi(accs, ex, orefs, ids):
                for hh in range(accs[0].shape[1] // HEAD_PAD):
                    sl = slice(hh * HEAD_PAD, (hh + 1) * HEAD_PAD)
                    orefs[0][:, sl] = (_rope(accs[0][:, sl], *ex) * scale).astype(BF)

            q_nope = _mm2(f"l{i}_mla_q", lat[0], wq_t[0], False, False, BF, epi=q_epi)
            q_rope = _mm2(f"l{i}_mla_qr", lat[0], wq_t[1], False, False, BF, epi=qr_epi, extras=(cos, sa, sb))
            q_all = jnp.stack([q_nope, q_rope])
            kv_all = jnp.stack([_mm2(f"l{i}_mla_k", lat[1], wkv_t[0], False, False, BF),
                                _mm2(f"l{i}_mla_v", lat[1], wkv_t[1], False, False, BF)])
            o, lse = _attn_fwd(f"l{i}_attn", q_all, kv_all, kr)
            xs = _mm2(f"l{i}_mla_out", o, w_o_nat, False, False, F32, res=xs)
            sm = (x_mix, h, proj, lat, kr, q_all, kv_all, o, lse)
        fetch(f"ffn2_{i}", xs)
        xs, s2 = _ffn_fwd(f"l{i}_ffn2", xs, ln_ffn2[i:i + 1], gathered[("ffn2_w_in", i)],
                          lambda after, i=i: gathered[("ffn2_w_out", i)])
        saved.append((s1, sm, s2))

    loss_row, d, d_bf, g_ln_final = _final_loss("final_loss", xs, ln_final[None, :], loss_target[0])
    loss = lax.psum(loss_row[0, 0], ("x", "y", "c"))

    sent = []

    def send(tag, keys, grads):
        st = _xstart(f"scatter_start_{tag}", grads, False)
        sent.append((tag, keys, st))
        return st["token"]

    g_ln1, g_ln2, g_lnm = [None] * L, [None] * L, [None] * L
    small_g = {}
    for i in reversed(range(L)):
        s1, sm, s2 = saved[i]
        d, d_bf, g_ln2[i] = _ffn_bwd(
            f"l{i}_ffn2", d, d_bf, s2, ln_ffn2[i:i + 1], gathered[("ffn2_w_in", i)], gathered[("ffn2_w_out", i)],
            lambda g_in, g_out, i=i: send(f"l{i}_ffn2", [("ffn2_w_in", i), ("ffn2_w_out", i)], [g_in, g_out]))
        j = i // 2
        if i % 2 == 0:
            x_mix, h, puv, gated = sm
            dgated = _rows_dact(f"l{i}_sgu_dgated", d_bf, gathered[("sgu_w_out", j)], [], [_sd(gated.shape, BF)], _store())[0]
            g_so = _rows_wgrad(f"l{i}_sgu_wgrad_out", gated, d_bf, None)
            dpuv, dgain, dbias, dwsp, dbsp = _sgu_mid_bwd(f"l{i}_sgu_mid_bwd", puv, dgated, sgu_v_gain, sgu_v_bias,
                                                          w_sp, b_sp)
            small_g.update(sgu_v_gain=dgain, sgu_v_bias=dbias, sgu_w_spatial=dwsp[None], sgu_b_spatial=dbsp[None, :, :, 0])
            g_si = _cols_wgrad(f"l{i}_sgu_wgrad_in", h, dpuv)
            token = send(f"l{i}_sgu", [("sgu_w_in", j), ("sgu_w_out", j)], [g_si, g_so])
            dh = _cols_dh(f"l{i}_sgu_dh", dpuv, gathered[("sgu_w_in", j)], deps=(token,))
        else:
            x_mix, h, proj, lat, kr, q_all, kv_all, o, lse = sm
            w_in_pad, wq_t, wkv_t, w_o_nat = mla_w[i]
            t = _attn_tile(S)
            do = _mm2(f"l{i}_mla_do", d_bf, w_o_nat, False, True, F32)
            g_wo = _mm2(f"l{i}_mla_wgrad_out", o, d_bf, True, False, BF)
            delta = _attn_delta(f"l{i}_attn_delta", do, o)
            dq_all = _attn_dq(f"l{i}_attn_dq", q_all, kv_all, kr, do, lse, delta, cos, sa, sb, scale)
            lse_row = lse[:, :, 0].reshape(H, S // t, 1, t)
            delta_row = delta[:, :, 0].reshape(H, S // t, 1, t)
            dkv_all, dkr_heads = _attn_dkv(f"l{i}_attn_dkv", q_all, kv_all, kr, do, lse_row, delta_row)
            dqn = _mm2(f"l{i}_mla_dqn", dq_all[0], wq_t[0], False, True, F32)
            dqn = _mm2(f"l{i}_mla_dqn2", dq_all[1], wq_t[1], False, True, F32, res=dqn)
            dkvn = _mm2(f"l{i}_mla_dkvn", dkv_all[0], wkv_t[0], False, True, F32)
            dkvn = _mm2(f"l{i}_mla_dkvn2", dkv_all[1], wkv_t[1], False, True, F32, res=dkvn)
            g_wq = [_mm2(f"l{i}_mla_wgrad_q{t2}", lat[0], dq_all[t2], True, False, BF) for t2 in range(2)]
            g_wkv = [_mm2(f"l{i}_mla_wgrad_kv{t2}", lat[1], dkv_all[t2], True, False, BF) for t2 in range(2)]
            dproj, g_qn, g_kvn = _mla_mid_bwd(f"l{i}_mla_mid_bwd", proj, dqn, dkvn, dkr_heads, q_gain, kv_gain, cos, sa, sb)
            g_win = _mm2(f"l{i}_mla_wgrad_in", h, dproj, True, False, BF, tn_cands=(384, 128))
            n_in = Q_LORA + KV_LORA + QK_ROPE
            gq_nat = jnp.concatenate([g_wq[0].reshape(Q_LORA, H, HEAD_PAD),
                                      g_wq[1].reshape(Q_LORA, H, HEAD_PAD)[:, :, :QK_ROPE]], axis=2)
            gkv_nat = jnp.concatenate([g_wkv[0].reshape(KV_LORA, H, HEAD_PAD), g_wkv[1].reshape(KV_LORA, H, HEAD_PAD)], axis=2)
            token = send(f"l{i}_mla", [(k, j) for k in mla_names],
                         [g_win[:, :n_in].reshape(N_DEV, D // N_DEV, n_in),
                          jnp.transpose(gq_nat.reshape(Q_LORA, N_DEV, -1), (1, 0, 2)),
                          jnp.transpose(gkv_nat.reshape(KV_LORA, N_DEV, -1), (1, 0, 2)),
                          g_wo.reshape(N_DEV, H * V_DIM // N_DEV, D)])
            small_g.update(mla_q_norm=g_qn, mla_kv_norm=g_kvn)
            dh = _mm2(f"l{i}_mla_dh", dproj, w_in_pad, False, True, F32, tn_cands=(512, 256, 128), deps=(token,))
        d, d_bf, g_lnm[i] = _rms_bwd(f"l{i}_mix_dnorm", dh, x_mix, ln_mix[i:i + 1], d)
        d, d_bf, g_ln1[i] = _ffn_bwd(
            f"l{i}_ffn1", d, d_bf, s1, ln_ffn1[i:i + 1], gathered[("ffn1_w_in", i)], gathered[("ffn1_w_out", i)],
            lambda g_in, g_out, i=i: send(f"l{i}_ffn1", [("ffn1_w_in", i), ("ffn1_w_out", i)], [g_in, g_out]))
    grad_x = d[None]

    landed, partial = {}, {}
    for tag, keys, st in sent:
        fulls, lands = _xwait(f"scatter_wait_{tag}", st, d, False)
        for k, full, land in zip(keys, fulls, lands):
            partial[k], landed[k] = full, land
    big_out = {}
    for k in names:
        w = big[k]
        rc = (math.prod(w.shape[1:-1]), w.shape[-1])
        flat = (w.shape[0] * rc[0], rc[1])
        lands = [landed[(k, l)].reshape((N_DEV - 1,) + rc) for l in range(w.shape[0])]
        fulls = [partial[(k, l)].reshape((N_DEV,) + rc) for l in range(w.shape[0])]
        res = _adamw_sharded(f"adamw_{k}", lands, fulls, me_arr, w.reshape(flat), big_m[k].reshape(flat),
                             big_v[k].reshape(flat))
        big_out[k] = [r.reshape(w.shape) for r in res]

    small_g.update(ln_ffn1=jnp.concatenate(g_ln1), ln_mix=jnp.concatenate(g_lnm), ln_ffn2=jnp.concatenate(g_ln2),
                   ln_final=g_ln_final[0])
    small_names = ["ln_ffn1", "ln_mix", "ln_ffn2", "sgu_v_gain", "sgu_v_bias", "sgu_w_spatial", "sgu_b_spatial",
                   "ln_final", "mla_q_norm", "mla_kv_norm"]
    summed = _unpack(_all_reduce_small("small_grads_all_reduce", _pack([small_g[k] for k in small_names])),
                     [small_g[k] for k in small_names])
    small_grad = dict(zip(small_names, summed))
    for k in ("mla_q_norm", "mla_kv_norm"):
        small_grad[k] = lax.dynamic_slice(small_grad[k], (0, me * nq_sh), (1, nq_sh))
    small_w = dict(ln_ffn1=ln_ffn1, ln_mix=ln_mix, ln_ffn2=ln_ffn2, sgu_v_gain=sgu_v_gain, sgu_v_bias=sgu_v_bias,
                   sgu_w_spatial=sgu_w_spatial, sgu_b_spatial=sgu_b_spatial, ln_final=ln_final, mla_q_norm=mla_q_norm,
                   mla_kv_norm=mla_kv_norm)
    small_m = dict(ln_ffn1=m_ln_ffn1, ln_mix=m_ln_mix, ln_ffn2=m_ln_ffn2, sgu_v_gain=m_sgu_v_gain, sgu_v_bias=m_sgu_v_bias,
                   sgu_w_spatial=m_sgu_w_spatial, sgu_b_spatial=m_sgu_b_spatial, ln_final=m_ln_final,
                   mla_q_norm=m_mla_q_norm, mla_kv_norm=m_mla_kv_norm)
    small_v = dict(ln_ffn1=v_ln_ffn1, ln_mix=v_ln_mix, ln_ffn2=v_ln_ffn2, sgu_v_gain=v_sgu_v_gain, sgu_v_bias=v_sgu_v_bias,
                   sgu_w_spatial=v_sgu_w_spatial, sgu_b_spatial=v_sgu_b_spatial, ln_final=v_ln_final,
                   mla_q_norm=v_mla_q_norm, mla_kv_norm=v_mla_kv_norm)
    like = [small_w[k] for k in small_names]
    packed = _adamw_packed("adamw_small", _pack(like), _pack([small_grad[k] for k in small_names]),
                           _pack([small_m[k] for k in small_names]), _pack([small_v[k] for k in small_names]))
    small_out = {}
    unpacked = [_unpack(p, like) for p in packed]
    for idx, k in enumerate(small_names):
        small_out[k] = [small_grad[k].reshape(small_w[k].shape)] + [u[idx] for u in unpacked]

    order = ["ln_ffn1", "ffn1_w_in", "ffn1_w_out", "ln_mix", "ln_ffn2", "ffn2_w_in", "ffn2_w_out", "sgu_w_in",
             "sgu_v_gain", "sgu_v_bias", "sgu_w_spatial", "sgu_b_spatial", "sgu_w_out", "mla_w_in", "mla_q_norm",
             "mla_w_q_up", "mla_kv_norm", "mla_w_kv_up", "mla_w_out", "ln_final"]
    res = {k: (big_out[k] if k in big_out else small_out[k]) for k in order}
    outs = [loss, grad_x]
    for part in range(4):
        outs.extend(res[k][part] for k in order)
    return tuple(outs)
```

```python
import math

import jax
import jax.numpy as jnp
from jax import lax
from jax.experimental import pallas as pl
from jax.experimental.pallas import tpu as pltpu

F32 = jnp.float32
BF = jnp.bfloat16
MESH = pl.DeviceIdType.MESH

N_DEV = 8
EPS = 1e-6
CHUNK = 64
SGU_BLOCK = 128
SGU_GROUPS = 8
Q_LORA = 512
KV_LORA = 512
QK_NOPE = 128
QK_ROPE = 64
V_DIM = 128
ROPE_THETA = 10000.0
HEAD_PAD = 128
LANES = 128
ADAM_LR = 0.001
ADAM_B1 = 0.9
ADAM_B2 = 0.999
ADAM_EPS = 1e-08
ADAM_WD = 0.01
ADAM_STEP = 10
V7X_VMEM_BYTES = 64 * 1024 * 1024
VMEM_CAP = V7X_VMEM_BYTES - 6 * 1024 * 1024
VMEM_FLOOR = 32 * 1024 * 1024
NEG = -1e30
ATTN_GROUPS = 1
ATTN_UNROLL = 4


def _pick(n, cands):
    for c in cands:
        if n % c == 0:
            return c
    return n


def _nbytes(shape, dtype):
    return math.prod(int(s) for s in shape if s is not None) * jnp.dtype(dtype).itemsize


def _params(sem, block_bytes):
    limit = int(min(VMEM_CAP, max(VMEM_FLOOR, block_bytes)))
    return pltpu.CompilerParams(dimension_semantics=sem, vmem_limit_bytes=limit)


def _spec(shape, fn):
    return pl.BlockSpec(shape, fn)


_ANY = pl.BlockSpec(memory_space=pl.ANY)


def _mm(name, grid, ops, pairs, extras, outs, epilogue, acc_shapes, deps=()):
    nk = grid[2]
    n_ops, n_ex, n_out = len(ops), len(extras), len(outs)

    def load(refs, idx):
        loader = ops[idx][2] if len(ops[idx]) > 2 else None
        return (refs[idx][...] if loader is None else loader(refs[idx])).astype(BF)

    def prod(refs, p):
        ia, ib, ta, tb, _ = p
        a = load(refs, ia)
        b = load(refs, ib)
        dims = (((0 if ta else 1,), (1 if tb else 0,)), ((), ()))
        return lax.dot_general(a, b, dims, preferred_element_type=F32)

    def body(*refs):
        op_refs = refs[:n_ops]
        ex_refs = refs[n_ops:n_ops + n_ex]
        n_in = n_ops + n_ex + len(deps)
        out_refs = refs[n_in:n_in + n_out]
        acc_refs = refs[n_in + n_out:]
        ids = (pl.program_id(0), pl.program_id(1))

        def finish(vals):
            epilogue(vals, [e[...] for e in ex_refs], out_refs, ids)

        if nk == 1:
            vals = [None] * len(acc_shapes)
            for p in pairs:
                r = prod(op_refs, p)
                vals[p[4]] = r if vals[p[4]] is None else vals[p[4]] + r
            finish(vals)
        else:
            k = pl.program_id(2)

            def products():
                vals = [None] * len(acc_shapes)
                for p in pairs:
                    r = prod(op_refs, p)
                    vals[p[4]] = r if vals[p[4]] is None else vals[p[4]] + r
                return vals

            @pl.when(k == 0)
            def _():
                for a, v in zip(acc_refs, products()):
                    a[...] = v

            @pl.when((k > 0) & (k < nk - 1))
            def _():
                for a, v in zip(acc_refs, products()):
                    a[...] += v

            @pl.when(k == nk - 1)
            def _():
                finish([a[...] + v for a, v in zip(acc_refs, products())])

    in_arrays = [o[0] for o in ops] + [e[0] for e in extras]
    in_specs = [o[1] for o in ops] + [e[1] for e in extras]
    in_arrays += list(deps)
    in_specs += [_ANY] * len(deps)
    blk = 0
    for entry in ops + extras:
        blk += 2 * _nbytes(entry[1].block_shape, entry[0].dtype)
    for sd, sp in outs:
        blk += 2 * _nbytes(sp.block_shape, sd.dtype)
    acc_b = sum(_nbytes(s, F32) for s in acc_shapes)
    blk += 6 * acc_b
    scratch = [pltpu.VMEM(s, F32) for s in acc_shapes] if nk > 1 else []
    res = pl.pallas_call(
        body, name=name, grid=grid, in_specs=in_specs,
        out_specs=[o[1] for o in outs], out_shape=[o[0] for o in outs],
        scratch_shapes=scratch,
        compiler_params=_params(("parallel", "parallel", "arbitrary"), blk))(*in_arrays)
    return res


def _store(scale=None):
    def epi(accs, ex, outs, ids):
        v = accs[0]
        if scale is not None:
            v = v * scale
        outs[0][...] = v.astype(outs[0].dtype)
    return epi


def _store_residual(scale):
    def epi(accs, ex, outs, ids):
        outs[0][...] = ex[0] + scale * accs[0]
    return epi


def _sd(shape, dtype):
    return jax.ShapeDtypeStruct(tuple(shape), dtype)


def _pair_in(name, h, w_sm, out_dtype, act):
    S, D = h.shape
    c = w_sm.shape[-1]
    tm = _pick(S, (256, 128))
    half = N_DEV // 2
    ops = [(h, _spec((tm, D), lambda j, i, k: (i, 0))),
           (w_sm, _spec((None, D, c), lambda j, i, k: (j, 0, 0))),
           (w_sm, _spec((None, D, c), lambda j, i, k: (j + half, 0, 0)))]
    outs = [(_sd((2, S, half * c), out_dtype), _spec((2, tm, c), lambda j, i, k: (0, i, j)))]
    if act is not None:
        outs.append((_sd((S, half * c), BF), _spec((tm, c), lambda j, i, k: (i, j))))

    def epi(accs, ex, orefs, ids):
        orefs[0][0] = accs[0].astype(out_dtype)
        orefs[0][1] = accs[1].astype(out_dtype)
        if act is not None:
            orefs[1][...] = act(accs[0], accs[1]).astype(BF)

    return _mm(name, (half, S // tm, 1), ops, [(0, 1, False, False, 0), (0, 2, False, False, 1)], [], outs, epi,
               [(tm, c), (tm, c)])


def _two_slabs(ref):
    return jnp.concatenate([ref[0], ref[1]], axis=0)


def _rows_out(name, a, w_sm, res, scale):
    S = a.shape[0]
    r, D = w_sm.shape[-2], w_sm.shape[-1]
    tm = _pick(S, (1024, 512, 256, 128))
    tn = _pick(D, (1024, 512, 256, 128))
    ops = [(a, _spec((tm, 2 * r), lambda i, j, k: (i, k))),
           (w_sm, _spec((2, r, tn), lambda i, j, k: (k, 0, j)), _two_slabs)]
    extras = [(res, _spec((tm, tn), lambda i, j, k: (i, j)))]
    outs = [(_sd((S, D), F32), _spec((tm, tn), lambda i, j, k: (i, j)))]
    return _mm(name, (S // tm, D // tn, N_DEV // 2), ops, [(0, 1, False, False, 0)], extras, outs,
               _store_residual(scale), [(tm, tn)])[0]


def _rows_dact(name, d_bf, w_sm, extras_arrays, out_shapes, epi):
    S, D = d_bf.shape
    r = w_sm.shape[-2]
    tm = _pick(S, (1024, 512, 256, 128))
    ops = [(d_bf, _spec((tm, D), lambda j, i, k: (i, 0))),
           (w_sm, _spec((2, r, D), lambda j, i, k: (j, 0, 0)), _two_slabs)]
    extras = []
    for arr in extras_arrays:
        if arr.ndim == 3:
            extras.append((arr, _spec((arr.shape[0], tm, 2 * r), lambda j, i, k: (0, i, j))))
        else:
            extras.append((arr, _spec((tm, 2 * r), lambda j, i, k: (i, j))))
    outs = []
    for sd in out_shapes:
        if len(sd.shape) == 3:
            outs.append((sd, _spec((sd.shape[0], tm, 2 * r), lambda j, i, k: (0, i, j))))
        else:
            outs.append((sd, _spec((tm, 2 * r), lambda j, i, k: (i, j))))
    return _mm(name, (N_DEV // 2, S // tm, 1), ops, [(0, 1, False, True, 0)], extras, outs, epi, [(tm, 2 * r)])


def _rows_wgrad(name, a, d_bf, scale):
    S, D = d_bf.shape
    r = a.shape[1] // N_DEV
    tn = _pick(D, (1024, 512, 256, 128))
    tk = _pick(S, (2048, 1024, 512, 256, 128))
    ops = [(a, _spec((tk, 2 * r), lambda s, j, k: (k, s))),
           (d_bf, _spec((tk, tn), lambda s, j, k: (k, j)))]
    outs = [(_sd((N_DEV, r, D), BF), _spec((2, r, tn), lambda s, j, k: (s, 0, j)))]

    def epi(accs, ex, orefs, ids):
        v = accs[0] if scale is None else accs[0] * scale
        orefs[0][0] = v[:r].astype(BF)
        orefs[0][1] = v[r:].astype(BF)

    return _mm(name, (N_DEV // 2, D // tn, S // tk), ops, [(0, 1, True, False, 0)], [], outs, epi, [(2 * r, tn)])[0]


def _cols_dh(name, dpair, w_sm, deps=()):
    _, S, _ = dpair.shape
    D, c = w_sm.shape[-2], w_sm.shape[-1]
    half = N_DEV // 2
    tm = _pick(S, (1024, 512, 256, 128))
    tn = _pick(D, (1024, 512, 256, 128))
    ops = [(dpair, _spec((None, tm, c), lambda i, j, k: (k // half, i, k % half))),
           (w_sm, _spec((None, tn, c), lambda i, j, k: (k, j, 0)))]
    outs = [(_sd((S, D), F32), _spec((tm, tn), lambda i, j, k: (i, j)))]
    return _mm(name, (S // tm, D // tn, N_DEV), ops, [(0, 1, False, True, 0)], [], outs, _store(), [(tm, tn)],
               deps=deps)[0]


def _cols_wgrad(name, h, dpair):
    S, D = h.shape
    half = N_DEV // 2
    c = dpair.shape[2] // half
    tm = _pick(D, (1024, 512, 256, 128))
    tk = _pick(S, (2048, 1024, 512, 256, 128))
    ops = [(h, _spec((tk, tm), lambda s, i, k: (k, i))),
           (dpair, _spec((None, tk, c), lambda s, i, k: (s // half, k, s % half)))]
    outs = [(_sd((N_DEV, D, c), BF), _spec((None, tm, c), lambda s, i, k: (s, i, 0)))]
    return _mm(name, (N_DEV, D // tm, S // tk), ops, [(0, 1, True, False, 0)], [], outs, _store(), [(tm, c)])[0]


def _mm2(name, a, b, ta, tb, out_dtype, epi=None, extras=(), res=None, tn_cands=(512, 384, 256, 128), deps=()):
    M = a.shape[1] if ta else a.shape[0]
    K = a.shape[0] if ta else a.shape[1]
    N = b.shape[0] if tb else b.shape[1]
    tm = _pick(M, (1024, 512, 256, 128))
    tn = _pick(N, tn_cands)
    tk = _pick(K, (2048, 1152, 1024, 512, 256, 128))
    a_spec = _spec((tk, tm), lambda i, j, k: (k, i)) if ta else _spec((tm, tk), lambda i, j, k: (i, k))
    b_spec = _spec((tn, tk), lambda i, j, k: (j, k)) if tb else _spec((tk, tn), lambda i, j, k: (k, j))
    ex = [(e, _spec((tm, e.shape[1]), lambda i, j, k: (i, 0))) for e in extras]
    if res is not None:
        ex = [(res, _spec((tm, tn), lambda i, j, k: (i, j)))]
        epi = _store_residual(1.0)
    outs = [(_sd((M, N), out_dtype), _spec((tm, tn), lambda i, j, k: (i, j)))]
    return _mm(name, (M // tm, N // tn, K // tk), [(a, a_spec), (b, b_spec)], [(0, 1, ta, tb, 0)], ex, outs,
               epi or _store(), [(tm, tn)], deps=deps)[0]


def _rms_fwd(name, x, g, deps=()):
    S, D = x.shape
    ts = _pick(S, (512, 256, 128))

    def body(x_ref, g_ref, *rest):
        h_ref = rest[-1]
        xv = x_ref[...]
        r = lax.rsqrt(jnp.mean(xv * xv, axis=-1, keepdims=True) + EPS)
        h_ref[...] = (xv * r * g_ref[...]).astype(BF)

    return pl.pallas_call(
        body, name=name, grid=(S // ts,),
        in_specs=[_spec((ts, D), lambda i: (i, 0)), _spec((1, D), lambda i: (0, 0))] + [_ANY] * len(deps),
        out_specs=_spec((ts, D), lambda i: (i, 0)), out_shape=_sd((S, D), BF),
        compiler_params=_params(("parallel",), 12 * ts * D * 4))(x, g, *deps)


def _rms_bwd(name, dh, x, g, dres, deps=()):
    S, D = x.shape
    ts = _pick(S, (256, 128))

    def body(dh_ref, x_ref, g_ref, dres_ref, *rest):
        dx_ref, dxb_ref, dg_ref = rest[len(deps):]
        xv = x_ref[...]
        dhv = dh_ref[...]
        r = lax.rsqrt(jnp.mean(xv * xv, axis=-1, keepdims=True) + EPS)
        xhat = xv * r
        dxh = dhv * g_ref[...]
        cm = jnp.mean(dxh * xhat, axis=-1, keepdims=True)
        dx = r * (dxh - xhat * cm) + dres_ref[...]
        dx_ref[...] = dx
        dxb_ref[...] = dx.astype(BF)

        @pl.when(pl.program_id(0) == 0)
        def _():
            dg_ref[...] = jnp.zeros_like(dg_ref)

        dg_ref[...] += jnp.sum(dhv * xhat, axis=0, keepdims=True)

    row = _spec((ts, D), lambda i: (i, 0))
    vec = _spec((1, D), lambda i: (0, 0))
    return pl.pallas_call(
        body, name=name, grid=(S // ts,),
        in_specs=[row, row, vec, row] + [_ANY] * len(deps), out_specs=[row, row, vec],
        out_shape=[_sd((S, D), F32), _sd((S, D), BF), _sd((1, D), F32)],
        compiler_params=_params(("arbitrary",), 20 * ts * D * 4))(dh, x, g, dres, *deps)


def _final_loss(name, x, g, target):
    S, D = x.shape
    ts = _pick(S, (256, 128))

    def body(x_ref, g_ref, t_ref, loss_ref, dx_ref, dxb_ref, dg_ref):
        xv = x_ref[...]
        gv = g_ref[...]
        r = lax.rsqrt(jnp.mean(xv * xv, axis=-1, keepdims=True) + EPS)
        xhat = xv * r
        err = xhat * gv - t_ref[...]
        part = 0.5 * jnp.sum(jnp.mean(err * err, axis=-1, keepdims=True), axis=0, keepdims=True)
        dy = err * (1.0 / D)
        dxh = dy * gv
        cm = jnp.mean(dxh * xhat, axis=-1, keepdims=True)
        dx = r * (dxh - xhat * cm)
        dx_ref[...] = dx
        dxb_ref[...] = dx.astype(BF)

        @pl.when(pl.program_id(0) == 0)
        def _():
            dg_ref[...] = jnp.zeros_like(dg_ref)
            loss_ref[...] = jnp.zeros_like(loss_ref)

        dg_ref[...] += jnp.sum(dy * xhat, axis=0, keepdims=True)
        loss_ref[...] += jnp.broadcast_to(part, loss_ref.shape)

    row = _spec((ts, D), lambda i: (i, 0))
    vec = _spec((1, D), lambda i: (0, 0))
    return pl.pallas_call(
        body, name=name, grid=(S // ts,),
        in_specs=[row, vec, row], out_specs=[_spec((1, LANES), lambda i: (0, 0)), row, row, vec],
        out_shape=[_sd((1, LANES), F32), _sd((S, D), F32), _sd((S, D), BF), _sd((1, D), F32)],
        compiler_params=_params(("arbitrary",), 20 * ts * D * 4))(x, g, target)


def _swiglu(gate, up):
    return gate * jax.nn.sigmoid(gate) * up


def _swiglu_bwd_epi(accs, ex, orefs, ids):
    da = 0.5 * accs[0]
    gate = ex[0][0].astype(F32)
    up = ex[0][1].astype(F32)
    sg = jax.nn.sigmoid(gate)
    orefs[0][0] = (da * up * (sg * (1.0 + gate * (1.0 - sg)))).astype(BF)
    orefs[0][1] = (da * gate * sg).astype(BF)


_GELU_C = math.sqrt(2.0 / math.pi)


def _gelu(x):
    return x * (0.5 * (1.0 + jnp.tanh(_GELU_C * (x + 0.044715 * (x * x * x)))))


def _gelu_grad(x):
    t = jnp.tanh(_GELU_C * (x + 0.044715 * (x * x * x)))
    return 0.5 * (1.0 + t) + x * (0.5 * (1.0 - t * t) * _GELU_C * (1.0 + 3.0 * 0.044715 * (x * x)))


def _causal_block_mask():
    row = lax.broadcasted_iota(jnp.int32, (SGU_BLOCK, SGU_BLOCK), 0) // CHUNK
    col = lax.broadcasted_iota(jnp.int32, (SGU_BLOCK, SGU_BLOCK), 1) // CHUNK
    return row >= col


def _sgu_mid_fwd(name, puv, gain, bias, w_sp, b_sp):
    _, S, W = puv.shape
    G = SGU_GROUPS
    C = W // G
    T = SGU_BLOCK

    def body(puv_ref, gain_ref, bias_ref, w_ref, b_ref, out_ref):
        mask = _causal_block_mask()
        v = _gelu(puv_ref[1])
        mu = jnp.mean(v, axis=-1, keepdims=True)
        vc = v - mu
        rs = lax.rsqrt(jnp.mean(vc * vc, axis=-1, keepdims=True) + EPS)
        vln = (vc * rs * gain_ref[...] + bias_ref[...]).astype(BF)
        for g in range(G):
            wg = jnp.where(mask, w_ref[g], 0.0).astype(BF)
            mixed = jnp.dot(wg, vln[:, g * C:(g + 1) * C], preferred_element_type=F32) + b_ref[g]
            out_ref[:, g * C:(g + 1) * C] = (_gelu(puv_ref[0, :, g * C:(g + 1) * C]) * mixed).astype(BF)

    return pl.pallas_call(
        body, name=name, grid=(S // T,),
        in_specs=[_spec((2, T, W), lambda i: (0, i, 0)), _spec((1, W), lambda i: (0, 0)), _spec((1, W), lambda i: (0, 0)),
                  _spec((G, T, T), lambda i: (0, 0, 0)), _spec((G, T, 1), lambda i: (0, 0, 0))],
        out_specs=_spec((T, W), lambda i: (i, 0)), out_shape=_sd((S, W), BF),
        compiler_params=_params(("parallel",), 16 * T * W * 4))(puv, gain, bias, w_sp, b_sp)


def _sgu_mid_bwd(name, puv, dgated, gain, bias, w_sp, b_sp):
    _, S, W = puv.shape
    G = SGU_GROUPS
    C = W // G
    T = SGU_BLOCK

    def body(puv_ref, dg_ref, gain_ref, bias_ref, w_ref, b_ref, dpuv_ref, dgain_ref, dbias_ref, dw_ref, db_ref, dvln_ref):
        @pl.when(pl.program_id(0) == 0)
        def _():
            dgain_ref[...] = jnp.zeros_like(dgain_ref)
            dbias_ref[...] = jnp.zeros_like(dbias_ref)
            dw_ref[...] = jnp.zeros_like(dw_ref)
            db_ref[...] = jnp.zeros_like(db_ref)

        mask = _causal_block_mask()
        pv = puv_ref[1]
        v = _gelu(pv)
        mu = jnp.mean(v, axis=-1, keepdims=True)
        vc = v - mu
        rs = lax.rsqrt(jnp.mean(vc * vc, axis=-1, keepdims=True) + EPS)
        vhat = vc * rs
        gain_v = gain_ref[...]
        vln = (vhat * gain_v + bias_ref[...]).astype(BF)
        for g in range(G):
            sl = slice(g * C, (g + 1) * C)
            wg = jnp.where(mask, w_ref[g], 0.0).astype(BF)
            vg = vln[:, sl]
            mixed = jnp.dot(wg, vg, preferred_element_type=F32) + b_ref[g]
            pu = puv_ref[0, :, sl]
            dgt = dg_ref[:, sl].astype(F32)
            dpuv_ref[0, :, sl] = (dgt * mixed * _gelu_grad(pu)).astype(BF)
            dmix = dgt * _gelu(pu)
            db_ref[g] += jnp.sum(dmix, axis=-1, keepdims=True)
            dmb = dmix.astype(BF)
            dwg = lax.dot_general(dmb, vg, (((1,), (1,)), ((), ())), preferred_element_type=F32)
            dw_ref[g] += jnp.where(mask, dwg, 0.0)
            dvln_ref[:, sl] = lax.dot_general(wg, dmb, (((0,), (0,)), ((), ())), preferred_element_type=F32)
        dvln = dvln_ref[...]
        dgain_ref[...] += jnp.sum(dvln * vhat, axis=0, keepdims=True)
        dbias_ref[...] += jnp.sum(dvln, axis=0, keepdims=True)
        dvh = dvln * gain_v
        m1 = jnp.mean(dvh, axis=-1, keepdims=True)
        m2 = jnp.mean(dvh * vhat, axis=-1, keepdims=True)
        dv = rs * (dvh - m1 - vhat * m2)
        dpuv_ref[1] = (dv * _gelu_grad(pv)).astype(BF)

    vec = _spec((1, W), lambda i: (0, 0))
    wsp = _spec((G, T, T), lambda i: (0, 0, 0))
    bsp = _spec((G, T, 1), lambda i: (0, 0, 0))
    return pl.pallas_call(
        body, name=name, grid=(S // T,),
        in_specs=[_spec((2, T, W), lambda i: (0, i, 0)), _spec((T, W), lambda i: (i, 0)), vec, vec, wsp, bsp],
        out_specs=[_spec((2, T, W), lambda i: (0, i, 0)), vec, vec, wsp, bsp],
        out_shape=[_sd((2, S, W), BF), _sd((1, W), F32), _sd((1, W), F32), _sd((G, T, T), F32), _sd((G, T, 1), F32)],
        scratch_shapes=[pltpu.VMEM((T, W), F32)],
        compiler_params=_params(("arbitrary",), 24 * T * W * 4))(puv, dgated, gain, bias, w_sp, b_sp)


def _rope_tables(positions):
    half = QK_ROPE // 2
    inv_freq = 1.0 / (ROPE_THETA ** (jnp.arange(half, dtype=F32) / half))
    ang = positions.astype(F32)[:, None] * inv_freq[None, :]
    cos, sin = jnp.cos(ang), jnp.sin(ang)
    z = jnp.zeros_like(cos)
    return (jnp.concatenate([cos, cos, z, z], axis=1), jnp.concatenate([-sin, z, z, z], axis=1),
            jnp.concatenate([z, sin, z, z], axis=1))


def _rope(x, cos, sa, sb):
    return x * cos + pltpu.roll(x, HEAD_PAD - QK_ROPE // 2, 1) * sa + pltpu.roll(x, QK_ROPE // 2, 1) * sb


def _rope_t(dy, cos, sa, sb):
    return dy * cos + pltpu.roll(dy * sa, QK_ROPE // 2, 1) + pltpu.roll(dy * sb, HEAD_PAD - QK_ROPE // 2, 1)


def _rms_rows(x, g):
    r = lax.rsqrt(jnp.mean(x * x, axis=-1, keepdims=True) + EPS)
    return x * r * g


def _rms_rows_bwd(dy, x, g):
    r = lax.rsqrt(jnp.mean(x * x, axis=-1, keepdims=True) + EPS)
    xhat = x * r
    dxh = dy * g
    cm = jnp.mean(dxh * xhat, axis=-1, keepdims=True)
    return r * (dxh - xhat * cm), jnp.sum(dy * xhat, axis=0, keepdims=True)


def _mla_mid_fwd(name, proj, qg, kvg, cos, sa, sb):
    S, P = proj.shape
    ts = _pick(S, (512, 256, 128))

    def body(p_ref, qg_ref, kvg_ref, cos_ref, sa_ref, sb_ref, lat_ref, kr_ref):
        lat_ref[0] = _rms_rows(p_ref[:, :Q_LORA], qg_ref[...]).astype(BF)
        lat_ref[1] = _rms_rows(p_ref[:, Q_LORA:Q_LORA + KV_LORA], kvg_ref[...]).astype(BF)
        kr_ref[...] = _rope(p_ref[:, Q_LORA + KV_LORA:], cos_ref[...], sa_ref[...], sb_ref[...]).astype(BF)

    tab = _spec((ts, HEAD_PAD), lambda i: (i, 0))
    return pl.pallas_call(
        body, name=name, grid=(S // ts,),
        in_specs=[_spec((ts, P), lambda i: (i, 0)), _spec((1, Q_LORA), lambda i: (0, 0)),
                  _spec((1, KV_LORA), lambda i: (0, 0)), tab, tab, tab],
        out_specs=[_spec((2, ts, Q_LORA), lambda i: (0, i, 0)), tab],
        out_shape=[_sd((2, S, Q_LORA), BF), _sd((S, HEAD_PAD), BF)],
        compiler_params=_params(("parallel",), 16 * ts * P * 4))(proj, qg, kvg, cos, sa, sb)


def _mla_mid_bwd(name, proj, dqn, dkvn, dkr_heads, qg, kvg, cos, sa, sb):
    S, P = proj.shape
    H = dkr_heads.shape[0]
    ts = _pick(S, (256, 128))

    def body(p_ref, dqn_ref, dkvn_ref, dkr_ref, qg_ref, kvg_ref, cos_ref, sa_ref, sb_ref, dp_ref, dqg_ref, dkvg_ref):
        @pl.when(pl.program_id(0) == 0)
        def _():
            dqg_ref[...] = jnp.zeros_like(dqg_ref)
            dkvg_ref[...] = jnp.zeros_like(dkvg_ref)

        dq, dqg = _rms_rows_bwd(dqn_ref[...], p_ref[:, :Q_LORA], qg_ref[...])
        dkv, dkvg = _rms_rows_bwd(dkvn_ref[...], p_ref[:, Q_LORA:Q_LORA + KV_LORA], kvg_ref[...])
        dqg_ref[...] += dqg
        dkvg_ref[...] += dkvg
        dkr = dkr_ref[0]
        for h in range(1, H):
            dkr = dkr + dkr_ref[h]
        dp_ref[:, :Q_LORA] = dq.astype(BF)
        dp_ref[:, Q_LORA:Q_LORA + KV_LORA] = dkv.astype(BF)
        dp_ref[:, Q_LORA + KV_LORA:] = _rope_t(dkr, cos_ref[...], sa_ref[...], sb_ref[...]).astype(BF)

    tab = _spec((ts, HEAD_PAD), lambda i: (i, 0))
    lat = _spec((ts, Q_LORA), lambda i: (i, 0))
    gq = _spec((1, Q_LORA), lambda i: (0, 0))
    return pl.pallas_call(
        body, name=name, grid=(S // ts,),
        in_specs=[_spec((ts, P), lambda i: (i, 0)), lat, lat, _spec((H, ts, HEAD_PAD), lambda i: (0, i, 0)),
                  gq, gq, tab, tab, tab],
        out_specs=[_spec((ts, P), lambda i: (i, 0)), gq, gq],
        out_shape=[_sd((S, P), BF), _sd((1, Q_LORA), F32), _sd((1, KV_LORA), F32)],
        compiler_params=_params(("arbitrary",), 24 * ts * P * 4))(proj, dqn, dkvn, dkr_heads, qg, kvg, cos, sa, sb)


def _attn_tile(S):
    return _pick(S, (512,)) if S >= 2048 else _pick(S, (128,))


def _diag_mask(t, transposed):
    q = lax.broadcasted_iota(jnp.int32, (t, t), 1 if transposed else 0) // CHUNK
    k = lax.broadcasted_iota(jnp.int32, (t, t), 0 if transposed else 1) // CHUNK
    return k <= q


_NT = (((1,), (1,)), ((), ()))


def _attn_fwd(name, q_all, kv_all, kr):
    _, S, HP = q_all.shape
    H = HP // HEAD_PAD
    t = _attn_tile(S)
    nq = S // t
    ng = ATTN_GROUPS
    tg = t // ng

    def body(q_ref, kv_ref, kr_ref, o_ref, lse_ref, kcat_ref):
        i = pl.program_id(1)

        @pl.when(i == 0)
        def _():
            kcat_ref[:, :HEAD_PAD] = kv_ref[0]
            kcat_ref[:, HEAD_PAD:] = kr_ref[...]

        qs = [jnp.concatenate([q_ref[0, g * tg:(g + 1) * tg], q_ref[1, g * tg:(g + 1) * tg]], axis=1) for g in range(ng)]

        def step(j, carry, masked):
            off = pl.multiple_of(j * t, t)
            kj = kcat_ref[pl.ds(off, t), :]
            vj = kv_ref[1, pl.ds(off, t), :]
            out = []
            for g in range(ng):
                m, l, acc = carry[g]
                s = lax.dot_general(qs[g], kj, _NT, preferred_element_type=F32)
                if masked:
                    s = jnp.where(_diag_mask(t, False)[g * tg:(g + 1) * tg], s, NEG)
                m2 = jnp.maximum(m, jnp.max(s, axis=-1, keepdims=True))
                al = jnp.exp(m - m2)
                p = jnp.exp(s - m2)
                l2 = al * l + jnp.sum(p, axis=-1, keepdims=True)
                acc2 = al * acc + jnp.dot(p.astype(BF), vj, preferred_element_type=F32)
                out.append((m2, l2, acc2))
            return tuple(out)

        init = tuple((jnp.full((tg, 1), NEG, F32), jnp.zeros((tg, 1), F32), jnp.zeros((tg, V_DIM), F32))
                     for _ in range(ng))
        def several(jj, c):
            for u in range(ATTN_UNROLL):
                c = step(jj * ATTN_UNROLL + u, c, False)
            return c

        carry = lax.fori_loop(0, i // ATTN_UNROLL, several, init)
        carry = lax.fori_loop((i // ATTN_UNROLL) * ATTN_UNROLL, i, lambda j, c: step(j, c, False), carry)
        carry = step(i, carry, True)
        for g in range(ng):
            m, l, acc = carry[g]
            o_ref[g * tg:(g + 1) * tg, :] = acc / l
            lse_ref[g * tg:(g + 1) * tg, :] = jnp.broadcast_to(m + jnp.log(l), (tg, LANES))

    return pl.pallas_call(
        body, name=name, grid=(H, nq),
        in_specs=[_spec((2, t, HEAD_PAD), lambda h, i: (0, i, h)), _spec((2, S, HEAD_PAD), lambda h, i: (0, 0, h)),
                  _spec((S, HEAD_PAD), lambda h, i: (0, 0))],
        out_specs=[_spec((t, HEAD_PAD), lambda h, i: (i, h)), _spec((None, t, LANES), lambda h, i: (h, i, 0))],
        out_shape=[_sd((S, HP), F32), _sd((H, S, LANES), F32)],
        scratch_shapes=[pltpu.VMEM((S, 2 * HEAD_PAD), BF)],
        compiler_params=_params(("parallel", "arbitrary"), 8 * S * HEAD_PAD * 2 + 24 * t * t * 4))(q_all, kv_all, kr)


def _attn_delta(name, do, o):
    S, HP = o.shape
    H = HP // HEAD_PAD
    ts = _pick(S, (512, 256, 128))

    def body(do_ref, o_ref, d_ref):
        d_ref[...] = jnp.broadcast_to(jnp.sum(do_ref[...] * o_ref[...], axis=-1, keepdims=True), (ts, LANES))

    tile = _spec((ts, HEAD_PAD), lambda h, i: (i, h))
    return pl.pallas_call(
        body, name=name, grid=(H, S // ts), in_specs=[tile, tile],
        out_specs=_spec((None, ts, LANES), lambda h, i: (h, i, 0)), out_shape=_sd((H, S, LANES), F32),
        compiler_params=_params(("parallel", "parallel"), VMEM_FLOOR))(do, o)


def _attn_dq(name, q_all, kv_all, kr, do, lse, delta, cos, sa, sb, scale):
    _, S, HP = q_all.shape
    H = HP // HEAD_PAD
    t = _attn_tile(S)
    nq = S // t

    def body(q_ref, kv_ref, kr_ref, do_ref, lse_ref, dl_ref, cos_ref, sa_ref, sb_ref, dq_ref, kcat_ref):
        i = pl.program_id(1)

        @pl.when(i == 0)
        def _():
            kcat_ref[:, :HEAD_PAD] = kv_ref[0]
            kcat_ref[:, HEAD_PAD:] = kr_ref[...]

        ng = ATTN_GROUPS
        tg = t // ng
        rows = [slice(g * tg, (g + 1) * tg) for g in range(ng)]
        qs = [jnp.concatenate([q_ref[0, r], q_ref[1, r]], axis=1) for r in rows]
        dobs = [do_ref[r, :].astype(BF) for r in rows]
        lses = [lse_ref[r, 0:1] for r in rows]
        dls = [dl_ref[r, 0:1] for r in rows]

        def step(j, dqs, masked):
            off = pl.multiple_of(j * t, t)
            kj = kcat_ref[pl.ds(off, t), :]
            vj = kv_ref[1, pl.ds(off, t), :]
            out = []
            for g in range(ng):
                s = lax.dot_general(qs[g], kj, _NT, preferred_element_type=F32)
                if masked:
                    s = jnp.where(_diag_mask(t, False)[rows[g]], s, NEG)
                p = jnp.exp(s - lses[g])
                dp = lax.dot_general(dobs[g], vj, _NT, preferred_element_type=F32)
                ds = (p * (dp - dls[g])).astype(BF)
                out.append(dqs[g] + jnp.dot(ds, kj, preferred_element_type=F32))
            return tuple(out)

        init = tuple(jnp.zeros((tg, 2 * HEAD_PAD), F32) for _ in range(ng))
        def several(jj, c):
            for u in range(ATTN_UNROLL):
                c = step(jj * ATTN_UNROLL + u, c, False)
            return c

        dqs = lax.fori_loop(0, i // ATTN_UNROLL, several, init)
        dqs = lax.fori_loop((i // ATTN_UNROLL) * ATTN_UNROLL, i, lambda j, c: step(j, c, False), dqs)
        dqs = step(i, dqs, True)
        for g in range(ng):
            dq_ref[0, rows[g]] = (dqs[g][:, :HEAD_PAD] * scale).astype(BF)
            dq_ref[1, rows[g]] = (_rope_t(dqs[g][:, HEAD_PAD:], cos_ref[rows[g], :], sa_ref[rows[g], :],
                                          sb_ref[rows[g], :]) * scale).astype(BF)

    tab = _spec((t, HEAD_PAD), lambda h, i: (i, 0))
    stat = _spec((None, t, LANES), lambda h, i: (h, i, 0))
    return pl.pallas_call(
        body, name=name, grid=(H, nq),
        in_specs=[_spec((2, t, HEAD_PAD), lambda h, i: (0, i, h)), _spec((2, S, HEAD_PAD), lambda h, i: (0, 0, h)),
                  _spec((S, HEAD_PAD), lambda h, i: (0, 0)), _spec((t, HEAD_PAD), lambda h, i: (i, h)), stat, stat,
                  tab, tab, tab],
        out_specs=_spec((2, t, HEAD_PAD), lambda h, i: (0, i, h)), out_shape=_sd((2, S, HP), BF),
        scratch_shapes=[pltpu.VMEM((S, 2 * HEAD_PAD), BF)],
        compiler_params=_params(("parallel", "arbitrary"), 8 * S * HEAD_PAD * 2 + 32 * t * t * 4))(
            q_all, kv_all, kr, do, lse, delta, cos, sa, sb)


def _attn_dkv(name, q_all, kv_all, kr, do, lse_row, delta_row):
    _, S, HP = q_all.shape
    H = HP // HEAD_PAD
    t = _attn_tile(S)
    nq = S // t

    def body(q_ref, kv_ref, kr_ref, do_ref, lse_ref, dl_ref, dkv_ref, dkr_ref, qcat_ref):
        j = pl.program_id(1)

        @pl.when(j == 0)
        def _():
            qcat_ref[:, :HEAD_PAD] = q_ref[0]
            qcat_ref[:, HEAD_PAD:] = q_ref[1]

        ng = ATTN_GROUPS
        tg = t // ng
        rows = [slice(g * tg, (g + 1) * tg) for g in range(ng)]
        kjs = [jnp.concatenate([kv_ref[0, r], kr_ref[r, :]], axis=1) for r in rows]
        vjs = [kv_ref[1, r] for r in rows]

        def step(i, carry, masked):
            off = pl.multiple_of(i * t, t)
            qi = qcat_ref[pl.ds(off, t), :]
            doi = do_ref[pl.ds(off, t), :].astype(BF)
            lse_i = lse_ref[i]
            dl_i = dl_ref[i]
            out = []
            for g in range(ng):
                dk, dv = carry[g]
                st = lax.dot_general(kjs[g], qi, _NT, preferred_element_type=F32)
                if masked:
                    st = jnp.where(_diag_mask(t, True)[rows[g]], st, NEG)
                pt = jnp.exp(st - lse_i)
                dv2 = dv + jnp.dot(pt.astype(BF), doi, preferred_element_type=F32)
                dpt = lax.dot_general(vjs[g], doi, _NT, preferred_element_type=F32)
                dst = (pt * (dpt - dl_i)).astype(BF)
                out.append((dk + jnp.dot(dst, qi, preferred_element_type=F32), dv2))
            return tuple(out)

        init = tuple((jnp.zeros((tg, 2 * HEAD_PAD), F32), jnp.zeros((tg, V_DIM), F32)) for _ in range(ng))
        def several(ii, c):
            for u in range(ATTN_UNROLL):
                c = step(j + 1 + ii * ATTN_UNROLL + u, c, False)
            return c

        carry = step(j, init, True)
        trips = (nq - 1 - j) // ATTN_UNROLL
        carry = lax.fori_loop(0, trips, several, carry)
        carry = lax.fori_loop(j + 1 + trips * ATTN_UNROLL, nq, lambda i, c: step(i, c, False), carry)
        for g in range(ng):
            dk, dv = carry[g]
            dkv_ref[0, rows[g]] = dk[:, :HEAD_PAD].astype(BF)
            dkv_ref[1, rows[g]] = dv.astype(BF)
            dkr_ref[rows[g], :] = dk[:, HEAD_PAD:]

    stat = _spec((None, nq, 1, t), lambda h, j: (h, 0, 0, 0))
    return pl.pallas_call(
        body, name=name, grid=(H, nq),
        in_specs=[_spec((2, S, HEAD_PAD), lambda h, j: (0, 0, h)), _spec((2, t, HEAD_PAD), lambda h, j: (0, j, h)),
                  _spec((t, HEAD_PAD), lambda h, j: (j, 0)), _spec((S, HEAD_PAD), lambda h, j: (0, h)), stat, stat],
        out_specs=[_spec((2, t, HEAD_PAD), lambda h, j: (0, j, h)), _spec((None, t, HEAD_PAD), lambda h, j: (h, j, 0))],
        out_shape=[_sd((2, S, HP), BF), _sd((H, S, HEAD_PAD), F32)],
        scratch_shapes=[pltpu.VMEM((S, 2 * HEAD_PAD), BF)],
        compiler_params=_params(("parallel", "arbitrary"), 8 * S * HEAD_PAD * 4 + 32 * t * t * 4))(
            q_all, kv_all, kr, do, lse_row, delta_row)


def _place():
    x, y, c = lax.axis_index("x"), lax.axis_index("y"), lax.axis_index("c")
    return x, y, c


def _all_gather(blocks):
    n = len(blocks)

    def body(*refs):
        ins, outs = refs[:n], refs[n:2 * n]
        send_sems, recv_sems, local_sems = refs[2 * n:]
        x, y, c = _place()
        me = 4 * x + 2 * y + c
        sibling = (x, y, 1 - c)
        chips = [(1 - x, y), (x, 1 - y), (1 - x, 1 - y)]

        def slab(a, px, py, pc):
            return outs[a].at[4 * px + 2 * py + pc]

        def copy(a, k, src, dst, to):
            return pltpu.make_async_remote_copy(src_ref=src, dst_ref=dst, send_sem=send_sems.at[a, k],
                                                recv_sem=recv_sems.at[a, k], device_id=to, device_id_type=MESH)

        local = [pltpu.make_async_copy(ins[a], outs[a].at[me], local_sems.at[a]) for a in range(n)]
        for cp in local:
            cp.start()
        sends = []
        for a in range(n):
            mine = slab(a, x, y, c)
            sends.append(copy(a, 0, ins[a], mine, sibling))
            for j, chip in enumerate(chips):
                sends.append(copy(a, 1 + j, ins[a], mine, (*chip, c)))
        for cp in sends:
            cp.start()
        for j, chip in enumerate(chips):
            for a in range(n):
                got = slab(a, *chip, c)
                copy(a, 1 + j, got, got, (x, y, c)).wait_recv()
                fwd = copy(a, 4 + j, got, got, sibling)
                fwd.start()
                sends.append(fwd)
        for a in range(n):
            got = slab(a, x, y, 1 - c)
            copy(a, 0, got, got, (x, y, c)).wait_recv()
            for j, chip in enumerate(chips):
                got = slab(a, *chip, 1 - c)
                copy(a, 4 + j, got, got, (x, y, c)).wait_recv()
        for cp in sends:
            cp.wait_send()
        for cp in local:
            cp.wait()

    return pl.pallas_call(
        body, name="weights_all_gather", in_specs=[_ANY] * n, out_specs=[_ANY] * n,
        out_shape=[_sd((N_DEV,) + b.shape, b.dtype) for b in blocks],
        scratch_shapes=[pltpu.SemaphoreType.DMA((n, 7)), pltpu.SemaphoreType.DMA((n, 7)), pltpu.SemaphoreType.DMA((n,))],
    )(*blocks)


_HBM = pl.BlockSpec(memory_space=pltpu.HBM)
_SEM = pl.BlockSpec(memory_space=pltpu.SEMAPHORE)
_EFFECT = pltpu.SideEffectType.DATAFLOW_SIDE_EFFECTING


def _peers():
    x, y, c = _place()
    out = []
    for m in range(1, N_DEV):
        px, py, pc = x ^ (m >> 2), y ^ ((m >> 1) & 1), c ^ (m & 1)
        out.append((m, (px, py, pc), 4 * px + 2 * py + pc))
    return 4 * x + 2 * y + c, out


def _exchange_copies(src, land, send_sem, recv_sem, gather):
    me, peers = _peers()
    cps = []
    for a in range(len(src)):
        for m, pos, idx in peers:
            s_ref, d_ref = (src[a], land[a].at[me]) if gather else (src[a].at[idx], land[a].at[m - 1])
            k = a * (N_DEV - 1) + m - 1
            cps.append(pltpu.make_async_remote_copy(src_ref=s_ref, dst_ref=d_ref, send_sem=send_sem.at[k],
                                                    recv_sem=recv_sem.at[k], device_id=pos, device_id_type=MESH))
    return cps


def _xstart(name, srcs, gather, after=()):
    n = len(srcs)
    if gather:
        land_shapes = [(N_DEV,) + s.shape for s in srcs]
    else:
        land_shapes = [(N_DEV - 1,) + s.shape[1:] for s in srcs]

    def body(*refs):
        src, land = refs[:n], refs[n:2 * n]
        send_sem, recv_sem = refs[2 * n + len(after)], refs[2 * n + len(after) + 1]
        token = refs[-1]
        for cp in _exchange_copies(src, land, send_sem, recv_sem, gather):
            cp.start()
        token[...] = jnp.zeros_like(token)

    sem = pltpu.SemaphoreType.DMA((n * (N_DEV - 1),))
    out_shape = ([sem, sem] + [pltpu.HBM(s.shape, s.dtype) for s in srcs]
                 + [pltpu.HBM(sh, s.dtype) for sh, s in zip(land_shapes, srcs)] + [_sd((8, LANES), F32)])
    args = [pltpu.with_memory_space_constraint(s, pltpu.HBM) for s in srcs]
    args += [pltpu.with_memory_space_constraint(lax.empty(sh, s.dtype), pltpu.HBM) for sh, s in zip(land_shapes, srcs)]
    res = pl.pallas_call(
        body, name=name, out_shape=out_shape, in_specs=[_HBM] * (2 * n) + [_ANY] * len(after),
        out_specs=[_SEM, _SEM] + [_HBM] * (2 * n) + [pl.BlockSpec(memory_space=pltpu.VMEM)],
        input_output_aliases={i: 2 + i for i in range(2 * n)},
        compiler_params=pltpu.CompilerParams(has_side_effects=_EFFECT))(*args, *after)
    return dict(send=res[0], recv=res[1], srcs=list(res[2:2 + n]), lands=list(res[2 + n:2 + 2 * n]), token=res[-1])


def _xwait(name, st, after, gather):
    n = len(st["srcs"])

    def body(*refs):
        src, land = refs[:n], refs[n:2 * n]
        send_sem, recv_sem = refs[2 * n], refs[2 * n + 1]
        for cp in _exchange_copies(src, land, send_sem, recv_sem, gather):
            cp.wait_send()
            cp.wait_recv()

    arrays = st["srcs"] + st["lands"]
    res = pl.pallas_call(
        body, name=name, out_shape=[pltpu.HBM(a.shape, a.dtype) for a in arrays],
        in_specs=[_HBM] * (2 * n) + [_SEM, _SEM, _ANY], out_specs=[_HBM] * (2 * n),
        input_output_aliases={i: i for i in range(2 * n)},
        compiler_params=pltpu.CompilerParams(has_side_effects=_EFFECT))(*arrays, st["send"], st["recv"], after)
    return list(res[:n]), list(res[n:])


def _put_own(name, land, block, me_arr):
    R, C = block.shape
    tr = _pick(R, (512, 256, 128, 64, 32, 16))

    def body(me_ref, b_ref, land_ref, o_ref):
        o_ref[...] = b_ref[...]

    return pl.pallas_call(
        body, name=name, out_shape=_sd(land.shape, land.dtype),
        grid_spec=pltpu.PrefetchScalarGridSpec(
            num_scalar_prefetch=1, grid=(R // tr,),
            in_specs=[_spec((tr, C), lambda i, me_ref: (i, 0)), _ANY],
            out_specs=_spec((None, tr, C), lambda i, me_ref: (me_ref[0], i, 0))),
        input_output_aliases={2: 0},
        compiler_params=_params(("parallel",), 8 * tr * C * 2))(me_arr, block, land)


def _all_reduce_small(name, part):
    R = part.shape[0]

    def body(p_ref, out_ref, gath_ref, send_sems, recv_sems):
        x, y, c = _place()
        me = 4 * x + 2 * y + c
        gath_ref[me] = p_ref[...]
        cps = []
        for m in range(1, N_DEV):
            to = (x ^ (m >> 2), y ^ ((m >> 1) & 1), c ^ (m & 1))
            cps.append(pltpu.make_async_remote_copy(
                src_ref=p_ref, dst_ref=gath_ref.at[me], send_sem=send_sems.at[m - 1], recv_sem=recv_sems.at[m - 1],
                device_id=to, device_id_type=MESH))
        for cp in cps:
            cp.start()
        for m in range(1, N_DEV):
            frm = 4 * (x ^ (m >> 2)) + 2 * (y ^ ((m >> 1) & 1)) + (c ^ (m & 1))
            pltpu.make_async_remote_copy(
                src_ref=p_ref, dst_ref=gath_ref.at[frm], send_sem=send_sems.at[m - 1], recv_sem=recv_sems.at[m - 1],
                device_id=(x, y, c), device_id_type=MESH).wait_recv()
        for cp in cps:
            cp.wait_send()
        tot = gath_ref[0]
        for k in range(1, N_DEV):
            tot = tot + gath_ref[k]
        out_ref[...] = tot

    vm = pl.BlockSpec(memory_space=pltpu.VMEM)
    return pl.pallas_call(
        body, name=name, in_specs=[vm], out_specs=vm, out_shape=_sd((R, LANES), F32),
        scratch_shapes=[pltpu.VMEM((N_DEV, R, LANES), F32), pltpu.SemaphoreType.DMA((N_DEV - 1,)),
                        pltpu.SemaphoreType.DMA((N_DEV - 1,))],
        compiler_params=pltpu.CompilerParams(vmem_limit_bytes=VMEM_FLOOR),
    )(part)


def _adam_math(w, g, m, v):
    m2 = ADAM_B1 * m + (1.0 - ADAM_B1) * g
    v2 = ADAM_B2 * v + (1.0 - ADAM_B2) * (g * g)
    m_hat = m2 / (1.0 - ADAM_B1 ** ADAM_STEP)
    v_hat = v2 / (1.0 - ADAM_B2 ** ADAM_STEP)
    delta = -ADAM_LR * (m_hat / (jnp.sqrt(v_hat) + ADAM_EPS) + ADAM_WD * w)
    return delta, m2, v2


def _adamw_sharded(name, lands, fulls, me_arr, w, m, v):
    n_l = len(lands)
    R, C = lands[0].shape[1], lands[0].shape[2]
    tr = _pick(R, (128, 64, 32, 16, 8))
    nr = R // tr

    def body(me_ref, *refs):
        land_refs, own_refs = refs[:n_l], refs[n_l:2 * n_l]
        w_ref, m_ref, v_ref, g_ref, d_ref, m2_ref, v2_ref = refs[2 * n_l:]
        layer = pl.program_id(0)
        for ll in range(n_l):
            @pl.when(layer == ll)
            def _(ll=ll):
                g = own_refs[ll][...].astype(F32)
                for j in range(N_DEV - 1):
                    g = g + land_refs[ll][j].astype(F32)
                delta, m2, v2 = _adam_math(w_ref[...], g, m_ref[...], v_ref[...])
                g_ref[...] = g
                d_ref[...] = delta
                m2_ref[...] = m2
                v2_ref[...] = v2

    def row_of(ll):
        return lambda l, i, me_ref: jnp.where(l == ll, i, 0)

    in_specs = [_spec((N_DEV - 1, tr, C), lambda l, i, me_ref, f=row_of(ll): (0, f(l, i, me_ref), 0)) for ll in range(n_l)]
    in_specs += [_spec((None, tr, C), lambda l, i, me_ref, f=row_of(ll): (me_ref[0], f(l, i, me_ref), 0))
                 for ll in range(n_l)]
    blk = _spec((tr, C), lambda l, i, me_ref: (l * nr + i, 0))
    return pl.pallas_call(
        body, name=name, out_shape=[_sd(w.shape, F32)] * 4,
        grid_spec=pltpu.PrefetchScalarGridSpec(num_scalar_prefetch=1, grid=(n_l, nr), in_specs=in_specs + [blk] * 3,
                                               out_specs=[blk] * 4),
        compiler_params=_params(("parallel", "parallel"), (4 * n_l * N_DEV + 40) * tr * C * 4))(
            me_arr, *lands, *fulls, w, m, v)


def _adamw_packed(name, w, g, m, v):
    R = w.shape[0]

    def body(w_ref, g_ref, m_ref, v_ref, d_ref, m2_ref, v2_ref):
        delta, m2, v2 = _adam_math(w_ref[...], g_ref[...], m_ref[...], v_ref[...])
        d_ref[...] = delta
        m2_ref[...] = m2
        v2_ref[...] = v2

    vm = pl.BlockSpec(memory_space=pltpu.VMEM)
    return pl.pallas_call(
        body, name=name, in_specs=[vm] * 4, out_specs=[vm] * 3, out_shape=[_sd((R, LANES), F32)] * 3,
        compiler_params=pltpu.CompilerParams(vmem_limit_bytes=VMEM_FLOOR))(w, g, m, v)


def _pack(arrays):
    flat = jnp.concatenate([a.reshape(-1).astype(F32) for a in arrays])
    pad = (-flat.shape[0]) % (8 * LANES)
    return jnp.pad(flat, (0, pad)).reshape(-1, LANES)


def _unpack(packed, like):
    flat = packed.reshape(-1)
    out, pos = [], 0
    for a in like:
        n = math.prod(a.shape)
        out.append(flat[pos:pos + n].reshape(a.shape))
        pos += n
    return out


def _ffn_fwd(tag, x, gain, w_in_sm, w_out_of, deps=()):
    h = _rms_fwd(tag + "_norm", x, gain, deps)
    gu, act = _pair_in(tag + "_in", h, w_in_sm, BF, _swiglu)
    x_new = _rows_out(tag + "_out", act, w_out_of(act), x, 0.5)
    return x_new, (x, h, gu, act)


def _ffn_bwd(tag, d, d_bf, saved, gain, w_in_sm, w_out_sm, send_grads):
    x, h, gu, act = saved
    dgu = _rows_dact(tag + "_dact", d_bf, w_out_sm, [gu], [_sd(gu.shape, BF)], _swiglu_bwd_epi)[0]
    g_out = _rows_wgrad(tag + "_wgrad_out", act, d_bf, 0.5)
    g_in = _cols_wgrad(tag + "_wgrad_in", h, dgu)
    token = send_grads(g_in, g_out)
    dh = _cols_dh(tag + "_dh", dgu, w_in_sm, deps=(token,))
    dx, dx_bf, dgain = _rms_bwd(tag + "_dnorm", dh, x, gain, d)
    return dx, dx_bf, dgain


def kernel(x, positions, ln_ffn1, ffn1_w_in, ffn1_w_out, ln_mix, ln_ffn2, ffn2_w_in, ffn2_w_out, sgu_w_in, sgu_v_gain, sgu_v_bias, sgu_w_spatial, sgu_b_spatial, sgu_w_out, mla_w_in, mla_q_norm, mla_w_q_up, mla_kv_norm, mla_w_kv_up, mla_w_out, ln_final, loss_target, m_ln_ffn1, m_ffn1_w_in, m_ffn1_w_out, m_ln_mix, m_ln_ffn2, m_ffn2_w_in, m_ffn2_w_out, m_sgu_w_in, m_sgu_v_gain, m_sgu_v_bias, m_sgu_w_spatial, m_sgu_b_spatial, m_sgu_w_out, m_mla_w_in, m_mla_q_norm, m_mla_w_q_up, m_mla_kv_norm, m_mla_w_kv_up, m_mla_w_out, m_ln_final, v_ln_ffn1, v_ffn1_w_in, v_ffn1_w_out, v_ln_mix, v_ln_ffn2, v_ffn2_w_in, v_ffn2_w_out, v_sgu_w_in, v_sgu_v_gain, v_sgu_v_bias, v_sgu_w_spatial, v_sgu_b_spatial, v_sgu_w_out, v_mla_w_in, v_mla_q_norm, v_mla_w_q_up, v_mla_kv_norm, v_mla_w_kv_up, v_mla_w_out, v_ln_final):
    S, D = x.shape[1], x.shape[2]
    L = ln_ffn1.shape[0]
    H = mla_w_q_up.shape[-1] * N_DEV // (QK_NOPE + QK_ROPE)
    xi, yi, ci = _place()
    me = 4 * xi + 2 * yi + ci
    me_arr = jnp.reshape(me, (1,)).astype(jnp.int32)
    big = dict(ffn1_w_in=ffn1_w_in, ffn1_w_out=ffn1_w_out, ffn2_w_in=ffn2_w_in, ffn2_w_out=ffn2_w_out,
               sgu_w_in=sgu_w_in, sgu_w_out=sgu_w_out, mla_w_in=mla_w_in, mla_w_q_up=mla_w_q_up,
               mla_w_kv_up=mla_w_kv_up, mla_w_out=mla_w_out)
    big_m = dict(ffn1_w_in=m_ffn1_w_in, ffn1_w_out=m_ffn1_w_out, ffn2_w_in=m_ffn2_w_in, ffn2_w_out=m_ffn2_w_out,
                 sgu_w_in=m_sgu_w_in, sgu_w_out=m_sgu_w_out, mla_w_in=m_mla_w_in, mla_w_q_up=m_mla_w_q_up,
                 mla_w_kv_up=m_mla_w_kv_up, mla_w_out=m_mla_w_out)
    big_v = dict(ffn1_w_in=v_ffn1_w_in, ffn1_w_out=v_ffn1_w_out, ffn2_w_in=v_ffn2_w_in, ffn2_w_out=v_ffn2_w_out,
                 sgu_w_in=v_sgu_w_in, sgu_w_out=v_sgu_w_out, mla_w_in=v_mla_w_in, mla_w_q_up=v_mla_w_q_up,
                 mla_w_kv_up=v_mla_w_kv_up, mla_w_out=v_mla_w_out)
    names = list(big)
    mla_names = ["mla_w_in", "mla_w_q_up", "mla_w_kv_up", "mla_w_out"]

    blocks = {(k, l): big[k][l].astype(BF) for k in names for l in range(big[k].shape[0])}
    first = ("ffn1_w_in", 0)
    groups = {}
    for i in range(L):
        if i > 0:
            groups[f"ffn1_in_{i}"] = [("ffn1_w_in", i)]
        groups[f"ffn1_out_{i}"] = [("ffn1_w_out", i)]
        groups[f"mix_{i}"] = [("sgu_w_in", i // 2), ("sgu_w_out", i // 2)] if i % 2 == 0 else [(k, i // 2) for k in mla_names]
        groups[f"ffn2_{i}"] = [("ffn2_w_in", i), ("ffn2_w_out", i)]
    gathered = {first: _all_gather([blocks[first]])[0]}
    started, order_after = {}, (gathered[first],)
    for tag, grp in groups.items():
        started[tag] = _xstart(f"gather_start_{tag}", [blocks[k] for k in grp], True, order_after)
        order_after = (started[tag]["token"],)

    def fetch(tag, after):
        own, lands = _xwait(f"gather_wait_{tag}", started[tag], after, True)
        for k, blk, land in zip(groups[tag], own, lands):
            gathered[k] = _put_own(f"gather_own_{k[0]}_{k[1]}", land, blk, me_arr)

    def w_out_of(name, tag):
        def get(after):
            fetch(tag, after)
            return gathered[name]
        return get

    norm_rows = jnp.zeros((N_DEV, LANES), F32)
    mine = jnp.concatenate([mla_q_norm[0], mla_kv_norm[0]])
    norm_rows = lax.dynamic_update_slice(norm_rows, mine[None, :], (me, 0))
    norm_all = _all_reduce_small("norm_gains_gather", norm_rows)
    nq_sh = mla_q_norm.shape[1]
    q_gain = norm_all[:, :nq_sh].reshape(1, Q_LORA)
    kv_gain = norm_all[:, nq_sh:2 * nq_sh].reshape(1, KV_LORA)
    cos, sa, sb = _rope_tables(positions[0])
    scale = float((QK_NOPE + QK_ROPE) ** -0.5)

    xs = x[0]
    w_sp = sgu_w_spatial[0]
    b_sp = sgu_b_spatial[0][:, :, None]
    saved = []
    mla_w = {}
    deps = order_after
    for i in range(L):
        if i > 0:
            fetch(f"ffn1_in_{i}", xs)
        xs, s1 = _ffn_fwd(f"l{i}_ffn1", xs, ln_ffn1[i:i + 1], gathered[("ffn1_w_in", i)],
                          w_out_of(("ffn1_w_out", i), f"ffn1_out_{i}"), deps)
        deps = ()
        fetch(f"mix_{i}", xs)
        x_mix = xs
        h = _rms_fwd(f"l{i}_mix_norm", xs, ln_mix[i:i + 1])
        j = i // 2
        if i % 2 == 0:
            puv = _pair_in(f"l{i}_sgu_in", h, gathered[("sgu_w_in", j)], F32, None)[0]
            gated = _sgu_mid_fwd(f"l{i}_sgu_mid", puv, sgu_v_gain, sgu_v_bias, w_sp, b_sp)
            xs = _rows_out(f"l{i}_sgu_out", gated, gathered[("sgu_w_out", j)], xs, 1.0)
            sm = (x_mix, h, puv, gated)
        else:
            w_in_nat = gathered[("mla_w_in", j)].reshape(D, Q_LORA + KV_LORA + QK_ROPE)
            w_in_pad = jnp.pad(w_in_nat, ((0, 0), (0, HEAD_PAD - QK_ROPE)))
            wq_nat = jnp.transpose(gathered[("mla_w_q_up", j)], (1, 0, 2)).reshape(Q_LORA, H, QK_NOPE + QK_ROPE)
            wq_t = jnp.stack([wq_nat[:, :, :QK_NOPE].reshape(Q_LORA, H * HEAD_PAD),
                              jnp.pad(wq_nat[:, :, QK_NOPE:], ((0, 0), (0, 0), (0, HEAD_PAD - QK_ROPE))).reshape(
                                  Q_LORA, H * HEAD_PAD)])
            wkv_nat = jnp.transpose(gathered[("mla_w_kv_up", j)], (1, 0, 2)).reshape(KV_LORA, H, QK_NOPE + V_DIM)
            wkv_t = jnp.stack([wkv_nat[:, :, :QK_NOPE].reshape(KV_LORA, H * HEAD_PAD),
                               wkv_nat[:, :, QK_NOPE:].reshape(KV_LORA, H * HEAD_PAD)])
            w_o_nat = gathered[("mla_w_out", j)].reshape(H * V_DIM, D)
            mla_w[i] = (w_in_pad, wq_t, wkv_t, w_o_nat)
            proj = _mm2(f"l{i}_mla_in", h, w_in_pad, False, False, F32, tn_cands=(384, 128))
            lat, kr = _mla_mid_fwd(f"l{i}_mla_mid", proj, q_gain, kv_gain, cos, sa, sb)

            def q_epi(accs, ex, orefs, ids):
                orefs[0][...] = (accs[0] * scale).astype(BF)

            def qr_epi(accs, ex, orefs, ids):
                for hh in range(accs[0].shape[1] // HEAD_PAD):
                    sl = slice(hh * HEAD_PAD, (hh + 1) * HEAD_PAD)
                    orefs[0][:, sl] = (_rope(accs[0][:, sl], *ex) * scale).astype(BF)

            q_nope = _mm2(f"l{i}_mla_q", lat[0], wq_t[0], False, False, BF, epi=q_epi)
            q_rope = _mm2(f"l{i}_mla_qr", lat[0], wq_t[1], False, False, BF, epi=qr_epi, extras=(cos, sa, sb))
            q_all = jnp.stack([q_nope, q_rope])
            kv_all = jnp.stack([_mm2(f"l{i}_mla_k", lat[1], wkv_t[0], False, False, BF),
                                _mm2(f"l{i}_mla_v", lat[1], wkv_t[1], False, False, BF)])
            o, lse = _attn_fwd(f"l{i}_attn", q_all, kv_all, kr)
            xs = _mm2(f"l{i}_mla_out", o, w_o_nat, False, False, F32, res=xs)
            sm = (x_mix, h, proj, lat, kr, q_all, kv_all, o, lse)
        fetch(f"ffn2_{i}", xs)
        xs, s2 = _ffn_fwd(f"l{i}_ffn2", xs, ln_ffn2[i:i + 1], gathered[("ffn2_w_in", i)],
                          lambda after, i=i: gathered[("ffn2_w_out", i)])
        saved.append((s1, sm, s2))

    loss_row, d, d_bf, g_ln_final = _final_loss("final_loss", xs, ln_final[None, :], loss_target[0])
    loss = lax.psum(loss_row[0, 0], ("x", "y", "c"))

    sent = []

    def send(tag, keys, grads):
        st = _xstart(f"scatter_start_{tag}", grads, False)
        sent.append((tag, keys, st))
        return st["token"]

    g_ln1, g_ln2, g_lnm = [None] * L, [None] * L, [None] * L
    small_g = {}
    for i in reversed(range(L)):
        s1, sm, s2 = saved[i]
        d, d_bf, g_ln2[i] = _ffn_bwd(
            f"l{i}_ffn2", d, d_bf, s2, ln_ffn2[i:i + 1], gathered[("ffn2_w_in", i)], gathered[("ffn2_w_out", i)],
            lambda g_in, g_out, i=i: send(f"l{i}_ffn2", [("ffn2_w_in", i), ("ffn2_w_out", i)], [g_in, g_out]))
        j = i // 2
        if i % 2 == 0:
            x_mix, h, puv, gated = sm
            dgated = _rows_dact(f"l{i}_sgu_dgated", d_bf, gathered[("sgu_w_out", j)], [], [_sd(gated.shape, BF)], _store())[0]
            g_so = _rows_wgrad(f"l{i}_sgu_wgrad_out", gated, d_bf, None)
            dpuv, dgain, dbias, dwsp, dbsp = _sgu_mid_bwd(f"l{i}_sgu_mid_bwd", puv, dgated, sgu_v_gain, sgu_v_bias,
                                                          w_sp, b_sp)
            small_g.update(sgu_v_gain=dgain, sgu_v_bias=dbias, sgu_w_spatial=dwsp[None], sgu_b_spatial=dbsp[None, :, :, 0])
            g_si = _cols_wgrad(f"l{i}_sgu_wgrad_in", h, dpuv)
            token = send(f"l{i}_sgu", [("sgu_w_in", j), ("sgu_w_out", j)], [g_si, g_so])
            dh = _cols_dh(f"l{i}_sgu_dh", dpuv, gathered[("sgu_w_in", j)], deps=(token,))
        else:
            x_mix, h, proj, lat, kr, q_all, kv_all, o, lse = sm
            w_in_pad, wq_t, wkv_t, w_o_nat = mla_w[i]
            t = _attn_tile(S)
            do = _mm2(f"l{i}_mla_do", d_bf, w_o_nat, False, True, F32)
            g_wo = _mm2(f"l{i}_mla_wgrad_out", o, d_bf, True, False, BF)
            delta = _attn_delta(f"l{i}_attn_delta", do, o)
            dq_all = _attn_dq(f"l{i}_attn_dq", q_all, kv_all, kr, do, lse, delta, cos, sa, sb, scale)
            lse_row = lse[:, :, 0].reshape(H, S // t, 1, t)
            delta_row = delta[:, :, 0].reshape(H, S // t, 1, t)
            dkv_all, dkr_heads = _attn_dkv(f"l{i}_attn_dkv", q_all, kv_all, kr, do, lse_row, delta_row)
            dqn = _mm2(f"l{i}_mla_dqn", dq_all[0], wq_t[0], False, True, F32)
            dqn = _mm2(f"l{i}_mla_dqn2", dq_all[1], wq_t[1], False, True, F32, res=dqn)
            dkvn = _mm2(f"l{i}_mla_dkvn", dkv_all[0], wkv_t[0], False, True, F32)
            dkvn = _mm2(f"l{i}_mla_dkvn2", dkv_all[1], wkv_t[1], False, True, F32, res=dkvn)
            g_wq = [_mm2(f"l{i}_mla_wgrad_q{t2}", lat[0], dq_all[t2], True, False, BF) for t2 in range(2)]
            g_wkv = [_mm2(f"l{i}_mla_wgrad_kv{t2}", lat[1], dkv_all[t2], True, False, BF) for t2 in range(2)]
            dproj, g_qn, g_kvn = _mla_mid_bwd(f"l{i}_mla_mid_bwd", proj, dqn, dkvn, dkr_heads, q_gain, kv_gain, cos, sa, sb)
            g_win = _mm2(f"l{i}_mla_wgrad_in", h, dproj, True, False, BF, tn_cands=(384, 128))
            n_in = Q_LORA + KV_LORA + QK_ROPE
            gq_nat = jnp.concatenate([g_wq[0].reshape(Q_LORA, H, HEAD_PAD),
                                      g_wq[1].reshape(Q_LORA, H, HEAD_PAD)[:, :, :QK_ROPE]], axis=2)
            gkv_nat = jnp.concatenate([g_wkv[0].reshape(KV_LORA, H, HEAD_PAD), g_wkv[1].reshape(KV_LORA, H, HEAD_PAD)], axis=2)
            token = send(f"l{i}_mla", [(k, j) for k in mla_names],
                         [g_win[:, :n_in].reshape(N_DEV, D // N_DEV, n_in),
                          jnp.transpose(gq_nat.reshape(Q_LORA, N_DEV, -1), (1, 0, 2)),
                          jnp.transpose(gkv_nat.reshape(KV_LORA, N_DEV, -1), (1, 0, 2)),
                          g_wo.reshape(N_DEV, H * V_DIM // N_DEV, D)])
            small_g.update(mla_q_norm=g_qn, mla_kv_norm=g_kvn)
            dh = _mm2(f"l{i}_mla_dh", dproj, w_in_pad, False, True, F32, tn_cands=(512, 256, 128), deps=(token,))
        d, d_bf, g_lnm[i] = _rms_bwd(f"l{i}_mix_dnorm", dh, x_mix, ln_mix[i:i + 1], d)
        d, d_bf, g_ln1[i] = _ffn_bwd(
            f"l{i}_ffn1", d, d_bf, s1, ln_ffn1[i:i + 1], gathered[("ffn1_w_in", i)], gathered[("ffn1_w_out", i)],
            lambda g_in, g_out, i=i: send(f"l{i}_ffn1", [("ffn1_w_in", i), ("ffn1_w_out", i)], [g_in, g_out]))
    grad_x = d[None]

    landed, partial = {}, {}
    for tag, keys, st in sent:
        fulls, lands = _xwait(f"scatter_wait_{tag}", st, d, False)
        for k, full, land in zip(keys, fulls, lands):
            partial[k], landed[k] = full, land
    big_out = {}
    for k in names:
        w = big[k]
        rc = (math.prod(w.shape[1:-1]), w.shape[-1])
        flat = (w.shape[0] * rc[0], rc[1])
        lands = [landed[(k, l)].reshape((N_DEV - 1,) + rc) for l in range(w.shape[0])]
        fulls = [partial[(k, l)].reshape((N_DEV,) + rc) for l in range(w.shape[0])]
        res = _adamw_sharded(f"adamw_{k}", lands, fulls, me_arr, w.reshape(flat), big_m[k].reshape(flat),
                             big_v[k].reshape(flat))
        big_out[k] = [r.reshape(w.shape) for r in res]

    small_g.update(ln_ffn1=jnp.concatenate(g_ln1), ln_mix=jnp.concatenate(g_lnm), ln_ffn2=jnp.concatenate(g_ln2),
                   ln_final=g_ln_final[0])
    small_names = ["ln_ffn1", "ln_mix", "ln_ffn2", "sgu_v_gain", "sgu_v_bias", "sgu_w_spatial", "sgu_b_spatial",
                   "ln_final", "mla_q_norm", "mla_kv_norm"]
    summed = _unpack(_all_reduce_small("small_grads_all_reduce", _pack([small_g[k] for k in small_names])),
                     [small_g[k] for k in small_names])
    small_grad = dict(zip(small_names, summed))
    for k in ("mla_q_norm", "mla_kv_norm"):
        small_grad[k] = lax.dynamic_slice(small_grad[k], (0, me * nq_sh), (1, nq_sh))
    small_w = dict(ln_ffn1=ln_ffn1, ln_mix=ln_mix, ln_ffn2=ln_ffn2, sgu_v_gain=sgu_v_gain, sgu_v_bias=sgu_v_bias,
                   sgu_w_spatial=sgu_w_spatial, sgu_b_spatial=sgu_b_spatial, ln_final=ln_final, mla_q_norm=mla_q_norm,
                   mla_kv_norm=mla_kv_norm)
    small_m = dict(ln_ffn1=m_ln_ffn1, ln_mix=m_ln_mix, ln_ffn2=m_ln_ffn2, sgu_v_gain=m_sgu_v_gain, sgu_v_bias=m_sgu_v_bias,
                   sgu_w_spatial=m_sgu_w_spatial, sgu_b_spatial=m_sgu_b_spatial, ln_final=m_ln_final,
                   mla_q_norm=m_mla_q_norm, mla_kv_norm=m_mla_kv_norm)
    small_v = dict(ln_ffn1=v_ln_ffn1, ln_mix=v_ln_mix, ln_ffn2=v_ln_ffn2, sgu_v_gain=v_sgu_v_gain, sgu_v_bias=v_sgu_v_bias,
                   sgu_w_spatial=v_sgu_w_spatial, sgu_b_spatial=v_sgu_b_spatial, ln_final=v_ln_final,
                   mla_q_norm=v_mla_q_norm, mla_kv_norm=v_mla_kv_norm)
    like = [small_w[k] for k in small_names]
    packed = _adamw_packed("adamw_small", _pack(like), _pack([small_grad[k] for k in small_names]),
                           _pack([small_m[k] for k in small_names]), _pack([small_v[k] for k in small_names]))
    small_out = {}
    unpacked = [_unpack(p, like) for p in packed]
    for idx, k in enumerate(small_names):
        small_out[k] = [small_grad[k].reshape(small_w[k].shape)] + [u[idx] for u in unpacked]

    order = ["ln_ffn1", "ffn1_w_in", "ffn1_w_out", "ln_mix", "ln_ffn2", "ffn2_w_in", "ffn2_w_out", "sgu_w_in",
             "sgu_v_gain", "sgu_v_bias", "sgu_w_spatial", "sgu_b_spatial", "sgu_w_out", "mla_w_in", "mla_q_norm",
             "mla_w_q_up", "mla_kv_norm", "mla_w_kv_up", "mla_w_out", "ln_final"]
    res = {k: (big_out[k] if k in big_out else small_out[k]) for k in order}
    outs = [loss, grad_x]
    for part in range(4):
        outs.extend(res[k][part] for k in order)
    return tuple(outs)
```

```python
import functools
import math

import jax
import jax.numpy as jnp
from jax import lax
from jax.experimental import pallas as pl
from jax.experimental.pallas import tpu as pltpu

F32 = jnp.float32
BF = jnp.bfloat16
MESH = pl.DeviceIdType.MESH

N_DEV = 8
EPS = 1e-6
CHUNK = 64
SGU_BLOCK = 128
SGU_GROUPS = 8
Q_LORA = 512
KV_LORA = 512
QK_NOPE = 128
QK_ROPE = 64
V_DIM = 128
ROPE_THETA = 10000.0
HEAD_PAD = 128
LANES = 128
ADAM_LR = 0.001
ADAM_B1 = 0.9
ADAM_B2 = 0.999
ADAM_EPS = 1e-08
ADAM_WD = 0.01
ADAM_STEP = 10
V7X_VMEM_BYTES = 64 * 1024 * 1024
VMEM_CAP = V7X_VMEM_BYTES - 6 * 1024 * 1024
VMEM_FLOOR = 32 * 1024 * 1024
NEG = -1e30
ATTN_GROUPS = 1
ATTN_UNROLL = 4


def _pick(n, cands):
    for c in cands:
        if n % c == 0:
            return c
    return n


def _nbytes(shape, dtype):
    return math.prod(int(s) for s in shape if s is not None) * jnp.dtype(dtype).itemsize


def _params(sem, block_bytes):
    limit = int(min(VMEM_CAP, max(VMEM_FLOOR, block_bytes)))
    return pltpu.CompilerParams(dimension_semantics=sem, vmem_limit_bytes=limit)


def _spec(shape, fn):
    return pl.BlockSpec(shape, fn)


_ANY = pl.BlockSpec(memory_space=pl.ANY)


def _mm(name, grid, ops, pairs, extras, outs, epilogue, acc_shapes, deps=()):
    nk = grid[2]
    n_ops, n_ex, n_out = len(ops), len(extras), len(outs)

    def load(refs, idx):
        loader = ops[idx][2] if len(ops[idx]) > 2 else None
        return (refs[idx][...] if loader is None else loader(refs[idx])).astype(BF)

    def prod(refs, p):
        ia, ib, ta, tb, _ = p
        a = load(refs, ia)
        b = load(refs, ib)
        dims = (((0 if ta else 1,), (1 if tb else 0,)), ((), ()))
        return lax.dot_general(a, b, dims, preferred_element_type=F32)

    def body(*refs):
        op_refs = refs[:n_ops]
        ex_refs = refs[n_ops:n_ops + n_ex]
        n_in = n_ops + n_ex + len(deps)
        out_refs = refs[n_in:n_in + n_out]
        acc_refs = refs[n_in + n_out:]
        ids = (pl.program_id(0), pl.program_id(1))

        def finish(vals):
            epilogue(vals, [e[...] for e in ex_refs], out_refs, ids)

        if nk == 1:
            vals = [None] * len(acc_shapes)
            for p in pairs:
                r = prod(op_refs, p)
                vals[p[4]] = r if vals[p[4]] is None else vals[p[4]] + r
            finish(vals)
        else:
            k = pl.program_id(2)

            def products():
                vals = [None] * len(acc_shapes)
                for p in pairs:
                    r = prod(op_refs, p)
                    vals[p[4]] = r if vals[p[4]] is None else vals[p[4]] + r
                return vals

            @pl.when(k == 0)
            def _():
                for a, v in zip(acc_refs, products()):
                    a[...] = v

            @pl.when((k > 0) & (k < nk - 1))
            def _():
                for a, v in zip(acc_refs, products()):
                    a[...] += v

            @pl.when(k == nk - 1)
            def _():
                finish([a[...] + v for a, v in zip(acc_refs, products())])

    in_arrays = [o[0] for o in ops] + [e[0] for e in extras]
    in_specs = [o[1] for o in ops] + [e[1] for e in extras]
    in_arrays += list(deps)
    in_specs += [_ANY] * len(deps)
    blk = 0
    for entry in ops + extras:
        blk += 2 * _nbytes(entry[1].block_shape, entry[0].dtype)
    for sd, sp in outs:
        blk += 2 * _nbytes(sp.block_shape, sd.dtype)
    acc_b = sum(_nbytes(s, F32) for s in acc_shapes)
    blk += 6 * acc_b
    scratch = [pltpu.VMEM(s, F32) for s in acc_shapes] if nk > 1 else []
    res = pl.pallas_call(
        body, name=name, grid=grid, in_specs=in_specs,
        out_specs=[o[1] for o in outs], out_shape=[o[0] for o in outs],
        scratch_shapes=scratch,
        compiler_params=_params(("parallel", "parallel", "arbitrary"), blk))(*in_arrays)
    return res


def _store(scale=None):
    def epi(accs, ex, outs, ids):
        v = accs[0]
        if scale is not None:
            v = v * scale
        outs[0][...] = v.astype(outs[0].dtype)
    return epi


def _store_residual(scale):
    def epi(accs, ex, outs, ids):
        outs[0][...] = ex[0] + scale * accs[0]
    return epi


def _sd(shape, dtype):
    return jax.ShapeDtypeStruct(tuple(shape), dtype)


def _pair_in(name, h, w_sm, out_dtype, act):
    S, D = h.shape
    c = w_sm.shape[-1]
    tm = _pick(S, (512, 256, 128))
    half = N_DEV // 2
    ops = [(h, _spec((tm, D), lambda j, i, k: (i, 0))),
           (w_sm, _spec((None, D, c), lambda j, i, k: (j, 0, 0))),
           (w_sm, _spec((None, D, c), lambda j, i, k: (j + half, 0, 0)))]
    outs = [(_sd((2, S, half * c), out_dtype), _spec((2, tm, c), lambda j, i, k: (0, i, j)))]
    if act is not None:
        outs.append((_sd((S, half * c), BF), _spec((tm, c), lambda j, i, k: (i, j))))

    def epi(accs, ex, orefs, ids):
        orefs[0][0] = accs[0].astype(out_dtype)
        orefs[0][1] = accs[1].astype(out_dtype)
        if act is not None:
            orefs[1][...] = act(accs[0], accs[1]).astype(BF)

    return _mm(name, (half, S // tm, 1), ops, [(0, 1, False, False, 0), (0, 2, False, False, 1)], [], outs, epi,
               [(tm, c), (tm, c)])


def _two_slabs(ref):
    return jnp.concatenate([ref[0], ref[1]], axis=0)


def _rows_out(name, a, w_sm, res, scale):
    S = a.shape[0]
    r, D = w_sm.shape[-2], w_sm.shape[-1]
    tm = _pick(S, (1024, 512, 256, 128))
    tn = _pick(D, (1024, 512, 256, 128))
    ops = [(a, _spec((tm, 2 * r), lambda i, j, k: (i, k))),
           (w_sm, _spec((2, r, tn), lambda i, j, k: (k, 0, j)), _two_slabs)]
    extras = [(res, _spec((tm, tn), lambda i, j, k: (i, j)))]
    outs = [(_sd((S, D), F32), _spec((tm, tn), lambda i, j, k: (i, j)))]
    return _mm(name, (S // tm, D // tn, N_DEV // 2), ops, [(0, 1, False, False, 0)], extras, outs,
               _store_residual(scale), [(tm, tn)])[0]


def _rows_dact(name, d_bf, w_sm, extras_arrays, out_shapes, epi):
    S, D = d_bf.shape
    r = w_sm.shape[-2]
    tm = _pick(S, (1024, 512, 256, 128))
    ops = [(d_bf, _spec((tm, D), lambda j, i, k: (i, 0))),
           (w_sm, _spec((2, r, D), lambda j, i, k: (j, 0, 0)), _two_slabs)]
    extras = []
    for arr in extras_arrays:
        if arr.ndim == 3:
            extras.append((arr, _spec((arr.shape[0], tm, 2 * r), lambda j, i, k: (0, i, j))))
        else:
            extras.append((arr, _spec((tm, 2 * r), lambda j, i, k: (i, j))))
    outs = []
    for sd in out_shapes:
        if len(sd.shape) == 3:
            outs.append((sd, _spec((sd.shape[0], tm, 2 * r), lambda j, i, k: (0, i, j))))
        else:
            outs.append((sd, _spec((tm, 2 * r), lambda j, i, k: (i, j))))
    return _mm(name, (N_DEV // 2, S // tm, 1), ops, [(0, 1, False, True, 0)], extras, outs, epi, [(tm, 2 * r)])


def _rows_wgrad(name, a, d_bf, scale):
    S, D = d_bf.shape
    r = a.shape[1] // N_DEV
    tn = _pick(D, (1024, 512, 256, 128))
    tk = _pick(S, (2048, 1024, 512, 256, 128))
    ops = [(a, _spec((tk, 2 * r), lambda s, j, k: (k, s))),
           (d_bf, _spec((tk, tn), lambda s, j, k: (k, j)))]
    outs = [(_sd((N_DEV, r, D), BF), _spec((2, r, tn), lambda s, j, k: (s, 0, j)))]

    def epi(accs, ex, orefs, ids):
        v = accs[0] if scale is None else accs[0] * scale
        orefs[0][0] = v[:r].astype(BF)
        orefs[0][1] = v[r:].astype(BF)

    return _mm(name, (N_DEV // 2, D // tn, S // tk), ops, [(0, 1, True, False, 0)], [], outs, epi, [(2 * r, tn)])[0]


def _cols_dh(name, dpair, w_sm, deps=()):
    _, S, _ = dpair.shape
    D, c = w_sm.shape[-2], w_sm.shape[-1]
    half = N_DEV // 2
    tm = _pick(S, (1024, 512, 256, 128))
    tn = _pick(D, (1024, 512, 256, 128))
    ops = [(dpair, _spec((None, tm, c), lambda i, j, k: (k // half, i, k % half))),
           (w_sm, _spec((None, tn, c), lambda i, j, k: (k, j, 0)))]
    outs = [(_sd((S, D), F32), _spec((tm, tn), lambda i, j, k: (i, j)))]
    return _mm(name, (S // tm, D // tn, N_DEV), ops, [(0, 1, False, True, 0)], [], outs, _store(), [(tm, tn)],
               deps=deps)[0]


def _cols_wgrad(name, h, dpair, deps=()):
    S, D = h.shape
    half = N_DEV // 2
    c = dpair.shape[2] // half
    tm = _pick(D, (1024, 512, 256, 128))
    tk = _pick(S, (2048, 1024, 512, 256, 128))
    ops = [(h, _spec((tk, tm), lambda s, i, k: (k, i))),
           (dpair, _spec((None, tk, c), lambda s, i, k: (s // half, k, s % half)))]
    outs = [(_sd((N_DEV, D, c), BF), _spec((None, tm, c), lambda s, i, k: (s, i, 0)))]
    return _mm(name, (N_DEV, D // tm, S // tk), ops, [(0, 1, True, False, 0)], [], outs, _store(), [(tm, c)],
               deps=deps)[0]


def _mm2(name, a, b, ta, tb, out_dtype, epi=None, extras=(), res=None, tn_cands=(512, 384, 256, 128), deps=()):
    a, a_lead = a if isinstance(a, tuple) else (a, None)
    b, b_lead = b if isinstance(b, tuple) else (b, None)
    M = a.shape[-1] if ta else a.shape[-2]
    K = a.shape[-2] if ta else a.shape[-1]
    N = b.shape[-2] if tb else b.shape[-1]
    tm = _pick(M, (1024, 512, 256, 128))
    tn = _pick(N, tn_cands)
    tk = _pick(K, (2048, 1152, 1024, 512, 256, 128))

    def matrix_spec(lead, shape, fn):
        if lead is None:
            return _spec(shape, fn)
        return _spec((None,) + shape, lambda i, j, k: (lead,) + fn(i, j, k))

    a_spec = matrix_spec(a_lead, (tk, tm), lambda i, j, k: (k, i)) if ta else matrix_spec(a_lead, (tm, tk), lambda i, j, k: (i, k))
    b_spec = matrix_spec(b_lead, (tn, tk), lambda i, j, k: (j, k)) if tb else matrix_spec(b_lead, (tk, tn), lambda i, j, k: (k, j))
    ex = [(e, _spec((tm, e.shape[1]), lambda i, j, k: (i, 0))) for e in extras]
    if res is not None:
        ex = [(res, _spec((tm, tn), lambda i, j, k: (i, j)))]
        epi = _store_residual(1.0)
    outs = [(_sd((M, N), out_dtype), _spec((tm, tn), lambda i, j, k: (i, j)))]
    return _mm(name, (M // tm, N // tn, K // tk), [(a, a_spec), (b, b_spec)], [(0, 1, ta, tb, 0)], ex, outs,
               epi or _store(), [(tm, tn)], deps=deps)[0]


def _mm_halves(name, a, w_t, out_dtype, epi_of_half=None, extras=()):
    a, lead = a
    M, K = a.shape[1:]
    N = w_t.shape[2]
    tm = _pick(M, (1024, 512, 256, 128))
    tn = _pick(N, (512, 256, 128))
    tk = _pick(K, (2048, 1024, 512, 256, 128))
    nj = N // tn
    ops = [(a, _spec((None, tm, tk), lambda i, j, k: (lead, i, k))),
           (w_t, _spec((None, tk, tn), lambda i, j, k: (j // nj, k, j % nj)))]
    ex = [(e, _spec((tm, e.shape[1]), lambda i, j, k: (i, 0))) for e in extras]
    outs = [(_sd((2, M, N), out_dtype), _spec((None, tm, tn), lambda i, j, k: (j // nj, i, j % nj)))]

    def epi(accs, ex_tiles, orefs, ids):
        if epi_of_half is None:
            orefs[0][...] = accs[0].astype(out_dtype)
        else:
            for half in range(2):
                pl.when(ids[1] // nj == half)(functools.partial(epi_of_half(half), accs, ex_tiles, orefs, ids))

    return _mm(name, (M // tm, 2 * nj, K // tk), ops, [(0, 1, False, False, 0)], ex, outs, epi, [(tm, tn)])[0]


def _rms_fwd(name, x, g, deps=()):
    S, D = x.shape
    ts = _pick(S, (512, 256, 128))

    def body(x_ref, g_ref, *rest):
        h_ref = rest[-1]
        xv = x_ref[...]
        r = lax.rsqrt(jnp.mean(xv * xv, axis=-1, keepdims=True) + EPS)
        h_ref[...] = (xv * r * g_ref[...]).astype(BF)

    return pl.pallas_call(
        body, name=name, grid=(S // ts,),
        in_specs=[_spec((ts, D), lambda i: (i, 0)), _spec((1, D), lambda i: (0, 0))] + [_ANY] * len(deps),
        out_specs=_spec((ts, D), lambda i: (i, 0)), out_shape=_sd((S, D), BF),
        compiler_params=_params(("parallel",), 12 * ts * D * 4))(x, g, *deps)


def _rms_bwd(name, dh, x, g, dres, deps=()):
    S, D = x.shape
    ts = _pick(S, (256, 128))

    def body(dh_ref, x_ref, g_ref, dres_ref, *rest):
        dx_ref, dxb_ref, dg_ref = rest[len(deps):]
        xv = x_ref[...]
        dhv = dh_ref[...]
        r = lax.rsqrt(jnp.mean(xv * xv, axis=-1, keepdims=True) + EPS)
        xhat = xv * r
        dxh = dhv * g_ref[...]
        cm = jnp.mean(dxh * xhat, axis=-1, keepdims=True)
        dx = r * (dxh - xhat * cm) + dres_ref[...]
        dx_ref[...] = dx
        dxb_ref[...] = dx.astype(BF)

        @pl.when(pl.program_id(0) == 0)
        def _():
            dg_ref[...] = jnp.zeros_like(dg_ref)

        dg_ref[...] += jnp.sum(dhv * xhat, axis=0, keepdims=True)

    row = _spec((ts, D), lambda i: (i, 0))
    vec = _spec((1, D), lambda i: (0, 0))
    return pl.pallas_call(
        body, name=name, grid=(S // ts,),
        in_specs=[row, row, vec, row] + [_ANY] * len(deps), out_specs=[row, row, vec],
        out_shape=[_sd((S, D), F32), _sd((S, D), BF), _sd((1, D), F32)],
        compiler_params=_params(("arbitrary",), 20 * ts * D * 4))(dh, x, g, dres, *deps)


def _final_loss(name, x, g, target):
    S, D = x.shape
    ts = _pick(S, (256, 128))

    def body(x_ref, g_ref, t_ref, loss_ref, dx_ref, dxb_ref, dg_ref):
        xv = x_ref[...]
        gv = g_ref[...]
        r = lax.rsqrt(jnp.mean(xv * xv, axis=-1, keepdims=True) + EPS)
        xhat = xv * r
        err = xhat * gv - t_ref[...]
        part = 0.5 * jnp.sum(jnp.mean(err * err, axis=-1, keepdims=True), axis=0, keepdims=True)
        dy = err * (1.0 / D)
        dxh = dy * gv
        cm = jnp.mean(dxh * xhat, axis=-1, keepdims=True)
        dx = r * (dxh - xhat * cm)
        dx_ref[...] = dx
        dxb_ref[...] = dx.astype(BF)

        @pl.when(pl.program_id(0) == 0)
        def _():
            dg_ref[...] = jnp.zeros_like(dg_ref)
            loss_ref[...] = jnp.zeros_like(loss_ref)

        dg_ref[...] += jnp.sum(dy * xhat, axis=0, keepdims=True)
        loss_ref[...] += jnp.broadcast_to(part, loss_ref.shape)

    row = _spec((ts, D), lambda i: (i, 0))
    vec = _spec((1, D), lambda i: (0, 0))
    return pl.pallas_call(
        body, name=name, grid=(S // ts,),
        in_specs=[row, vec, row], out_specs=[_spec((1, LANES), lambda i: (0, 0)), row, row, vec],
        out_shape=[_sd((1, LANES), F32), _sd((S, D), F32), _sd((S, D), BF), _sd((1, D), F32)],
        compiler_params=_params(("arbitrary",), 20 * ts * D * 4))(x, g, target)


def _swiglu(gate, up):
    return gate * jax.nn.sigmoid(gate) * up


def _swiglu_bwd_epi(accs, ex, orefs, ids):
    da = 0.5 * accs[0]
    gate = ex[0][0].astype(F32)
    up = ex[0][1].astype(F32)
    sg = jax.nn.sigmoid(gate)
    orefs[0][0] = (da * up * (sg * (1.0 + gate * (1.0 - sg)))).astype(BF)
    orefs[0][1] = (da * gate * sg).astype(BF)


_GELU_C = math.sqrt(2.0 / math.pi)


def _gelu(x):
    return x * (0.5 * (1.0 + jnp.tanh(_GELU_C * (x + 0.044715 * (x * x * x)))))


def _gelu_grad(x):
    t = jnp.tanh(_GELU_C * (x + 0.044715 * (x * x * x)))
    return 0.5 * (1.0 + t) + x * (0.5 * (1.0 - t * t) * _GELU_C * (1.0 + 3.0 * 0.044715 * (x * x)))


def _causal_block_mask():
    row = lax.broadcasted_iota(jnp.int32, (SGU_BLOCK, SGU_BLOCK), 0) // CHUNK
    col = lax.broadcasted_iota(jnp.int32, (SGU_BLOCK, SGU_BLOCK), 1) // CHUNK
    return row >= col


def _sgu_mid_fwd(name, puv, gain, bias, w_sp, b_sp):
    _, S, W = puv.shape
    G = SGU_GROUPS
    C = W // G
    T = SGU_BLOCK

    def body(puv_ref, gain_ref, bias_ref, w_ref, b_ref, out_ref):
        mask = _causal_block_mask()
        v = _gelu(puv_ref[1])
        mu = jnp.mean(v, axis=-1, keepdims=True)
        vc = v - mu
        rs = lax.rsqrt(jnp.mean(vc * vc, axis=-1, keepdims=True) + EPS)
        vln = (vc * rs * gain_ref[...] + bias_ref[...]).astype(BF)
        for g in range(G):
            wg = jnp.where(mask, w_ref[g], 0.0).astype(BF)
            mixed = jnp.dot(wg, vln[:, g * C:(g + 1) * C], preferred_element_type=F32) + b_ref[g]
            out_ref[:, g * C:(g + 1) * C] = (_gelu(puv_ref[0, :, g * C:(g + 1) * C]) * mixed).astype(BF)

    return pl.pallas_call(
        body, name=name, grid=(S // T,),
        in_specs=[_spec((2, T, W), lambda i: (0, i, 0)), _spec((1, W), lambda i: (0, 0)), _spec((1, W), lambda i: (0, 0)),
                  _spec((G, T, T), lambda i: (0, 0, 0)), _spec((G, T, 1), lambda i: (0, 0, 0))],
        out_specs=_spec((T, W), lambda i: (i, 0)), out_shape=_sd((S, W), BF),
        compiler_params=_params(("parallel",), 16 * T * W * 4))(puv, gain, bias, w_sp, b_sp)


def _sgu_mid_bwd(name, puv, dgated, gain, bias, w_sp, b_sp):
    _, S, W = puv.shape
    G = SGU_GROUPS
    C = W // G
    T = SGU_BLOCK

    def body(puv_ref, dg_ref, gain_ref, bias_ref, w_ref, b_ref, dpuv_ref, dgain_ref, dbias_ref, dw_ref, db_ref, dvln_ref):
        @pl.when(pl.program_id(0) == 0)
        def _():
            dgain_ref[...] = jnp.zeros_like(dgain_ref)
            dbias_ref[...] = jnp.zeros_like(dbias_ref)
            dw_ref[...] = jnp.zeros_like(dw_ref)
            db_ref[...] = jnp.zeros_like(db_ref)

        mask = _causal_block_mask()
        pv = puv_ref[1]
        v = _gelu(pv)
        mu = jnp.mean(v, axis=-1, keepdims=True)
        vc = v - mu
        rs = lax.rsqrt(jnp.mean(vc * vc, axis=-1, keepdims=True) + EPS)
        vhat = vc * rs
        gain_v = gain_ref[...]
        vln = (vhat * gain_v + bias_ref[...]).astype(BF)
        for g in range(G):
            sl = slice(g * C, (g + 1) * C)
            wg = jnp.where(mask, w_ref[g], 0.0).astype(BF)
            vg = vln[:, sl]
            mixed = jnp.dot(wg, vg, preferred_element_type=F32) + b_ref[g]
            pu = puv_ref[0, :, sl]
            dgt = dg_ref[:, sl].astype(F32)
            dpuv_ref[0, :, sl] = (dgt * mixed * _gelu_grad(pu)).astype(BF)
            dmix = dgt * _gelu(pu)
            db_ref[g] += jnp.sum(dmix, axis=-1, keepdims=True)
            dmb = dmix.astype(BF)
            dwg = lax.dot_general(dmb, vg, (((1,), (1,)), ((), ())), preferred_element_type=F32)
            dw_ref[g] += jnp.where(mask, dwg, 0.0)
            dvln_ref[:, sl] = lax.dot_general(wg, dmb, (((0,), (0,)), ((), ())), preferred_element_type=F32)
        dvln = dvln_ref[...]
        dgain_ref[...] += jnp.sum(dvln * vhat, axis=0, keepdims=True)
        dbias_ref[...] += jnp.sum(dvln, axis=0, keepdims=True)
        dvh = dvln * gain_v
        m1 = jnp.mean(dvh, axis=-1, keepdims=True)
        m2 = jnp.mean(dvh * vhat, axis=-1, keepdims=True)
        dv = rs * (dvh - m1 - vhat * m2)
        dpuv_ref[1] = (dv * _gelu_grad(pv)).astype(BF)

    vec = _spec((1, W), lambda i: (0, 0))
    wsp = _spec((G, T, T), lambda i: (0, 0, 0))
    bsp = _spec((G, T, 1), lambda i: (0, 0, 0))
    return pl.pallas_call(
        body, name=name, grid=(S // T,),
        in_specs=[_spec((2, T, W), lambda i: (0, i, 0)), _spec((T, W), lambda i: (i, 0)), vec, vec, wsp, bsp],
        out_specs=[_spec((2, T, W), lambda i: (0, i, 0)), vec, vec, wsp, bsp],
        out_shape=[_sd((2, S, W), BF), _sd((1, W), F32), _sd((1, W), F32), _sd((G, T, T), F32), _sd((G, T, 1), F32)],
        scratch_shapes=[pltpu.VMEM((T, W), F32)],
        compiler_params=_params(("arbitrary",), 24 * T * W * 4))(puv, dgated, gain, bias, w_sp, b_sp)


def _rope_tables(positions):
    half = QK_ROPE // 2
    inv_freq = 1.0 / (ROPE_THETA ** (jnp.arange(half, dtype=F32) / half))
    ang = positions.astype(F32)[:, None] * inv_freq[None, :]
    cos, sin = jnp.cos(ang), jnp.sin(ang)
    z = jnp.zeros_like(cos)
    return (jnp.concatenate([cos, cos, z, z], axis=1), jnp.concatenate([-sin, z, z, z], axis=1),
            jnp.concatenate([z, sin, z, z], axis=1))


def _rope(x, cos, sa, sb):
    return x * cos + pltpu.roll(x, HEAD_PAD - QK_ROPE // 2, 1) * sa + pltpu.roll(x, QK_ROPE // 2, 1) * sb


def _rope_t(dy, cos, sa, sb):
    return dy * cos + pltpu.roll(dy * sa, QK_ROPE // 2, 1) + pltpu.roll(dy * sb, HEAD_PAD - QK_ROPE // 2, 1)


def _rms_rows(x, g):
    r = lax.rsqrt(jnp.mean(x * x, axis=-1, keepdims=True) + EPS)
    return x * r * g


def _rms_rows_bwd(dy, x, g):
    r = lax.rsqrt(jnp.mean(x * x, axis=-1, keepdims=True) + EPS)
    xhat = x * r
    dxh = dy * g
    cm = jnp.mean(dxh * xhat, axis=-1, keepdims=True)
    return r * (dxh - xhat * cm), jnp.sum(dy * xhat, axis=0, keepdims=True)


def _mla_mid_fwd(name, proj, qg, kvg, cos, sa, sb):
    S, P = proj.shape
    ts = _pick(S, (512, 256, 128))

    def body(p_ref, qg_ref, kvg_ref, cos_ref, sa_ref, sb_ref, lat_ref, kr_ref):
        lat_ref[0] = _rms_rows(p_ref[:, :Q_LORA], qg_ref[...]).astype(BF)
        lat_ref[1] = _rms_rows(p_ref[:, Q_LORA:Q_LORA + KV_LORA], kvg_ref[...]).astype(BF)
        kr_ref[...] = _rope(p_ref[:, Q_LORA + KV_LORA:], cos_ref[...], sa_ref[...], sb_ref[...]).astype(BF)

    tab = _spec((ts, HEAD_PAD), lambda i: (i, 0))
    return pl.pallas_call(
        body, name=name, grid=(S // ts,),
        in_specs=[_spec((ts, P), lambda i: (i, 0)), _spec((1, Q_LORA), lambda i: (0, 0)),
                  _spec((1, KV_LORA), lambda i: (0, 0)), tab, tab, tab],
        out_specs=[_spec((2, ts, Q_LORA), lambda i: (0, i, 0)), tab],
        out_shape=[_sd((2, S, Q_LORA), BF), _sd((S, HEAD_PAD), BF)],
        compiler_params=_params(("parallel",), 16 * ts * P * 4))(proj, qg, kvg, cos, sa, sb)


def _mla_mid_bwd(name, proj, dqn, dkvn, dkr_heads, qg, kvg, cos, sa, sb):
    S, P = proj.shape
    H = dkr_heads.shape[0]
    ts = _pick(S, (256, 128))

    def body(p_ref, dqn_ref, dkvn_ref, dkr_ref, qg_ref, kvg_ref, cos_ref, sa_ref, sb_ref, dp_ref, dqg_ref, dkvg_ref):
        @pl.when(pl.program_id(0) == 0)
        def _():
            dqg_ref[...] = jnp.zeros_like(dqg_ref)
            dkvg_ref[...] = jnp.zeros_like(dkvg_ref)

        dq, dqg = _rms_rows_bwd(dqn_ref[...], p_ref[:, :Q_LORA], qg_ref[...])
        dkv, dkvg = _rms_rows_bwd(dkvn_ref[...], p_ref[:, Q_LORA:Q_LORA + KV_LORA], kvg_ref[...])
        dqg_ref[...] += dqg
        dkvg_ref[...] += dkvg
        dkr = dkr_ref[0]
        for h in range(1, H):
            dkr = dkr + dkr_ref[h]
        dp_ref[:, :Q_LORA] = dq.astype(BF)
        dp_ref[:, Q_LORA:Q_LORA + KV_LORA] = dkv.astype(BF)
        dp_ref[:, Q_LORA + KV_LORA:] = _rope_t(dkr, cos_ref[...], sa_ref[...], sb_ref[...]).astype(BF)

    tab = _spec((ts, HEAD_PAD), lambda i: (i, 0))
    lat = _spec((ts, Q_LORA), lambda i: (i, 0))
    gq = _spec((1, Q_LORA), lambda i: (0, 0))
    return pl.pallas_call(
        body, name=name, grid=(S // ts,),
        in_specs=[_spec((ts, P), lambda i: (i, 0)), lat, lat, _spec((H, ts, HEAD_PAD), lambda i: (0, i, 0)),
                  gq, gq, tab, tab, tab],
        out_specs=[_spec((ts, P), lambda i: (i, 0)), gq, gq],
        out_shape=[_sd((S, P), BF), _sd((1, Q_LORA), F32), _sd((1, KV_LORA), F32)],
        compiler_params=_params(("arbitrary",), 24 * ts * P * 4))(proj, dqn, dkvn, dkr_heads, qg, kvg, cos, sa, sb)


def _attn_tile(S):
    return _pick(S, (512,)) if S >= 2048 else _pick(S, (128,))


def _diag_mask(t, transposed):
    q = lax.broadcasted_iota(jnp.int32, (t, t), 1 if transposed else 0) // CHUNK
    k = lax.broadcasted_iota(jnp.int32, (t, t), 0 if transposed else 1) // CHUNK
    return k <= q


_NT = (((1,), (1,)), ((), ()))


def _attn_fwd(name, q_all, kv_all, kr):
    _, S, HP = q_all.shape
    H = HP // HEAD_PAD
    t = _attn_tile(S)
    nq = S // t
    ng = ATTN_GROUPS
    tg = t // ng

    def body(q_ref, kv_ref, kr_ref, o_ref, lse_ref, kcat_ref):
        i = pl.program_id(1)

        @pl.when(i == 0)
        def _():
            kcat_ref[:, :HEAD_PAD] = kv_ref[0]
            kcat_ref[:, HEAD_PAD:] = kr_ref[...]

        qs = [jnp.concatenate([q_ref[0, g * tg:(g + 1) * tg], q_ref[1, g * tg:(g + 1) * tg]], axis=1) for g in range(ng)]

        def step(j, carry, masked):
            off = pl.multiple_of(j * t, t)
            kj = kcat_ref[pl.ds(off, t), :]
            vj = kv_ref[1, pl.ds(off, t), :]
            out = []
            for g in range(ng):
                m, l, acc = carry[g]
                s = lax.dot_general(qs[g], kj, _NT, preferred_element_type=F32)
                if masked:
                    s = jnp.where(_diag_mask(t, False)[g * tg:(g + 1) * tg], s, NEG)
                m2 = jnp.maximum(m, jnp.max(s, axis=-1, keepdims=True))
                al = jnp.exp(m - m2)
                p = jnp.exp(s - m2)
                l2 = al * l + jnp.sum(p, axis=-1, keepdims=True)
                acc2 = al * acc + jnp.dot(p.astype(BF), vj, preferred_element_type=F32)
                out.append((m2, l2, acc2))
            return tuple(out)

        init = tuple((jnp.full((tg, 1), NEG, F32), jnp.zeros((tg, 1), F32), jnp.zeros((tg, V_DIM), F32))
                     for _ in range(ng))
        def several(jj, c):
            for u in range(ATTN_UNROLL):
                c = step(jj * ATTN_UNROLL + u, c, False)
            return c

        carry = lax.fori_loop(0, i // ATTN_UNROLL, several, init)
        carry = lax.fori_loop((i // ATTN_UNROLL) * ATTN_UNROLL, i, lambda j, c: step(j, c, False), carry)
        carry = step(i, carry, True)
        for g in range(ng):
            m, l, acc = carry[g]
            o_ref[g * tg:(g + 1) * tg, :] = acc / l
            lse_ref[g * tg:(g + 1) * tg, :] = jnp.broadcast_to(m + jnp.log(l), (tg, LANES))

    return pl.pallas_call(
        body, name=name, grid=(H, nq),
        in_specs=[_spec((2, t, HEAD_PAD), lambda h, i: (0, i, h)), _spec((2, S, HEAD_PAD), lambda h, i: (0, 0, h)),
                  _spec((S, HEAD_PAD), lambda h, i: (0, 0))],
        out_specs=[_spec((t, HEAD_PAD), lambda h, i: (i, h)), _spec((None, t, LANES), lambda h, i: (h, i, 0))],
        out_shape=[_sd((S, HP), F32), _sd((H, S, LANES), F32)],
        scratch_shapes=[pltpu.VMEM((S, 2 * HEAD_PAD), BF)],
        compiler_params=_params(("parallel", "arbitrary"), 8 * S * HEAD_PAD * 2 + 24 * t * t * 4))(q_all, kv_all, kr)


def _attn_delta(name, do, o):
    S, HP = o.shape
    H = HP // HEAD_PAD
    ts = _pick(S, (512, 256, 128))

    def body(do_ref, o_ref, d_ref):
        d_ref[...] = jnp.broadcast_to(jnp.sum(do_ref[...] * o_ref[...], axis=-1, keepdims=True), (ts, LANES))

    tile = _spec((ts, HEAD_PAD), lambda h, i: (i, h))
    return pl.pallas_call(
        body, name=name, grid=(H, S // ts), in_specs=[tile, tile],
        out_specs=_spec((None, ts, LANES), lambda h, i: (h, i, 0)), out_shape=_sd((H, S, LANES), F32),
        compiler_params=_params(("parallel", "parallel"), VMEM_FLOOR))(do, o)


def _attn_dq(name, q_all, kv_all, kr, do, lse, delta, cos, sa, sb, scale):
    _, S, HP = q_all.shape
    H = HP // HEAD_PAD
    t = _attn_tile(S)
    nq = S // t

    def body(q_ref, kv_ref, kr_ref, do_ref, lse_ref, dl_ref, cos_ref, sa_ref, sb_ref, dq_ref, kcat_ref):
        i = pl.program_id(1)

        @pl.when(i == 0)
        def _():
            kcat_ref[:, :HEAD_PAD] = kv_ref[0]
            kcat_ref[:, HEAD_PAD:] = kr_ref[...]

        ng = ATTN_GROUPS
        tg = t // ng
        rows = [slice(g * tg, (g + 1) * tg) for g in range(ng)]
        qs = [jnp.concatenate([q_ref[0, r], q_ref[1, r]], axis=1) for r in rows]
        dobs = [do_ref[r, :].astype(BF) for r in rows]
        lses = [lse_ref[r, 0:1] for r in rows]
        dls = [dl_ref[r, 0:1] for r in rows]

        def step(j, dqs, masked):
            off = pl.multiple_of(j * t, t)
            kj = kcat_ref[pl.ds(off, t), :]
            vj = kv_ref[1, pl.ds(off, t), :]
            out = []
            for g in range(ng):
                s = lax.dot_general(qs[g], kj, _NT, preferred_element_type=F32)
                if masked:
                    s = jnp.where(_diag_mask(t, False)[rows[g]], s, NEG)
                p = jnp.exp(s - lses[g])
                dp = lax.dot_general(dobs[g], vj, _NT, preferred_element_type=F32)
                ds = (p * (dp - dls[g])).astype(BF)
                out.append(dqs[g] + jnp.dot(ds, kj, preferred_element_type=F32))
            return tuple(out)

        init = tuple(jnp.zeros((tg, 2 * HEAD_PAD), F32) for _ in range(ng))
        def several(jj, c):
            for u in range(ATTN_UNROLL):
                c = step(jj * ATTN_UNROLL + u, c, False)
            return c

        dqs = lax.fori_loop(0, i // ATTN_UNROLL, several, init)
        dqs = lax.fori_loop((i // ATTN_UNROLL) * ATTN_UNROLL, i, lambda j, c: step(j, c, False), dqs)
        dqs = step(i, dqs, True)
        for g in range(ng):
            dq_ref[0, rows[g]] = (dqs[g][:, :HEAD_PAD] * scale).astype(BF)
            dq_ref[1, rows[g]] = (_rope_t(dqs[g][:, HEAD_PAD:], cos_ref[rows[g], :], sa_ref[rows[g], :],
                                          sb_ref[rows[g], :]) * scale).astype(BF)

    tab = _spec((t, HEAD_PAD), lambda h, i: (i, 0))
    stat = _spec((None, t, LANES), lambda h, i: (h, i, 0))
    return pl.pallas_call(
        body, name=name, grid=(H, nq),
        in_specs=[_spec((2, t, HEAD_PAD), lambda h, i: (0, i, h)), _spec((2, S, HEAD_PAD), lambda h, i: (0, 0, h)),
                  _spec((S, HEAD_PAD), lambda h, i: (0, 0)), _spec((t, HEAD_PAD), lambda h, i: (i, h)), stat, stat,
                  tab, tab, tab],
        out_specs=_spec((2, t, HEAD_PAD), lambda h, i: (0, i, h)), out_shape=_sd((2, S, HP), BF),
        scratch_shapes=[pltpu.VMEM((S, 2 * HEAD_PAD), BF)],
        compiler_params=_params(("parallel", "arbitrary"), 8 * S * HEAD_PAD * 2 + 32 * t * t * 4))(
            q_all, kv_all, kr, do, lse, delta, cos, sa, sb)


def _attn_dkv(name, q_all, kv_all, kr, do, lse_row, delta_row):
    _, S, HP = q_all.shape
    H = HP // HEAD_PAD
    t = _attn_tile(S)
    nq = S // t

    def body(q_ref, kv_ref, kr_ref, do_ref, lse_ref, dl_ref, dkv_ref, dkr_ref, qcat_ref):
        j = pl.program_id(1)

        @pl.when(j == 0)
        def _():
            qcat_ref[:, :HEAD_PAD] = q_ref[0]
            qcat_ref[:, HEAD_PAD:] = q_ref[1]

        ng = ATTN_GROUPS
        tg = t // ng
        rows = [slice(g * tg, (g + 1) * tg) for g in range(ng)]
        kjs = [jnp.concatenate([kv_ref[0, r], kr_ref[r, :]], axis=1) for r in rows]
        vjs = [kv_ref[1, r] for r in rows]

        def step(i, carry, masked):
            off = pl.multiple_of(i * t, t)
            qi = qcat_ref[pl.ds(off, t), :]
            doi = do_ref[pl.ds(off, t), :].astype(BF)
            lse_i = lse_ref[i]
            dl_i = dl_ref[i]
            out = []
            for g in range(ng):
                dk, dv = carry[g]
                st = lax.dot_general(kjs[g], qi, _NT, preferred_element_type=F32)
                if masked:
                    st = jnp.where(_diag_mask(t, True)[rows[g]], st, NEG)
                pt = jnp.exp(st - lse_i)
                dv2 = dv + jnp.dot(pt.astype(BF), doi, preferred_element_type=F32)
                dpt = lax.dot_general(vjs[g], doi, _NT, preferred_element_type=F32)
                dst = (pt * (dpt - dl_i)).astype(BF)
                out.append((dk + jnp.dot(dst, qi, preferred_element_type=F32), dv2))
            return tuple(out)

        init = tuple((jnp.zeros((tg, 2 * HEAD_PAD), F32), jnp.zeros((tg, V_DIM), F32)) for _ in range(ng))
        def several(ii, c):
            for u in range(ATTN_UNROLL):
                c = step(j + 1 + ii * ATTN_UNROLL + u, c, False)
            return c

        carry = step(j, init, True)
        trips = (nq - 1 - j) // ATTN_UNROLL
        carry = lax.fori_loop(0, trips, several, carry)
        carry = lax.fori_loop(j + 1 + trips * ATTN_UNROLL, nq, lambda i, c: step(i, c, False), carry)
        for g in range(ng):
            dk, dv = carry[g]
            dkv_ref[0, rows[g]] = dk[:, :HEAD_PAD].astype(BF)
            dkv_ref[1, rows[g]] = dv.astype(BF)
            dkr_ref[rows[g], :] = dk[:, HEAD_PAD:]

    stat = _spec((None, nq, 1, t), lambda h, j: (h, 0, 0, 0))
    return pl.pallas_call(
        body, name=name, grid=(H, nq),
        in_specs=[_spec((2, S, HEAD_PAD), lambda h, j: (0, 0, h)), _spec((2, t, HEAD_PAD), lambda h, j: (0, j, h)),
                  _spec((t, HEAD_PAD), lambda h, j: (j, 0)), _spec((S, HEAD_PAD), lambda h, j: (0, h)), stat, stat],
        out_specs=[_spec((2, t, HEAD_PAD), lambda h, j: (0, j, h)), _spec((None, t, HEAD_PAD), lambda h, j: (h, j, 0))],
        out_shape=[_sd((2, S, HP), BF), _sd((H, S, HEAD_PAD), F32)],
        scratch_shapes=[pltpu.VMEM((S, 2 * HEAD_PAD), BF)],
        compiler_params=_params(("parallel", "arbitrary"), 8 * S * HEAD_PAD * 4 + 32 * t * t * 4))(
            q_all, kv_all, kr, do, lse_row, delta_row)


def _place():
    x, y, c = lax.axis_index("x"), lax.axis_index("y"), lax.axis_index("c")
    return x, y, c


def _all_gather(blocks):
    n = len(blocks)

    def body(*refs):
        ins, outs = refs[:n], refs[n:2 * n]
        send_sems, recv_sems, local_sems = refs[2 * n:]
        x, y, c = _place()
        me = 4 * x + 2 * y + c
        sibling = (x, y, 1 - c)
        chips = [(1 - x, y), (x, 1 - y), (1 - x, 1 - y)]

        def slab(a, px, py, pc):
            return outs[a].at[4 * px + 2 * py + pc]

        def copy(a, k, src, dst, to):
            return pltpu.make_async_remote_copy(src_ref=src, dst_ref=dst, send_sem=send_sems.at[a, k],
                                                recv_sem=recv_sems.at[a, k], device_id=to, device_id_type=MESH)

        local = [pltpu.make_async_copy(ins[a], outs[a].at[me], local_sems.at[a]) for a in range(n)]
        for cp in local:
            cp.start()
        sends = []
        for a in range(n):
            mine = slab(a, x, y, c)
            sends.append(copy(a, 0, ins[a], mine, sibling))
            for j, chip in enumerate(chips):
                sends.append(copy(a, 1 + j, ins[a], mine, (*chip, c)))
        for cp in sends:
            cp.start()
        for j, chip in enumerate(chips):
            for a in range(n):
                got = slab(a, *chip, c)
                copy(a, 1 + j, got, got, (x, y, c)).wait_recv()
                fwd = copy(a, 4 + j, got, got, sibling)
                fwd.start()
                sends.append(fwd)
        for a in range(n):
            got = slab(a, x, y, 1 - c)
            copy(a, 0, got, got, (x, y, c)).wait_recv()
            for j, chip in enumerate(chips):
                got = slab(a, *chip, 1 - c)
                copy(a, 4 + j, got, got, (x, y, c)).wait_recv()
        for cp in sends:
            cp.wait_send()
        for cp in local:
            cp.wait()

    return pl.pallas_call(
        body, name="weights_all_gather", in_specs=[_ANY] * n, out_specs=[_ANY] * n,
        out_shape=[_sd((N_DEV,) + b.shape, b.dtype) for b in blocks],
        scratch_shapes=[pltpu.SemaphoreType.DMA((n, 7)), pltpu.SemaphoreType.DMA((n, 7)), pltpu.SemaphoreType.DMA((n,))],
    )(*blocks)


_HBM = pl.BlockSpec(memory_space=pltpu.HBM)
_SEM = pl.BlockSpec(memory_space=pltpu.SEMAPHORE)
_EFFECT = pltpu.SideEffectType.DATAFLOW_SIDE_EFFECTING


def _peers():
    x, y, c = _place()
    out = []
    for m in range(1, N_DEV):
        px, py, pc = x ^ (m >> 2), y ^ ((m >> 1) & 1), c ^ (m & 1)
        out.append((m, (px, py, pc), 4 * px + 2 * py + pc))
    return 4 * x + 2 * y + c, out


def _exchange_copies(src, land, send_sem, recv_sem, gather):
    me, peers = _peers()
    cps = []
    for a in range(len(src)):
        for m, pos, idx in peers:
            s_ref, d_ref = (src[a], land[a].at[me]) if gather else (src[a].at[idx], land[a].at[m - 1])
            k = a * (N_DEV - 1) + m - 1
            cps.append(pltpu.make_async_remote_copy(src_ref=s_ref, dst_ref=d_ref, send_sem=send_sem.at[k],
                                                    recv_sem=recv_sem.at[k], device_id=pos, device_id_type=MESH))
    return cps


def _xstart(name, srcs, gather, after=()):
    n = len(srcs)
    if gather:
        land_shapes = [(N_DEV,) + s.shape for s in srcs]
    else:
        land_shapes = [(N_DEV - 1,) + s.shape[1:] for s in srcs]

    def body(*refs):
        src, land = refs[:n], refs[n:2 * n]
        send_sem, recv_sem = refs[2 * n + len(after)], refs[2 * n + len(after) + 1]
        token = refs[-1]
        for cp in _exchange_copies(src, land, send_sem, recv_sem, gather):
            cp.start()
        token[...] = jnp.zeros_like(token)

    sem = pltpu.SemaphoreType.DMA((n * (N_DEV - 1),))
    out_shape = ([sem, sem] + [pltpu.HBM(s.shape, s.dtype) for s in srcs]
                 + [pltpu.HBM(sh, s.dtype) for sh, s in zip(land_shapes, srcs)] + [_sd((8, LANES), F32)])
    args = [pltpu.with_memory_space_constraint(s, pltpu.HBM) for s in srcs]
    args += [pltpu.with_memory_space_constraint(lax.empty(sh, s.dtype), pltpu.HBM) for sh, s in zip(land_shapes, srcs)]
    res = pl.pallas_call(
        body, name=name, out_shape=out_shape, in_specs=[_HBM] * (2 * n) + [_ANY] * len(after),
        out_specs=[_SEM, _SEM] + [_HBM] * (2 * n) + [pl.BlockSpec(memory_space=pltpu.VMEM)],
        input_output_aliases={i: 2 + i for i in range(2 * n)},
        compiler_params=pltpu.CompilerParams(has_side_effects=_EFFECT))(*args, *after)
    return dict(send=res[0], recv=res[1], srcs=list(res[2:2 + n]), lands=list(res[2 + n:2 + 2 * n]), token=res[-1])


def _xwait(name, st, after, gather):
    n = len(st["srcs"])

    def body(*refs):
        src, land = refs[:n], refs[n:2 * n]
        send_sem, recv_sem = refs[2 * n], refs[2 * n + 1]
        for cp in _exchange_copies(src, land, send_sem, recv_sem, gather):
            cp.wait_send()
            cp.wait_recv()

    arrays = st["srcs"] + st["lands"]
    res = pl.pallas_call(
        body, name=name, out_shape=[pltpu.HBM(a.shape, a.dtype) for a in arrays],
        in_specs=[_HBM] * (2 * n) + [_SEM, _SEM, _ANY], out_specs=[_HBM] * (2 * n),
        input_output_aliases={i: i for i in range(2 * n)},
        compiler_params=pltpu.CompilerParams(has_side_effects=_EFFECT))(*arrays, st["send"], st["recv"], after)
    return list(res[:n]), list(res[n:])


def _put_own(name, land, block, me_arr):
    R, C = block.shape
    tr = _pick(R, (512, 256, 128, 64, 32, 16))

    def body(me_ref, b_ref, land_ref, o_ref):
        o_ref[...] = b_ref[...]

    return pl.pallas_call(
        body, name=name, out_shape=_sd(land.shape, land.dtype),
        grid_spec=pltpu.PrefetchScalarGridSpec(
            num_scalar_prefetch=1, grid=(R // tr,),
            in_specs=[_spec((tr, C), lambda i, me_ref: (i, 0)), _ANY],
            out_specs=_spec((None, tr, C), lambda i, me_ref: (me_ref[0], i, 0))),
        input_output_aliases={2: 0},
        compiler_params=_params(("parallel",), 8 * tr * C * 2))(me_arr, block, land)


def _all_reduce_small(name, part):
    R = part.shape[0]

    def body(p_ref, out_ref, gath_ref, send_sems, recv_sems):
        x, y, c = _place()
        me = 4 * x + 2 * y + c
        gath_ref[me] = p_ref[...]
        cps = []
        for m in range(1, N_DEV):
            to = (x ^ (m >> 2), y ^ ((m >> 1) & 1), c ^ (m & 1))
            cps.append(pltpu.make_async_remote_copy(
                src_ref=p_ref, dst_ref=gath_ref.at[me], send_sem=send_sems.at[m - 1], recv_sem=recv_sems.at[m - 1],
                device_id=to, device_id_type=MESH))
        for cp in cps:
            cp.start()
        for m in range(1, N_DEV):
            frm = 4 * (x ^ (m >> 2)) + 2 * (y ^ ((m >> 1) & 1)) + (c ^ (m & 1))
            pltpu.make_async_remote_copy(
                src_ref=p_ref, dst_ref=gath_ref.at[frm], send_sem=send_sems.at[m - 1], recv_sem=recv_sems.at[m - 1],
                device_id=(x, y, c), device_id_type=MESH).wait_recv()
        for cp in cps:
            cp.wait_send()
        tot = gath_ref[0]
        for k in range(1, N_DEV):
            tot = tot + gath_ref[k]
        out_ref[...] = tot

    vm = pl.BlockSpec(memory_space=pltpu.VMEM)
    return pl.pallas_call(
        body, name=name, in_specs=[vm], out_specs=vm, out_shape=_sd((R, LANES), F32),
        scratch_shapes=[pltpu.VMEM((N_DEV, R, LANES), F32), pltpu.SemaphoreType.DMA((N_DEV - 1,)),
                        pltpu.SemaphoreType.DMA((N_DEV - 1,))],
        compiler_params=pltpu.CompilerParams(vmem_limit_bytes=VMEM_FLOOR),
    )(part)


def _adam_math(w, g, m, v):
    m2 = ADAM_B1 * m + (1.0 - ADAM_B1) * g
    v2 = ADAM_B2 * v + (1.0 - ADAM_B2) * (g * g)
    m_hat = m2 / (1.0 - ADAM_B1 ** ADAM_STEP)
    v_hat = v2 / (1.0 - ADAM_B2 ** ADAM_STEP)
    delta = -ADAM_LR * (m_hat / (jnp.sqrt(v_hat) + ADAM_EPS) + ADAM_WD * w)
    return delta, m2, v2


def _adamw_sharded(name, lands, fulls, me_arr, w, m, v):
    n_l = len(lands)
    R, C = lands[0].shape[1], lands[0].shape[2]
    tr = _pick(R, (128, 64, 32, 16, 8))
    nr = R // tr

    def body(me_ref, *refs):
        land_refs, own_refs = refs[:n_l], refs[n_l:2 * n_l]
        w_ref, m_ref, v_ref, g_ref, d_ref, m2_ref, v2_ref = refs[2 * n_l:]
        layer = pl.program_id(0)
        for ll in range(n_l):
            @pl.when(layer == ll)
            def _(ll=ll):
                g = own_refs[ll][...].astype(F32)
                for j in range(N_DEV - 1):
                    g = g + land_refs[ll][j].astype(F32)
                delta, m2, v2 = _adam_math(w_ref[...], g, m_ref[...], v_ref[...])
                g_ref[...] = g
                d_ref[...] = delta
                m2_ref[...] = m2
                v2_ref[...] = v2

    def row_of(ll):
        return lambda l, i, me_ref: jnp.where(l == ll, i, 0)

    in_specs = [_spec((N_DEV - 1, tr, C), lambda l, i, me_ref, f=row_of(ll): (0, f(l, i, me_ref), 0)) for ll in range(n_l)]
    in_specs += [_spec((None, tr, C), lambda l, i, me_ref, f=row_of(ll): (me_ref[0], f(l, i, me_ref), 0))
                 for ll in range(n_l)]
    blk = _spec((tr, C), lambda l, i, me_ref: (l * nr + i, 0))
    return pl.pallas_call(
        body, name=name, out_shape=[_sd(w.shape, F32)] * 4,
        grid_spec=pltpu.PrefetchScalarGridSpec(num_scalar_prefetch=1, grid=(n_l, nr), in_specs=in_specs + [blk] * 3,
                                               out_specs=[blk] * 4),
        compiler_params=_params(("parallel", "parallel"), (4 * n_l * N_DEV + 40) * tr * C * 4))(
            me_arr, *lands, *fulls, w, m, v)


def _adamw_packed(name, w, g, m, v):
    R = w.shape[0]

    def body(w_ref, g_ref, m_ref, v_ref, d_ref, m2_ref, v2_ref):
        delta, m2, v2 = _adam_math(w_ref[...], g_ref[...], m_ref[...], v_ref[...])
        d_ref[...] = delta
        m2_ref[...] = m2
        v2_ref[...] = v2

    vm = pl.BlockSpec(memory_space=pltpu.VMEM)
    return pl.pallas_call(
        body, name=name, in_specs=[vm] * 4, out_specs=[vm] * 3, out_shape=[_sd((R, LANES), F32)] * 3,
        compiler_params=pltpu.CompilerParams(vmem_limit_bytes=VMEM_FLOOR))(w, g, m, v)


def _pack(arrays):
    flat = jnp.concatenate([a.reshape(-1).astype(F32) for a in arrays])
    pad = (-flat.shape[0]) % (8 * LANES)
    return jnp.pad(flat, (0, pad)).reshape(-1, LANES)


def _unpack(packed, like):
    flat = packed.reshape(-1)
    out, pos = [], 0
    for a in like:
        n = math.prod(a.shape)
        out.append(flat[pos:pos + n].reshape(a.shape))
        pos += n
    return out


def _ffn_fwd(tag, x, gain, w_in_sm, w_out_of, deps=()):
    h = _rms_fwd(tag + "_norm", x, gain, deps)
    gu, act = _pair_in(tag + "_in", h, w_in_sm, BF, _swiglu)
    x_new = _rows_out(tag + "_out", act, w_out_of(act), x, 0.5)
    return x_new, (x, h, gu, act)


def _ffn_bwd(tag, d, d_bf, saved, gain, w_in_sm, w_out_sm, send_out, send_in):
    x, h, gu, act = saved
    dgu = _rows_dact(tag + "_dact", d_bf, w_out_sm, [gu], [_sd(gu.shape, BF)], _swiglu_bwd_epi)[0]
    g_out = _rows_wgrad(tag + "_wgrad_out", act, d_bf, 0.5)
    token_out = send_out(g_out)
    g_in = _cols_wgrad(tag + "_wgrad_in", h, dgu, deps=(token_out,))
    token = send_in(g_in)
    dh = _cols_dh(tag + "_dh", dgu, w_in_sm, deps=(token,))
    dx, dx_bf, dgain = _rms_bwd(tag + "_dnorm", dh, x, gain, d)
    return dx, dx_bf, dgain


def kernel(x, positions, ln_ffn1, ffn1_w_in, ffn1_w_out, ln_mix, ln_ffn2, ffn2_w_in, ffn2_w_out, sgu_w_in, sgu_v_gain, sgu_v_bias, sgu_w_spatial, sgu_b_spatial, sgu_w_out, mla_w_in, mla_q_norm, mla_w_q_up, mla_kv_norm, mla_w_kv_up, mla_w_out, ln_final, loss_target, m_ln_ffn1, m_ffn1_w_in, m_ffn1_w_out, m_ln_mix, m_ln_ffn2, m_ffn2_w_in, m_ffn2_w_out, m_sgu_w_in, m_sgu_v_gain, m_sgu_v_bias, m_sgu_w_spatial, m_sgu_b_spatial, m_sgu_w_out, m_mla_w_in, m_mla_q_norm, m_mla_w_q_up, m_mla_kv_norm, m_mla_w_kv_up, m_mla_w_out, m_ln_final, v_ln_ffn1, v_ffn1_w_in, v_ffn1_w_out, v_ln_mix, v_ln_ffn2, v_ffn2_w_in, v_ffn2_w_out, v_sgu_w_in, v_sgu_v_gain, v_sgu_v_bias, v_sgu_w_spatial, v_sgu_b_spatial, v_sgu_w_out, v_mla_w_in, v_mla_q_norm, v_mla_w_q_up, v_mla_kv_norm, v_mla_w_kv_up, v_mla_w_out, v_ln_final):
    S, D = x.shape[1], x.shape[2]
    L = ln_ffn1.shape[0]
    H = mla_w_q_up.shape[-1] * N_DEV // (QK_NOPE + QK_ROPE)
    xi, yi, ci = _place()
    me = 4 * xi + 2 * yi + ci
    me_arr = jnp.reshape(me, (1,)).astype(jnp.int32)
    big = dict(ffn1_w_in=ffn1_w_in, ffn1_w_out=ffn1_w_out, ffn2_w_in=ffn2_w_in, ffn2_w_out=ffn2_w_out,
               sgu_w_in=sgu_w_in, sgu_w_out=sgu_w_out, mla_w_in=mla_w_in, mla_w_q_up=mla_w_q_up,
               mla_w_kv_up=mla_w_kv_up, mla_w_out=mla_w_out)
    big_m = dict(ffn1_w_in=m_ffn1_w_in, ffn1_w_out=m_ffn1_w_out, ffn2_w_in=m_ffn2_w_in, ffn2_w_out=m_ffn2_w_out,
                 sgu_w_in=m_sgu_w_in, sgu_w_out=m_sgu_w_out, mla_w_in=m_mla_w_in, mla_w_q_up=m_mla_w_q_up,
                 mla_w_kv_up=m_mla_w_kv_up, mla_w_out=m_mla_w_out)
    big_v = dict(ffn1_w_in=v_ffn1_w_in, ffn1_w_out=v_ffn1_w_out, ffn2_w_in=v_ffn2_w_in, ffn2_w_out=v_ffn2_w_out,
                 sgu_w_in=v_sgu_w_in, sgu_w_out=v_sgu_w_out, mla_w_in=v_mla_w_in, mla_w_q_up=v_mla_w_q_up,
                 mla_w_kv_up=v_mla_w_kv_up, mla_w_out=v_mla_w_out)
    names = list(big)
    mla_names = ["mla_w_in", "mla_w_q_up", "mla_w_kv_up", "mla_w_out"]

    blocks = {(k, l): big[k][l].astype(BF) for k in names for l in range(big[k].shape[0])}
    first = ("ffn1_w_in", 0)
    groups = {}
    for i in range(L):
        if i > 0:
            groups[f"ffn1_in_{i}"] = [("ffn1_w_in", i)]
        groups[f"ffn1_out_{i}"] = [("ffn1_w_out", i)]
        groups[f"mix_{i}"] = [("sgu_w_in", i // 2), ("sgu_w_out", i // 2)] if i % 2 == 0 else [(k, i // 2) for k in mla_names]
        groups[f"ffn2_{i}"] = [("ffn2_w_in", i), ("ffn2_w_out", i)]
    gathered = {first: _all_gather([blocks[first]])[0]}
    started, order_after = {}, (gathered[first],)
    for tag, grp in groups.items():
        started[tag] = _xstart(f"gather_start_{tag}", [blocks[k] for k in grp], True, order_after)
        order_after = (started[tag]["token"],)

    def fetch(tag, after):
        own, lands = _xwait(f"gather_wait_{tag}", started[tag], after, True)
        for k, blk, land in zip(groups[tag], own, lands):
            gathered[k] = _put_own(f"gather_own_{k[0]}_{k[1]}", land, blk, me_arr)

    def w_out_of(name, tag):
        def get(after):
            fetch(tag, after)
            return gathered[name]
        return get

    norm_rows = jnp.zeros((N_DEV, LANES), F32)
    mine = jnp.concatenate([mla_q_norm[0], mla_kv_norm[0]])
    norm_rows = lax.dynamic_update_slice(norm_rows, mine[None, :], (me, 0))
    norm_all = _all_reduce_small("norm_gains_gather", norm_rows)
    nq_sh = mla_q_norm.shape[1]
    q_gain = norm_all[:, :nq_sh].reshape(1, Q_LORA)
    kv_gain = norm_all[:, nq_sh:2 * nq_sh].reshape(1, KV_LORA)
    cos, sa, sb = _rope_tables(positions[0])
    scale = float((QK_NOPE + QK_ROPE) ** -0.5)

    xs = x[0]
    w_sp = sgu_w_spatial[0]
    b_sp = sgu_b_spatial[0][:, :, None]
    saved = []
    mla_w = {}
    deps = order_after
    for i in range(L):
        if i > 0:
            fetch(f"ffn1_in_{i}", xs)
        xs, s1 = _ffn_fwd(f"l{i}_ffn1", xs, ln_ffn1[i:i + 1], gathered[("ffn1_w_in", i)],
                          w_out_of(("ffn1_w_out", i), f"ffn1_out_{i}"), deps)
        deps = ()
        fetch(f"mix_{i}", xs)
        x_mix = xs
        h = _rms_fwd(f"l{i}_mix_norm", xs, ln_mix[i:i + 1])
        j = i // 2
        if i % 2 == 0:
            puv = _pair_in(f"l{i}_sgu_in", h, gathered[("sgu_w_in", j)], F32, None)[0]
            gated = _sgu_mid_fwd(f"l{i}_sgu_mid", puv, sgu_v_gain, sgu_v_bias, w_sp, b_sp)
            xs = _rows_out(f"l{i}_sgu_out", gated, gathered[("sgu_w_out", j)], xs, 1.0)
            sm = (x_mix, h, puv, gated)
        else:
            w_in_nat = gathered[("mla_w_in", j)].reshape(D, Q_LORA + KV_LORA + QK_ROPE)
            w_in_pad = jnp.pad(w_in_nat, ((0, 0), (0, HEAD_PAD - QK_ROPE)))
            wq_nat = jnp.transpose(gathered[("mla_w_q_up", j)], (1, 0, 2)).reshape(Q_LORA, H, QK_NOPE + QK_ROPE)
            wq_t = jnp.stack([wq_nat[:, :, :QK_NOPE].reshape(Q_LORA, H * HEAD_PAD),
                              jnp.pad(wq_nat[:, :, QK_NOPE:], ((0, 0), (0, 0), (0, HEAD_PAD - QK_ROPE))).reshape(
                                  Q_LORA, H * HEAD_PAD)])
            wkv_nat = jnp.transpose(gathered[("mla_w_kv_up", j)], (1, 0, 2)).reshape(KV_LORA, H, QK_NOPE + V_DIM)
            wkv_t = jnp.stack([wkv_nat[:, :, :QK_NOPE].reshape(KV_LORA, H * HEAD_PAD),
                               wkv_nat[:, :, QK_NOPE:].reshape(KV_LORA, H * HEAD_PAD)])
            w_o_nat = gathered[("mla_w_out", j)].reshape(H * V_DIM, D)
            mla_w[i] = (w_in_pad, wq_t, wkv_t, w_o_nat)
            proj = _mm2(f"l{i}_mla_in", h, w_in_pad, False, False, F32, tn_cands=(384, 128))
            lat, kr = _mla_mid_fwd(f"l{i}_mla_mid", proj, q_gain, kv_gain, cos, sa, sb)

            def q_epi(accs, ex, orefs, ids):
                orefs[0][...] = (accs[0] * scale).astype(BF)

            def qr_epi(accs, ex, orefs, ids):
                for hh in range(accs[0].shape[1] // HEAD_PAD):
                    sl = slice(hh * HEAD_PAD, (hh + 1) * HEAD_PAD)
                    orefs[0][:, sl] = (_rope(accs[0][:, sl], *ex) * scale).astype(BF)

            q_all = _mm_halves(f"l{i}_mla_q", (lat, 0), wq_t, BF, lambda half: (q_epi, qr_epi)[half], extras=(cos, sa, sb))
            kv_all = _mm_halves(f"l{i}_mla_kv", (lat, 1), wkv_t, BF)
            o, lse = _attn_fwd(f"l{i}_attn", q_all, kv_all, kr)
            xs = _mm2(f"l{i}_mla_out", o, w_o_nat, False, False, F32, res=xs)
            sm = (x_mix, h, proj, lat, kr, q_all, kv_all, o, lse)
        fetch(f"ffn2_{i}", xs)
        xs, s2 = _ffn_fwd(f"l{i}_ffn2", xs, ln_ffn2[i:i + 1], gathered[("ffn2_w_in", i)],
                          lambda after, i=i: gathered[("ffn2_w_out", i)])
        saved.append((s1, sm, s2))

    loss_row, d, d_bf, g_ln_final = _final_loss("final_loss", xs, ln_final[None, :], loss_target[0])
    loss = lax.psum(loss_row[0, 0], ("x", "y", "c"))

    sent = []

    def send(tag, keys, grads):
        st = _xstart(f"scatter_start_{tag}", grads, False)
        sent.append((tag, keys, st))
        return st["token"]

    g_ln1, g_ln2, g_lnm = [None] * L, [None] * L, [None] * L
    small_g = {}
    for i in reversed(range(L)):
        s1, sm, s2 = saved[i]
        d, d_bf, g_ln2[i] = _ffn_bwd(
            f"l{i}_ffn2", d, d_bf, s2, ln_ffn2[i:i + 1], gathered[("ffn2_w_in", i)], gathered[("ffn2_w_out", i)],
            lambda g, i=i: send(f"l{i}_ffn2_out", [("ffn2_w_out", i)], [g]),
            lambda g, i=i: send(f"l{i}_ffn2_in", [("ffn2_w_in", i)], [g]))
        j = i // 2
        if i % 2 == 0:
            x_mix, h, puv, gated = sm
            dgated = _rows_dact(f"l{i}_sgu_dgated", d_bf, gathered[("sgu_w_out", j)], [], [_sd(gated.shape, BF)], _store())[0]
            g_so = _rows_wgrad(f"l{i}_sgu_wgrad_out", gated, d_bf, None)
            token_out = send(f"l{i}_sgu_out", [("sgu_w_out", j)], [g_so])
            dpuv, dgain, dbias, dwsp, dbsp = _sgu_mid_bwd(f"l{i}_sgu_mid_bwd", puv, dgated, sgu_v_gain, sgu_v_bias,
                                                          w_sp, b_sp)
            small_g.update(sgu_v_gain=dgain, sgu_v_bias=dbias, sgu_w_spatial=dwsp[None], sgu_b_spatial=dbsp[None, :, :, 0])
            g_si = _cols_wgrad(f"l{i}_sgu_wgrad_in", h, dpuv, deps=(token_out,))
            token = send(f"l{i}_sgu_in", [("sgu_w_in", j)], [g_si])
            dh = _cols_dh(f"l{i}_sgu_dh", dpuv, gathered[("sgu_w_in", j)], deps=(token,))
        else:
            x_mix, h, proj, lat, kr, q_all, kv_all, o, lse = sm
            w_in_pad, wq_t, wkv_t, w_o_nat = mla_w[i]
            t = _attn_tile(S)
            do = _mm2(f"l{i}_mla_do", d_bf, w_o_nat, False, True, F32)
            g_wo = _mm2(f"l{i}_mla_wgrad_out", o, d_bf, True, False, BF)
            delta = _attn_delta(f"l{i}_attn_delta", do, o)
            dq_all = _attn_dq(f"l{i}_attn_dq", q_all, kv_all, kr, do, lse, delta, cos, sa, sb, scale)
            lse_row = lse[:, :, 0].reshape(H, S // t, 1, t)
            delta_row = delta[:, :, 0].reshape(H, S // t, 1, t)
            dkv_all, dkr_heads = _attn_dkv(f"l{i}_attn_dkv", q_all, kv_all, kr, do, lse_row, delta_row)
            dqn = _mm2(f"l{i}_mla_dqn", (dq_all, 0), (wq_t, 0), False, True, F32)
            dqn = _mm2(f"l{i}_mla_dqn2", (dq_all, 1), (wq_t, 1), False, True, F32, res=dqn)
            dkvn = _mm2(f"l{i}_mla_dkvn", (dkv_all, 0), (wkv_t, 0), False, True, F32)
            dkvn = _mm2(f"l{i}_mla_dkvn2", (dkv_all, 1), (wkv_t, 1), False, True, F32, res=dkvn)
            g_wq = [_mm2(f"l{i}_mla_wgrad_q{t2}", (lat, 0), (dq_all, t2), True, False, BF) for t2 in range(2)]
            g_wkv = [_mm2(f"l{i}_mla_wgrad_kv{t2}", (lat, 1), (dkv_all, t2), True, False, BF) for t2 in range(2)]
            dproj, g_qn, g_kvn = _mla_mid_bwd(f"l{i}_mla_mid_bwd", proj, dqn, dkvn, dkr_heads, q_gain, kv_gain, cos, sa, sb)
            g_win = _mm2(f"l{i}_mla_wgrad_in", h, dproj, True, False, BF, tn_cands=(384, 128))
            n_in = Q_LORA + KV_LORA + QK_ROPE
            gq_nat = jnp.concatenate([g_wq[0].reshape(Q_LORA, H, HEAD_PAD),
                                      g_wq[1].reshape(Q_LORA, H, HEAD_PAD)[:, :, :QK_ROPE]], axis=2)
            gkv_nat = jnp.concatenate([g_wkv[0].reshape(KV_LORA, H, HEAD_PAD), g_wkv[1].reshape(KV_LORA, H, HEAD_PAD)], axis=2)
            token = send(f"l{i}_mla", [(k, j) for k in mla_names],
                         [g_win[:, :n_in].reshape(N_DEV, D // N_DEV, n_in),
                          jnp.transpose(gq_nat.reshape(Q_LORA, N_DEV, -1), (1, 0, 2)),
                          jnp.transpose(gkv_nat.reshape(KV_LORA, N_DEV, -1), (1, 0, 2)),
                          g_wo.reshape(N_DEV, H * V_DIM // N_DEV, D)])
            small_g.update(mla_q_norm=g_qn, mla_kv_norm=g_kvn)
            dh = _mm2(f"l{i}_mla_dh", dproj, w_in_pad, False, True, F32, tn_cands=(512, 256, 128), deps=(token,))
        d, d_bf, g_lnm[i] = _rms_bwd(f"l{i}_mix_dnorm", dh, x_mix, ln_mix[i:i + 1], d)
        d, d_bf, g_ln1[i] = _ffn_bwd(
            f"l{i}_ffn1", d, d_bf, s1, ln_ffn1[i:i + 1], gathered[("ffn1_w_in", i)], gathered[("ffn1_w_out", i)],
            lambda g, i=i: send(f"l{i}_ffn1_out", [("ffn1_w_out", i)], [g]),
            lambda g, i=i: send(f"l{i}_ffn1_in", [("ffn1_w_in", i)], [g]))
    grad_x = d[None]

    landed, partial = {}, {}
    for tag, keys, st in sent:
        fulls, lands = _xwait(f"scatter_wait_{tag}", st, d, False)
        for k, full, land in zip(keys, fulls, lands):
            partial[k], landed[k] = full, land
    big_out = {}
    for k in names:
        w = big[k]
        rc = (math.prod(w.shape[1:-1]), w.shape[-1])
        flat = (w.shape[0] * rc[0], rc[1])
        lands = [landed[(k, l)].reshape((N_DEV - 1,) + rc) for l in range(w.shape[0])]
        fulls = [partial[(k, l)].reshape((N_DEV,) + rc) for l in range(w.shape[0])]
        res = _adamw_sharded(f"adamw_{k}", lands, fulls, me_arr, w.reshape(flat), big_m[k].reshape(flat),
                             big_v[k].reshape(flat))
        big_out[k] = [r.reshape(w.shape) for r in res]

    small_g.update(ln_ffn1=jnp.concatenate(g_ln1), ln_mix=jnp.concatenate(g_lnm), ln_ffn2=jnp.concatenate(g_ln2),
                   ln_final=g_ln_final[0])
    small_names = ["ln_ffn1", "ln_mix", "ln_ffn2", "sgu_v_gain", "sgu_v_bias", "sgu_w_spatial", "sgu_b_spatial",
                   "ln_final", "mla_q_norm", "mla_kv_norm"]
    summed = _unpack(_all_reduce_small("small_grads_all_reduce", _pack([small_g[k] for k in small_names])),
                     [small_g[k] for k in small_names])
    small_grad = dict(zip(small_names, summed))
    for k in ("mla_q_norm", "mla_kv_norm"):
        small_grad[k] = lax.dynamic_slice(small_grad[k], (0, me * nq_sh), (1, nq_sh))
    small_w = dict(ln_ffn1=ln_ffn1, ln_mix=ln_mix, ln_ffn2=ln_ffn2, sgu_v_gain=sgu_v_gain, sgu_v_bias=sgu_v_bias,
                   sgu_w_spatial=sgu_w_spatial, sgu_b_spatial=sgu_b_spatial, ln_final=ln_final, mla_q_norm=mla_q_norm,
                   mla_kv_norm=mla_kv_norm)
    small_m = dict(ln_ffn1=m_ln_ffn1, ln_mix=m_ln_mix, ln_ffn2=m_ln_ffn2, sgu_v_gain=m_sgu_v_gain, sgu_v_bias=m_sgu_v_bias,
                   sgu_w_spatial=m_sgu_w_spatial, sgu_b_spatial=m_sgu_b_spatial, ln_final=m_ln_final,
                   mla_q_norm=m_mla_q_norm, mla_kv_norm=m_mla_kv_norm)
    small_v = dict(ln_ffn1=v_ln_ffn1, ln_mix=v_ln_mix, ln_ffn2=v_ln_ffn2, sgu_v_gain=v_sgu_v_gain, sgu_v_bias=v_sgu_v_bias,
                   sgu_w_spatial=v_sgu_w_spatial, sgu_b_spatial=v_sgu_b_spatial, ln_final=v_ln_final,
                   mla_q_norm=v_mla_q_norm, mla_kv_norm=v_mla_kv_norm)
    like = [small_w[k] for k in small_names]
    packed = _adamw_packed("adamw_small", _pack(like), _pack([small_grad[k] for k in small_names]),
                           _pack([small_m[k] for k in small_names]), _pack([small_v[k] for k in small_names]))
    small_out = {}
    unpacked = [_unpack(p, like) for p in packed]
    for idx, k in enumerate(small_names):
        small_out[k] = [small_grad[k].reshape(small_w[k].shape)] + [u[idx] for u in unpacked]

    order = ["ln_ffn1", "ffn1_w_in", "ffn1_w_out", "ln_mix", "ln_ffn2", "ffn2_w_in", "ffn2_w_out", "sgu_w_in",
             "sgu_v_gain", "sgu_v_bias", "sgu_w_spatial", "sgu_b_spatial", "sgu_w_out", "mla_w_in", "mla_q_norm",
             "mla_w_q_up", "mla_kv_norm", "mla_w_kv_up", "mla_w_out", "ln_final"]
    res = {k: (big_out[k] if k in big_out else small_out[k]) for k in order}
    outs = [loss, grad_x]
    for part in range(4):
        outs.extend(res[k][part] for k in order)
    return tuple(outs)
```

```python
import functools
import math

import jax
import jax.numpy as jnp
from jax import lax
from jax.experimental import pallas as pl
from jax.experimental.pallas import tpu as pltpu

F32 = jnp.float32
BF = jnp.bfloat16
MESH = pl.DeviceIdType.MESH

N_DEV = 8
EPS = 1e-6
CHUNK = 64
SGU_BLOCK = 128
SGU_GROUPS = 8
Q_LORA = 512
KV_LORA = 512
QK_NOPE = 128
QK_ROPE = 64
V_DIM = 128
ROPE_THETA = 10000.0
HEAD_PAD = 128
LANES = 128
ADAM_LR = 0.001
ADAM_B1 = 0.9
ADAM_B2 = 0.999
ADAM_EPS = 1e-08
ADAM_WD = 0.01
ADAM_STEP = 10
V7X_VMEM_BYTES = 64 * 1024 * 1024
VMEM_CAP = V7X_VMEM_BYTES - 6 * 1024 * 1024
VMEM_FLOOR = 32 * 1024 * 1024
NEG = -1e30
ATTN_GROUPS = 1
ATTN_UNROLL = 4


def _pick(n, cands):
    for c in cands:
        if n % c == 0:
            return c
    return n


def _nbytes(shape, dtype):
    return math.prod(int(s) for s in shape if s is not None) * jnp.dtype(dtype).itemsize


def _params(sem, block_bytes):
    limit = int(min(VMEM_CAP, max(VMEM_FLOOR, block_bytes)))
    return pltpu.CompilerParams(dimension_semantics=sem, vmem_limit_bytes=limit)


def _spec(shape, fn):
    return pl.BlockSpec(shape, fn)


_ANY = pl.BlockSpec(memory_space=pl.ANY)


def _mm(name, grid, ops, pairs, extras, outs, epilogue, acc_shapes, deps=()):
    nk = grid[2]
    n_ops, n_ex, n_out = len(ops), len(extras), len(outs)

    def load(refs, idx):
        loader = ops[idx][2] if len(ops[idx]) > 2 else None
        return (refs[idx][...] if loader is None else loader(refs[idx])).astype(BF)

    def prod(refs, p):
        ia, ib, ta, tb, _ = p
        a = load(refs, ia)
        b = load(refs, ib)
        dims = (((0 if ta else 1,), (1 if tb else 0,)), ((), ()))
        return lax.dot_general(a, b, dims, preferred_element_type=F32)

    def body(*refs):
        op_refs = refs[:n_ops]
        ex_refs = refs[n_ops:n_ops + n_ex]
        n_in = n_ops + n_ex + len(deps)
        out_refs = refs[n_in:n_in + n_out]
        acc_refs = refs[n_in + n_out:]
        ids = (pl.program_id(0), pl.program_id(1))

        def finish(vals):
            epilogue(vals, [e[...] for e in ex_refs], out_refs, ids)

        if nk == 1:
            vals = [None] * len(acc_shapes)
            for p in pairs:
                r = prod(op_refs, p)
                vals[p[4]] = r if vals[p[4]] is None else vals[p[4]] + r
            finish(vals)
        else:
            k = pl.program_id(2)

            def products():
                vals = [None] * len(acc_shapes)
                for p in pairs:
                    r = prod(op_refs, p)
                    vals[p[4]] = r if vals[p[4]] is None else vals[p[4]] + r
                return vals

            @pl.when(k == 0)
            def _():
                for a, v in zip(acc_refs, products()):
                    a[...] = v

            @pl.when((k > 0) & (k < nk - 1))
            def _():
                for a, v in zip(acc_refs, products()):
                    a[...] += v

            @pl.when(k == nk - 1)
            def _():
                finish([a[...] + v for a, v in zip(acc_refs, products())])

    in_arrays = [o[0] for o in ops] + [e[0] for e in extras]
    in_specs = [o[1] for o in ops] + [e[1] for e in extras]
    in_arrays += list(deps)
    in_specs += [_ANY] * len(deps)
    blk = 0
    for entry in ops + extras:
        blk += 2 * _nbytes(entry[1].block_shape, entry[0].dtype)
    for sd, sp in outs:
        blk += 2 * _nbytes(sp.block_shape, sd.dtype)
    acc_b = sum(_nbytes(s, F32) for s in acc_shapes)
    blk += 6 * acc_b
    scratch = [pltpu.VMEM(s, F32) for s in acc_shapes] if nk > 1 else []
    res = pl.pallas_call(
        body, name=name, grid=grid, in_specs=in_specs,
        out_specs=[o[1] for o in outs], out_shape=[o[0] for o in outs],
        scratch_shapes=scratch,
        compiler_params=_params(("parallel", "parallel", "arbitrary"), blk))(*in_arrays)
    return res


def _store(scale=None):
    def epi(accs, ex, outs, ids):
        v = accs[0]
        if scale is not None:
            v = v * scale
        outs[0][...] = v.astype(outs[0].dtype)
    return epi


def _store_residual(scale):
    def epi(accs, ex, outs, ids):
        outs[0][...] = ex[0] + scale * accs[0]
    return epi


def _sd(shape, dtype):
    return jax.ShapeDtypeStruct(tuple(shape), dtype)


def _pair_in(name, h, w_sm, out_dtype, act):
    S, D = h.shape
    c = w_sm.shape[-1]
    tm = _pick(S, (512, 256, 128))
    half = N_DEV // 2
    ops = [(h, _spec((tm, D), lambda j, i, k: (i, 0))),
           (w_sm, _spec((None, D, c), lambda j, i, k: (j, 0, 0))),
           (w_sm, _spec((None, D, c), lambda j, i, k: (j + half, 0, 0)))]
    outs = [(_sd((2, S, half * c), out_dtype), _spec((2, tm, c), lambda j, i, k: (0, i, j)))]
    if act is not None:
        outs.append((_sd((S, half * c), BF), _spec((tm, c), lambda j, i, k: (i, j))))

    def epi(accs, ex, orefs, ids):
        if act is None:
            orefs[0][0] = accs[0].astype(out_dtype)
            orefs[0][1] = accs[1].astype(out_dtype)
        else:
            keep0, keep1, out = act(accs[0], accs[1])
            orefs[0][0] = keep0.astype(out_dtype)
            orefs[0][1] = keep1.astype(out_dtype)
            orefs[1][...] = out.astype(BF)

    return _mm(name, (half, S // tm, 1), ops, [(0, 1, False, False, 0), (0, 2, False, False, 1)], [], outs, epi,
               [(tm, c), (tm, c)])


def _two_slabs(ref):
    return jnp.concatenate([ref[0], ref[1]], axis=0)


def _rows_out(name, a, w_sm, res, scale):
    S = a.shape[0]
    r, D = w_sm.shape[-2], w_sm.shape[-1]
    tm = _pick(S, (1024, 512, 256, 128))
    tn = _pick(D, (1024, 512, 256, 128))
    ops = [(a, _spec((tm, 2 * r), lambda i, j, k: (i, k))),
           (w_sm, _spec((2, r, tn), lambda i, j, k: (k, 0, j)), _two_slabs)]
    extras = [(res, _spec((tm, tn), lambda i, j, k: (i, j)))]
    outs = [(_sd((S, D), F32), _spec((tm, tn), lambda i, j, k: (i, j)))]
    return _mm(name, (S // tm, D // tn, N_DEV // 2), ops, [(0, 1, False, False, 0)], extras, outs,
               _store_residual(scale), [(tm, tn)])[0]


def _rows_dact(name, d_bf, w_sm, extras_arrays, out_shapes, epi):
    S, D = d_bf.shape
    r = w_sm.shape[-2]
    tm = _pick(S, (1024, 512, 256, 128))
    ops = [(d_bf, _spec((tm, D), lambda j, i, k: (i, 0))),
           (w_sm, _spec((2, r, D), lambda j, i, k: (j, 0, 0)), _two_slabs)]
    extras = []
    for arr in extras_arrays:
        if arr.ndim == 3:
            extras.append((arr, _spec((arr.shape[0], tm, 2 * r), lambda j, i, k: (0, i, j))))
        else:
            extras.append((arr, _spec((tm, 2 * r), lambda j, i, k: (i, j))))
    outs = []
    for sd in out_shapes:
        if len(sd.shape) == 3:
            outs.append((sd, _spec((sd.shape[0], tm, 2 * r), lambda j, i, k: (0, i, j))))
        else:
            outs.append((sd, _spec((tm, 2 * r), lambda j, i, k: (i, j))))
    return _mm(name, (N_DEV // 2, S // tm, 1), ops, [(0, 1, False, True, 0)], extras, outs, epi, [(tm, 2 * r)])


def _rows_wgrad(name, a, d_bf, scale):
    S, D = d_bf.shape
    r = a.shape[1] // N_DEV
    tn = _pick(D, (1024, 512, 256, 128))
    tk = _pick(S, (2048, 1024, 512, 256, 128))
    ops = [(a, _spec((tk, 2 * r), lambda s, j, k: (k, s))),
           (d_bf, _spec((tk, tn), lambda s, j, k: (k, j)))]
    outs = [(_sd((N_DEV, r, D), BF), _spec((2, r, tn), lambda s, j, k: (s, 0, j)))]

    def epi(accs, ex, orefs, ids):
        v = accs[0] if scale is None else accs[0] * scale
        orefs[0][0] = v[:r].astype(BF)
        orefs[0][1] = v[r:].astype(BF)

    return _mm(name, (N_DEV // 2, D // tn, S // tk), ops, [(0, 1, True, False, 0)], [], outs, epi, [(2 * r, tn)])[0]


def _cols_dh(name, dpair, w_sm, deps=()):
    _, S, _ = dpair.shape
    D, c = w_sm.shape[-2], w_sm.shape[-1]
    half = N_DEV // 2
    tm = _pick(S, (1024, 512, 256, 128))
    tn = _pick(D, (1024, 512, 256, 128))
    ops = [(dpair, _spec((None, tm, c), lambda i, j, k: (k // half, i, k % half))),
           (w_sm, _spec((None, tn, c), lambda i, j, k: (k, j, 0)))]
    outs = [(_sd((S, D), F32), _spec((tm, tn), lambda i, j, k: (i, j)))]
    return _mm(name, (S // tm, D // tn, N_DEV), ops, [(0, 1, False, True, 0)], [], outs, _store(), [(tm, tn)],
               deps=deps)[0]


def _cols_wgrad(name, h, dpair, deps=()):
    S, D = h.shape
    half = N_DEV // 2
    c = dpair.shape[2] // half
    tm = _pick(D, (1024, 512, 256, 128))
    tk = _pick(S, (2048, 1024, 512, 256, 128))
    ops = [(h, _spec((tk, tm), lambda s, i, k: (k, i))),
           (dpair, _spec((None, tk, c), lambda s, i, k: (s // half, k, s % half)))]
    outs = [(_sd((N_DEV, D, c), BF), _spec((None, tm, c), lambda s, i, k: (s, i, 0)))]
    return _mm(name, (N_DEV, D // tm, S // tk), ops, [(0, 1, True, False, 0)], [], outs, _store(), [(tm, c)],
               deps=deps)[0]


def _mm2(name, a, b, ta, tb, out_dtype, epi=None, extras=(), res=None, tn_cands=(512, 384, 256, 128), deps=()):
    a, a_lead = a if isinstance(a, tuple) else (a, None)
    b, b_lead = b if isinstance(b, tuple) else (b, None)
    M = a.shape[-1] if ta else a.shape[-2]
    K = a.shape[-2] if ta else a.shape[-1]
    N = b.shape[-2] if tb else b.shape[-1]
    tm = _pick(M, (1024, 512, 256, 128))
    tn = _pick(N, tn_cands)
    tk = _pick(K, (2048, 1152, 1024, 512, 256, 128))

    def matrix_spec(lead, shape, fn):
        if lead is None:
            return _spec(shape, fn)
        return _spec((None,) + shape, lambda i, j, k: (lead,) + fn(i, j, k))

    a_spec = matrix_spec(a_lead, (tk, tm), lambda i, j, k: (k, i)) if ta else matrix_spec(a_lead, (tm, tk), lambda i, j, k: (i, k))
    b_spec = matrix_spec(b_lead, (tn, tk), lambda i, j, k: (j, k)) if tb else matrix_spec(b_lead, (tk, tn), lambda i, j, k: (k, j))
    ex = [(e, _spec((tm, e.shape[1]), lambda i, j, k: (i, 0))) for e in extras]
    if res is not None:
        ex = [(res, _spec((tm, tn), lambda i, j, k: (i, j)))]
        epi = _store_residual(1.0)
    outs = [(_sd((M, N), out_dtype), _spec((tm, tn), lambda i, j, k: (i, j)))]
    return _mm(name, (M // tm, N // tn, K // tk), [(a, a_spec), (b, b_spec)], [(0, 1, ta, tb, 0)], ex, outs,
               epi or _store(), [(tm, tn)], deps=deps)[0]


def _mm_halves(name, a, w_t, out_dtype, epi_of_half=None, extras=()):
    a, lead = a
    M, K = a.shape[1:]
    N = w_t.shape[2]
    tm = _pick(M, (1024, 512, 256, 128))
    tn = _pick(N, (512, 256, 128))
    tk = _pick(K, (2048, 1024, 512, 256, 128))
    nj = N // tn
    ops = [(a, _spec((None, tm, tk), lambda i, j, k: (lead, i, k))),
           (w_t, _spec((None, tk, tn), lambda i, j, k: (j // nj, k, j % nj)))]
    ex = [(e, _spec((tm, e.shape[1]), lambda i, j, k: (i, 0))) for e in extras]
    outs = [(_sd((2, M, N), out_dtype), _spec((None, tm, tn), lambda i, j, k: (j // nj, i, j % nj)))]

    def epi(accs, ex_tiles, orefs, ids):
        if epi_of_half is None:
            orefs[0][...] = accs[0].astype(out_dtype)
        else:
            for half in range(2):
                pl.when(ids[1] // nj == half)(functools.partial(epi_of_half(half), accs, ex_tiles, orefs, ids))

    return _mm(name, (M // tm, 2 * nj, K // tk), ops, [(0, 1, False, False, 0)], ex, outs, epi, [(tm, tn)])[0]


def _rms_fwd(name, x, g, deps=()):
    S, D = x.shape
    ts = _pick(S, (512, 256, 128))

    def body(x_ref, g_ref, *rest):
        h_ref = rest[-1]
        xv = x_ref[...]
        r = lax.rsqrt(jnp.mean(xv * xv, axis=-1, keepdims=True) + EPS)
        h_ref[...] = (xv * r * g_ref[...]).astype(BF)

    return pl.pallas_call(
        body, name=name, grid=(S // ts,),
        in_specs=[_spec((ts, D), lambda i: (i, 0)), _spec((1, D), lambda i: (0, 0))] + [_ANY] * len(deps),
        out_specs=_spec((ts, D), lambda i: (i, 0)), out_shape=_sd((S, D), BF),
        compiler_params=_params(("parallel",), 12 * ts * D * 4))(x, g, *deps)


def _rms_bwd(name, dh, x, g, dres, deps=()):
    S, D = x.shape
    ts = _pick(S, (256, 128))

    def body(dh_ref, x_ref, g_ref, dres_ref, *rest):
        dx_ref, dxb_ref, dg_ref = rest[len(deps):]
        xv = x_ref[...]
        dhv = dh_ref[...]
        r = lax.rsqrt(jnp.mean(xv * xv, axis=-1, keepdims=True) + EPS)
        xhat = xv * r
        dxh = dhv * g_ref[...]
        cm = jnp.mean(dxh * xhat, axis=-1, keepdims=True)
        dx = r * (dxh - xhat * cm) + dres_ref[...]
        dx_ref[...] = dx
        dxb_ref[...] = dx.astype(BF)

        @pl.when(pl.program_id(0) == 0)
        def _():
            dg_ref[...] = jnp.zeros_like(dg_ref)

        dg_ref[...] += jnp.sum(dhv * xhat, axis=0, keepdims=True)

    row = _spec((ts, D), lambda i: (i, 0))
    vec = _spec((1, D), lambda i: (0, 0))
    return pl.pallas_call(
        body, name=name, grid=(S // ts,),
        in_specs=[row, row, vec, row] + [_ANY] * len(deps), out_specs=[row, row, vec],
        out_shape=[_sd((S, D), F32), _sd((S, D), BF), _sd((1, D), F32)],
        compiler_params=_params(("arbitrary",), 20 * ts * D * 4))(dh, x, g, dres, *deps)


def _final_loss(name, x, g, target):
    S, D = x.shape
    ts = _pick(S, (256, 128))

    def body(x_ref, g_ref, t_ref, loss_ref, dx_ref, dxb_ref, dg_ref):
        xv = x_ref[...]
        gv = g_ref[...]
        r = lax.rsqrt(jnp.mean(xv * xv, axis=-1, keepdims=True) + EPS)
        xhat = xv * r
        err = xhat * gv - t_ref[...]
        part = 0.5 * jnp.sum(jnp.mean(err * err, axis=-1, keepdims=True), axis=0, keepdims=True)
        dy = err * (1.0 / D)
        dxh = dy * gv
        cm = jnp.mean(dxh * xhat, axis=-1, keepdims=True)
        dx = r * (dxh - xhat * cm)
        dx_ref[...] = dx
        dxb_ref[...] = dx.astype(BF)

        @pl.when(pl.program_id(0) == 0)
        def _():
            dg_ref[...] = jnp.zeros_like(dg_ref)
            loss_ref[...] = jnp.zeros_like(loss_ref)

        dg_ref[...] += jnp.sum(dy * xhat, axis=0, keepdims=True)
        loss_ref[...] += jnp.broadcast_to(part, loss_ref.shape)

    row = _spec((ts, D), lambda i: (i, 0))
    vec = _spec((1, D), lambda i: (0, 0))
    return pl.pallas_call(
        body, name=name, grid=(S // ts,),
        in_specs=[row, vec, row], out_specs=[_spec((1, LANES), lambda i: (0, 0)), row, row, vec],
        out_shape=[_sd((1, LANES), F32), _sd((S, D), F32), _sd((S, D), BF), _sd((1, D), F32)],
        compiler_params=_params(("arbitrary",), 20 * ts * D * 4))(x, g, target)


def _swiglu(gate, up):
    sg = jax.nn.sigmoid(gate)
    silu = gate * sg
    return up * (sg * (1.0 + gate * (1.0 - sg))), silu, silu * up


def _swiglu_bwd_epi(accs, ex, orefs, ids):
    da = 0.5 * accs[0]
    orefs[0][0] = (da * ex[0][0].astype(F32)).astype(BF)
    orefs[0][1] = (da * ex[0][1].astype(F32)).astype(BF)


_GELU_C = math.sqrt(2.0 / math.pi)


def _gelu(x):
    return x * (0.5 * (1.0 + jnp.tanh(_GELU_C * (x + 0.044715 * (x * x * x)))))


def _gelu_and_grad(x):
    x2 = x * x
    t = jnp.tanh(_GELU_C * (x + 0.044715 * (x2 * x)))
    cdf = 0.5 * (1.0 + t)
    return x * cdf, cdf + x * ((0.5 * _GELU_C) * (1.0 - t * t) * (1.0 + (3.0 * 0.044715) * x2))


def _causal_block_mask():
    row = lax.broadcasted_iota(jnp.int32, (SGU_BLOCK, SGU_BLOCK), 0) // CHUNK
    col = lax.broadcasted_iota(jnp.int32, (SGU_BLOCK, SGU_BLOCK), 1) // CHUNK
    return row >= col


def _sgu_mid_fwd(name, puv, gain, bias, w_sp, b_sp):
    _, S, W = puv.shape
    G = SGU_GROUPS
    C = W // G
    T = SGU_BLOCK

    def body(puv_ref, gain_ref, bias_ref, w_ref, b_ref, out_ref):
        mask = _causal_block_mask()
        v = _gelu(puv_ref[1])
        mu = jnp.mean(v, axis=-1, keepdims=True)
        vc = v - mu
        rs = lax.rsqrt(jnp.mean(vc * vc, axis=-1, keepdims=True) + EPS)
        vln = (vc * rs * gain_ref[...] + bias_ref[...]).astype(BF)
        for g in range(G):
            wg = jnp.where(mask, w_ref[g], 0.0).astype(BF)
            mixed = jnp.dot(wg, vln[:, g * C:(g + 1) * C], preferred_element_type=F32) + b_ref[g]
            out_ref[:, g * C:(g + 1) * C] = (_gelu(puv_ref[0, :, g * C:(g + 1) * C]) * mixed).astype(BF)

    return pl.pallas_call(
        body, name=name, grid=(S // T,),
        in_specs=[_spec((2, T, W), lambda i: (0, i, 0)), _spec((1, W), lambda i: (0, 0)), _spec((1, W), lambda i: (0, 0)),
                  _spec((G, T, T), lambda i: (0, 0, 0)), _spec((G, T, 1), lambda i: (0, 0, 0))],
        out_specs=_spec((T, W), lambda i: (i, 0)), out_shape=_sd((S, W), BF),
        compiler_params=_params(("parallel",), 16 * T * W * 4))(puv, gain, bias, w_sp, b_sp)


def _sgu_mid_bwd(name, puv, dgated, gain, bias, w_sp, b_sp):
    _, S, W = puv.shape
    G = SGU_GROUPS
    C = W // G
    T = SGU_BLOCK

    def body(puv_ref, dg_ref, gain_ref, bias_ref, w_ref, b_ref, dpuv_ref, dgain_ref, dbias_ref, dw_ref, db_ref, dvln_ref):
        @pl.when(pl.program_id(0) == 0)
        def _():
            dgain_ref[...] = jnp.zeros_like(dgain_ref)
            dbias_ref[...] = jnp.zeros_like(dbias_ref)
            dw_ref[...] = jnp.zeros_like(dw_ref)
            db_ref[...] = jnp.zeros_like(db_ref)

        mask = _causal_block_mask()
        v, v_grad = _gelu_and_grad(puv_ref[1])
        mu = jnp.mean(v, axis=-1, keepdims=True)
        vc = v - mu
        rs = lax.rsqrt(jnp.mean(vc * vc, axis=-1, keepdims=True) + EPS)
        vhat = vc * rs
        gain_v = gain_ref[...]
        vln = (vhat * gain_v + bias_ref[...]).astype(BF)
        for g in range(G):
            sl = slice(g * C, (g + 1) * C)
            wg = jnp.where(mask, w_ref[g], 0.0).astype(BF)
            vg = vln[:, sl]
            mixed = jnp.dot(wg, vg, preferred_element_type=F32) + b_ref[g]
            u, u_grad = _gelu_and_grad(puv_ref[0, :, sl])
            dgt = dg_ref[:, sl].astype(F32)
            dpuv_ref[0, :, sl] = (dgt * mixed * u_grad).astype(BF)
            dmix = dgt * u
            db_ref[g] += jnp.sum(dmix, axis=-1, keepdims=True)
            dmb = dmix.astype(BF)
            dwg = lax.dot_general(dmb, vg, (((1,), (1,)), ((), ())), preferred_element_type=F32)
            dw_ref[g] += jnp.where(mask, dwg, 0.0)
            dvln_ref[:, sl] = lax.dot_general(wg, dmb, (((0,), (0,)), ((), ())), preferred_element_type=F32)
        dvln = dvln_ref[...]
        dgain_ref[...] += jnp.sum(dvln * vhat, axis=0, keepdims=True)
        dbias_ref[...] += jnp.sum(dvln, axis=0, keepdims=True)
        dvh = dvln * gain_v
        m1 = jnp.mean(dvh, axis=-1, keepdims=True)
        m2 = jnp.mean(dvh * vhat, axis=-1, keepdims=True)
        dv = rs * (dvh - m1 - vhat * m2)
        dpuv_ref[1] = (dv * v_grad).astype(BF)

    vec = _spec((1, W), lambda i: (0, 0))
    wsp = _spec((G, T, T), lambda i: (0, 0, 0))
    bsp = _spec((G, T, 1), lambda i: (0, 0, 0))
    return pl.pallas_call(
        body, name=name, grid=(S // T,),
        in_specs=[_spec((2, T, W), lambda i: (0, i, 0)), _spec((T, W), lambda i: (i, 0)), vec, vec, wsp, bsp],
        out_specs=[_spec((2, T, W), lambda i: (0, i, 0)), vec, vec, wsp, bsp],
        out_shape=[_sd((2, S, W), BF), _sd((1, W), F32), _sd((1, W), F32), _sd((G, T, T), F32), _sd((G, T, 1), F32)],
        scratch_shapes=[pltpu.VMEM((T, W), F32)],
        compiler_params=_params(("arbitrary",), 24 * T * W * 4))(puv, dgated, gain, bias, w_sp, b_sp)


def _rope_tables(positions):
    half = QK_ROPE // 2
    inv_freq = 1.0 / (ROPE_THETA ** (jnp.arange(half, dtype=F32) / half))
    ang = positions.astype(F32)[:, None] * inv_freq[None, :]
    cos, sin = jnp.cos(ang), jnp.sin(ang)
    z = jnp.zeros_like(cos)
    return (jnp.concatenate([cos, cos, z, z], axis=1), jnp.concatenate([-sin, z, z, z], axis=1),
            jnp.concatenate([z, sin, z, z], axis=1))


def _rope(x, cos, sa, sb):
    return x * cos + pltpu.roll(x, HEAD_PAD - QK_ROPE // 2, 1) * sa + pltpu.roll(x, QK_ROPE // 2, 1) * sb


def _rope_t(dy, cos, sa, sb):
    return dy * cos + pltpu.roll(dy * sa, QK_ROPE // 2, 1) + pltpu.roll(dy * sb, HEAD_PAD - QK_ROPE // 2, 1)


def _rms_rows(x, g):
    r = lax.rsqrt(jnp.mean(x * x, axis=-1, keepdims=True) + EPS)
    return x * r * g


def _rms_rows_bwd(dy, x, g):
    r = lax.rsqrt(jnp.mean(x * x, axis=-1, keepdims=True) + EPS)
    xhat = x * r
    dxh = dy * g
    cm = jnp.mean(dxh * xhat, axis=-1, keepdims=True)
    return r * (dxh - xhat * cm), jnp.sum(dy * xhat, axis=0, keepdims=True)


def _mla_mid_fwd(name, proj, qg, kvg, cos, sa, sb):
    S, P = proj.shape
    ts = _pick(S, (512, 256, 128))

    def body(p_ref, qg_ref, kvg_ref, cos_ref, sa_ref, sb_ref, lat_ref, kr_ref):
        lat_ref[0] = _rms_rows(p_ref[:, :Q_LORA], qg_ref[...]).astype(BF)
        lat_ref[1] = _rms_rows(p_ref[:, Q_LORA:Q_LORA + KV_LORA], kvg_ref[...]).astype(BF)
        kr_ref[...] = _rope(p_ref[:, Q_LORA + KV_LORA:], cos_ref[...], sa_ref[...], sb_ref[...]).astype(BF)

    tab = _spec((ts, HEAD_PAD), lambda i: (i, 0))
    return pl.pallas_call(
        body, name=name, grid=(S // ts,),
        in_specs=[_spec((ts, P), lambda i: (i, 0)), _spec((1, Q_LORA), lambda i: (0, 0)),
                  _spec((1, KV_LORA), lambda i: (0, 0)), tab, tab, tab],
        out_specs=[_spec((2, ts, Q_LORA), lambda i: (0, i, 0)), tab],
        out_shape=[_sd((2, S, Q_LORA), BF), _sd((S, HEAD_PAD), BF)],
        compiler_params=_params(("parallel",), 16 * ts * P * 4))(proj, qg, kvg, cos, sa, sb)


def _mla_mid_bwd(name, proj, dqn, dkvn, dkr_heads, qg, kvg, cos, sa, sb):
    S, P = proj.shape
    H = dkr_heads.shape[0]
    ts = _pick(S, (256, 128))

    def body(p_ref, dqn_ref, dkvn_ref, dkr_ref, qg_ref, kvg_ref, cos_ref, sa_ref, sb_ref, dp_ref, dqg_ref, dkvg_ref):
        @pl.when(pl.program_id(0) == 0)
        def _():
            dqg_ref[...] = jnp.zeros_like(dqg_ref)
            dkvg_ref[...] = jnp.zeros_like(dkvg_ref)

        dq, dqg = _rms_rows_bwd(dqn_ref[...], p_ref[:, :Q_LORA], qg_ref[...])
        dkv, dkvg = _rms_rows_bwd(dkvn_ref[...], p_ref[:, Q_LORA:Q_LORA + KV_LORA], kvg_ref[...])
        dqg_ref[...] += dqg
        dkvg_ref[...] += dkvg
        dkr = dkr_ref[0]
        for h in range(1, H):
            dkr = dkr + dkr_ref[h]
        dp_ref[:, :Q_LORA] = dq.astype(BF)
        dp_ref[:, Q_LORA:Q_LORA + KV_LORA] = dkv.astype(BF)
        dp_ref[:, Q_LORA + KV_LORA:] = _rope_t(dkr, cos_ref[...], sa_ref[...], sb_ref[...]).astype(BF)

    tab = _spec((ts, HEAD_PAD), lambda i: (i, 0))
    lat = _spec((ts, Q_LORA), lambda i: (i, 0))
    gq = _spec((1, Q_LORA), lambda i: (0, 0))
    return pl.pallas_call(
        body, name=name, grid=(S // ts,),
        in_specs=[_spec((ts, P), lambda i: (i, 0)), lat, lat, _spec((H, ts, HEAD_PAD), lambda i: (0, i, 0)),
                  gq, gq, tab, tab, tab],
        out_specs=[_spec((ts, P), lambda i: (i, 0)), gq, gq],
        out_shape=[_sd((S, P), BF), _sd((1, Q_LORA), F32), _sd((1, KV_LORA), F32)],
        compiler_params=_params(("arbitrary",), 24 * ts * P * 4))(proj, dqn, dkvn, dkr_heads, qg, kvg, cos, sa, sb)


def _attn_tile(S):
    return _pick(S, (512,)) if S >= 2048 else _pick(S, (128,))


def _diag_mask(t, transposed):
    q = lax.broadcasted_iota(jnp.int32, (t, t), 1 if transposed else 0) // CHUNK
    k = lax.broadcasted_iota(jnp.int32, (t, t), 0 if transposed else 1) // CHUNK
    return k <= q


_NT = (((1,), (1,)), ((), ()))


def _attn_fwd(name, q_all, kv_all, kr):
    _, S, HP = q_all.shape
    H = HP // HEAD_PAD
    t = _attn_tile(S)
    nq = S // t
    ng = ATTN_GROUPS
    tg = t // ng

    def body(q_ref, kv_ref, kr_ref, o_ref, lse_ref, kcat_ref):
        i = pl.program_id(1)

        @pl.when(i == 0)
        def _():
            kcat_ref[:, :HEAD_PAD] = kv_ref[0]
            kcat_ref[:, HEAD_PAD:] = kr_ref[...]

        qs = [jnp.concatenate([q_ref[0, g * tg:(g + 1) * tg], q_ref[1, g * tg:(g + 1) * tg]], axis=1) for g in range(ng)]

        def step(j, carry, masked):
            off = pl.multiple_of(j * t, t)
            kj = kcat_ref[pl.ds(off, t), :]
            vj = kv_ref[1, pl.ds(off, t), :]
            out = []
            for g in range(ng):
                m, l, acc = carry[g]
                s = lax.dot_general(qs[g], kj, _NT, preferred_element_type=F32)
                if masked:
                    s = jnp.where(_diag_mask(t, False)[g * tg:(g + 1) * tg], s, NEG)
                m2 = jnp.maximum(m, jnp.max(s, axis=-1, keepdims=True))
                al = jnp.exp(m - m2)
                p = jnp.exp(s - m2)
                l2 = al * l + jnp.sum(p, axis=-1, keepdims=True)
                acc2 = al * acc + jnp.dot(p.astype(BF), vj, preferred_element_type=F32)
                out.append((m2, l2, acc2))
            return tuple(out)

        init = tuple((jnp.full((tg, 1), NEG, F32), jnp.zeros((tg, 1), F32), jnp.zeros((tg, V_DIM), F32))
                     for _ in range(ng))
        def several(jj, c):
            for u in range(ATTN_UNROLL):
                c = step(jj * ATTN_UNROLL + u, c, False)
            return c

        carry = lax.fori_loop(0, i // ATTN_UNROLL, several, init)
        carry = lax.fori_loop((i // ATTN_UNROLL) * ATTN_UNROLL, i, lambda j, c: step(j, c, False), carry)
        carry = step(i, carry, True)
        for g in range(ng):
            m, l, acc = carry[g]
            o_ref[g * tg:(g + 1) * tg, :] = acc / l
            lse_ref[g * tg:(g + 1) * tg, :] = jnp.broadcast_to(m + jnp.log(l), (tg, LANES))

    return pl.pallas_call(
        body, name=name, grid=(H, nq),
        in_specs=[_spec((2, t, HEAD_PAD), lambda h, i: (0, i, h)), _spec((2, S, HEAD_PAD), lambda h, i: (0, 0, h)),
                  _spec((S, HEAD_PAD), lambda h, i: (0, 0))],
        out_specs=[_spec((t, HEAD_PAD), lambda h, i: (i, h)), _spec((None, t, LANES), lambda h, i: (h, i, 0))],
        out_shape=[_sd((S, HP), F32), _sd((H, S, LANES), F32)],
        scratch_shapes=[pltpu.VMEM((S, 2 * HEAD_PAD), BF)],
        compiler_params=_params(("parallel", "arbitrary"), 8 * S * HEAD_PAD * 2 + 24 * t * t * 4))(q_all, kv_all, kr)


def _attn_dq(name, q_all, kv_all, kr, do, o, lse, cos, sa, sb, scale):
    _, S, HP = q_all.shape
    H = HP // HEAD_PAD
    t = _attn_tile(S)
    nq = S // t

    def body(q_ref, kv_ref, kr_ref, do_ref, o_ref, lse_ref, cos_ref, sa_ref, sb_ref, dq_ref, dl_ref, kcat_ref):
        i = pl.program_id(1)

        @pl.when(i == 0)
        def _():
            kcat_ref[:, :HEAD_PAD] = kv_ref[0]
            kcat_ref[:, HEAD_PAD:] = kr_ref[...]

        ng = ATTN_GROUPS
        tg = t // ng
        rows = [slice(g * tg, (g + 1) * tg) for g in range(ng)]
        qs = [jnp.concatenate([q_ref[0, r], q_ref[1, r]], axis=1) for r in rows]
        dobs = [do_ref[r, :].astype(BF) for r in rows]
        lses = [lse_ref[r, 0:1] for r in rows]
        dls = [jnp.sum(do_ref[r, :] * o_ref[r, :], axis=-1, keepdims=True) for r in rows]
        for g in range(ng):
            dl_ref[rows[g], :] = jnp.broadcast_to(dls[g], (tg, LANES))

        def step(j, dqs, masked):
            off = pl.multiple_of(j * t, t)
            kj = kcat_ref[pl.ds(off, t), :]
            vj = kv_ref[1, pl.ds(off, t), :]
            out = []
            for g in range(ng):
                s = lax.dot_general(qs[g], kj, _NT, preferred_element_type=F32)
                if masked:
                    s = jnp.where(_diag_mask(t, False)[rows[g]], s, NEG)
                p = jnp.exp(s - lses[g])
                dp = lax.dot_general(dobs[g], vj, _NT, preferred_element_type=F32)
                ds = (p * (dp - dls[g])).astype(BF)
                out.append(dqs[g] + jnp.dot(ds, kj, preferred_element_type=F32))
            return tuple(out)

        init = tuple(jnp.zeros((tg, 2 * HEAD_PAD), F32) for _ in range(ng))
        def several(jj, c):
            for u in range(ATTN_UNROLL):
                c = step(jj * ATTN_UNROLL + u, c, False)
            return c

        dqs = lax.fori_loop(0, i // ATTN_UNROLL, several, init)
        dqs = lax.fori_loop((i // ATTN_UNROLL) * ATTN_UNROLL, i, lambda j, c: step(j, c, False), dqs)
        dqs = step(i, dqs, True)
        for g in range(ng):
            dq_ref[0, rows[g]] = (dqs[g][:, :HEAD_PAD] * scale).astype(BF)
            dq_ref[1, rows[g]] = (_rope_t(dqs[g][:, HEAD_PAD:], cos_ref[rows[g], :], sa_ref[rows[g], :],
                                          sb_ref[rows[g], :]) * scale).astype(BF)

    tab = _spec((t, HEAD_PAD), lambda h, i: (i, 0))
    stat = _spec((None, t, LANES), lambda h, i: (h, i, 0))
    head_tile = _spec((t, HEAD_PAD), lambda h, i: (i, h))
    return pl.pallas_call(
        body, name=name, grid=(H, nq),
        in_specs=[_spec((2, t, HEAD_PAD), lambda h, i: (0, i, h)), _spec((2, S, HEAD_PAD), lambda h, i: (0, 0, h)),
                  _spec((S, HEAD_PAD), lambda h, i: (0, 0)), head_tile, head_tile, stat, tab, tab, tab],
        out_specs=[_spec((2, t, HEAD_PAD), lambda h, i: (0, i, h)), stat],
        out_shape=[_sd((2, S, HP), BF), _sd((H, S, LANES), F32)],
        scratch_shapes=[pltpu.VMEM((S, 2 * HEAD_PAD), BF)],
        compiler_params=_params(("parallel", "arbitrary"), 8 * S * HEAD_PAD * 2 + 32 * t * t * 4))(
            q_all, kv_all, kr, do, o, lse, cos, sa, sb)


def _attn_dkv(name, q_all, kv_all, kr, do, lse_row, delta_row):
    _, S, HP = q_all.shape
    H = HP // HEAD_PAD
    t = _attn_tile(S)
    nq = S // t

    def body(q_ref, kv_ref, kr_ref, do_ref, lse_ref, dl_ref, dkv_ref, dkr_ref, qcat_ref):
        j = pl.program_id(1)

        @pl.when(j == 0)
        def _():
            qcat_ref[:, :HEAD_PAD] = q_ref[0]
            qcat_ref[:, HEAD_PAD:] = q_ref[1]

        ng = ATTN_GROUPS
        tg = t // ng
        rows = [slice(g * tg, (g + 1) * tg) for g in range(ng)]
        kjs = [jnp.concatenate([kv_ref[0, r], kr_ref[r, :]], axis=1) for r in rows]
        vjs = [kv_ref[1, r] for r in rows]

        def step(i, carry, masked):
            off = pl.multiple_of(i * t, t)
            qi = qcat_ref[pl.ds(off, t), :]
            doi = do_ref[pl.ds(off, t), :].astype(BF)
            lse_i = lse_ref[i]
            dl_i = dl_ref[i]
            out = []
            for g in range(ng):
                dk, dv = carry[g]
                st = lax.dot_general(kjs[g], qi, _NT, preferred_element_type=F32)
                if masked:
                    st = jnp.where(_diag_mask(t, True)[rows[g]], st, NEG)
                pt = jnp.exp(st - lse_i)
                dv2 = dv + jnp.dot(pt.astype(BF), doi, preferred_element_type=F32)
                dpt = lax.dot_general(vjs[g], doi, _NT, preferred_element_type=F32)
                dst = (pt * (dpt - dl_i)).astype(BF)
                out.append((dk + jnp.dot(dst, qi, preferred_element_type=F32), dv2))
            return tuple(out)

        init = tuple((jnp.zeros((tg, 2 * HEAD_PAD), F32), jnp.zeros((tg, V_DIM), F32)) for _ in range(ng))
        def several(ii, c):
            for u in range(ATTN_UNROLL):
                c = step(j + 1 + ii * ATTN_UNROLL + u, c, False)
            return c

        carry = step(j, init, True)
        trips = (nq - 1 - j) // ATTN_UNROLL
        carry = lax.fori_loop(0, trips, several, carry)
        carry = lax.fori_loop(j + 1 + trips * ATTN_UNROLL, nq, lambda i, c: step(i, c, False), carry)
        for g in range(ng):
            dk, dv = carry[g]
            dkv_ref[0, rows[g]] = dk[:, :HEAD_PAD].astype(BF)
            dkv_ref[1, rows[g]] = dv.astype(BF)
            dkr_ref[rows[g], :] = dk[:, HEAD_PAD:]

    stat = _spec((None, nq, 1, t), lambda h, j: (h, 0, 0, 0))
    return pl.pallas_call(
        body, name=name, grid=(H, nq),
        in_specs=[_spec((2, S, HEAD_PAD), lambda h, j: (0, 0, h)), _spec((2, t, HEAD_PAD), lambda h, j: (0, j, h)),
                  _spec((t, HEAD_PAD), lambda h, j: (j, 0)), _spec((S, HEAD_PAD), lambda h, j: (0, h)), stat, stat],
        out_specs=[_spec((2, t, HEAD_PAD), lambda h, j: (0, j, h)), _spec((None, t, HEAD_PAD), lambda h, j: (h, j, 0))],
        out_shape=[_sd((2, S, HP), BF), _sd((H, S, HEAD_PAD), F32)],
        scratch_shapes=[pltpu.VMEM((S, 2 * HEAD_PAD), BF)],
        compiler_params=_params(("parallel", "arbitrary"), 8 * S * HEAD_PAD * 4 + 32 * t * t * 4))(
            q_all, kv_all, kr, do, lse_row, delta_row)


def _place():
    x, y, c = lax.axis_index("x"), lax.axis_index("y"), lax.axis_index("c")
    return x, y, c


def _all_gather(blocks):
    n = len(blocks)

    def body(*refs):
        ins, outs = refs[:n], refs[n:2 * n]
        send_sems, recv_sems, local_sems = refs[2 * n:]
        x, y, c = _place()
        me = 4 * x + 2 * y + c
        sibling = (x, y, 1 - c)
        chips = [(1 - x, y), (x, 1 - y), (1 - x, 1 - y)]

        def slab(a, px, py, pc):
            return outs[a].at[4 * px + 2 * py + pc]

        def copy(a, k, src, dst, to):
            return pltpu.make_async_remote_copy(src_ref=src, dst_ref=dst, send_sem=send_sems.at[a, k],
                                                recv_sem=recv_sems.at[a, k], device_id=to, device_id_type=MESH)

        local = [pltpu.make_async_copy(ins[a], outs[a].at[me], local_sems.at[a]) for a in range(n)]
        for cp in local:
            cp.start()
        sends = []
        for a in range(n):
            mine = slab(a, x, y, c)
            sends.append(copy(a, 0, ins[a], mine, sibling))
            for j, chip in enumerate(chips):
                sends.append(copy(a, 1 + j, ins[a], mine, (*chip, c)))
        for cp in sends:
            cp.start()
        for j, chip in enumerate(chips):
            for a in range(n):
                got = slab(a, *chip, c)
                copy(a, 1 + j, got, got, (x, y, c)).wait_recv()
                fwd = copy(a, 4 + j, got, got, sibling)
                fwd.start()
                sends.append(fwd)
        for a in range(n):
            got = slab(a, x, y, 1 - c)
            copy(a, 0, got, got, (x, y, c)).wait_recv()
            for j, chip in enumerate(chips):
                got = slab(a, *chip, 1 - c)
                copy(a, 4 + j, got, got, (x, y, c)).wait_recv()
        for cp in sends:
            cp.wait_send()
        for cp in local:
            cp.wait()

    return pl.pallas_call(
        body, name="weights_all_gather", in_specs=[_ANY] * n, out_specs=[_ANY] * n,
        out_shape=[_sd((N_DEV,) + b.shape, b.dtype) for b in blocks],
        scratch_shapes=[pltpu.SemaphoreType.DMA((n, 7)), pltpu.SemaphoreType.DMA((n, 7)), pltpu.SemaphoreType.DMA((n,))],
    )(*blocks)


_HBM = pl.BlockSpec(memory_space=pltpu.HBM)
_SEM = pl.BlockSpec(memory_space=pltpu.SEMAPHORE)
_EFFECT = pltpu.SideEffectType.DATAFLOW_SIDE_EFFECTING


def _peers():
    x, y, c = _place()
    out = []
    for m in range(1, N_DEV):
        px, py, pc = x ^ (m >> 2), y ^ ((m >> 1) & 1), c ^ (m & 1)
        out.append((m, (px, py, pc), 4 * px + 2 * py + pc))
    return 4 * x + 2 * y + c, out


def _exchange_copies(src, land, send_sem, recv_sem, gather):
    me, peers = _peers()
    cps = []
    for a in range(len(src)):
        for m, pos, idx in peers:
            s_ref, d_ref = (src[a], land[a].at[me]) if gather else (src[a].at[idx], land[a].at[m - 1])
            k = a * (N_DEV - 1) + m - 1
            cps.append(pltpu.make_async_remote_copy(src_ref=s_ref, dst_ref=d_ref, send_sem=send_sem.at[k],
                                                    recv_sem=recv_sem.at[k], device_id=pos, device_id_type=MESH))
    return cps


def _xstart(name, srcs, gather, after=()):
    n = len(srcs)
    if gather:
        land_shapes = [(N_DEV,) + s.shape for s in srcs]
    else:
        land_shapes = [(N_DEV - 1,) + s.shape[1:] for s in srcs]

    def body(*refs):
        src, land = refs[:n], refs[n:2 * n]
        send_sem, recv_sem = refs[2 * n + len(after)], refs[2 * n + len(after) + 1]
        token = refs[-1]
        for cp in _exchange_copies(src, land, send_sem, recv_sem, gather):
            cp.start()
        token[...] = jnp.zeros_like(token)

    sem = pltpu.SemaphoreType.DMA((n * (N_DEV - 1),))
    out_shape = ([sem, sem] + [pltpu.HBM(s.shape, s.dtype) for s in srcs]
                 + [pltpu.HBM(sh, s.dtype) for sh, s in zip(land_shapes, srcs)] + [_sd((8, LANES), F32)])
    args = [pltpu.with_memory_space_constraint(s, pltpu.HBM) for s in srcs]
    args += [pltpu.with_memory_space_constraint(lax.empty(sh, s.dtype), pltpu.HBM) for sh, s in zip(land_shapes, srcs)]
    res = pl.pallas_call(
        body, name=name, out_shape=out_shape, in_specs=[_HBM] * (2 * n) + [_ANY] * len(after),
        out_specs=[_SEM, _SEM] + [_HBM] * (2 * n) + [pl.BlockSpec(memory_space=pltpu.VMEM)],
        input_output_aliases={i: 2 + i for i in range(2 * n)},
        compiler_params=pltpu.CompilerParams(has_side_effects=_EFFECT))(*args, *after)
    return dict(send=res[0], recv=res[1], srcs=list(res[2:2 + n]), lands=list(res[2 + n:2 + 2 * n]), token=res[-1])


def _xwait(name, st, after, gather):
    n = len(st["srcs"])

    def body(*refs):
        src, land = refs[:n], refs[n:2 * n]
        send_sem, recv_sem = refs[2 * n], refs[2 * n + 1]
        for cp in _exchange_copies(src, land, send_sem, recv_sem, gather):
            cp.wait_send()
            cp.wait_recv()

    arrays = st["srcs"] + st["lands"]
    res = pl.pallas_call(
        body, name=name, out_shape=[pltpu.HBM(a.shape, a.dtype) for a in arrays],
        in_specs=[_HBM] * (2 * n) + [_SEM, _SEM, _ANY], out_specs=[_HBM] * (2 * n),
        input_output_aliases={i: i for i in range(2 * n)},
        compiler_params=pltpu.CompilerParams(has_side_effects=_EFFECT))(*arrays, st["send"], st["recv"], after)
    return list(res[:n]), list(res[n:])


def _put_own(name, land, block, me_arr):
    R, C = block.shape
    tr = _pick(R, (512, 256, 128, 64, 32, 16))

    def body(me_ref, b_ref, land_ref, o_ref):
        o_ref[...] = b_ref[...]

    return pl.pallas_call(
        body, name=name, out_shape=_sd(land.shape, land.dtype),
        grid_spec=pltpu.PrefetchScalarGridSpec(
            num_scalar_prefetch=1, grid=(R // tr,),
            in_specs=[_spec((tr, C), lambda i, me_ref: (i, 0)), _ANY],
            out_specs=_spec((None, tr, C), lambda i, me_ref: (me_ref[0], i, 0))),
        input_output_aliases={2: 0},
        compiler_params=_params(("parallel",), 8 * tr * C * 2))(me_arr, block, land)


def _all_reduce_small(name, part):
    R = part.shape[0]

    def body(p_ref, out_ref, gath_ref, send_sems, recv_sems):
        x, y, c = _place()
        me = 4 * x + 2 * y + c
        gath_ref[me] = p_ref[...]
        cps = []
        for m in range(1, N_DEV):
            to = (x ^ (m >> 2), y ^ ((m >> 1) & 1), c ^ (m & 1))
            cps.append(pltpu.make_async_remote_copy(
                src_ref=p_ref, dst_ref=gath_ref.at[me], send_sem=send_sems.at[m - 1], recv_sem=recv_sems.at[m - 1],
                device_id=to, device_id_type=MESH))
        for cp in cps:
            cp.start()
        for m in range(1, N_DEV):
            frm = 4 * (x ^ (m >> 2)) + 2 * (y ^ ((m >> 1) & 1)) + (c ^ (m & 1))
            pltpu.make_async_remote_copy(
                src_ref=p_ref, dst_ref=gath_ref.at[frm], send_sem=send_sems.at[m - 1], recv_sem=recv_sems.at[m - 1],
                device_id=(x, y, c), device_id_type=MESH).wait_recv()
        for cp in cps:
            cp.wait_send()
        tot = gath_ref[0]
        for k in range(1, N_DEV):
            tot = tot + gath_ref[k]
        out_ref[...] = tot

    vm = pl.BlockSpec(memory_space=pltpu.VMEM)
    return pl.pallas_call(
        body, name=name, in_specs=[vm], out_specs=vm, out_shape=_sd((R, LANES), F32),
        scratch_shapes=[pltpu.VMEM((N_DEV, R, LANES), F32), pltpu.SemaphoreType.DMA((N_DEV - 1,)),
                        pltpu.SemaphoreType.DMA((N_DEV - 1,))],
        compiler_params=pltpu.CompilerParams(vmem_limit_bytes=VMEM_FLOOR),
    )(part)


def _adam_math(w, g, m, v):
    m2 = ADAM_B1 * m + (1.0 - ADAM_B1) * g
    v2 = ADAM_B2 * v + (1.0 - ADAM_B2) * (g * g)
    m_hat = m2 / (1.0 - ADAM_B1 ** ADAM_STEP)
    v_hat = v2 / (1.0 - ADAM_B2 ** ADAM_STEP)
    delta = -ADAM_LR * (m_hat / (jnp.sqrt(v_hat) + ADAM_EPS) + ADAM_WD * w)
    return delta, m2, v2


def _adamw_sharded(name, lands, fulls, me_arr, w, m, v):
    n_l = len(lands)
    R, C = lands[0].shape[1], lands[0].shape[2]
    tr = _pick(R, (128, 64, 32, 16, 8))
    nr = R // tr

    def body(me_ref, *refs):
        land_refs, own_refs = refs[:n_l], refs[n_l:2 * n_l]
        w_ref, m_ref, v_ref, g_ref, d_ref, m2_ref, v2_ref = refs[2 * n_l:]
        layer = pl.program_id(0)
        for ll in range(n_l):
            @pl.when(layer == ll)
            def _(ll=ll):
                g = own_refs[ll][...].astype(F32)
                for j in range(N_DEV - 1):
                    g = g + land_refs[ll][j].astype(F32)
                delta, m2, v2 = _adam_math(w_ref[...], g, m_ref[...], v_ref[...])
                g_ref[...] = g
                d_ref[...] = delta
                m2_ref[...] = m2
                v2_ref[...] = v2

    def row_of(ll):
        return lambda l, i, me_ref: jnp.where(l == ll, i, 0)

    in_specs = [_spec((N_DEV - 1, tr, C), lambda l, i, me_ref, f=row_of(ll): (0, f(l, i, me_ref), 0)) for ll in range(n_l)]
    in_specs += [_spec((None, tr, C), lambda l, i, me_ref, f=row_of(ll): (me_ref[0], f(l, i, me_ref), 0))
                 for ll in range(n_l)]
    blk = _spec((tr, C), lambda l, i, me_ref: (l * nr + i, 0))
    return pl.pallas_call(
        body, name=name, out_shape=[_sd(w.shape, F32)] * 4,
        grid_spec=pltpu.PrefetchScalarGridSpec(num_scalar_prefetch=1, grid=(n_l, nr), in_specs=in_specs + [blk] * 3,
                                               out_specs=[blk] * 4),
        compiler_params=_params(("parallel", "parallel"), (4 * n_l * N_DEV + 40) * tr * C * 4))(
            me_arr, *lands, *fulls, w, m, v)


def _adamw_packed(name, w, g, m, v):
    R = w.shape[0]

    def body(w_ref, g_ref, m_ref, v_ref, d_ref, m2_ref, v2_ref):
        delta, m2, v2 = _adam_math(w_ref[...], g_ref[...], m_ref[...], v_ref[...])
        d_ref[...] = delta
        m2_ref[...] = m2
        v2_ref[...] = v2

    vm = pl.BlockSpec(memory_space=pltpu.VMEM)
    return pl.pallas_call(
        body, name=name, in_specs=[vm] * 4, out_specs=[vm] * 3, out_shape=[_sd((R, LANES), F32)] * 3,
        compiler_params=pltpu.CompilerParams(vmem_limit_bytes=VMEM_FLOOR))(w, g, m, v)


def _pack(arrays):
    flat = jnp.concatenate([a.reshape(-1).astype(F32) for a in arrays])
    pad = (-flat.shape[0]) % (8 * LANES)
    return jnp.pad(flat, (0, pad)).reshape(-1, LANES)


def _unpack(packed, like):
    flat = packed.reshape(-1)
    out, pos = [], 0
    for a in like:
        n = math.prod(a.shape)
        out.append(flat[pos:pos + n].reshape(a.shape))
        pos += n
    return out


def _ffn_fwd(tag, x, gain, w_in_sm, w_out_of, deps=()):
    h = _rms_fwd(tag + "_norm", x, gain, deps)
    gu, act = _pair_in(tag + "_in", h, w_in_sm, BF, _swiglu)
    x_new = _rows_out(tag + "_out", act, w_out_of(act), x, 0.5)
    return x_new, (x, h, gu, act)


def _ffn_bwd(tag, d, d_bf, saved, gain, w_in_sm, w_out_sm, send_out, send_in):
    x, h, gu, act = saved
    dgu = _rows_dact(tag + "_dact", d_bf, w_out_sm, [gu], [_sd(gu.shape, BF)], _swiglu_bwd_epi)[0]
    g_out = _rows_wgrad(tag + "_wgrad_out", act, d_bf, 0.5)
    token_out = send_out(g_out)
    g_in = _cols_wgrad(tag + "_wgrad_in", h, dgu, deps=(token_out,))
    token = send_in(g_in)
    dh = _cols_dh(tag + "_dh", dgu, w_in_sm, deps=(token,))
    dx, dx_bf, dgain = _rms_bwd(tag + "_dnorm", dh, x, gain, d)
    return dx, dx_bf, dgain


def kernel(x, positions, ln_ffn1, ffn1_w_in, ffn1_w_out, ln_mix, ln_ffn2, ffn2_w_in, ffn2_w_out, sgu_w_in, sgu_v_gain, sgu_v_bias, sgu_w_spatial, sgu_b_spatial, sgu_w_out, mla_w_in, mla_q_norm, mla_w_q_up, mla_kv_norm, mla_w_kv_up, mla_w_out, ln_final, loss_target, m_ln_ffn1, m_ffn1_w_in, m_ffn1_w_out, m_ln_mix, m_ln_ffn2, m_ffn2_w_in, m_ffn2_w_out, m_sgu_w_in, m_sgu_v_gain, m_sgu_v_bias, m_sgu_w_spatial, m_sgu_b_spatial, m_sgu_w_out, m_mla_w_in, m_mla_q_norm, m_mla_w_q_up, m_mla_kv_norm, m_mla_w_kv_up, m_mla_w_out, m_ln_final, v_ln_ffn1, v_ffn1_w_in, v_ffn1_w_out, v_ln_mix, v_ln_ffn2, v_ffn2_w_in, v_ffn2_w_out, v_sgu_w_in, v_sgu_v_gain, v_sgu_v_bias, v_sgu_w_spatial, v_sgu_b_spatial, v_sgu_w_out, v_mla_w_in, v_mla_q_norm, v_mla_w_q_up, v_mla_kv_norm, v_mla_w_kv_up, v_mla_w_out, v_ln_final):
    S, D = x.shape[1], x.shape[2]
    L = ln_ffn1.shape[0]
    H = mla_w_q_up.shape[-1] * N_DEV // (QK_NOPE + QK_ROPE)
    xi, yi, ci = _place()
    me = 4 * xi + 2 * yi + ci
    me_arr = jnp.reshape(me, (1,)).astype(jnp.int32)
    big = dict(ffn1_w_in=ffn1_w_in, ffn1_w_out=ffn1_w_out, ffn2_w_in=ffn2_w_in, ffn2_w_out=ffn2_w_out,
               sgu_w_in=sgu_w_in, sgu_w_out=sgu_w_out, mla_w_in=mla_w_in, mla_w_q_up=mla_w_q_up,
               mla_w_kv_up=mla_w_kv_up, mla_w_out=mla_w_out)
    big_m = dict(ffn1_w_in=m_ffn1_w_in, ffn1_w_out=m_ffn1_w_out, ffn2_w_in=m_ffn2_w_in, ffn2_w_out=m_ffn2_w_out,
                 sgu_w_in=m_sgu_w_in, sgu_w_out=m_sgu_w_out, mla_w_in=m_mla_w_in, mla_w_q_up=m_mla_w_q_up,
                 mla_w_kv_up=m_mla_w_kv_up, mla_w_out=m_mla_w_out)
    big_v = dict(ffn1_w_in=v_ffn1_w_in, ffn1_w_out=v_ffn1_w_out, ffn2_w_in=v_ffn2_w_in, ffn2_w_out=v_ffn2_w_out,
                 sgu_w_in=v_sgu_w_in, sgu_w_out=v_sgu_w_out, mla_w_in=v_mla_w_in, mla_w_q_up=v_mla_w_q_up,
                 mla_w_kv_up=v_mla_w_kv_up, mla_w_out=v_mla_w_out)
    names = list(big)
    mla_names = ["mla_w_in", "mla_w_q_up", "mla_w_kv_up", "mla_w_out"]

    blocks = {(k, l): big[k][l].astype(BF) for k in names for l in range(big[k].shape[0])}
    first = ("ffn1_w_in", 0)
    groups = {}
    for i in range(L):
        if i > 0:
            groups[f"ffn1_in_{i}"] = [("ffn1_w_in", i)]
        groups[f"ffn1_out_{i}"] = [("ffn1_w_out", i)]
        groups[f"mix_{i}"] = [("sgu_w_in", i // 2), ("sgu_w_out", i // 2)] if i % 2 == 0 else [(k, i // 2) for k in mla_names]
        groups[f"ffn2_{i}"] = [("ffn2_w_in", i), ("ffn2_w_out", i)]
    gathered = {first: _all_gather([blocks[first]])[0]}
    started, order_after = {}, (gathered[first],)
    for tag, grp in groups.items():
        started[tag] = _xstart(f"gather_start_{tag}", [blocks[k] for k in grp], True, order_after)
        order_after = (started[tag]["token"],)

    def fetch(tag, after):
        own, lands = _xwait(f"gather_wait_{tag}", started[tag], after, True)
        for k, blk, land in zip(groups[tag], own, lands):
            gathered[k] = _put_own(f"gather_own_{k[0]}_{k[1]}", land, blk, me_arr)

    def w_out_of(name, tag):
        def get(after):
            fetch(tag, after)
            return gathered[name]
        return get

    norm_rows = jnp.zeros((N_DEV, LANES), F32)
    mine = jnp.concatenate([mla_q_norm[0], mla_kv_norm[0]])
    norm_rows = lax.dynamic_update_slice(norm_rows, mine[None, :], (me, 0))
    norm_all = _all_reduce_small("norm_gains_gather", norm_rows)
    nq_sh = mla_q_norm.shape[1]
    q_gain = norm_all[:, :nq_sh].reshape(1, Q_LORA)
    kv_gain = norm_all[:, nq_sh:2 * nq_sh].reshape(1, KV_LORA)
    cos, sa, sb = _rope_tables(positions[0])
    scale = float((QK_NOPE + QK_ROPE) ** -0.5)

    xs = x[0]
    w_sp = sgu_w_spatial[0]
    b_sp = sgu_b_spatial[0][:, :, None]
    saved = []
    mla_w = {}
    deps = order_after
    for i in range(L):
        if i > 0:
            fetch(f"ffn1_in_{i}", xs)
        xs, s1 = _ffn_fwd(f"l{i}_ffn1", xs, ln_ffn1[i:i + 1], gathered[("ffn1_w_in", i)],
                          w_out_of(("ffn1_w_out", i), f"ffn1_out_{i}"), deps)
        deps = ()
        fetch(f"mix_{i}", xs)
        x_mix = xs
        h = _rms_fwd(f"l{i}_mix_norm", xs, ln_mix[i:i + 1])
        j = i // 2
        if i % 2 == 0:
            puv = _pair_in(f"l{i}_sgu_in", h, gathered[("sgu_w_in", j)], F32, None)[0]
            gated = _sgu_mid_fwd(f"l{i}_sgu_mid", puv, sgu_v_gain, sgu_v_bias, w_sp, b_sp)
            xs = _rows_out(f"l{i}_sgu_out", gated, gathered[("sgu_w_out", j)], xs, 1.0)
            sm = (x_mix, h, puv, gated)
        else:
            w_in_nat = gathered[("mla_w_in", j)].reshape(D, Q_LORA + KV_LORA + QK_ROPE)
            w_in_pad = jnp.pad(w_in_nat, ((0, 0), (0, HEAD_PAD - QK_ROPE)))
            wq_nat = jnp.transpose(gathered[("mla_w_q_up", j)], (1, 0, 2)).reshape(Q_LORA, H, QK_NOPE + QK_ROPE)
            wq_t = jnp.stack([wq_nat[:, :, :QK_NOPE].reshape(Q_LORA, H * HEAD_PAD),
                              jnp.pad(wq_nat[:, :, QK_NOPE:], ((0, 0), (0, 0), (0, HEAD_PAD - QK_ROPE))).reshape(
                                  Q_LORA, H * HEAD_PAD)])
            wkv_nat = jnp.transpose(gathered[("mla_w_kv_up", j)], (1, 0, 2)).reshape(KV_LORA, H, QK_NOPE + V_DIM)
            wkv_t = jnp.stack([wkv_nat[:, :, :QK_NOPE].reshape(KV_LORA, H * HEAD_PAD),
                               wkv_nat[:, :, QK_NOPE:].reshape(KV_LORA, H * HEAD_PAD)])
            w_o_nat = gathered[("mla_w_out", j)].reshape(H * V_DIM, D)
            mla_w[i] = (w_in_pad, wq_t, wkv_t, w_o_nat)
            proj = _mm2(f"l{i}_mla_in", h, w_in_pad, False, False, F32, tn_cands=(384, 128))
            lat, kr = _mla_mid_fwd(f"l{i}_mla_mid", proj, q_gain, kv_gain, cos, sa, sb)

            def q_epi(accs, ex, orefs, ids):
                orefs[0][...] = (accs[0] * scale).astype(BF)

            def qr_epi(accs, ex, orefs, ids):
                for hh in range(accs[0].shape[1] // HEAD_PAD):
                    sl = slice(hh * HEAD_PAD, (hh + 1) * HEAD_PAD)
                    orefs[0][:, sl] = (_rope(accs[0][:, sl], *ex) * scale).astype(BF)

            q_all = _mm_halves(f"l{i}_mla_q", (lat, 0), wq_t, BF, lambda half: (q_epi, qr_epi)[half], extras=(cos, sa, sb))
            kv_all = _mm_halves(f"l{i}_mla_kv", (lat, 1), wkv_t, BF)
            o, lse = _attn_fwd(f"l{i}_attn", q_all, kv_all, kr)
            xs = _mm2(f"l{i}_mla_out", o, w_o_nat, False, False, F32, res=xs)
            sm = (x_mix, h, proj, lat, kr, q_all, kv_all, o, lse)
        fetch(f"ffn2_{i}", xs)
        xs, s2 = _ffn_fwd(f"l{i}_ffn2", xs, ln_ffn2[i:i + 1], gathered[("ffn2_w_in", i)],
                          lambda after, i=i: gathered[("ffn2_w_out", i)])
        saved.append((s1, sm, s2))

    loss_row, d, d_bf, g_ln_final = _final_loss("final_loss", xs, ln_final[None, :], loss_target[0])
    loss = lax.psum(loss_row[0, 0], ("x", "y", "c"))

    sent = []

    def send(tag, keys, grads):
        st = _xstart(f"scatter_start_{tag}", grads, False)
        sent.append((tag, keys, st))
        return st["token"]

    g_ln1, g_ln2, g_lnm = [None] * L, [None] * L, [None] * L
    small_g = {}
    for i in reversed(range(L)):
        s1, sm, s2 = saved[i]
        d, d_bf, g_ln2[i] = _ffn_bwd(
            f"l{i}_ffn2", d, d_bf, s2, ln_ffn2[i:i + 1], gathered[("ffn2_w_in", i)], gathered[("ffn2_w_out", i)],
            lambda g, i=i: send(f"l{i}_ffn2_out", [("ffn2_w_out", i)], [g]),
            lambda g, i=i: send(f"l{i}_ffn2_in", [("ffn2_w_in", i)], [g]))
        j = i // 2
        if i % 2 == 0:
            x_mix, h, puv, gated = sm
            dgated = _rows_dact(f"l{i}_sgu_dgated", d_bf, gathered[("sgu_w_out", j)], [], [_sd(gated.shape, BF)], _store())[0]
            g_so = _rows_wgrad(f"l{i}_sgu_wgrad_out", gated, d_bf, None)
            token_out = send(f"l{i}_sgu_out", [("sgu_w_out", j)], [g_so])
            dpuv, dgain, dbias, dwsp, dbsp = _sgu_mid_bwd(f"l{i}_sgu_mid_bwd", puv, dgated, sgu_v_gain, sgu_v_bias,
                                                          w_sp, b_sp)
            small_g.update(sgu_v_gain=dgain, sgu_v_bias=dbias, sgu_w_spatial=dwsp[None], sgu_b_spatial=dbsp[None, :, :, 0])
            g_si = _cols_wgrad(f"l{i}_sgu_wgrad_in", h, dpuv, deps=(token_out,))
            token = send(f"l{i}_sgu_in", [("sgu_w_in", j)], [g_si])
            dh = _cols_dh(f"l{i}_sgu_dh", dpuv, gathered[("sgu_w_in", j)], deps=(token,))
        else:
            x_mix, h, proj, lat, kr, q_all, kv_all, o, lse = sm
            w_in_pad, wq_t, wkv_t, w_o_nat = mla_w[i]
            t = _attn_tile(S)
            do = _mm2(f"l{i}_mla_do", d_bf, w_o_nat, False, True, F32)
            g_wo = _mm2(f"l{i}_mla_wgrad_out", o, d_bf, True, False, BF)
            dq_all, delta = _attn_dq(f"l{i}_attn_dq", q_all, kv_all, kr, do, o, lse, cos, sa, sb, scale)
            lse_row = lse[:, :, 0].reshape(H, S // t, 1, t)
            delta_row = delta[:, :, 0].reshape(H, S // t, 1, t)
            dkv_all, dkr_heads = _attn_dkv(f"l{i}_attn_dkv", q_all, kv_all, kr, do, lse_row, delta_row)
            dqn = _mm2(f"l{i}_mla_dqn", (dq_all, 0), (wq_t, 0), False, True, F32)
            dqn = _mm2(f"l{i}_mla_dqn2", (dq_all, 1), (wq_t, 1), False, True, F32, res=dqn)
            dkvn = _mm2(f"l{i}_mla_dkvn", (dkv_all, 0), (wkv_t, 0), False, True, F32)
            dkvn = _mm2(f"l{i}_mla_dkvn2", (dkv_all, 1), (wkv_t, 1), False, True, F32, res=dkvn)
            g_wq = [_mm2(f"l{i}_mla_wgrad_q{t2}", (lat, 0), (dq_all, t2), True, False, BF) for t2 in range(2)]
            g_wkv = [_mm2(f"l{i}_mla_wgrad_kv{t2}", (lat, 1), (dkv_all, t2), True, False, BF) for t2 in range(2)]
            dproj, g_qn, g_kvn = _mla_mid_bwd(f"l{i}_mla_mid_bwd", proj, dqn, dkvn, dkr_heads, q_gain, kv_gain, cos, sa, sb)
            g_win = _mm2(f"l{i}_mla_wgrad_in", h, dproj, True, False, BF, tn_cands=(384, 128))
            n_in = Q_LORA + KV_LORA + QK_ROPE
            gq_nat = jnp.concatenate([g_wq[0].reshape(Q_LORA, H, HEAD_PAD),
                                      g_wq[1].reshape(Q_LORA, H, HEAD_PAD)[:, :, :QK_ROPE]], axis=2)
            gkv_nat = jnp.concatenate([g_wkv[0].reshape(KV_LORA, H, HEAD_PAD), g_wkv[1].reshape(KV_LORA, H, HEAD_PAD)], axis=2)
            token = send(f"l{i}_mla", [(k, j) for k in mla_names],
                         [g_win[:, :n_in].reshape(N_DEV, D // N_DEV, n_in),
                          jnp.transpose(gq_nat.reshape(Q_LORA, N_DEV, -1), (1, 0, 2)),
                          jnp.transpose(gkv_nat.reshape(KV_LORA, N_DEV, -1), (1, 0, 2)),
                          g_wo.reshape(N_DEV, H * V_DIM // N_DEV, D)])
            small_g.update(mla_q_norm=g_qn, mla_kv_norm=g_kvn)
            dh = _mm2(f"l{i}_mla_dh", dproj, w_in_pad, False, True, F32, tn_cands=(512, 256, 128), deps=(token,))
        d, d_bf, g_lnm[i] = _rms_bwd(f"l{i}_mix_dnorm", dh, x_mix, ln_mix[i:i + 1], d)
        d, d_bf, g_ln1[i] = _ffn_bwd(
            f"l{i}_ffn1", d, d_bf, s1, ln_ffn1[i:i + 1], gathered[("ffn1_w_in", i)], gathered[("ffn1_w_out", i)],
            lambda g, i=i: send(f"l{i}_ffn1_out", [("ffn1_w_out", i)], [g]),
            lambda g, i=i: send(f"l{i}_ffn1_in", [("ffn1_w_in", i)], [g]))
    grad_x = d[None]

    landed, partial = {}, {}
    for tag, keys, st in sent:
        fulls, lands = _xwait(f"scatter_wait_{tag}", st, d, False)
        for k, full, land in zip(keys, fulls, lands):
            partial[k], landed[k] = full, land
    big_out = {}
    for k in names:
        w = big[k]
        rc = (math.prod(w.shape[1:-1]), w.shape[-1])
        flat = (w.shape[0] * rc[0], rc[1])
        lands = [landed[(k, l)].reshape((N_DEV - 1,) + rc) for l in range(w.shape[0])]
        fulls = [partial[(k, l)].reshape((N_DEV,) + rc) for l in range(w.shape[0])]
        res = _adamw_sharded(f"adamw_{k}", lands, fulls, me_arr, w.reshape(flat), big_m[k].reshape(flat),
                             big_v[k].reshape(flat))
        big_out[k] = [r.reshape(w.shape) for r in res]

    small_g.update(ln_ffn1=jnp.concatenate(g_ln1), ln_mix=jnp.concatenate(g_lnm), ln_ffn2=jnp.concatenate(g_ln2),
                   ln_final=g_ln_final[0])
    small_names = ["ln_ffn1", "ln_mix", "ln_ffn2", "sgu_v_gain", "sgu_v_bias", "sgu_w_spatial", "sgu_b_spatial",
                   "ln_final", "mla_q_norm", "mla_kv_norm"]
    summed = _unpack(_all_reduce_small("small_grads_all_reduce", _pack([small_g[k] for k in small_names])),
                     [small_g[k] for k in small_names])
    small_grad = dict(zip(small_names, summed))
    for k in ("mla_q_norm", "mla_kv_norm"):
        small_grad[k] = lax.dynamic_slice(small_grad[k], (0, me * nq_sh), (1, nq_sh))
    small_w = dict(ln_ffn1=ln_ffn1, ln_mix=ln_mix, ln_ffn2=ln_ffn2, sgu_v_gain=sgu_v_gain, sgu_v_bias=sgu_v_bias,
                   sgu_w_spatial=sgu_w_spatial, sgu_b_spatial=sgu_b_spatial, ln_final=ln_final, mla_q_norm=mla_q_norm,
                   mla_kv_norm=mla_kv_norm)
    small_m = dict(ln_ffn1=m_ln_ffn1, ln_mix=m_ln_mix, ln_ffn2=m_ln_ffn2, sgu_v_gain=m_sgu_v_gain, sgu_v_bias=m_sgu_v_bias,
                   sgu_w_spatial=m_sgu_w_spatial, sgu_b_spatial=m_sgu_b_spatial, ln_final=m_ln_final,
                   mla_q_norm=m_mla_q_norm, mla_kv_norm=m_mla_kv_norm)
    small_v = dict(ln_ffn1=v_ln_ffn1, ln_mix=v_ln_mix, ln_ffn2=v_ln_ffn2, sgu_v_gain=v_sgu_v_gain, sgu_v_bias=v_sgu_v_bias,
                   sgu_w_spatial=v_sgu_w_spatial, sgu_b_spatial=v_sgu_b_spatial, ln_final=v_ln_final,
                   mla_q_norm=v_mla_q_norm, mla_kv_norm=v_mla_kv_norm)
    like = [small_w[k] for k in small_names]
    packed = _adamw_packed("adamw_small", _pack(like), _pack([small_grad[k] for k in small_names]),
                           _pack([small_m[k] for k in small_names]), _pack([small_v[k] for k in small_names]))
    small_out = {}
    unpacked = [_unpack(p, like) for p in packed]
    for idx, k in enumerate(small_names):
        small_out[k] = [small_grad[k].reshape(small_w[k].shape)] + [u[idx] for u in unpacked]

    order = ["ln_ffn1", "ffn1_w_in", "ffn1_w_out", "ln_mix", "ln_ffn2", "ffn2_w_in", "ffn2_w_out", "sgu_w_in",
             "sgu_v_gain", "sgu_v_bias", "sgu_w_spatial", "sgu_b_spatial", "sgu_w_out", "mla_w_in", "mla_q_norm",
             "mla_w_q_up", "mla_kv_norm", "mla_w_kv_up", "mla_w_out", "ln_final"]
    res = {k: (big_out[k] if k in big_out else small_out[k]) for k in order}
    outs = [loss, grad_x]
    for part in range(4):
        outs.extend(res[k][part] for k in order)
    return tuple(outs)
```

```python
import functools
import math

import jax
import jax.numpy as jnp
from jax import lax
from jax.experimental import pallas as pl
from jax.experimental.pallas import tpu as pltpu

F32 = jnp.float32
BF = jnp.bfloat16
MESH = pl.DeviceIdType.MESH

N_DEV = 8
EPS = 1e-6
CHUNK = 64
SGU_BLOCK = 128
SGU_GROUPS = 8
Q_LORA = 512
KV_LORA = 512
QK_NOPE = 128
QK_ROPE = 64
V_DIM = 128
ROPE_THETA = 10000.0
HEAD_PAD = 128
LANES = 128
ADAM_LR = 0.001
ADAM_B1 = 0.9
ADAM_B2 = 0.999
ADAM_EPS = 1e-08
ADAM_WD = 0.01
ADAM_STEP = 10
V7X_VMEM_BYTES = 64 * 1024 * 1024
VMEM_CAP = V7X_VMEM_BYTES - 6 * 1024 * 1024
VMEM_FLOOR = 32 * 1024 * 1024
NEG = -1e30
ATTN_GROUPS = 1
ATTN_UNROLLS = (4, 2)


def _pick(n, cands):
    for c in cands:
        if n % c == 0:
            return c
    return n


def _nbytes(shape, dtype):
    return math.prod(int(s) for s in shape if s is not None) * jnp.dtype(dtype).itemsize


def _params(sem, block_bytes):
    limit = int(min(VMEM_CAP, max(VMEM_FLOOR, block_bytes)))
    return pltpu.CompilerParams(dimension_semantics=sem, vmem_limit_bytes=limit)


def _spec(shape, fn):
    return pl.BlockSpec(shape, fn)


_ANY = pl.BlockSpec(memory_space=pl.ANY)


def _mm(name, grid, ops, pairs, extras, outs, epilogue, acc_shapes, deps=()):
    nk = grid[2]
    n_ops, n_ex, n_out = len(ops), len(extras), len(outs)

    def load(refs, idx):
        loader = ops[idx][2] if len(ops[idx]) > 2 else None
        return (refs[idx][...] if loader is None else loader(refs[idx])).astype(BF)

    def prod(refs, p):
        ia, ib, ta, tb, _ = p
        a = load(refs, ia)
        b = load(refs, ib)
        dims = (((0 if ta else 1,), (1 if tb else 0,)), ((), ()))
        return lax.dot_general(a, b, dims, preferred_element_type=F32)

    def body(*refs):
        op_refs = refs[:n_ops]
        ex_refs = refs[n_ops:n_ops + n_ex]
        n_in = n_ops + n_ex + len(deps)
        out_refs = refs[n_in:n_in + n_out]
        acc_refs = refs[n_in + n_out:]
        ids = (pl.program_id(0), pl.program_id(1))

        def finish(vals):
            epilogue(vals, [e[...] for e in ex_refs], out_refs, ids)

        if nk == 1:
            vals = [None] * len(acc_shapes)
            for p in pairs:
                r = prod(op_refs, p)
                vals[p[4]] = r if vals[p[4]] is None else vals[p[4]] + r
            finish(vals)
        else:
            k = pl.program_id(2)

            def products():
                vals = [None] * len(acc_shapes)
                for p in pairs:
                    r = prod(op_refs, p)
                    vals[p[4]] = r if vals[p[4]] is None else vals[p[4]] + r
                return vals

            @pl.when(k == 0)
            def _():
                for a, v in zip(acc_refs, products()):
                    a[...] = v

            @pl.when((k > 0) & (k < nk - 1))
            def _():
                for a, v in zip(acc_refs, products()):
                    a[...] += v

            @pl.when(k == nk - 1)
            def _():
                finish([a[...] + v for a, v in zip(acc_refs, products())])

    in_arrays = [o[0] for o in ops] + [e[0] for e in extras]
    in_specs = [o[1] for o in ops] + [e[1] for e in extras]
    in_arrays += list(deps)
    in_specs += [_ANY] * len(deps)
    blk = 0
    for entry in ops + extras:
        blk += 2 * _nbytes(entry[1].block_shape, entry[0].dtype)
    for sd, sp in outs:
        blk += 2 * _nbytes(sp.block_shape, sd.dtype)
    acc_b = sum(_nbytes(s, F32) for s in acc_shapes)
    blk += 6 * acc_b
    scratch = [pltpu.VMEM(s, F32) for s in acc_shapes] if nk > 1 else []
    res = pl.pallas_call(
        body, name=name, grid=grid, in_specs=in_specs,
        out_specs=[o[1] for o in outs], out_shape=[o[0] for o in outs],
        scratch_shapes=scratch,
        compiler_params=_params(("parallel", "parallel", "arbitrary"), blk))(*in_arrays)
    return res


def _store(scale=None):
    def epi(accs, ex, outs, ids):
        v = accs[0]
        if scale is not None:
            v = v * scale
        outs[0][...] = v.astype(outs[0].dtype)
    return epi


def _store_residual(scale):
    def epi(accs, ex, outs, ids):
        outs[0][...] = ex[0] + scale * accs[0]
    return epi


def _sd(shape, dtype):
    return jax.ShapeDtypeStruct(tuple(shape), dtype)


def _pair_in(name, h, w_sm, out_dtype, act):
    S, D = h.shape
    c = w_sm.shape[-1]
    tm = _pick(S, (512, 256, 128))
    half = N_DEV // 2
    ops = [(h, _spec((tm, D), lambda j, i, k: (i, 0))),
           (w_sm, _spec((None, D, c), lambda j, i, k: (j, 0, 0))),
           (w_sm, _spec((None, D, c), lambda j, i, k: (j + half, 0, 0)))]
    outs = [(_sd((2, S, half * c), out_dtype), _spec((2, tm, c), lambda j, i, k: (0, i, j)))]
    if act is not None:
        outs.append((_sd((S, half * c), BF), _spec((tm, c), lambda j, i, k: (i, j))))

    def epi(accs, ex, orefs, ids):
        if act is None:
            orefs[0][0] = accs[0].astype(out_dtype)
            orefs[0][1] = accs[1].astype(out_dtype)
        else:
            keep0, keep1, out = act(accs[0], accs[1])
            orefs[0][0] = keep0.astype(out_dtype)
            orefs[0][1] = keep1.astype(out_dtype)
            orefs[1][...] = out.astype(BF)

    return _mm(name, (half, S // tm, 1), ops, [(0, 1, False, False, 0), (0, 2, False, False, 1)], [], outs, epi,
               [(tm, c), (tm, c)])


def _two_slabs(ref):
    return jnp.concatenate([ref[0], ref[1]], axis=0)


def _rows_out(name, a, w_sm, res, scale):
    S = a.shape[0]
    r, D = w_sm.shape[-2], w_sm.shape[-1]
    tm = _pick(S, (1024, 512, 256, 128))
    tn = _pick(D, (1024, 512, 256, 128))
    ops = [(a, _spec((tm, 2 * r), lambda i, j, k: (i, k))),
           (w_sm, _spec((2, r, tn), lambda i, j, k: (k, 0, j)), _two_slabs)]
    extras = [(res, _spec((tm, tn), lambda i, j, k: (i, j)))]
    outs = [(_sd((S, D), F32), _spec((tm, tn), lambda i, j, k: (i, j)))]
    return _mm(name, (S // tm, D // tn, N_DEV // 2), ops, [(0, 1, False, False, 0)], extras, outs,
               _store_residual(scale), [(tm, tn)])[0]


def _rows_dact(name, d_bf, w_sm, extras_arrays, out_shapes, epi):
    S, D = d_bf.shape
    r = w_sm.shape[-2]
    tm = _pick(S, (1024, 512, 256, 128))
    ops = [(d_bf, _spec((tm, D), lambda j, i, k: (i, 0))),
           (w_sm, _spec((2, r, D), lambda j, i, k: (j, 0, 0)), _two_slabs)]
    extras = []
    for arr in extras_arrays:
        if arr.ndim == 3:
            extras.append((arr, _spec((arr.shape[0], tm, 2 * r), lambda j, i, k: (0, i, j))))
        else:
            extras.append((arr, _spec((tm, 2 * r), lambda j, i, k: (i, j))))
    outs = []
    for sd in out_shapes:
        if len(sd.shape) == 3:
            outs.append((sd, _spec((sd.shape[0], tm, 2 * r), lambda j, i, k: (0, i, j))))
        else:
            outs.append((sd, _spec((tm, 2 * r), lambda j, i, k: (i, j))))
    return _mm(name, (N_DEV // 2, S // tm, 1), ops, [(0, 1, False, True, 0)], extras, outs, epi, [(tm, 2 * r)])


def _rows_wgrad(name, a, d_bf, scale):
    S, D = d_bf.shape
    r = a.shape[1] // N_DEV
    tn = _pick(D, (1024, 512, 256, 128))
    tk = _pick(S, (2048, 1024, 512, 256, 128))
    ops = [(a, _spec((tk, 2 * r), lambda s, j, k: (k, s))),
           (d_bf, _spec((tk, tn), lambda s, j, k: (k, j)))]
    outs = [(_sd((N_DEV, r, D), BF), _spec((2, r, tn), lambda s, j, k: (s, 0, j)))]

    def epi(accs, ex, orefs, ids):
        v = accs[0] if scale is None else accs[0] * scale
        orefs[0][0] = v[:r].astype(BF)
        orefs[0][1] = v[r:].astype(BF)

    return _mm(name, (N_DEV // 2, D // tn, S // tk), ops, [(0, 1, True, False, 0)], [], outs, epi, [(2 * r, tn)])[0]


def _cols_dh(name, dpair, w_sm, deps=()):
    _, S, _ = dpair.shape
    D, c = w_sm.shape[-2], w_sm.shape[-1]
    half = N_DEV // 2
    tm = _pick(S, (1024, 512, 256, 128))
    tn = _pick(D, (1024, 512, 256, 128))
    ops = [(dpair, _spec((None, tm, c), lambda i, j, k: (k // half, i, k % half))),
           (w_sm, _spec((None, tn, c), lambda i, j, k: (k, j, 0)))]
    outs = [(_sd((S, D), F32), _spec((tm, tn), lambda i, j, k: (i, j)))]
    return _mm(name, (S // tm, D // tn, N_DEV), ops, [(0, 1, False, True, 0)], [], outs, _store(), [(tm, tn)],
               deps=deps)[0]


def _cols_wgrad(name, h, dpair, deps=()):
    S, D = h.shape
    half = N_DEV // 2
    c = dpair.shape[2] // half
    tm = _pick(D, (1024, 512, 256, 128))
    tk = _pick(S, (2048, 1024, 512, 256, 128))
    ops = [(h, _spec((tk, tm), lambda s, i, k: (k, i))),
           (dpair, _spec((None, tk, c), lambda s, i, k: (s // half, k, s % half)))]
    outs = [(_sd((N_DEV, D, c), BF), _spec((None, tm, c), lambda s, i, k: (s, i, 0)))]
    return _mm(name, (N_DEV, D // tm, S // tk), ops, [(0, 1, True, False, 0)], [], outs, _store(), [(tm, c)],
               deps=deps)[0]


def _mm2(name, a, b, ta, tb, out_dtype, epi=None, extras=(), res=None, tn_cands=(512, 384, 256, 128), deps=()):
    a, a_lead = a if isinstance(a, tuple) else (a, None)
    b, b_lead = b if isinstance(b, tuple) else (b, None)
    M = a.shape[-1] if ta else a.shape[-2]
    K = a.shape[-2] if ta else a.shape[-1]
    N = b.shape[-2] if tb else b.shape[-1]
    tm = _pick(M, (1024, 512, 256, 128))
    tn = _pick(N, tn_cands)
    tk = _pick(K, (2048, 1152, 1024, 512, 256, 128))

    def matrix_spec(lead, shape, fn):
        if lead is None:
            return _spec(shape, fn)
        return _spec((None,) + shape, lambda i, j, k: (lead,) + fn(i, j, k))

    a_spec = matrix_spec(a_lead, (tk, tm), lambda i, j, k: (k, i)) if ta else matrix_spec(a_lead, (tm, tk), lambda i, j, k: (i, k))
    b_spec = matrix_spec(b_lead, (tn, tk), lambda i, j, k: (j, k)) if tb else matrix_spec(b_lead, (tk, tn), lambda i, j, k: (k, j))
    ex = [(e, _spec((tm, e.shape[1]), lambda i, j, k: (i, 0))) for e in extras]
    if res is not None:
        ex = [(res, _spec((tm, tn), lambda i, j, k: (i, j)))]
        epi = _store_residual(1.0)
    outs = [(_sd((M, N), out_dtype), _spec((tm, tn), lambda i, j, k: (i, j)))]
    return _mm(name, (M // tm, N // tn, K // tk), [(a, a_spec), (b, b_spec)], [(0, 1, ta, tb, 0)], ex, outs,
               epi or _store(), [(tm, tn)], deps=deps)[0]


def _mm_halves(name, a, w_t, out_dtype, epi_of_half=None, extras=()):
    a, lead = a
    M, K = a.shape[1:]
    N = w_t.shape[2]
    tm = _pick(M, (1024, 512, 256, 128))
    tn = _pick(N, (512, 256, 128))
    tk = _pick(K, (2048, 1024, 512, 256, 128))
    nj = N // tn
    ops = [(a, _spec((None, tm, tk), lambda i, j, k: (lead, i, k))),
           (w_t, _spec((None, tk, tn), lambda i, j, k: (j // nj, k, j % nj)))]
    ex = [(e, _spec((tm, e.shape[1]), lambda i, j, k: (i, 0))) for e in extras]
    outs = [(_sd((2, M, N), out_dtype), _spec((None, tm, tn), lambda i, j, k: (j // nj, i, j % nj)))]

    def epi(accs, ex_tiles, orefs, ids):
        if epi_of_half is None:
            orefs[0][...] = accs[0].astype(out_dtype)
        else:
            for half in range(2):
                pl.when(ids[1] // nj == half)(functools.partial(epi_of_half(half), accs, ex_tiles, orefs, ids))

    return _mm(name, (M // tm, 2 * nj, K // tk), ops, [(0, 1, False, False, 0)], ex, outs, epi, [(tm, tn)])[0]


def _rms_fwd(name, x, g, deps=()):
    S, D = x.shape
    ts = _pick(S, (512, 256, 128))

    def body(x_ref, g_ref, *rest):
        h_ref = rest[-1]
        xv = x_ref[...]
        r = lax.rsqrt(jnp.mean(xv * xv, axis=-1, keepdims=True) + EPS)
        h_ref[...] = (xv * r * g_ref[...]).astype(BF)

    return pl.pallas_call(
        body, name=name, grid=(S // ts,),
        in_specs=[_spec((ts, D), lambda i: (i, 0)), _spec((1, D), lambda i: (0, 0))] + [_ANY] * len(deps),
        out_specs=_spec((ts, D), lambda i: (i, 0)), out_shape=_sd((S, D), BF),
        compiler_params=_params(("parallel",), 12 * ts * D * 4))(x, g, *deps)


def _rms_bwd(name, dh, x, g, dres, deps=()):
    S, D = x.shape
    ts = _pick(S, (256, 128))

    def body(dh_ref, x_ref, g_ref, dres_ref, *rest):
        dx_ref, dxb_ref, dg_ref = rest[len(deps):]
        xv = x_ref[...]
        dhv = dh_ref[...]
        r = lax.rsqrt(jnp.mean(xv * xv, axis=-1, keepdims=True) + EPS)
        xhat = xv * r
        dxh = dhv * g_ref[...]
        cm = jnp.mean(dxh * xhat, axis=-1, keepdims=True)
        dx = r * (dxh - xhat * cm) + dres_ref[...]
        dx_ref[...] = dx
        dxb_ref[...] = dx.astype(BF)

        @pl.when(pl.program_id(0) == 0)
        def _():
            dg_ref[...] = jnp.zeros_like(dg_ref)

        dg_ref[...] += jnp.sum(dhv * xhat, axis=0, keepdims=True)

    row = _spec((ts, D), lambda i: (i, 0))
    vec = _spec((1, D), lambda i: (0, 0))
    return pl.pallas_call(
        body, name=name, grid=(S // ts,),
        in_specs=[row, row, vec, row] + [_ANY] * len(deps), out_specs=[row, row, vec],
        out_shape=[_sd((S, D), F32), _sd((S, D), BF), _sd((1, D), F32)],
        compiler_params=_params(("arbitrary",), 20 * ts * D * 4))(dh, x, g, dres, *deps)


def _final_loss(name, x, g, target):
    S, D = x.shape
    ts = _pick(S, (256, 128))

    def body(x_ref, g_ref, t_ref, loss_ref, dx_ref, dxb_ref, dg_ref):
        xv = x_ref[...]
        gv = g_ref[...]
        r = lax.rsqrt(jnp.mean(xv * xv, axis=-1, keepdims=True) + EPS)
        xhat = xv * r
        err = xhat * gv - t_ref[...]
        part = 0.5 * jnp.sum(jnp.mean(err * err, axis=-1, keepdims=True), axis=0, keepdims=True)
        dy = err * (1.0 / D)
        dxh = dy * gv
        cm = jnp.mean(dxh * xhat, axis=-1, keepdims=True)
        dx = r * (dxh - xhat * cm)
        dx_ref[...] = dx
        dxb_ref[...] = dx.astype(BF)

        @pl.when(pl.program_id(0) == 0)
        def _():
            dg_ref[...] = jnp.zeros_like(dg_ref)
            loss_ref[...] = jnp.zeros_like(loss_ref)

        dg_ref[...] += jnp.sum(dy * xhat, axis=0, keepdims=True)
        loss_ref[...] += jnp.broadcast_to(part, loss_ref.shape)

    row = _spec((ts, D), lambda i: (i, 0))
    vec = _spec((1, D), lambda i: (0, 0))
    return pl.pallas_call(
        body, name=name, grid=(S // ts,),
        in_specs=[row, vec, row], out_specs=[_spec((1, LANES), lambda i: (0, 0)), row, row, vec],
        out_shape=[_sd((1, LANES), F32), _sd((S, D), F32), _sd((S, D), BF), _sd((1, D), F32)],
        compiler_params=_params(("arbitrary",), 20 * ts * D * 4))(x, g, target)


def _swiglu(gate, up):
    sg = jax.nn.sigmoid(gate)
    silu = gate * sg
    return up * (sg * (1.0 + gate * (1.0 - sg))), silu, silu * up


def _swiglu_bwd_epi(accs, ex, orefs, ids):
    da = 0.5 * accs[0]
    orefs[0][0] = (da * ex[0][0].astype(F32)).astype(BF)
    orefs[0][1] = (da * ex[0][1].astype(F32)).astype(BF)


_GELU_C = math.sqrt(2.0 / math.pi)


def _gelu(x):
    return x * (0.5 * (1.0 + jnp.tanh(_GELU_C * (x + 0.044715 * (x * x * x)))))


def _gelu_and_grad(x):
    x2 = x * x
    t = jnp.tanh(_GELU_C * (x + 0.044715 * (x2 * x)))
    cdf = 0.5 * (1.0 + t)
    return x * cdf, cdf + x * ((0.5 * _GELU_C) * (1.0 - t * t) * (1.0 + (3.0 * 0.044715) * x2))


def _causal_block_mask():
    row = lax.broadcasted_iota(jnp.int32, (SGU_BLOCK, SGU_BLOCK), 0) // CHUNK
    col = lax.broadcasted_iota(jnp.int32, (SGU_BLOCK, SGU_BLOCK), 1) // CHUNK
    return row >= col


def _sgu_mid_fwd(name, puv, gain, bias, w_sp, b_sp):
    _, S, W = puv.shape
    G = SGU_GROUPS
    C = W // G
    T = SGU_BLOCK

    def body(puv_ref, gain_ref, bias_ref, w_ref, b_ref, out_ref):
        mask = _causal_block_mask()
        v = _gelu(puv_ref[1])
        mu = jnp.mean(v, axis=-1, keepdims=True)
        vc = v - mu
        rs = lax.rsqrt(jnp.mean(vc * vc, axis=-1, keepdims=True) + EPS)
        vln = (vc * rs * gain_ref[...] + bias_ref[...]).astype(BF)
        for g in range(G):
            wg = jnp.where(mask, w_ref[g], 0.0).astype(BF)
            mixed = jnp.dot(wg, vln[:, g * C:(g + 1) * C], preferred_element_type=F32) + b_ref[g]
            out_ref[:, g * C:(g + 1) * C] = (_gelu(puv_ref[0, :, g * C:(g + 1) * C]) * mixed).astype(BF)

    return pl.pallas_call(
        body, name=name, grid=(S // T,),
        in_specs=[_spec((2, T, W), lambda i: (0, i, 0)), _spec((1, W), lambda i: (0, 0)), _spec((1, W), lambda i: (0, 0)),
                  _spec((G, T, T), lambda i: (0, 0, 0)), _spec((G, T, 1), lambda i: (0, 0, 0))],
        out_specs=_spec((T, W), lambda i: (i, 0)), out_shape=_sd((S, W), BF),
        compiler_params=_params(("parallel",), 16 * T * W * 4))(puv, gain, bias, w_sp, b_sp)


def _sgu_mid_bwd(name, puv, dgated, gain, bias, w_sp, b_sp):
    _, S, W = puv.shape
    G = SGU_GROUPS
    C = W // G
    T = SGU_BLOCK

    def body(puv_ref, dg_ref, gain_ref, bias_ref, w_ref, b_ref, dpuv_ref, dgain_ref, dbias_ref, dw_ref, db_ref, dvln_ref):
        @pl.when(pl.program_id(0) == 0)
        def _():
            dgain_ref[...] = jnp.zeros_like(dgain_ref)
            dbias_ref[...] = jnp.zeros_like(dbias_ref)
            dw_ref[...] = jnp.zeros_like(dw_ref)
            db_ref[...] = jnp.zeros_like(db_ref)

        mask = _causal_block_mask()
        v, v_grad = _gelu_and_grad(puv_ref[1])
        mu = jnp.mean(v, axis=-1, keepdims=True)
        vc = v - mu
        rs = lax.rsqrt(jnp.mean(vc * vc, axis=-1, keepdims=True) + EPS)
        vhat = vc * rs
        gain_v = gain_ref[...]
        vln = (vhat * gain_v + bias_ref[...]).astype(BF)
        for g in range(G):
            sl = slice(g * C, (g + 1) * C)
            wg = jnp.where(mask, w_ref[g], 0.0).astype(BF)
            vg = vln[:, sl]
            mixed = jnp.dot(wg, vg, preferred_element_type=F32) + b_ref[g]
            u, u_grad = _gelu_and_grad(puv_ref[0, :, sl])
            dgt = dg_ref[:, sl].astype(F32)
            dpuv_ref[0, :, sl] = (dgt * mixed * u_grad).astype(BF)
            dmix = dgt * u
            db_ref[g] += jnp.sum(dmix, axis=-1, keepdims=True)
            dmb = dmix.astype(BF)
            dwg = lax.dot_general(dmb, vg, (((1,), (1,)), ((), ())), preferred_element_type=F32)
            dw_ref[g] += jnp.where(mask, dwg, 0.0)
            dvln_ref[:, sl] = lax.dot_general(wg, dmb, (((0,), (0,)), ((), ())), preferred_element_type=F32)
        dvln = dvln_ref[...]
        dgain_ref[...] += jnp.sum(dvln * vhat, axis=0, keepdims=True)
        dbias_ref[...] += jnp.sum(dvln, axis=0, keepdims=True)
        dvh = dvln * gain_v
        m1 = jnp.mean(dvh, axis=-1, keepdims=True)
        m2 = jnp.mean(dvh * vhat, axis=-1, keepdims=True)
        dv = rs * (dvh - m1 - vhat * m2)
        dpuv_ref[1] = (dv * v_grad).astype(BF)

    vec = _spec((1, W), lambda i: (0, 0))
    wsp = _spec((G, T, T), lambda i: (0, 0, 0))
    bsp = _spec((G, T, 1), lambda i: (0, 0, 0))
    return pl.pallas_call(
        body, name=name, grid=(S // T,),
        in_specs=[_spec((2, T, W), lambda i: (0, i, 0)), _spec((T, W), lambda i: (i, 0)), vec, vec, wsp, bsp],
        out_specs=[_spec((2, T, W), lambda i: (0, i, 0)), vec, vec, wsp, bsp],
        out_shape=[_sd((2, S, W), BF), _sd((1, W), F32), _sd((1, W), F32), _sd((G, T, T), F32), _sd((G, T, 1), F32)],
        scratch_shapes=[pltpu.VMEM((T, W), F32)],
        compiler_params=_params(("arbitrary",), 24 * T * W * 4))(puv, dgated, gain, bias, w_sp, b_sp)


def _rope_tables(positions):
    half = QK_ROPE // 2
    inv_freq = 1.0 / (ROPE_THETA ** (jnp.arange(half, dtype=F32) / half))
    ang = positions.astype(F32)[:, None] * inv_freq[None, :]
    cos, sin = jnp.cos(ang), jnp.sin(ang)
    z = jnp.zeros_like(cos)
    return (jnp.concatenate([cos, cos, z, z], axis=1), jnp.concatenate([-sin, z, z, z], axis=1),
            jnp.concatenate([z, sin, z, z], axis=1))


def _rope(x, cos, sa, sb):
    return x * cos + pltpu.roll(x, HEAD_PAD - QK_ROPE // 2, 1) * sa + pltpu.roll(x, QK_ROPE // 2, 1) * sb


def _rope_t(dy, cos, sa, sb):
    return dy * cos + pltpu.roll(dy * sa, QK_ROPE // 2, 1) + pltpu.roll(dy * sb, HEAD_PAD - QK_ROPE // 2, 1)


def _rms_rows(x, g):
    r = lax.rsqrt(jnp.mean(x * x, axis=-1, keepdims=True) + EPS)
    return x * r * g


def _rms_rows_bwd(dy, x, g):
    r = lax.rsqrt(jnp.mean(x * x, axis=-1, keepdims=True) + EPS)
    xhat = x * r
    dxh = dy * g
    cm = jnp.mean(dxh * xhat, axis=-1, keepdims=True)
    return r * (dxh - xhat * cm), jnp.sum(dy * xhat, axis=0, keepdims=True)


def _mla_mid_fwd(name, proj, qg, kvg, cos, sa, sb):
    S, P = proj.shape
    ts = _pick(S, (512, 256, 128))

    def body(p_ref, qg_ref, kvg_ref, cos_ref, sa_ref, sb_ref, lat_ref, kr_ref):
        lat_ref[0] = _rms_rows(p_ref[:, :Q_LORA], qg_ref[...]).astype(BF)
        lat_ref[1] = _rms_rows(p_ref[:, Q_LORA:Q_LORA + KV_LORA], kvg_ref[...]).astype(BF)
        kr_ref[...] = _rope(p_ref[:, Q_LORA + KV_LORA:], cos_ref[...], sa_ref[...], sb_ref[...]).astype(BF)

    tab = _spec((ts, HEAD_PAD), lambda i: (i, 0))
    return pl.pallas_call(
        body, name=name, grid=(S // ts,),
        in_specs=[_spec((ts, P), lambda i: (i, 0)), _spec((1, Q_LORA), lambda i: (0, 0)),
                  _spec((1, KV_LORA), lambda i: (0, 0)), tab, tab, tab],
        out_specs=[_spec((2, ts, Q_LORA), lambda i: (0, i, 0)), tab],
        out_shape=[_sd((2, S, Q_LORA), BF), _sd((S, HEAD_PAD), BF)],
        compiler_params=_params(("parallel",), 16 * ts * P * 4))(proj, qg, kvg, cos, sa, sb)


def _mla_mid_bwd(name, proj, dqn, dkvn, dkr_heads, qg, kvg, cos, sa, sb):
    S, P = proj.shape
    H = dkr_heads.shape[0]
    ts = _pick(S, (256, 128))

    def body(p_ref, dqn_ref, dkvn_ref, dkr_ref, qg_ref, kvg_ref, cos_ref, sa_ref, sb_ref, dp_ref, dqg_ref, dkvg_ref):
        @pl.when(pl.program_id(0) == 0)
        def _():
            dqg_ref[...] = jnp.zeros_like(dqg_ref)
            dkvg_ref[...] = jnp.zeros_like(dkvg_ref)

        dq, dqg = _rms_rows_bwd(dqn_ref[...], p_ref[:, :Q_LORA], qg_ref[...])
        dkv, dkvg = _rms_rows_bwd(dkvn_ref[...], p_ref[:, Q_LORA:Q_LORA + KV_LORA], kvg_ref[...])
        dqg_ref[...] += dqg
        dkvg_ref[...] += dkvg
        dkr = dkr_ref[0]
        for h in range(1, H):
            dkr = dkr + dkr_ref[h]
        dp_ref[:, :Q_LORA] = dq.astype(BF)
        dp_ref[:, Q_LORA:Q_LORA + KV_LORA] = dkv.astype(BF)
        dp_ref[:, Q_LORA + KV_LORA:] = _rope_t(dkr, cos_ref[...], sa_ref[...], sb_ref[...]).astype(BF)

    tab = _spec((ts, HEAD_PAD), lambda i: (i, 0))
    lat = _spec((ts, Q_LORA), lambda i: (i, 0))
    gq = _spec((1, Q_LORA), lambda i: (0, 0))
    return pl.pallas_call(
        body, name=name, grid=(S // ts,),
        in_specs=[_spec((ts, P), lambda i: (i, 0)), lat, lat, _spec((H, ts, HEAD_PAD), lambda i: (0, i, 0)),
                  gq, gq, tab, tab, tab],
        out_specs=[_spec((ts, P), lambda i: (i, 0)), gq, gq],
        out_shape=[_sd((S, P), BF), _sd((1, Q_LORA), F32), _sd((1, KV_LORA), F32)],
        compiler_params=_params(("arbitrary",), 24 * ts * P * 4))(proj, dqn, dkvn, dkr_heads, qg, kvg, cos, sa, sb)


def _attn_tile(S):
    return _pick(S, (512,)) if S >= 2048 else _pick(S, (128,))


def _diag_mask(t, transposed):
    q = lax.broadcasted_iota(jnp.int32, (t, t), 1 if transposed else 0) // CHUNK
    k = lax.broadcasted_iota(jnp.int32, (t, t), 0 if transposed else 1) // CHUNK
    return k <= q


_NT = (((1,), (1,)), ((), ()))


def _tile_loop(lo, hi, step, carry):
    pos = lo
    for unroll in ATTN_UNROLLS:
        trips = (hi - pos) // unroll

        def several(tt, c, unroll=unroll, pos=pos):
            for u in range(unroll):
                c = step(pos + tt * unroll + u, c)
            return c

        carry = lax.fori_loop(0, trips, several, carry)
        pos = pos + trips * unroll
    return lax.fori_loop(pos, hi, step, carry)


def _attn_fwd(name, q_all, kv_all, kr):
    _, S, HP = q_all.shape
    H = HP // HEAD_PAD
    t = _attn_tile(S)
    nq = S // t
    ng = ATTN_GROUPS
    tg = t // ng

    def body(q_ref, kv_ref, kr_ref, o_ref, lse_ref, kcat_ref):
        i = pl.program_id(1)

        @pl.when(i == 0)
        def _():
            kcat_ref[:, :HEAD_PAD] = kv_ref[0]
            kcat_ref[:, HEAD_PAD:] = kr_ref[...]

        qs = [jnp.concatenate([q_ref[0, g * tg:(g + 1) * tg], q_ref[1, g * tg:(g + 1) * tg]], axis=1) for g in range(ng)]

        def step(j, carry, masked):
            off = pl.multiple_of(j * t, t)
            kj = kcat_ref[pl.ds(off, t), :]
            vj = kv_ref[1, pl.ds(off, t), :]
            out = []
            for g in range(ng):
                m, l, acc = carry[g]
                s = lax.dot_general(qs[g], kj, _NT, preferred_element_type=F32)
                if masked:
                    s = jnp.where(_diag_mask(t, False)[g * tg:(g + 1) * tg], s, NEG)
                m2 = jnp.maximum(m, jnp.max(s, axis=-1, keepdims=True))
                al = jnp.exp(m - m2)
                p = jnp.exp(s - m2)
                l2 = al * l + jnp.sum(p, axis=-1, keepdims=True)
                acc2 = al * acc + jnp.dot(p.astype(BF), vj, preferred_element_type=F32)
                out.append((m2, l2, acc2))
            return tuple(out)

        init = tuple((jnp.full((tg, 1), NEG, F32), jnp.zeros((tg, 1), F32), jnp.zeros((tg, V_DIM), F32))
                     for _ in range(ng))
        carry = _tile_loop(0, i, lambda j, c: step(j, c, False), init)
        carry = step(i, carry, True)
        for g in range(ng):
            m, l, acc = carry[g]
            o_ref[g * tg:(g + 1) * tg, :] = acc / l
            lse_ref[g * tg:(g + 1) * tg, :] = jnp.broadcast_to(m + jnp.log(l), (tg, LANES))

    return pl.pallas_call(
        body, name=name, grid=(H, nq),
        in_specs=[_spec((2, t, HEAD_PAD), lambda h, i: (0, i, h)), _spec((2, S, HEAD_PAD), lambda h, i: (0, 0, h)),
                  _spec((S, HEAD_PAD), lambda h, i: (0, 0))],
        out_specs=[_spec((t, HEAD_PAD), lambda h, i: (i, h)), _spec((None, t, LANES), lambda h, i: (h, i, 0))],
        out_shape=[_sd((S, HP), F32), _sd((H, S, LANES), F32)],
        scratch_shapes=[pltpu.VMEM((S, 2 * HEAD_PAD), BF)],
        compiler_params=_params(("parallel", "arbitrary"), 8 * S * HEAD_PAD * 2 + 24 * t * t * 4))(q_all, kv_all, kr)


def _attn_dq(name, q_all, kv_all, kr, do, o, lse, cos, sa, sb, scale):
    _, S, HP = q_all.shape
    H = HP // HEAD_PAD
    t = _attn_tile(S)
    nq = S // t

    def body(q_ref, kv_ref, kr_ref, do_ref, o_ref, lse_ref, cos_ref, sa_ref, sb_ref, dq_ref, dl_ref, kcat_ref):
        i = pl.program_id(1)

        @pl.when(i == 0)
        def _():
            kcat_ref[:, :HEAD_PAD] = kv_ref[0]
            kcat_ref[:, HEAD_PAD:] = kr_ref[...]

        ng = ATTN_GROUPS
        tg = t // ng
        rows = [slice(g * tg, (g + 1) * tg) for g in range(ng)]
        qs = [jnp.concatenate([q_ref[0, r], q_ref[1, r]], axis=1) for r in rows]
        dobs = [do_ref[r, :].astype(BF) for r in rows]
        lses = [lse_ref[r, 0:1] for r in rows]
        dls = [jnp.sum(do_ref[r, :] * o_ref[r, :], axis=-1, keepdims=True) for r in rows]
        for g in range(ng):
            dl_ref[rows[g], :] = jnp.broadcast_to(dls[g], (tg, LANES))

        def step(j, dqs, masked):
            off = pl.multiple_of(j * t, t)
            kj = kcat_ref[pl.ds(off, t), :]
            vj = kv_ref[1, pl.ds(off, t), :]
            out = []
            for g in range(ng):
                s = lax.dot_general(qs[g], kj, _NT, preferred_element_type=F32)
                if masked:
                    s = jnp.where(_diag_mask(t, False)[rows[g]], s, NEG)
                p = jnp.exp(s - lses[g])
                dp = lax.dot_general(dobs[g], vj, _NT, preferred_element_type=F32)
                ds = (p * (dp - dls[g])).astype(BF)
                out.append(dqs[g] + jnp.dot(ds, kj, preferred_element_type=F32))
            return tuple(out)

        init = tuple(jnp.zeros((tg, 2 * HEAD_PAD), F32) for _ in range(ng))
        dqs = _tile_loop(0, i, lambda j, c: step(j, c, False), init)
        dqs = step(i, dqs, True)
        for g in range(ng):
            dq_ref[0, rows[g]] = (dqs[g][:, :HEAD_PAD] * scale).astype(BF)
            dq_ref[1, rows[g]] = (_rope_t(dqs[g][:, HEAD_PAD:], cos_ref[rows[g], :], sa_ref[rows[g], :],
                                          sb_ref[rows[g], :]) * scale).astype(BF)

    tab = _spec((t, HEAD_PAD), lambda h, i: (i, 0))
    stat = _spec((None, t, LANES), lambda h, i: (h, i, 0))
    head_tile = _spec((t, HEAD_PAD), lambda h, i: (i, h))
    return pl.pallas_call(
        body, name=name, grid=(H, nq),
        in_specs=[_spec((2, t, HEAD_PAD), lambda h, i: (0, i, h)), _spec((2, S, HEAD_PAD), lambda h, i: (0, 0, h)),
                  _spec((S, HEAD_PAD), lambda h, i: (0, 0)), head_tile, head_tile, stat, tab, tab, tab],
        out_specs=[_spec((2, t, HEAD_PAD), lambda h, i: (0, i, h)), stat],
        out_shape=[_sd((2, S, HP), BF), _sd((H, S, LANES), F32)],
        scratch_shapes=[pltpu.VMEM((S, 2 * HEAD_PAD), BF)],
        compiler_params=_params(("parallel", "arbitrary"), 8 * S * HEAD_PAD * 2 + 32 * t * t * 4))(
            q_all, kv_all, kr, do, o, lse, cos, sa, sb)


def _attn_dkv(name, q_all, kv_all, kr, do, lse_row, delta_row):
    _, S, HP = q_all.shape
    H = HP // HEAD_PAD
    t = _attn_tile(S)
    nq = S // t

    def body(q_ref, kv_ref, kr_ref, do_ref, lse_ref, dl_ref, dkv_ref, dkr_ref, qcat_ref):
        j = pl.program_id(1)

        @pl.when(j == 0)
        def _():
            qcat_ref[:, :HEAD_PAD] = q_ref[0]
            qcat_ref[:, HEAD_PAD:] = q_ref[1]

        ng = ATTN_GROUPS
        tg = t // ng
        rows = [slice(g * tg, (g + 1) * tg) for g in range(ng)]
        kjs = [jnp.concatenate([kv_ref[0, r], kr_ref[r, :]], axis=1) for r in rows]
        vjs = [kv_ref[1, r] for r in rows]

        def step(i, carry, masked):
            off = pl.multiple_of(i * t, t)
            qi = qcat_ref[pl.ds(off, t), :]
            doi = do_ref[pl.ds(off, t), :].astype(BF)
            lse_i = lse_ref[i]
            dl_i = dl_ref[i]
            out = []
            for g in range(ng):
                dk, dv = carry[g]
                st = lax.dot_general(kjs[g], qi, _NT, preferred_element_type=F32)
                if masked:
                    st = jnp.where(_diag_mask(t, True)[rows[g]], st, NEG)
                pt = jnp.exp(st - lse_i)
                dv2 = dv + jnp.dot(pt.astype(BF), doi, preferred_element_type=F32)
                dpt = lax.dot_general(vjs[g], doi, _NT, preferred_element_type=F32)
                dst = (pt * (dpt - dl_i)).astype(BF)
                out.append((dk + jnp.dot(dst, qi, preferred_element_type=F32), dv2))
            return tuple(out)

        init = tuple((jnp.zeros((tg, 2 * HEAD_PAD), F32), jnp.zeros((tg, V_DIM), F32)) for _ in range(ng))
        carry = step(j, init, True)
        carry = _tile_loop(j + 1, nq, lambda i, c: step(i, c, False), carry)
        for g in range(ng):
            dk, dv = carry[g]
            dkv_ref[0, rows[g]] = dk[:, :HEAD_PAD].astype(BF)
            dkv_ref[1, rows[g]] = dv.astype(BF)
            dkr_ref[rows[g], :] = dk[:, HEAD_PAD:]

    stat = _spec((None, nq, 1, t), lambda h, j: (h, 0, 0, 0))
    return pl.pallas_call(
        body, name=name, grid=(H, nq),
        in_specs=[_spec((2, S, HEAD_PAD), lambda h, j: (0, 0, h)), _spec((2, t, HEAD_PAD), lambda h, j: (0, j, h)),
                  _spec((t, HEAD_PAD), lambda h, j: (j, 0)), _spec((S, HEAD_PAD), lambda h, j: (0, h)), stat, stat],
        out_specs=[_spec((2, t, HEAD_PAD), lambda h, j: (0, j, h)), _spec((None, t, HEAD_PAD), lambda h, j: (h, j, 0))],
        out_shape=[_sd((2, S, HP), BF), _sd((H, S, HEAD_PAD), F32)],
        scratch_shapes=[pltpu.VMEM((S, 2 * HEAD_PAD), BF)],
        compiler_params=_params(("parallel", "arbitrary"), 8 * S * HEAD_PAD * 4 + 32 * t * t * 4))(
            q_all, kv_all, kr, do, lse_row, delta_row)


def _place():
    x, y, c = lax.axis_index("x"), lax.axis_index("y"), lax.axis_index("c")
    return x, y, c


def _all_gather(blocks):
    n = len(blocks)

    def body(*refs):
        ins, outs = refs[:n], refs[n:2 * n]
        send_sems, recv_sems, local_sems = refs[2 * n:]
        x, y, c = _place()
        me = 4 * x + 2 * y + c
        sibling = (x, y, 1 - c)
        chips = [(1 - x, y), (x, 1 - y), (1 - x, 1 - y)]

        def slab(a, px, py, pc):
            return outs[a].at[4 * px + 2 * py + pc]

        def copy(a, k, src, dst, to):
            return pltpu.make_async_remote_copy(src_ref=src, dst_ref=dst, send_sem=send_sems.at[a, k],
                                                recv_sem=recv_sems.at[a, k], device_id=to, device_id_type=MESH)

        local = [pltpu.make_async_copy(ins[a], outs[a].at[me], local_sems.at[a]) for a in range(n)]
        for cp in local:
            cp.start()
        sends = []
        for a in range(n):
            mine = slab(a, x, y, c)
            sends.append(copy(a, 0, ins[a], mine, sibling))
            for j, chip in enumerate(chips):
                sends.append(copy(a, 1 + j, ins[a], mine, (*chip, c)))
        for cp in sends:
            cp.start()
        for j, chip in enumerate(chips):
            for a in range(n):
                got = slab(a, *chip, c)
                copy(a, 1 + j, got, got, (x, y, c)).wait_recv()
                fwd = copy(a, 4 + j, got, got, sibling)
                fwd.start()
                sends.append(fwd)
        for a in range(n):
            got = slab(a, x, y, 1 - c)
            copy(a, 0, got, got, (x, y, c)).wait_recv()
            for j, chip in enumerate(chips):
                got = slab(a, *chip, 1 - c)
                copy(a, 4 + j, got, got, (x, y, c)).wait_recv()
        for cp in sends:
            cp.wait_send()
        for cp in local:
            cp.wait()

    return pl.pallas_call(
        body, name="weights_all_gather", in_specs=[_ANY] * n, out_specs=[_ANY] * n,
        out_shape=[_sd((N_DEV,) + b.shape, b.dtype) for b in blocks],
        scratch_shapes=[pltpu.SemaphoreType.DMA((n, 7)), pltpu.SemaphoreType.DMA((n, 7)), pltpu.SemaphoreType.DMA((n,))],
    )(*blocks)


_HBM = pl.BlockSpec(memory_space=pltpu.HBM)
_SEM = pl.BlockSpec(memory_space=pltpu.SEMAPHORE)
_EFFECT = pltpu.SideEffectType.DATAFLOW_SIDE_EFFECTING


def _peers():
    x, y, c = _place()
    out = []
    for m in range(1, N_DEV):
        px, py, pc = x ^ (m >> 2), y ^ ((m >> 1) & 1), c ^ (m & 1)
        out.append((m, (px, py, pc), 4 * px + 2 * py + pc))
    return 4 * x + 2 * y + c, out


def _exchange_copies(src, land, send_sem, recv_sem, gather):
    me, peers = _peers()
    cps = []
    for a in range(len(src)):
        for m, pos, idx in peers:
            s_ref, d_ref = (src[a], land[a].at[me]) if gather else (src[a].at[idx], land[a].at[m - 1])
            k = a * (N_DEV - 1) + m - 1
            cps.append(pltpu.make_async_remote_copy(src_ref=s_ref, dst_ref=d_ref, send_sem=send_sem.at[k],
                                                    recv_sem=recv_sem.at[k], device_id=pos, device_id_type=MESH))
    return cps


def _xstart(name, srcs, gather, after=()):
    n = len(srcs)
    if gather:
        land_shapes = [(N_DEV,) + s.shape for s in srcs]
    else:
        land_shapes = [(N_DEV - 1,) + s.shape[1:] for s in srcs]

    def body(*refs):
        src, land = refs[:n], refs[n:2 * n]
        send_sem, recv_sem = refs[2 * n + len(after)], refs[2 * n + len(after) + 1]
        token = refs[-1]
        for cp in _exchange_copies(src, land, send_sem, recv_sem, gather):
            cp.start()
        token[...] = jnp.zeros_like(token)

    sem = pltpu.SemaphoreType.DMA((n * (N_DEV - 1),))
    out_shape = ([sem, sem] + [pltpu.HBM(s.shape, s.dtype) for s in srcs]
                 + [pltpu.HBM(sh, s.dtype) for sh, s in zip(land_shapes, srcs)] + [_sd((8, LANES), F32)])
    args = [pltpu.with_memory_space_constraint(s, pltpu.HBM) for s in srcs]
    args += [pltpu.with_memory_space_constraint(lax.empty(sh, s.dtype), pltpu.HBM) for sh, s in zip(land_shapes, srcs)]
    res = pl.pallas_call(
        body, name=name, out_shape=out_shape, in_specs=[_HBM] * (2 * n) + [_ANY] * len(after),
        out_specs=[_SEM, _SEM] + [_HBM] * (2 * n) + [pl.BlockSpec(memory_space=pltpu.VMEM)],
        input_output_aliases={i: 2 + i for i in range(2 * n)},
        compiler_params=pltpu.CompilerParams(has_side_effects=_EFFECT))(*args, *after)
    return dict(send=res[0], recv=res[1], srcs=list(res[2:2 + n]), lands=list(res[2 + n:2 + 2 * n]), token=res[-1])


def _xwait(name, st, after, gather):
    n = len(st["srcs"])

    def body(*refs):
        src, land = refs[:n], refs[n:2 * n]
        send_sem, recv_sem = refs[2 * n], refs[2 * n + 1]
        for cp in _exchange_copies(src, land, send_sem, recv_sem, gather):
            cp.wait_send()
            cp.wait_recv()

    arrays = st["srcs"] + st["lands"]
    res = pl.pallas_call(
        body, name=name, out_shape=[pltpu.HBM(a.shape, a.dtype) for a in arrays],
        in_specs=[_HBM] * (2 * n) + [_SEM, _SEM, _ANY], out_specs=[_HBM] * (2 * n),
        input_output_aliases={i: i for i in range(2 * n)},
        compiler_params=pltpu.CompilerParams(has_side_effects=_EFFECT))(*arrays, st["send"], st["recv"], after)
    return list(res[:n]), list(res[n:])


def _put_own(name, land, block, me_arr):
    R, C = block.shape
    tr = _pick(R, (512, 256, 128, 64, 32, 16))

    def body(me_ref, b_ref, land_ref, o_ref):
        o_ref[...] = b_ref[...]

    return pl.pallas_call(
        body, name=name, out_shape=_sd(land.shape, land.dtype),
        grid_spec=pltpu.PrefetchScalarGridSpec(
            num_scalar_prefetch=1, grid=(R // tr,),
            in_specs=[_spec((tr, C), lambda i, me_ref: (i, 0)), _ANY],
            out_specs=_spec((None, tr, C), lambda i, me_ref: (me_ref[0], i, 0))),
        input_output_aliases={2: 0},
        compiler_params=_params(("parallel",), 8 * tr * C * 2))(me_arr, block, land)


def _all_reduce_small(name, part):
    R = part.shape[0]

    def body(p_ref, out_ref, gath_ref, send_sems, recv_sems):
        x, y, c = _place()
        me = 4 * x + 2 * y + c
        gath_ref[me] = p_ref[...]
        cps = []
        for m in range(1, N_DEV):
            to = (x ^ (m >> 2), y ^ ((m >> 1) & 1), c ^ (m & 1))
            cps.append(pltpu.make_async_remote_copy(
                src_ref=p_ref, dst_ref=gath_ref.at[me], send_sem=send_sems.at[m - 1], recv_sem=recv_sems.at[m - 1],
                device_id=to, device_id_type=MESH))
        for cp in cps:
            cp.start()
        for m in range(1, N_DEV):
            frm = 4 * (x ^ (m >> 2)) + 2 * (y ^ ((m >> 1) & 1)) + (c ^ (m & 1))
            pltpu.make_async_remote_copy(
                src_ref=p_ref, dst_ref=gath_ref.at[frm], send_sem=send_sems.at[m - 1], recv_sem=recv_sems.at[m - 1],
                device_id=(x, y, c), device_id_type=MESH).wait_recv()
        for cp in cps:
            cp.wait_send()
        tot = gath_ref[0]
        for k in range(1, N_DEV):
            tot = tot + gath_ref[k]
        out_ref[...] = tot

    vm = pl.BlockSpec(memory_space=pltpu.VMEM)
    return pl.pallas_call(
        body, name=name, in_specs=[vm], out_specs=vm, out_shape=_sd((R, LANES), F32),
        scratch_shapes=[pltpu.VMEM((N_DEV, R, LANES), F32), pltpu.SemaphoreType.DMA((N_DEV - 1,)),
                        pltpu.SemaphoreType.DMA((N_DEV - 1,))],
        compiler_params=pltpu.CompilerParams(vmem_limit_bytes=VMEM_FLOOR),
    )(part)


def _adam_math(w, g, m, v):
    m2 = ADAM_B1 * m + (1.0 - ADAM_B1) * g
    v2 = ADAM_B2 * v + (1.0 - ADAM_B2) * (g * g)
    m_hat = m2 / (1.0 - ADAM_B1 ** ADAM_STEP)
    v_hat = v2 / (1.0 - ADAM_B2 ** ADAM_STEP)
    delta = -ADAM_LR * (m_hat / (jnp.sqrt(v_hat) + ADAM_EPS) + ADAM_WD * w)
    return delta, m2, v2


def _adamw_sharded(name, lands, fulls, me_arr, w, m, v):
    n_l = len(lands)
    R, C = lands[0].shape[1], lands[0].shape[2]
    tr = _pick(R, (128, 64, 32, 16, 8))
    nr = R // tr

    def body(me_ref, *refs):
        land_refs, own_refs = refs[:n_l], refs[n_l:2 * n_l]
        w_ref, m_ref, v_ref, g_ref, d_ref, m2_ref, v2_ref = refs[2 * n_l:]
        layer = pl.program_id(0)
        for ll in range(n_l):
            @pl.when(layer == ll)
            def _(ll=ll):
                g = own_refs[ll][...].astype(F32)
                for j in range(N_DEV - 1):
                    g = g + land_refs[ll][j].astype(F32)
                delta, m2, v2 = _adam_math(w_ref[...], g, m_ref[...], v_ref[...])
                g_ref[...] = g
                d_ref[...] = delta
                m2_ref[...] = m2
                v2_ref[...] = v2

    def row_of(ll):
        return lambda l, i, me_ref: jnp.where(l == ll, i, 0)

    in_specs = [_spec((N_DEV - 1, tr, C), lambda l, i, me_ref, f=row_of(ll): (0, f(l, i, me_ref), 0)) for ll in range(n_l)]
    in_specs += [_spec((None, tr, C), lambda l, i, me_ref, f=row_of(ll): (me_ref[0], f(l, i, me_ref), 0))
                 for ll in range(n_l)]
    blk = _spec((tr, C), lambda l, i, me_ref: (l * nr + i, 0))
    return pl.pallas_call(
        body, name=name, out_shape=[_sd(w.shape, F32)] * 4,
        grid_spec=pltpu.PrefetchScalarGridSpec(num_scalar_prefetch=1, grid=(n_l, nr), in_specs=in_specs + [blk] * 3,
                                               out_specs=[blk] * 4),
        compiler_params=_params(("parallel", "parallel"), (4 * n_l * N_DEV + 40) * tr * C * 4))(
            me_arr, *lands, *fulls, w, m, v)


def _adamw_packed(name, w, g, m, v):
    R = w.shape[0]

    def body(w_ref, g_ref, m_ref, v_ref, d_ref, m2_ref, v2_ref):
        delta, m2, v2 = _adam_math(w_ref[...], g_ref[...], m_ref[...], v_ref[...])
        d_ref[...] = delta
        m2_ref[...] = m2
        v2_ref[...] = v2

    vm = pl.BlockSpec(memory_space=pltpu.VMEM)
    return pl.pallas_call(
        body, name=name, in_specs=[vm] * 4, out_specs=[vm] * 3, out_shape=[_sd((R, LANES), F32)] * 3,
        compiler_params=pltpu.CompilerParams(vmem_limit_bytes=VMEM_FLOOR))(w, g, m, v)


def _pack(arrays):
    flat = jnp.concatenate([a.reshape(-1).astype(F32) for a in arrays])
    pad = (-flat.shape[0]) % (8 * LANES)
    return jnp.pad(flat, (0, pad)).reshape(-1, LANES)


def _unpack(packed, like):
    flat = packed.reshape(-1)
    out, pos = [], 0
    for a in like:
        n = math.prod(a.shape)
        out.append(flat[pos:pos + n].reshape(a.shape))
        pos += n
    return out


def _ffn_fwd(tag, x, gain, w_in_sm, w_out_of, deps=()):
    h = _rms_fwd(tag + "_norm", x, gain, deps)
    gu, act = _pair_in(tag + "_in", h, w_in_sm, BF, _swiglu)
    x_new = _rows_out(tag + "_out", act, w_out_of(act), x, 0.5)
    return x_new, (x, h, gu, act)


def _ffn_bwd(tag, d, d_bf, saved, gain, w_in_sm, w_out_sm, send_out, send_in):
    x, h, gu, act = saved
    dgu = _rows_dact(tag + "_dact", d_bf, w_out_sm, [gu], [_sd(gu.shape, BF)], _swiglu_bwd_epi)[0]
    g_out = _rows_wgrad(tag + "_wgrad_out", act, d_bf, 0.5)
    token_out = send_out(g_out)
    g_in = _cols_wgrad(tag + "_wgrad_in", h, dgu, deps=(token_out,))
    token = send_in(g_in)
    dh = _cols_dh(tag + "_dh", dgu, w_in_sm, deps=(token,))
    dx, dx_bf, dgain = _rms_bwd(tag + "_dnorm", dh, x, gain, d)
    return dx, dx_bf, dgain


def kernel(x, positions, ln_ffn1, ffn1_w_in, ffn1_w_out, ln_mix, ln_ffn2, ffn2_w_in, ffn2_w_out, sgu_w_in, sgu_v_gain, sgu_v_bias, sgu_w_spatial, sgu_b_spatial, sgu_w_out, mla_w_in, mla_q_norm, mla_w_q_up, mla_kv_norm, mla_w_kv_up, mla_w_out, ln_final, loss_target, m_ln_ffn1, m_ffn1_w_in, m_ffn1_w_out, m_ln_mix, m_ln_ffn2, m_ffn2_w_in, m_ffn2_w_out, m_sgu_w_in, m_sgu_v_gain, m_sgu_v_bias, m_sgu_w_spatial, m_sgu_b_spatial, m_sgu_w_out, m_mla_w_in, m_mla_q_norm, m_mla_w_q_up, m_mla_kv_norm, m_mla_w_kv_up, m_mla_w_out, m_ln_final, v_ln_ffn1, v_ffn1_w_in, v_ffn1_w_out, v_ln_mix, v_ln_ffn2, v_ffn2_w_in, v_ffn2_w_out, v_sgu_w_in, v_sgu_v_gain, v_sgu_v_bias, v_sgu_w_spatial, v_sgu_b_spatial, v_sgu_w_out, v_mla_w_in, v_mla_q_norm, v_mla_w_q_up, v_mla_kv_norm, v_mla_w_kv_up, v_mla_w_out, v_ln_final):
    S, D = x.shape[1], x.shape[2]
    L = ln_ffn1.shape[0]
    H = mla_w_q_up.shape[-1] * N_DEV // (QK_NOPE + QK_ROPE)
    xi, yi, ci = _place()
    me = 4 * xi + 2 * yi + ci
    me_arr = jnp.reshape(me, (1,)).astype(jnp.int32)
    big = dict(ffn1_w_in=ffn1_w_in, ffn1_w_out=ffn1_w_out, ffn2_w_in=ffn2_w_in, ffn2_w_out=ffn2_w_out,
               sgu_w_in=sgu_w_in, sgu_w_out=sgu_w_out, mla_w_in=mla_w_in, mla_w_q_up=mla_w_q_up,
               mla_w_kv_up=mla_w_kv_up, mla_w_out=mla_w_out)
    big_m = dict(ffn1_w_in=m_ffn1_w_in, ffn1_w_out=m_ffn1_w_out, ffn2_w_in=m_ffn2_w_in, ffn2_w_out=m_ffn2_w_out,
                 sgu_w_in=m_sgu_w_in, sgu_w_out=m_sgu_w_out, mla_w_in=m_mla_w_in, mla_w_q_up=m_mla_w_q_up,
                 mla_w_kv_up=m_mla_w_kv_up, mla_w_out=m_mla_w_out)
    big_v = dict(ffn1_w_in=v_ffn1_w_in, ffn1_w_out=v_ffn1_w_out, ffn2_w_in=v_ffn2_w_in, ffn2_w_out=v_ffn2_w_out,
                 sgu_w_in=v_sgu_w_in, sgu_w_out=v_sgu_w_out, mla_w_in=v_mla_w_in, mla_w_q_up=v_mla_w_q_up,
                 mla_w_kv_up=v_mla_w_kv_up, mla_w_out=v_mla_w_out)
    names = list(big)
    mla_names = ["mla_w_in", "mla_w_q_up", "mla_w_kv_up", "mla_w_out"]

    blocks = {(k, l): big[k][l].astype(BF) for k in names for l in range(big[k].shape[0])}
    first = ("ffn1_w_in", 0)
    groups = {}
    for i in range(L):
        if i > 0:
            groups[f"ffn1_in_{i}"] = [("ffn1_w_in", i)]
        groups[f"ffn1_out_{i}"] = [("ffn1_w_out", i)]
        groups[f"mix_{i}"] = [("sgu_w_in", i // 2), ("sgu_w_out", i // 2)] if i % 2 == 0 else [(k, i // 2) for k in mla_names]
        groups[f"ffn2_{i}"] = [("ffn2_w_in", i), ("ffn2_w_out", i)]
    gathered = {first: _all_gather([blocks[first]])[0]}
    started, order_after = {}, (gathered[first],)
    for tag, grp in groups.items():
        started[tag] = _xstart(f"gather_start_{tag}", [blocks[k] for k in grp], True, order_after)
        order_after = (started[tag]["token"],)

    def fetch(tag, after):
        own, lands = _xwait(f"gather_wait_{tag}", started[tag], after, True)
        for k, blk, land in zip(groups[tag], own, lands):
            gathered[k] = _put_own(f"gather_own_{k[0]}_{k[1]}", land, blk, me_arr)

    def w_out_of(name, tag):
        def get(after):
            fetch(tag, after)
            return gathered[name]
        return get

    norm_rows = jnp.zeros((N_DEV, LANES), F32)
    mine = jnp.concatenate([mla_q_norm[0], mla_kv_norm[0]])
    norm_rows = lax.dynamic_update_slice(norm_rows, mine[None, :], (me, 0))
    norm_all = _all_reduce_small("norm_gains_gather", norm_rows)
    nq_sh = mla_q_norm.shape[1]
    q_gain = norm_all[:, :nq_sh].reshape(1, Q_LORA)
    kv_gain = norm_all[:, nq_sh:2 * nq_sh].reshape(1, KV_LORA)
    cos, sa, sb = _rope_tables(positions[0])
    scale = float((QK_NOPE + QK_ROPE) ** -0.5)

    xs = x[0]
    w_sp = sgu_w_spatial[0]
    b_sp = sgu_b_spatial[0][:, :, None]
    saved = []
    mla_w = {}
    deps = order_after
    for i in range(L):
        if i > 0:
            fetch(f"ffn1_in_{i}", xs)
        xs, s1 = _ffn_fwd(f"l{i}_ffn1", xs, ln_ffn1[i:i + 1], gathered[("ffn1_w_in", i)],
                          w_out_of(("ffn1_w_out", i), f"ffn1_out_{i}"), deps)
        deps = ()
        fetch(f"mix_{i}", xs)
        x_mix = xs
        h = _rms_fwd(f"l{i}_mix_norm", xs, ln_mix[i:i + 1])
        j = i // 2
        if i % 2 == 0:
            puv = _pair_in(f"l{i}_sgu_in", h, gathered[("sgu_w_in", j)], F32, None)[0]
            gated = _sgu_mid_fwd(f"l{i}_sgu_mid", puv, sgu_v_gain, sgu_v_bias, w_sp, b_sp)
            xs = _rows_out(f"l{i}_sgu_out", gated, gathered[("sgu_w_out", j)], xs, 1.0)
            sm = (x_mix, h, puv, gated)
        else:
            w_in_nat = gathered[("mla_w_in", j)].reshape(D, Q_LORA + KV_LORA + QK_ROPE)
            w_in_pad = jnp.pad(w_in_nat, ((0, 0), (0, HEAD_PAD - QK_ROPE)))
            wq_nat = jnp.transpose(gathered[("mla_w_q_up", j)], (1, 0, 2)).reshape(Q_LORA, H, QK_NOPE + QK_ROPE)
            wq_t = jnp.stack([wq_nat[:, :, :QK_NOPE].reshape(Q_LORA, H * HEAD_PAD),
                              jnp.pad(wq_nat[:, :, QK_NOPE:], ((0, 0), (0, 0), (0, HEAD_PAD - QK_ROPE))).reshape(
                                  Q_LORA, H * HEAD_PAD)])
            wkv_nat = jnp.transpose(gathered[("mla_w_kv_up", j)], (1, 0, 2)).reshape(KV_LORA, H, QK_NOPE + V_DIM)
            wkv_t = jnp.stack([wkv_nat[:, :, :QK_NOPE].reshape(KV_LORA, H * HEAD_PAD),
                               wkv_nat[:, :, QK_NOPE:].reshape(KV_LORA, H * HEAD_PAD)])
            w_o_nat = gathered[("mla_w_out", j)].reshape(H * V_DIM, D)
            mla_w[i] = (w_in_pad, wq_t, wkv_t, w_o_nat)
            proj = _mm2(f"l{i}_mla_in", h, w_in_pad, False, False, F32, tn_cands=(384, 128))
            lat, kr = _mla_mid_fwd(f"l{i}_mla_mid", proj, q_gain, kv_gain, cos, sa, sb)

            def q_epi(accs, ex, orefs, ids):
                orefs[0][...] = (accs[0] * scale).astype(BF)

            def qr_epi(accs, ex, orefs, ids):
                for hh in range(accs[0].shape[1] // HEAD_PAD):
                    sl = slice(hh * HEAD_PAD, (hh + 1) * HEAD_PAD)
                    orefs[0][:, sl] = (_rope(accs[0][:, sl], *ex) * scale).astype(BF)

            q_all = _mm_halves(f"l{i}_mla_q", (lat, 0), wq_t, BF, lambda half: (q_epi, qr_epi)[half], extras=(cos, sa, sb))
            kv_all = _mm_halves(f"l{i}_mla_kv", (lat, 1), wkv_t, BF)
            o, lse = _attn_fwd(f"l{i}_attn", q_all, kv_all, kr)
            xs = _mm2(f"l{i}_mla_out", o, w_o_nat, False, False, F32, res=xs)
            sm = (x_mix, h, proj, lat, kr, q_all, kv_all, o, lse)
        fetch(f"ffn2_{i}", xs)
        xs, s2 = _ffn_fwd(f"l{i}_ffn2", xs, ln_ffn2[i:i + 1], gathered[("ffn2_w_in", i)],
                          lambda after, i=i: gathered[("ffn2_w_out", i)])
        saved.append((s1, sm, s2))

    loss_row, d, d_bf, g_ln_final = _final_loss("final_loss", xs, ln_final[None, :], loss_target[0])
    loss = lax.psum(loss_row[0, 0], ("x", "y", "c"))

    sent = []

    def send(tag, keys, grads):
        st = _xstart(f"scatter_start_{tag}", grads, False)
        sent.append((tag, keys, st))
        return st["token"]

    g_ln1, g_ln2, g_lnm = [None] * L, [None] * L, [None] * L
    small_g = {}
    for i in reversed(range(L)):
        s1, sm, s2 = saved[i]
        d, d_bf, g_ln2[i] = _ffn_bwd(
            f"l{i}_ffn2", d, d_bf, s2, ln_ffn2[i:i + 1], gathered[("ffn2_w_in", i)], gathered[("ffn2_w_out", i)],
            lambda g, i=i: send(f"l{i}_ffn2_out", [("ffn2_w_out", i)], [g]),
            lambda g, i=i: send(f"l{i}_ffn2_in", [("ffn2_w_in", i)], [g]))
        j = i // 2
        if i % 2 == 0:
            x_mix, h, puv, gated = sm
            dgated = _rows_dact(f"l{i}_sgu_dgated", d_bf, gathered[("sgu_w_out", j)], [], [_sd(gated.shape, BF)], _store())[0]
            g_so = _rows_wgrad(f"l{i}_sgu_wgrad_out", gated, d_bf, None)
            token_out = send(f"l{i}_sgu_out", [("sgu_w_out", j)], [g_so])
            dpuv, dgain, dbias, dwsp, dbsp = _sgu_mid_bwd(f"l{i}_sgu_mid_bwd", puv, dgated, sgu_v_gain, sgu_v_bias,
                                                          w_sp, b_sp)
            small_g.update(sgu_v_gain=dgain, sgu_v_bias=dbias, sgu_w_spatial=dwsp[None], sgu_b_spatial=dbsp[None, :, :, 0])
            g_si = _cols_wgrad(f"l{i}_sgu_wgrad_in", h, dpuv, deps=(token_out,))
            token = send(f"l{i}_sgu_in", [("sgu_w_in", j)], [g_si])
            dh = _cols_dh(f"l{i}_sgu_dh", dpuv, gathered[("sgu_w_in", j)], deps=(token,))
        else:
            x_mix, h, proj, lat, kr, q_all, kv_all, o, lse = sm
            w_in_pad, wq_t, wkv_t, w_o_nat = mla_w[i]
            t = _attn_tile(S)
            do = _mm2(f"l{i}_mla_do", d_bf, w_o_nat, False, True, F32)
            g_wo = _mm2(f"l{i}_mla_wgrad_out", o, d_bf, True, False, BF)
            dq_all, delta = _attn_dq(f"l{i}_attn_dq", q_all, kv_all, kr, do, o, lse, cos, sa, sb, scale)
            lse_row = lse[:, :, 0].reshape(H, S // t, 1, t)
            delta_row = delta[:, :, 0].reshape(H, S // t, 1, t)
            dkv_all, dkr_heads = _attn_dkv(f"l{i}_attn_dkv", q_all, kv_all, kr, do, lse_row, delta_row)
            dqn = _mm2(f"l{i}_mla_dqn", (dq_all, 0), (wq_t, 0), False, True, F32)
            dqn = _mm2(f"l{i}_mla_dqn2", (dq_all, 1), (wq_t, 1), False, True, F32, res=dqn)
            dkvn = _mm2(f"l{i}_mla_dkvn", (dkv_all, 0), (wkv_t, 0), False, True, F32)
            dkvn = _mm2(f"l{i}_mla_dkvn2", (dkv_all, 1), (wkv_t, 1), False, True, F32, res=dkvn)
            g_wq = [_mm2(f"l{i}_mla_wgrad_q{t2}", (lat, 0), (dq_all, t2), True, False, BF) for t2 in range(2)]
            g_wkv = [_mm2(f"l{i}_mla_wgrad_kv{t2}", (lat, 1), (dkv_all, t2), True, False, BF) for t2 in range(2)]
            dproj, g_qn, g_kvn = _mla_mid_bwd(f"l{i}_mla_mid_bwd", proj, dqn, dkvn, dkr_heads, q_gain, kv_gain, cos, sa, sb)
            g_win = _mm2(f"l{i}_mla_wgrad_in", h, dproj, True, False, BF, tn_cands=(384, 128))
            n_in = Q_LORA + KV_LORA + QK_ROPE
            gq_nat = jnp.concatenate([g_wq[0].reshape(Q_LORA, H, HEAD_PAD),
                                      g_wq[1].reshape(Q_LORA, H, HEAD_PAD)[:, :, :QK_ROPE]], axis=2)
            gkv_nat = jnp.concatenate([g_wkv[0].reshape(KV_LORA, H, HEAD_PAD), g_wkv[1].reshape(KV_LORA, H, HEAD_PAD)], axis=2)
            token = send(f"l{i}_mla", [(k, j) for k in mla_names],
                         [g_win[:, :n_in].reshape(N_DEV, D // N_DEV, n_in),
                          jnp.transpose(gq_nat.reshape(Q_LORA, N_DEV, -1), (1, 0, 2)),
                          jnp.transpose(gkv_nat.reshape(KV_LORA, N_DEV, -1), (1, 0, 2)),
                          g_wo.reshape(N_DEV, H * V_DIM // N_DEV, D)])
            small_g.update(mla_q_norm=g_qn, mla_kv_norm=g_kvn)
            dh = _mm2(f"l{i}_mla_dh", dproj, w_in_pad, False, True, F32, tn_cands=(512, 256, 128), deps=(token,))
        d, d_bf, g_lnm[i] = _rms_bwd(f"l{i}_mix_dnorm", dh, x_mix, ln_mix[i:i + 1], d)
        d, d_bf, g_ln1[i] = _ffn_bwd(
            f"l{i}_ffn1", d, d_bf, s1, ln_ffn1[i:i + 1], gathered[("ffn1_w_in", i)], gathered[("ffn1_w_out", i)],
            lambda g, i=i: send(f"l{i}_ffn1_out", [("ffn1_w_out", i)], [g]),
            lambda g, i=i: send(f"l{i}_ffn1_in", [("ffn1_w_in", i)], [g]))
    grad_x = d[None]

    landed, partial = {}, {}
    for tag, keys, st in sent:
        fulls, lands = _xwait(f"scatter_wait_{tag}", st, d, False)
        for k, full, land in zip(keys, fulls, lands):
            partial[k], landed[k] = full, land
    big_out = {}
    for k in names:
        w = big[k]
        rc = (math.prod(w.shape[1:-1]), w.shape[-1])
        flat = (w.shape[0] * rc[0], rc[1])
        lands = [landed[(k, l)].reshape((N_DEV - 1,) + rc) for l in range(w.shape[0])]
        fulls = [partial[(k, l)].reshape((N_DEV,) + rc) for l in range(w.shape[0])]
        res = _adamw_sharded(f"adamw_{k}", lands, fulls, me_arr, w.reshape(flat), big_m[k].reshape(flat),
                             big_v[k].reshape(flat))
        big_out[k] = [r.reshape(w.shape) for r in res]

    small_g.update(ln_ffn1=jnp.concatenate(g_ln1), ln_mix=jnp.concatenate(g_lnm), ln_ffn2=jnp.concatenate(g_ln2),
                   ln_final=g_ln_final[0])
    small_names = ["ln_ffn1", "ln_mix", "ln_ffn2", "sgu_v_gain", "sgu_v_bias", "sgu_w_spatial", "sgu_b_spatial",
                   "ln_final", "mla_q_norm", "mla_kv_norm"]
    summed = _unpack(_all_reduce_small("small_grads_all_reduce", _pack([small_g[k] for k in small_names])),
                     [small_g[k] for k in small_names])
    small_grad = dict(zip(small_names, summed))
    for k in ("mla_q_norm", "mla_kv_norm"):
        small_grad[k] = lax.dynamic_slice(small_grad[k], (0, me * nq_sh), (1, nq_sh))
    small_w = dict(ln_ffn1=ln_ffn1, ln_mix=ln_mix, ln_ffn2=ln_ffn2, sgu_v_gain=sgu_v_gain, sgu_v_bias=sgu_v_bias,
                   sgu_w_spatial=sgu_w_spatial, sgu_b_spatial=sgu_b_spatial, ln_final=ln_final, mla_q_norm=mla_q_norm,
                   mla_kv_norm=mla_kv_norm)
    small_m = dict(ln_ffn1=m_ln_ffn1, ln_mix=m_ln_mix, ln_ffn2=m_ln_ffn2, sgu_v_gain=m_sgu_v_gain, sgu_v_bias=m_sgu_v_bias,
                   sgu_w_spatial=m_sgu_w_spatial, sgu_b_spatial=m_sgu_b_spatial, ln_final=m_ln_final,
                   mla_q_norm=m_mla_q_norm, mla_kv_norm=m_mla_kv_norm)
    small_v = dict(ln_ffn1=v_ln_ffn1, ln_mix=v_ln_mix, ln_ffn2=v_ln_ffn2, sgu_v_gain=v_sgu_v_gain, sgu_v_bias=v_sgu_v_bias,
                   sgu_w_spatial=v_sgu_w_spatial, sgu_b_spatial=v_sgu_b_spatial, ln_final=v_ln_final,
                   mla_q_norm=v_mla_q_norm, mla_kv_norm=v_mla_kv_norm)
    like = [small_w[k] for k in small_names]
    packed = _adamw_packed("adamw_small", _pack(like), _pack([small_grad[k] for k in small_names]),
                           _pack([small_m[k] for k in small_names]), _pack([small_v[k] for k in small_names]))
    small_out = {}
    unpacked = [_unpack(p, like) for p in packed]
    for idx, k in enumerate(small_names):
        small_out[k] = [small_grad[k].reshape(small_w[k].shape)] + [u[idx] for u in unpacked]

    order = ["ln_ffn1", "ffn1_w_in", "ffn1_w_out", "ln_mix", "ln_ffn2", "ffn2_w_in", "ffn2_w_out", "sgu_w_in",
             "sgu_v_gain", "sgu_v_bias", "sgu_w_spatial", "sgu_b_spatial", "sgu_w_out", "mla_w_in", "mla_q_norm",
             "mla_w_q_up", "mla_kv_norm", "mla_w_kv_up", "mla_w_out", "ln_final"]
    res = {k: (big_out[k] if k in big_out else small_out[k]) for k in order}
    outs = [loss, grad_x]
    for part in range(4):
        outs.extend(res[k][part] for k in order)
    return tuple(outs)
```

```python
import functools
import math

import jax
import jax.numpy as jnp
from jax import lax
from jax.experimental import pallas as pl
from jax.experimental.pallas import tpu as pltpu

F32 = jnp.float32
BF = jnp.bfloat16
MESH = pl.DeviceIdType.MESH

N_DEV = 8
EPS = 1e-6
CHUNK = 64
SGU_BLOCK = 128
SGU_GROUPS = 8
Q_LORA = 512
KV_LORA = 512
QK_NOPE = 128
QK_ROPE = 64
V_DIM = 128
ROPE_THETA = 10000.0
HEAD_PAD = 128
LANES = 128
ADAM_LR = 0.001
ADAM_B1 = 0.9
ADAM_B2 = 0.999
ADAM_EPS = 1e-08
ADAM_WD = 0.01
ADAM_STEP = 10
V7X_VMEM_BYTES = 64 * 1024 * 1024
VMEM_CAP = V7X_VMEM_BYTES - 6 * 1024 * 1024
VMEM_FLOOR = 32 * 1024 * 1024
NEG = -1e30
ATTN_GROUPS = 1
ATTN_UNROLLS = (8, 4, 2)


def _pick(n, cands):
    for c in cands:
        if n % c == 0:
            return c
    return n


def _nbytes(shape, dtype):
    return math.prod(int(s) for s in shape if s is not None) * jnp.dtype(dtype).itemsize


def _params(sem, block_bytes):
    limit = int(min(VMEM_CAP, max(VMEM_FLOOR, block_bytes)))
    return pltpu.CompilerParams(dimension_semantics=sem, vmem_limit_bytes=limit)


def _spec(shape, fn):
    return pl.BlockSpec(shape, fn)


_ANY = pl.BlockSpec(memory_space=pl.ANY)


def _mm(name, grid, ops, pairs, extras, outs, epilogue, acc_shapes, deps=()):
    nk = grid[2]
    n_ops, n_ex, n_out = len(ops), len(extras), len(outs)

    def load(refs, idx):
        loader = ops[idx][2] if len(ops[idx]) > 2 else None
        return (refs[idx][...] if loader is None else loader(refs[idx])).astype(BF)

    def prod(refs, p):
        ia, ib, ta, tb, _ = p
        a = load(refs, ia)
        b = load(refs, ib)
        dims = (((0 if ta else 1,), (1 if tb else 0,)), ((), ()))
        return lax.dot_general(a, b, dims, preferred_element_type=F32)

    def body(*refs):
        op_refs = refs[:n_ops]
        ex_refs = refs[n_ops:n_ops + n_ex]
        n_in = n_ops + n_ex + len(deps)
        out_refs = refs[n_in:n_in + n_out]
        acc_refs = refs[n_in + n_out:]
        ids = (pl.program_id(0), pl.program_id(1))

        def finish(vals):
            epilogue(vals, [e[...] for e in ex_refs], out_refs, ids)

        if nk == 1:
            vals = [None] * len(acc_shapes)
            for p in pairs:
                r = prod(op_refs, p)
                vals[p[4]] = r if vals[p[4]] is None else vals[p[4]] + r
            finish(vals)
        else:
            k = pl.program_id(2)

            def products():
                vals = [None] * len(acc_shapes)
                for p in pairs:
                    r = prod(op_refs, p)
                    vals[p[4]] = r if vals[p[4]] is None else vals[p[4]] + r
                return vals

            @pl.when(k == 0)
            def _():
                for a, v in zip(acc_refs, products()):
                    a[...] = v

            @pl.when((k > 0) & (k < nk - 1))
            def _():
                for a, v in zip(acc_refs, products()):
                    a[...] += v

            @pl.when(k == nk - 1)
            def _():
                finish([a[...] + v for a, v in zip(acc_refs, products())])

    in_arrays = [o[0] for o in ops] + [e[0] for e in extras]
    in_specs = [o[1] for o in ops] + [e[1] for e in extras]
    in_arrays += list(deps)
    in_specs += [_ANY] * len(deps)
    blk = 0
    for entry in ops + extras:
        blk += 2 * _nbytes(entry[1].block_shape, entry[0].dtype)
    for sd, sp in outs:
        blk += 2 * _nbytes(sp.block_shape, sd.dtype)
    acc_b = sum(_nbytes(s, F32) for s in acc_shapes)
    blk += 6 * acc_b
    scratch = [pltpu.VMEM(s, F32) for s in acc_shapes] if nk > 1 else []
    res = pl.pallas_call(
        body, name=name, grid=grid, in_specs=in_specs,
        out_specs=[o[1] for o in outs], out_shape=[o[0] for o in outs],
        scratch_shapes=scratch,
        compiler_params=_params(("parallel", "parallel", "arbitrary"), blk))(*in_arrays)
    return res


def _store(scale=None):
    def epi(accs, ex, outs, ids):
        v = accs[0]
        if scale is not None:
            v = v * scale
        outs[0][...] = v.astype(outs[0].dtype)
    return epi


def _store_residual(scale):
    def epi(accs, ex, outs, ids):
        outs[0][...] = ex[0] + scale * accs[0]
    return epi


def _sd(shape, dtype):
    return jax.ShapeDtypeStruct(tuple(shape), dtype)


def _pair_in(name, h, w_sm, out_dtype, act):
    S, D = h.shape
    c = w_sm.shape[-1]
    tm = _pick(S, (512, 256, 128))
    half = N_DEV // 2
    ops = [(h, _spec((tm, D), lambda j, i, k: (i, 0))),
           (w_sm, _spec((None, D, c), lambda j, i, k: (j, 0, 0))),
           (w_sm, _spec((None, D, c), lambda j, i, k: (j + half, 0, 0)))]
    outs = [(_sd((2, S, half * c), out_dtype), _spec((2, tm, c), lambda j, i, k: (0, i, j)))]
    if act is not None:
        outs.append((_sd((S, half * c), BF), _spec((tm, c), lambda j, i, k: (i, j))))

    def epi(accs, ex, orefs, ids):
        if act is None:
            orefs[0][0] = accs[0].astype(out_dtype)
            orefs[0][1] = accs[1].astype(out_dtype)
        else:
            keep0, keep1, out = act(accs[0], accs[1])
            orefs[0][0] = keep0.astype(out_dtype)
            orefs[0][1] = keep1.astype(out_dtype)
            orefs[1][...] = out.astype(BF)

    return _mm(name, (half, S // tm, 1), ops, [(0, 1, False, False, 0), (0, 2, False, False, 1)], [], outs, epi,
               [(tm, c), (tm, c)])


def _two_slabs(ref):
    return jnp.concatenate([ref[0], ref[1]], axis=0)


def _rows_out(name, a, w_sm, res, scale):
    S = a.shape[0]
    r, D = w_sm.shape[-2], w_sm.shape[-1]
    tm = _pick(S, (1024, 512, 256, 128))
    tn = _pick(D, (1024, 512, 256, 128))
    ops = [(a, _spec((tm, 2 * r), lambda i, j, k: (i, k))),
           (w_sm, _spec((2, r, tn), lambda i, j, k: (k, 0, j)), _two_slabs)]
    extras = [(res, _spec((tm, tn), lambda i, j, k: (i, j)))]
    outs = [(_sd((S, D), F32), _spec((tm, tn), lambda i, j, k: (i, j)))]
    return _mm(name, (S // tm, D // tn, N_DEV // 2), ops, [(0, 1, False, False, 0)], extras, outs,
               _store_residual(scale), [(tm, tn)])[0]


def _rows_dact(name, d_bf, w_sm, extras_arrays, out_shapes, epi):
    S, D = d_bf.shape
    r = w_sm.shape[-2]
    tm = _pick(S, (1024, 512, 256, 128))
    ops = [(d_bf, _spec((tm, D), lambda j, i, k: (i, 0))),
           (w_sm, _spec((2, r, D), lambda j, i, k: (j, 0, 0)), _two_slabs)]
    extras = []
    for arr in extras_arrays:
        if arr.ndim == 3:
            extras.append((arr, _spec((arr.shape[0], tm, 2 * r), lambda j, i, k: (0, i, j))))
        else:
            extras.append((arr, _spec((tm, 2 * r), lambda j, i, k: (i, j))))
    outs = []
    for sd in out_shapes:
        if len(sd.shape) == 3:
            outs.append((sd, _spec((sd.shape[0], tm, 2 * r), lambda j, i, k: (0, i, j))))
        else:
            outs.append((sd, _spec((tm, 2 * r), lambda j, i, k: (i, j))))
    return _mm(name, (N_DEV // 2, S // tm, 1), ops, [(0, 1, False, True, 0)], extras, outs, epi, [(tm, 2 * r)])


def _rows_wgrad(name, a, d_bf, scale):
    S, D = d_bf.shape
    r = a.shape[1] // N_DEV
    tn = _pick(D, (1024, 512, 256, 128))
    tk = _pick(S, (2048, 1024, 512, 256, 128))
    ops = [(a, _spec((tk, 2 * r), lambda s, j, k: (k, s))),
           (d_bf, _spec((tk, tn), lambda s, j, k: (k, j)))]
    outs = [(_sd((N_DEV, r, D), BF), _spec((2, r, tn), lambda s, j, k: (s, 0, j)))]

    def epi(accs, ex, orefs, ids):
        v = accs[0] if scale is None else accs[0] * scale
        orefs[0][0] = v[:r].astype(BF)
        orefs[0][1] = v[r:].astype(BF)

    return _mm(name, (N_DEV // 2, D // tn, S // tk), ops, [(0, 1, True, False, 0)], [], outs, epi, [(2 * r, tn)])[0]


def _cols_dh(name, dpair, w_sm, deps=()):
    _, S, _ = dpair.shape
    D, c = w_sm.shape[-2], w_sm.shape[-1]
    half = N_DEV // 2
    tm = _pick(S, (1024, 512, 256, 128))
    tn = _pick(D, (1024, 512, 256, 128))
    ops = [(dpair, _spec((None, tm, c), lambda i, j, k: (k // half, i, k % half))),
           (w_sm, _spec((None, tn, c), lambda i, j, k: (k, j, 0)))]
    outs = [(_sd((S, D), F32), _spec((tm, tn), lambda i, j, k: (i, j)))]
    return _mm(name, (S // tm, D // tn, N_DEV), ops, [(0, 1, False, True, 0)], [], outs, _store(), [(tm, tn)],
               deps=deps)[0]


def _cols_wgrad(name, h, dpair, deps=()):
    S, D = h.shape
    half = N_DEV // 2
    c = dpair.shape[2] // half
    tm = _pick(D, (1024, 512, 256, 128))
    tk = _pick(S, (2048, 1024, 512, 256, 128))
    ops = [(h, _spec((tk, tm), lambda s, i, k: (k, i))),
           (dpair, _spec((None, tk, c), lambda s, i, k: (s // half, k, s % half)))]
    outs = [(_sd((N_DEV, D, c), BF), _spec((None, tm, c), lambda s, i, k: (s, i, 0)))]
    return _mm(name, (N_DEV, D // tm, S // tk), ops, [(0, 1, True, False, 0)], [], outs, _store(), [(tm, c)],
               deps=deps)[0]


def _mm2(name, a, b, ta, tb, out_dtype, epi=None, extras=(), res=None, tn_cands=(512, 384, 256, 128), deps=()):
    a, a_lead = a if isinstance(a, tuple) else (a, None)
    b, b_lead = b if isinstance(b, tuple) else (b, None)
    M = a.shape[-1] if ta else a.shape[-2]
    K = a.shape[-2] if ta else a.shape[-1]
    N = b.shape[-2] if tb else b.shape[-1]
    tm = _pick(M, (1024, 512, 256, 128))
    tn = _pick(N, tn_cands)
    tk = _pick(K, (2048, 1152, 1024, 512, 256, 128))

    def matrix_spec(lead, shape, fn):
        if lead is None:
            return _spec(shape, fn)
        return _spec((None,) + shape, lambda i, j, k: (lead,) + fn(i, j, k))

    a_spec = matrix_spec(a_lead, (tk, tm), lambda i, j, k: (k, i)) if ta else matrix_spec(a_lead, (tm, tk), lambda i, j, k: (i, k))
    b_spec = matrix_spec(b_lead, (tn, tk), lambda i, j, k: (j, k)) if tb else matrix_spec(b_lead, (tk, tn), lambda i, j, k: (k, j))
    ex = [(e, _spec((tm, e.shape[1]), lambda i, j, k: (i, 0))) for e in extras]
    if res is not None:
        ex = [(res, _spec((tm, tn), lambda i, j, k: (i, j)))]
        epi = _store_residual(1.0)
    outs = [(_sd((M, N), out_dtype), _spec((tm, tn), lambda i, j, k: (i, j)))]
    return _mm(name, (M // tm, N // tn, K // tk), [(a, a_spec), (b, b_spec)], [(0, 1, ta, tb, 0)], ex, outs,
               epi or _store(), [(tm, tn)], deps=deps)[0]


def _mm_halves(name, a, w_t, out_dtype, epi_of_half=None, extras=()):
    a, lead = a
    M, K = a.shape[1:]
    N = w_t.shape[2]
    tm = _pick(M, (1024, 512, 256, 128))
    tn = _pick(N, (512, 256, 128))
    tk = _pick(K, (2048, 1024, 512, 256, 128))
    nj = N // tn
    ops = [(a, _spec((None, tm, tk), lambda i, j, k: (lead, i, k))),
           (w_t, _spec((None, tk, tn), lambda i, j, k: (j // nj, k, j % nj)))]
    ex = [(e, _spec((tm, e.shape[1]), lambda i, j, k: (i, 0))) for e in extras]
    outs = [(_sd((2, M, N), out_dtype), _spec((None, tm, tn), lambda i, j, k: (j // nj, i, j % nj)))]

    def epi(accs, ex_tiles, orefs, ids):
        if epi_of_half is None:
            orefs[0][...] = accs[0].astype(out_dtype)
        else:
            for half in range(2):
                pl.when(ids[1] // nj == half)(functools.partial(epi_of_half(half), accs, ex_tiles, orefs, ids))

    return _mm(name, (M // tm, 2 * nj, K // tk), ops, [(0, 1, False, False, 0)], ex, outs, epi, [(tm, tn)])[0]


def _rms_fwd(name, x, g, deps=()):
    S, D = x.shape
    ts = _pick(S, (512, 256, 128))

    def body(x_ref, g_ref, *rest):
        h_ref = rest[-1]
        xv = x_ref[...]
        r = lax.rsqrt(jnp.mean(xv * xv, axis=-1, keepdims=True) + EPS)
        h_ref[...] = (xv * r * g_ref[...]).astype(BF)

    return pl.pallas_call(
        body, name=name, grid=(S // ts,),
        in_specs=[_spec((ts, D), lambda i: (i, 0)), _spec((1, D), lambda i: (0, 0))] + [_ANY] * len(deps),
        out_specs=_spec((ts, D), lambda i: (i, 0)), out_shape=_sd((S, D), BF),
        compiler_params=_params(("parallel",), 12 * ts * D * 4))(x, g, *deps)


def _rms_bwd(name, dh, x, g, dres, deps=()):
    S, D = x.shape
    ts = _pick(S, (256, 128))

    def body(dh_ref, x_ref, g_ref, dres_ref, *rest):
        dx_ref, dxb_ref, dg_ref = rest[len(deps):]
        xv = x_ref[...]
        dhv = dh_ref[...]
        r = lax.rsqrt(jnp.mean(xv * xv, axis=-1, keepdims=True) + EPS)
        xhat = xv * r
        dxh = dhv * g_ref[...]
        cm = jnp.mean(dxh * xhat, axis=-1, keepdims=True)
        dx = r * (dxh - xhat * cm) + dres_ref[...]
        dx_ref[...] = dx
        dxb_ref[...] = dx.astype(BF)

        @pl.when(pl.program_id(0) == 0)
        def _():
            dg_ref[...] = jnp.zeros_like(dg_ref)

        dg_ref[...] += jnp.sum(dhv * xhat, axis=0, keepdims=True)

    row = _spec((ts, D), lambda i: (i, 0))
    vec = _spec((1, D), lambda i: (0, 0))
    return pl.pallas_call(
        body, name=name, grid=(S // ts,),
        in_specs=[row, row, vec, row] + [_ANY] * len(deps), out_specs=[row, row, vec],
        out_shape=[_sd((S, D), F32), _sd((S, D), BF), _sd((1, D), F32)],
        compiler_params=_params(("arbitrary",), 20 * ts * D * 4))(dh, x, g, dres, *deps)


def _final_loss(name, x, g, target):
    S, D = x.shape
    ts = _pick(S, (256, 128))

    def body(x_ref, g_ref, t_ref, loss_ref, dx_ref, dxb_ref, dg_ref):
        xv = x_ref[...]
        gv = g_ref[...]
        r = lax.rsqrt(jnp.mean(xv * xv, axis=-1, keepdims=True) + EPS)
        xhat = xv * r
        err = xhat * gv - t_ref[...]
        part = 0.5 * jnp.sum(jnp.mean(err * err, axis=-1, keepdims=True), axis=0, keepdims=True)
        dy = err * (1.0 / D)
        dxh = dy * gv
        cm = jnp.mean(dxh * xhat, axis=-1, keepdims=True)
        dx = r * (dxh - xhat * cm)
        dx_ref[...] = dx
        dxb_ref[...] = dx.astype(BF)

        @pl.when(pl.program_id(0) == 0)
        def _():
            dg_ref[...] = jnp.zeros_like(dg_ref)
            loss_ref[...] = jnp.zeros_like(loss_ref)

        dg_ref[...] += jnp.sum(dy * xhat, axis=0, keepdims=True)
        loss_ref[...] += jnp.broadcast_to(part, loss_ref.shape)

    row = _spec((ts, D), lambda i: (i, 0))
    vec = _spec((1, D), lambda i: (0, 0))
    return pl.pallas_call(
        body, name=name, grid=(S // ts,),
        in_specs=[row, vec, row], out_specs=[_spec((1, LANES), lambda i: (0, 0)), row, row, vec],
        out_shape=[_sd((1, LANES), F32), _sd((S, D), F32), _sd((S, D), BF), _sd((1, D), F32)],
        compiler_params=_params(("arbitrary",), 20 * ts * D * 4))(x, g, target)


def _swiglu(gate, up):
    sg = jax.nn.sigmoid(gate)
    silu = gate * sg
    return up * (sg * (1.0 + gate * (1.0 - sg))), silu, silu * up


def _swiglu_bwd_epi(accs, ex, orefs, ids):
    da = 0.5 * accs[0]
    orefs[0][0] = (da * ex[0][0].astype(F32)).astype(BF)
    orefs[0][1] = (da * ex[0][1].astype(F32)).astype(BF)


_GELU_C = math.sqrt(2.0 / math.pi)


def _gelu(x):
    return x * (0.5 * (1.0 + jnp.tanh(_GELU_C * (x + 0.044715 * (x * x * x)))))


def _gelu_and_grad(x):
    x2 = x * x
    t = jnp.tanh(_GELU_C * (x + 0.044715 * (x2 * x)))
    cdf = 0.5 * (1.0 + t)
    return x * cdf, cdf + x * ((0.5 * _GELU_C) * (1.0 - t * t) * (1.0 + (3.0 * 0.044715) * x2))


def _causal_block_mask():
    row = lax.broadcasted_iota(jnp.int32, (SGU_BLOCK, SGU_BLOCK), 0) // CHUNK
    col = lax.broadcasted_iota(jnp.int32, (SGU_BLOCK, SGU_BLOCK), 1) // CHUNK
    return row >= col


def _sgu_mid_fwd(name, puv, gain, bias, w_sp, b_sp):
    _, S, W = puv.shape
    G = SGU_GROUPS
    C = W // G
    T = SGU_BLOCK

    def body(puv_ref, gain_ref, bias_ref, w_ref, b_ref, out_ref):
        mask = _causal_block_mask()
        v = _gelu(puv_ref[1])
        mu = jnp.mean(v, axis=-1, keepdims=True)
        vc = v - mu
        rs = lax.rsqrt(jnp.mean(vc * vc, axis=-1, keepdims=True) + EPS)
        vln = (vc * rs * gain_ref[...] + bias_ref[...]).astype(BF)
        for g in range(G):
            wg = jnp.where(mask, w_ref[g], 0.0).astype(BF)
            mixed = jnp.dot(wg, vln[:, g * C:(g + 1) * C], preferred_element_type=F32) + b_ref[g]
            out_ref[:, g * C:(g + 1) * C] = (_gelu(puv_ref[0, :, g * C:(g + 1) * C]) * mixed).astype(BF)

    return pl.pallas_call(
        body, name=name, grid=(S // T,),
        in_specs=[_spec((2, T, W), lambda i: (0, i, 0)), _spec((1, W), lambda i: (0, 0)), _spec((1, W), lambda i: (0, 0)),
                  _spec((G, T, T), lambda i: (0, 0, 0)), _spec((G, T, 1), lambda i: (0, 0, 0))],
        out_specs=_spec((T, W), lambda i: (i, 0)), out_shape=_sd((S, W), BF),
        compiler_params=_params(("parallel",), 16 * T * W * 4))(puv, gain, bias, w_sp, b_sp)


def _sgu_mid_bwd(name, puv, dgated, gain, bias, w_sp, b_sp):
    _, S, W = puv.shape
    G = SGU_GROUPS
    C = W // G
    T = SGU_BLOCK

    def body(puv_ref, dg_ref, gain_ref, bias_ref, w_ref, b_ref, dpuv_ref, dgain_ref, dbias_ref, dw_ref, db_ref, dvln_ref):
        @pl.when(pl.program_id(0) == 0)
        def _():
            dgain_ref[...] = jnp.zeros_like(dgain_ref)
            dbias_ref[...] = jnp.zeros_like(dbias_ref)
            dw_ref[...] = jnp.zeros_like(dw_ref)
            db_ref[...] = jnp.zeros_like(db_ref)

        mask = _causal_block_mask()
        v, v_grad = _gelu_and_grad(puv_ref[1])
        mu = jnp.mean(v, axis=-1, keepdims=True)
        vc = v - mu
        rs = lax.rsqrt(jnp.mean(vc * vc, axis=-1, keepdims=True) + EPS)
        vhat = vc * rs
        gain_v = gain_ref[...]
        vln = (vhat * gain_v + bias_ref[...]).astype(BF)
        for g in range(G):
            sl = slice(g * C, (g + 1) * C)
            wg = jnp.where(mask, w_ref[g], 0.0).astype(BF)
            vg = vln[:, sl]
            mixed = jnp.dot(wg, vg, preferred_element_type=F32) + b_ref[g]
            u, u_grad = _gelu_and_grad(puv_ref[0, :, sl])
            dgt = dg_ref[:, sl].astype(F32)
            dpuv_ref[0, :, sl] = (dgt * mixed * u_grad).astype(BF)
            dmix = dgt * u
            db_ref[g] += jnp.sum(dmix, axis=-1, keepdims=True)
            dmb = dmix.astype(BF)
            dwg = lax.dot_general(dmb, vg, (((1,), (1,)), ((), ())), preferred_element_type=F32)
            dw_ref[g] += jnp.where(mask, dwg, 0.0)
            dvln_ref[:, sl] = lax.dot_general(wg, dmb, (((0,), (0,)), ((), ())), preferred_element_type=F32)
        dvln = dvln_ref[...]
        dgain_ref[...] += jnp.sum(dvln * vhat, axis=0, keepdims=True)
        dbias_ref[...] += jnp.sum(dvln, axis=0, keepdims=True)
        dvh = dvln * gain_v
        m1 = jnp.mean(dvh, axis=-1, keepdims=True)
        m2 = jnp.mean(dvh * vhat, axis=-1, keepdims=True)
        dv = rs * (dvh - m1 - vhat * m2)
        dpuv_ref[1] = (dv * v_grad).astype(BF)

    vec = _spec((1, W), lambda i: (0, 0))
    wsp = _spec((G, T, T), lambda i: (0, 0, 0))
    bsp = _spec((G, T, 1), lambda i: (0, 0, 0))
    return pl.pallas_call(
        body, name=name, grid=(S // T,),
        in_specs=[_spec((2, T, W), lambda i: (0, i, 0)), _spec((T, W), lambda i: (i, 0)), vec, vec, wsp, bsp],
        out_specs=[_spec((2, T, W), lambda i: (0, i, 0)), vec, vec, wsp, bsp],
        out_shape=[_sd((2, S, W), BF), _sd((1, W), F32), _sd((1, W), F32), _sd((G, T, T), F32), _sd((G, T, 1), F32)],
        scratch_shapes=[pltpu.VMEM((T, W), F32)],
        compiler_params=_params(("arbitrary",), 24 * T * W * 4))(puv, dgated, gain, bias, w_sp, b_sp)


def _rope_tables(positions):
    half = QK_ROPE // 2
    inv_freq = 1.0 / (ROPE_THETA ** (jnp.arange(half, dtype=F32) / half))
    ang = positions.astype(F32)[:, None] * inv_freq[None, :]
    cos, sin = jnp.cos(ang), jnp.sin(ang)
    z = jnp.zeros_like(cos)
    return (jnp.concatenate([cos, cos, z, z], axis=1), jnp.concatenate([-sin, z, z, z], axis=1),
            jnp.concatenate([z, sin, z, z], axis=1))


def _rope(x, cos, sa, sb):
    return x * cos + pltpu.roll(x, HEAD_PAD - QK_ROPE // 2, 1) * sa + pltpu.roll(x, QK_ROPE // 2, 1) * sb


def _rope_t(dy, cos, sa, sb):
    return dy * cos + pltpu.roll(dy * sa, QK_ROPE // 2, 1) + pltpu.roll(dy * sb, HEAD_PAD - QK_ROPE // 2, 1)


def _rms_rows(x, g):
    r = lax.rsqrt(jnp.mean(x * x, axis=-1, keepdims=True) + EPS)
    return x * r * g


def _rms_rows_bwd(dy, x, g):
    r = lax.rsqrt(jnp.mean(x * x, axis=-1, keepdims=True) + EPS)
    xhat = x * r
    dxh = dy * g
    cm = jnp.mean(dxh * xhat, axis=-1, keepdims=True)
    return r * (dxh - xhat * cm), jnp.sum(dy * xhat, axis=0, keepdims=True)


def _mla_mid_fwd(name, proj, qg, kvg, cos, sa, sb):
    S, P = proj.shape
    ts = _pick(S, (512, 256, 128))

    def body(p_ref, qg_ref, kvg_ref, cos_ref, sa_ref, sb_ref, lat_ref, kr_ref):
        lat_ref[0] = _rms_rows(p_ref[:, :Q_LORA], qg_ref[...]).astype(BF)
        lat_ref[1] = _rms_rows(p_ref[:, Q_LORA:Q_LORA + KV_LORA], kvg_ref[...]).astype(BF)
        kr_ref[...] = _rope(p_ref[:, Q_LORA + KV_LORA:], cos_ref[...], sa_ref[...], sb_ref[...]).astype(BF)

    tab = _spec((ts, HEAD_PAD), lambda i: (i, 0))
    return pl.pallas_call(
        body, name=name, grid=(S // ts,),
        in_specs=[_spec((ts, P), lambda i: (i, 0)), _spec((1, Q_LORA), lambda i: (0, 0)),
                  _spec((1, KV_LORA), lambda i: (0, 0)), tab, tab, tab],
        out_specs=[_spec((2, ts, Q_LORA), lambda i: (0, i, 0)), tab],
        out_shape=[_sd((2, S, Q_LORA), BF), _sd((S, HEAD_PAD), BF)],
        compiler_params=_params(("parallel",), 16 * ts * P * 4))(proj, qg, kvg, cos, sa, sb)


def _mla_mid_bwd(name, proj, dqn, dkvn, dkr_heads, qg, kvg, cos, sa, sb):
    S, P = proj.shape
    H = dkr_heads.shape[0]
    ts = _pick(S, (256, 128))

    def body(p_ref, dqn_ref, dkvn_ref, dkr_ref, qg_ref, kvg_ref, cos_ref, sa_ref, sb_ref, dp_ref, dqg_ref, dkvg_ref):
        @pl.when(pl.program_id(0) == 0)
        def _():
            dqg_ref[...] = jnp.zeros_like(dqg_ref)
            dkvg_ref[...] = jnp.zeros_like(dkvg_ref)

        dq, dqg = _rms_rows_bwd(dqn_ref[...], p_ref[:, :Q_LORA], qg_ref[...])
        dkv, dkvg = _rms_rows_bwd(dkvn_ref[...], p_ref[:, Q_LORA:Q_LORA + KV_LORA], kvg_ref[...])
        dqg_ref[...] += dqg
        dkvg_ref[...] += dkvg
        dkr = dkr_ref[0]
        for h in range(1, H):
            dkr = dkr + dkr_ref[h]
        dp_ref[:, :Q_LORA] = dq.astype(BF)
        dp_ref[:, Q_LORA:Q_LORA + KV_LORA] = dkv.astype(BF)
        dp_ref[:, Q_LORA + KV_LORA:] = _rope_t(dkr, cos_ref[...], sa_ref[...], sb_ref[...]).astype(BF)

    tab = _spec((ts, HEAD_PAD), lambda i: (i, 0))
    lat = _spec((ts, Q_LORA), lambda i: (i, 0))
    gq = _spec((1, Q_LORA), lambda i: (0, 0))
    return pl.pallas_call(
        body, name=name, grid=(S // ts,),
        in_specs=[_spec((ts, P), lambda i: (i, 0)), lat, lat, _spec((H, ts, HEAD_PAD), lambda i: (0, i, 0)),
                  gq, gq, tab, tab, tab],
        out_specs=[_spec((ts, P), lambda i: (i, 0)), gq, gq],
        out_shape=[_sd((S, P), BF), _sd((1, Q_LORA), F32), _sd((1, KV_LORA), F32)],
        compiler_params=_params(("arbitrary",), 24 * ts * P * 4))(proj, dqn, dkvn, dkr_heads, qg, kvg, cos, sa, sb)


def _attn_tile(S):
    return _pick(S, (512,)) if S >= 2048 else _pick(S, (128,))


def _diag_mask(t, transposed):
    q = lax.broadcasted_iota(jnp.int32, (t, t), 1 if transposed else 0) // CHUNK
    k = lax.broadcasted_iota(jnp.int32, (t, t), 0 if transposed else 1) // CHUNK
    return k <= q


_NT = (((1,), (1,)), ((), ()))


def _tile_loop(lo, hi, step, carry):
    pos = lo
    for unroll in ATTN_UNROLLS:
        trips = (hi - pos) // unroll

        def several(tt, c, unroll=unroll, pos=pos):
            for u in range(unroll):
                c = step(pos + tt * unroll + u, c)
            return c

        carry = lax.fori_loop(0, trips, several, carry)
        pos = pos + trips * unroll
    return lax.fori_loop(pos, hi, step, carry)


def _attn_fwd(name, q_all, kv_all, kr):
    _, S, HP = q_all.shape
    H = HP // HEAD_PAD
    t = _attn_tile(S)
    nq = S // t
    ng = ATTN_GROUPS
    tg = t // ng

    def body(q_ref, kv_ref, kr_ref, o_ref, lse_ref, kcat_ref):
        i = pl.program_id(1)

        @pl.when(i == 0)
        def _():
            kcat_ref[:, :HEAD_PAD] = kv_ref[0]
            kcat_ref[:, HEAD_PAD:] = kr_ref[...]

        qs = [jnp.concatenate([q_ref[0, g * tg:(g + 1) * tg], q_ref[1, g * tg:(g + 1) * tg]], axis=1) for g in range(ng)]

        def step(j, carry, masked):
            off = pl.multiple_of(j * t, t)
            kj = kcat_ref[pl.ds(off, t), :]
            vj = kv_ref[1, pl.ds(off, t), :]
            out = []
            for g in range(ng):
                m, l, acc = carry[g]
                s = lax.dot_general(qs[g], kj, _NT, preferred_element_type=F32)
                if masked:
                    s = jnp.where(_diag_mask(t, False)[g * tg:(g + 1) * tg], s, NEG)
                m2 = jnp.maximum(m, jnp.max(s, axis=-1, keepdims=True))
                al = jnp.exp(m - m2)
                p = jnp.exp(s - m2)
                l2 = al * l + jnp.sum(p, axis=-1, keepdims=True)
                acc2 = al * acc + jnp.dot(p.astype(BF), vj, preferred_element_type=F32)
                out.append((m2, l2, acc2))
            return tuple(out)

        init = tuple((jnp.full((tg, 1), NEG, F32), jnp.zeros((tg, 1), F32), jnp.zeros((tg, V_DIM), F32))
                     for _ in range(ng))
        carry = _tile_loop(0, i, lambda j, c: step(j, c, False), init)
        carry = step(i, carry, True)
        for g in range(ng):
            m, l, acc = carry[g]
            o_ref[g * tg:(g + 1) * tg, :] = acc / l
            lse_ref[g * tg:(g + 1) * tg, :] = jnp.broadcast_to(m + jnp.log(l), (tg, LANES))

    return pl.pallas_call(
        body, name=name, grid=(H, nq),
        in_specs=[_spec((2, t, HEAD_PAD), lambda h, i: (0, i, h)), _spec((2, S, HEAD_PAD), lambda h, i: (0, 0, h)),
                  _spec((S, HEAD_PAD), lambda h, i: (0, 0))],
        out_specs=[_spec((t, HEAD_PAD), lambda h, i: (i, h)), _spec((None, t, LANES), lambda h, i: (h, i, 0))],
        out_shape=[_sd((S, HP), F32), _sd((H, S, LANES), F32)],
        scratch_shapes=[pltpu.VMEM((S, 2 * HEAD_PAD), BF)],
        compiler_params=_params(("parallel", "arbitrary"), 8 * S * HEAD_PAD * 2 + 24 * t * t * 4))(q_all, kv_all, kr)


def _attn_dq(name, q_all, kv_all, kr, do, o, lse, cos, sa, sb, scale):
    _, S, HP = q_all.shape
    H = HP // HEAD_PAD
    t = _attn_tile(S)
    nq = S // t

    def body(q_ref, kv_ref, kr_ref, do_ref, o_ref, lse_ref, cos_ref, sa_ref, sb_ref, dq_ref, dl_ref, kcat_ref):
        i = pl.program_id(1)

        @pl.when(i == 0)
        def _():
            kcat_ref[:, :HEAD_PAD] = kv_ref[0]
            kcat_ref[:, HEAD_PAD:] = kr_ref[...]

        ng = ATTN_GROUPS
        tg = t // ng
        rows = [slice(g * tg, (g + 1) * tg) for g in range(ng)]
        qs = [jnp.concatenate([q_ref[0, r], q_ref[1, r]], axis=1) for r in rows]
        dobs = [do_ref[r, :].astype(BF) for r in rows]
        lses = [lse_ref[r, 0:1] for r in rows]
        dls = [jnp.sum(do_ref[r, :] * o_ref[r, :], axis=-1, keepdims=True) for r in rows]
        for g in range(ng):
            dl_ref[rows[g], :] = jnp.broadcast_to(dls[g], (tg, LANES))

        def step(j, dqs, masked):
            off = pl.multiple_of(j * t, t)
            kj = kcat_ref[pl.ds(off, t), :]
            vj = kv_ref[1, pl.ds(off, t), :]
            out = []
            for g in range(ng):
                s = lax.dot_general(qs[g], kj, _NT, preferred_element_type=F32)
                if masked:
                    s = jnp.where(_diag_mask(t, False)[rows[g]], s, NEG)
                p = jnp.exp(s - lses[g])
                dp = lax.dot_general(dobs[g], vj, _NT, preferred_element_type=F32)
                ds = (p * (dp - dls[g])).astype(BF)
                out.append(dqs[g] + jnp.dot(ds, kj, preferred_element_type=F32))
            return tuple(out)

        init = tuple(jnp.zeros((tg, 2 * HEAD_PAD), F32) for _ in range(ng))
        dqs = _tile_loop(0, i, lambda j, c: step(j, c, False), init)
        dqs = step(i, dqs, True)
        for g in range(ng):
            dq_ref[0, rows[g]] = (dqs[g][:, :HEAD_PAD] * scale).astype(BF)
            dq_ref[1, rows[g]] = (_rope_t(dqs[g][:, HEAD_PAD:], cos_ref[rows[g], :], sa_ref[rows[g], :],
                                          sb_ref[rows[g], :]) * scale).astype(BF)

    tab = _spec((t, HEAD_PAD), lambda h, i: (i, 0))
    stat = _spec((None, t, LANES), lambda h, i: (h, i, 0))
    head_tile = _spec((t, HEAD_PAD), lambda h, i: (i, h))
    return pl.pallas_call(
        body, name=name, grid=(H, nq),
        in_specs=[_spec((2, t, HEAD_PAD), lambda h, i: (0, i, h)), _spec((2, S, HEAD_PAD), lambda h, i: (0, 0, h)),
                  _spec((S, HEAD_PAD), lambda h, i: (0, 0)), head_tile, head_tile, stat, tab, tab, tab],
        out_specs=[_spec((2, t, HEAD_PAD), lambda h, i: (0, i, h)), stat],
        out_shape=[_sd((2, S, HP), BF), _sd((H, S, LANES), F32)],
        scratch_shapes=[pltpu.VMEM((S, 2 * HEAD_PAD), BF)],
        compiler_params=_params(("parallel", "arbitrary"), 8 * S * HEAD_PAD * 2 + 32 * t * t * 4))(
            q_all, kv_all, kr, do, o, lse, cos, sa, sb)


def _attn_dkv(name, q_all, kv_all, kr, do, lse_row, delta_row):
    _, S, HP = q_all.shape
    H = HP // HEAD_PAD
    t = _attn_tile(S)
    nq = S // t

    def body(q_ref, kv_ref, kr_ref, do_ref, lse_ref, dl_ref, dkv_ref, dkr_ref, qcat_ref):
        j = pl.program_id(1)

        @pl.when(j == 0)
        def _():
            qcat_ref[:, :HEAD_PAD] = q_ref[0]
            qcat_ref[:, HEAD_PAD:] = q_ref[1]

        ng = ATTN_GROUPS
        tg = t // ng
        rows = [slice(g * tg, (g + 1) * tg) for g in range(ng)]
        kjs = [jnp.concatenate([kv_ref[0, r], kr_ref[r, :]], axis=1) for r in rows]
        vjs = [kv_ref[1, r] for r in rows]

        def step(i, carry, masked):
            off = pl.multiple_of(i * t, t)
            qi = qcat_ref[pl.ds(off, t), :]
            doi = do_ref[pl.ds(off, t), :].astype(BF)
            lse_i = lse_ref[i]
            dl_i = dl_ref[i]
            out = []
            for g in range(ng):
                dk, dv = carry[g]
                st = lax.dot_general(kjs[g], qi, _NT, preferred_element_type=F32)
                if masked:
                    st = jnp.where(_diag_mask(t, True)[rows[g]], st, NEG)
                pt = jnp.exp(st - lse_i)
                dv2 = dv + jnp.dot(pt.astype(BF), doi, preferred_element_type=F32)
                dpt = lax.dot_general(vjs[g], doi, _NT, preferred_element_type=F32)
                dst = (pt * (dpt - dl_i)).astype(BF)
                out.append((dk + jnp.dot(dst, qi, preferred_element_type=F32), dv2))
            return tuple(out)

        init = tuple((jnp.zeros((tg, 2 * HEAD_PAD), F32), jnp.zeros((tg, V_DIM), F32)) for _ in range(ng))
        carry = step(j, init, True)
        carry = _tile_loop(j + 1, nq, lambda i, c: step(i, c, False), carry)
        for g in range(ng):
            dk, dv = carry[g]
            dkv_ref[0, rows[g]] = dk[:, :HEAD_PAD].astype(BF)
            dkv_ref[1, rows[g]] = dv.astype(BF)
            dkr_ref[rows[g], :] = dk[:, HEAD_PAD:]

    stat = _spec((None, nq, 1, t), lambda h, j: (h, 0, 0, 0))
    return pl.pallas_call(
        body, name=name, grid=(H, nq),
        in_specs=[_spec((2, S, HEAD_PAD), lambda h, j: (0, 0, h)), _spec((2, t, HEAD_PAD), lambda h, j: (0, j, h)),
                  _spec((t, HEAD_PAD), lambda h, j: (j, 0)), _spec((S, HEAD_PAD), lambda h, j: (0, h)), stat, stat],
        out_specs=[_spec((2, t, HEAD_PAD), lambda h, j: (0, j, h)), _spec((None, t, HEAD_PAD), lambda h, j: (h, j, 0))],
        out_shape=[_sd((2, S, HP), BF), _sd((H, S, HEAD_PAD), F32)],
        scratch_shapes=[pltpu.VMEM((S, 2 * HEAD_PAD), BF)],
        compiler_params=_params(("parallel", "arbitrary"), 8 * S * HEAD_PAD * 4 + 32 * t * t * 4))(
            q_all, kv_all, kr, do, lse_row, delta_row)


def _place():
    x, y, c = lax.axis_index("x"), lax.axis_index("y"), lax.axis_index("c")
    return x, y, c


def _all_gather(blocks):
    n = len(blocks)

    def body(*refs):
        ins, outs = refs[:n], refs[n:2 * n]
        send_sems, recv_sems, local_sems = refs[2 * n:]
        x, y, c = _place()
        me = 4 * x + 2 * y + c
        sibling = (x, y, 1 - c)
        chips = [(1 - x, y), (x, 1 - y), (1 - x, 1 - y)]

        def slab(a, px, py, pc):
            return outs[a].at[4 * px + 2 * py + pc]

        def copy(a, k, src, dst, to):
            return pltpu.make_async_remote_copy(src_ref=src, dst_ref=dst, send_sem=send_sems.at[a, k],
                                                recv_sem=recv_sems.at[a, k], device_id=to, device_id_type=MESH)

        local = [pltpu.make_async_copy(ins[a], outs[a].at[me], local_sems.at[a]) for a in range(n)]
        for cp in local:
            cp.start()
        sends = []
        for a in range(n):
            mine = slab(a, x, y, c)
            sends.append(copy(a, 0, ins[a], mine, sibling))
            for j, chip in enumerate(chips):
                sends.append(copy(a, 1 + j, ins[a], mine, (*chip, c)))
        for cp in sends:
            cp.start()
        for j, chip in enumerate(chips):
            for a in range(n):
                got = slab(a, *chip, c)
                copy(a, 1 + j, got, got, (x, y, c)).wait_recv()
                fwd = copy(a, 4 + j, got, got, sibling)
                fwd.start()
                sends.append(fwd)
        for a in range(n):
            got = slab(a, x, y, 1 - c)
            copy(a, 0, got, got, (x, y, c)).wait_recv()
            for j, chip in enumerate(chips):
                got = slab(a, *chip, 1 - c)
                copy(a, 4 + j, got, got, (x, y, c)).wait_recv()
        for cp in sends:
            cp.wait_send()
        for cp in local:
            cp.wait()

    return pl.pallas_call(
        body, name="weights_all_gather", in_specs=[_ANY] * n, out_specs=[_ANY] * n,
        out_shape=[_sd((N_DEV,) + b.shape, b.dtype) for b in blocks],
        scratch_shapes=[pltpu.SemaphoreType.DMA((n, 7)), pltpu.SemaphoreType.DMA((n, 7)), pltpu.SemaphoreType.DMA((n,))],
    )(*blocks)


_HBM = pl.BlockSpec(memory_space=pltpu.HBM)
_SEM = pl.BlockSpec(memory_space=pltpu.SEMAPHORE)
_EFFECT = pltpu.SideEffectType.DATAFLOW_SIDE_EFFECTING


def _peers():
    x, y, c = _place()
    out = []
    for m in range(1, N_DEV):
        px, py, pc = x ^ (m >> 2), y ^ ((m >> 1) & 1), c ^ (m & 1)
        out.append((m, (px, py, pc), 4 * px + 2 * py + pc))
    return 4 * x + 2 * y + c, out


def _exchange_copies(src, land, send_sem, recv_sem, gather):
    me, peers = _peers()
    cps = []
    for a in range(len(src)):
        for m, pos, idx in peers:
            s_ref, d_ref = (src[a], land[a].at[me]) if gather else (src[a].at[idx], land[a].at[m - 1])
            k = a * (N_DEV - 1) + m - 1
            cps.append(pltpu.make_async_remote_copy(src_ref=s_ref, dst_ref=d_ref, send_sem=send_sem.at[k],
                                                    recv_sem=recv_sem.at[k], device_id=pos, device_id_type=MESH))
    return cps


def _xstart(name, srcs, gather, after=()):
    n = len(srcs)
    if gather:
        land_shapes = [(N_DEV,) + s.shape for s in srcs]
    else:
        land_shapes = [(N_DEV - 1,) + s.shape[1:] for s in srcs]

    def body(*refs):
        src, land = refs[:n], refs[n:2 * n]
        send_sem, recv_sem = refs[2 * n + len(after)], refs[2 * n + len(after) + 1]
        token = refs[-1]
        for cp in _exchange_copies(src, land, send_sem, recv_sem, gather):
            cp.start()
        token[...] = jnp.zeros_like(token)

    sem = pltpu.SemaphoreType.DMA((n * (N_DEV - 1),))
    out_shape = ([sem, sem] + [pltpu.HBM(s.shape, s.dtype) for s in srcs]
                 + [pltpu.HBM(sh, s.dtype) for sh, s in zip(land_shapes, srcs)] + [_sd((8, LANES), F32)])
    args = [pltpu.with_memory_space_constraint(s, pltpu.HBM) for s in srcs]
    args += [pltpu.with_memory_space_constraint(lax.empty(sh, s.dtype), pltpu.HBM) for sh, s in zip(land_shapes, srcs)]
    res = pl.pallas_call(
        body, name=name, out_shape=out_shape, in_specs=[_HBM] * (2 * n) + [_ANY] * len(after),
        out_specs=[_SEM, _SEM] + [_HBM] * (2 * n) + [pl.BlockSpec(memory_space=pltpu.VMEM)],
        input_output_aliases={i: 2 + i for i in range(2 * n)},
        compiler_params=pltpu.CompilerParams(has_side_effects=_EFFECT))(*args, *after)
    return dict(send=res[0], recv=res[1], srcs=list(res[2:2 + n]), lands=list(res[2 + n:2 + 2 * n]), token=res[-1])


def _xwait(name, st, after, gather):
    n = len(st["srcs"])

    def body(*refs):
        src, land = refs[:n], refs[n:2 * n]
        send_sem, recv_sem = refs[2 * n], refs[2 * n + 1]
        for cp in _exchange_copies(src, land, send_sem, recv_sem, gather):
            cp.wait_send()
            cp.wait_recv()

    arrays = st["srcs"] + st["lands"]
    res = pl.pallas_call(
        body, name=name, out_shape=[pltpu.HBM(a.shape, a.dtype) for a in arrays],
        in_specs=[_HBM] * (2 * n) + [_SEM, _SEM, _ANY], out_specs=[_HBM] * (2 * n),
        input_output_aliases={i: i for i in range(2 * n)},
        compiler_params=pltpu.CompilerParams(has_side_effects=_EFFECT))(*arrays, st["send"], st["recv"], after)
    return list(res[:n]), list(res[n:])


def _put_own(name, land, block, me_arr):
    R, C = block.shape
    tr = _pick(R, (512, 256, 128, 64, 32, 16))

    def body(me_ref, b_ref, land_ref, o_ref):
        o_ref[...] = b_ref[...]

    return pl.pallas_call(
        body, name=name, out_shape=_sd(land.shape, land.dtype),
        grid_spec=pltpu.PrefetchScalarGridSpec(
            num_scalar_prefetch=1, grid=(R // tr,),
            in_specs=[_spec((tr, C), lambda i, me_ref: (i, 0)), _ANY],
            out_specs=_spec((None, tr, C), lambda i, me_ref: (me_ref[0], i, 0))),
        input_output_aliases={2: 0},
        compiler_params=_params(("parallel",), 8 * tr * C * 2))(me_arr, block, land)


def _all_reduce_small(name, part):
    R = part.shape[0]

    def body(p_ref, out_ref, gath_ref, send_sems, recv_sems):
        x, y, c = _place()
        me = 4 * x + 2 * y + c
        gath_ref[me] = p_ref[...]
        cps = []
        for m in range(1, N_DEV):
            to = (x ^ (m >> 2), y ^ ((m >> 1) & 1), c ^ (m & 1))
            cps.append(pltpu.make_async_remote_copy(
                src_ref=p_ref, dst_ref=gath_ref.at[me], send_sem=send_sems.at[m - 1], recv_sem=recv_sems.at[m - 1],
                device_id=to, device_id_type=MESH))
        for cp in cps:
            cp.start()
        for m in range(1, N_DEV):
            frm = 4 * (x ^ (m >> 2)) + 2 * (y ^ ((m >> 1) & 1)) + (c ^ (m & 1))
            pltpu.make_async_remote_copy(
                src_ref=p_ref, dst_ref=gath_ref.at[frm], send_sem=send_sems.at[m - 1], recv_sem=recv_sems.at[m - 1],
                device_id=(x, y, c), device_id_type=MESH).wait_recv()
        for cp in cps:
            cp.wait_send()
        tot = gath_ref[0]
        for k in range(1, N_DEV):
            tot = tot + gath_ref[k]
        out_ref[...] = tot

    vm = pl.BlockSpec(memory_space=pltpu.VMEM)
    return pl.pallas_call(
        body, name=name, in_specs=[vm], out_specs=vm, out_shape=_sd((R, LANES), F32),
        scratch_shapes=[pltpu.VMEM((N_DEV, R, LANES), F32), pltpu.SemaphoreType.DMA((N_DEV - 1,)),
                        pltpu.SemaphoreType.DMA((N_DEV - 1,))],
        compiler_params=pltpu.CompilerParams(vmem_limit_bytes=VMEM_FLOOR),
    )(part)


def _adam_math(w, g, m, v):
    m2 = ADAM_B1 * m + (1.0 - ADAM_B1) * g
    v2 = ADAM_B2 * v + (1.0 - ADAM_B2) * (g * g)
    m_hat = m2 / (1.0 - ADAM_B1 ** ADAM_STEP)
    v_hat = v2 / (1.0 - ADAM_B2 ** ADAM_STEP)
    delta = -ADAM_LR * (m_hat / (jnp.sqrt(v_hat) + ADAM_EPS) + ADAM_WD * w)
    return delta, m2, v2


def _adamw_sharded(name, lands, fulls, me_arr, w, m, v):
    n_l = len(lands)
    R, C = lands[0].shape[1], lands[0].shape[2]
    tr = _pick(R, (128, 64, 32, 16, 8))
    nr = R // tr

    def body(me_ref, *refs):
        land_refs, own_refs = refs[:n_l], refs[n_l:2 * n_l]
        w_ref, m_ref, v_ref, g_ref, d_ref, m2_ref, v2_ref = refs[2 * n_l:]
        layer = pl.program_id(0)
        for ll in range(n_l):
            @pl.when(layer == ll)
            def _(ll=ll):
                g = own_refs[ll][...].astype(F32)
                for j in range(N_DEV - 1):
                    g = g + land_refs[ll][j].astype(F32)
                delta, m2, v2 = _adam_math(w_ref[...], g, m_ref[...], v_ref[...])
                g_ref[...] = g
                d_ref[...] = delta
                m2_ref[...] = m2
                v2_ref[...] = v2

    def row_of(ll):
        return lambda l, i, me_ref: jnp.where(l == ll, i, 0)

    in_specs = [_spec((N_DEV - 1, tr, C), lambda l, i, me_ref, f=row_of(ll): (0, f(l, i, me_ref), 0)) for ll in range(n_l)]
    in_specs += [_spec((None, tr, C), lambda l, i, me_ref, f=row_of(ll): (me_ref[0], f(l, i, me_ref), 0))
                 for ll in range(n_l)]
    blk = _spec((tr, C), lambda l, i, me_ref: (l * nr + i, 0))
    return pl.pallas_call(
        body, name=name, out_shape=[_sd(w.shape, F32)] * 4,
        grid_spec=pltpu.PrefetchScalarGridSpec(num_scalar_prefetch=1, grid=(n_l, nr), in_specs=in_specs + [blk] * 3,
                                               out_specs=[blk] * 4),
        compiler_params=_params(("parallel", "parallel"), (4 * n_l * N_DEV + 40) * tr * C * 4))(
            me_arr, *lands, *fulls, w, m, v)


def _adamw_packed(name, w, g, m, v):
    R = w.shape[0]

    def body(w_ref, g_ref, m_ref, v_ref, d_ref, m2_ref, v2_ref):
        delta, m2, v2 = _adam_math(w_ref[...], g_ref[...], m_ref[...], v_ref[...])
        d_ref[...] = delta
        m2_ref[...] = m2
        v2_ref[...] = v2

    vm = pl.BlockSpec(memory_space=pltpu.VMEM)
    return pl.pallas_call(
        body, name=name, in_specs=[vm] * 4, out_specs=[vm] * 3, out_shape=[_sd((R, LANES), F32)] * 3,
        compiler_params=pltpu.CompilerParams(vmem_limit_bytes=VMEM_FLOOR))(w, g, m, v)


def _pack(arrays):
    flat = jnp.concatenate([a.reshape(-1).astype(F32) for a in arrays])
    pad = (-flat.shape[0]) % (8 * LANES)
    return jnp.pad(flat, (0, pad)).reshape(-1, LANES)


def _unpack(packed, like):
    flat = packed.reshape(-1)
    out, pos = [], 0
    for a in like:
        n = math.prod(a.shape)
        out.append(flat[pos:pos + n].reshape(a.shape))
        pos += n
    return out


def _ffn_fwd(tag, x, gain, w_in_sm, w_out_of, deps=()):
    h = _rms_fwd(tag + "_norm", x, gain, deps)
    gu, act = _pair_in(tag + "_in", h, w_in_sm, BF, _swiglu)
    x_new = _rows_out(tag + "_out", act, w_out_of(act), x, 0.5)
    return x_new, (x, h, gu, act)


def _ffn_bwd(tag, d, d_bf, saved, gain, w_in_sm, w_out_sm, send_out, send_in):
    x, h, gu, act = saved
    dgu = _rows_dact(tag + "_dact", d_bf, w_out_sm, [gu], [_sd(gu.shape, BF)], _swiglu_bwd_epi)[0]
    g_out = _rows_wgrad(tag + "_wgrad_out", act, d_bf, 0.5)
    token_out = send_out(g_out)
    g_in = _cols_wgrad(tag + "_wgrad_in", h, dgu, deps=(token_out,))
    token = send_in(g_in)
    dh = _cols_dh(tag + "_dh", dgu, w_in_sm, deps=(token,))
    dx, dx_bf, dgain = _rms_bwd(tag + "_dnorm", dh, x, gain, d)
    return dx, dx_bf, dgain


def kernel(x, positions, ln_ffn1, ffn1_w_in, ffn1_w_out, ln_mix, ln_ffn2, ffn2_w_in, ffn2_w_out, sgu_w_in, sgu_v_gain, sgu_v_bias, sgu_w_spatial, sgu_b_spatial, sgu_w_out, mla_w_in, mla_q_norm, mla_w_q_up, mla_kv_norm, mla_w_kv_up, mla_w_out, ln_final, loss_target, m_ln_ffn1, m_ffn1_w_in, m_ffn1_w_out, m_ln_mix, m_ln_ffn2, m_ffn2_w_in, m_ffn2_w_out, m_sgu_w_in, m_sgu_v_gain, m_sgu_v_bias, m_sgu_w_spatial, m_sgu_b_spatial, m_sgu_w_out, m_mla_w_in, m_mla_q_norm, m_mla_w_q_up, m_mla_kv_norm, m_mla_w_kv_up, m_mla_w_out, m_ln_final, v_ln_ffn1, v_ffn1_w_in, v_ffn1_w_out, v_ln_mix, v_ln_ffn2, v_ffn2_w_in, v_ffn2_w_out, v_sgu_w_in, v_sgu_v_gain, v_sgu_v_bias, v_sgu_w_spatial, v_sgu_b_spatial, v_sgu_w_out, v_mla_w_in, v_mla_q_norm, v_mla_w_q_up, v_mla_kv_norm, v_mla_w_kv_up, v_mla_w_out, v_ln_final):
    S, D = x.shape[1], x.shape[2]
    L = ln_ffn1.shape[0]
    H = mla_w_q_up.shape[-1] * N_DEV // (QK_NOPE + QK_ROPE)
    xi, yi, ci = _place()
    me = 4 * xi + 2 * yi + ci
    me_arr = jnp.reshape(me, (1,)).astype(jnp.int32)
    big = dict(ffn1_w_in=ffn1_w_in, ffn1_w_out=ffn1_w_out, ffn2_w_in=ffn2_w_in, ffn2_w_out=ffn2_w_out,
               sgu_w_in=sgu_w_in, sgu_w_out=sgu_w_out, mla_w_in=mla_w_in, mla_w_q_up=mla_w_q_up,
               mla_w_kv_up=mla_w_kv_up, mla_w_out=mla_w_out)
    big_m = dict(ffn1_w_in=m_ffn1_w_in, ffn1_w_out=m_ffn1_w_out, ffn2_w_in=m_ffn2_w_in, ffn2_w_out=m_ffn2_w_out,
                 sgu_w_in=m_sgu_w_in, sgu_w_out=m_sgu_w_out, mla_w_in=m_mla_w_in, mla_w_q_up=m_mla_w_q_up,
                 mla_w_kv_up=m_mla_w_kv_up, mla_w_out=m_mla_w_out)
    big_v = dict(ffn1_w_in=v_ffn1_w_in, ffn1_w_out=v_ffn1_w_out, ffn2_w_in=v_ffn2_w_in, ffn2_w_out=v_ffn2_w_out,
                 sgu_w_in=v_sgu_w_in, sgu_w_out=v_sgu_w_out, mla_w_in=v_mla_w_in, mla_w_q_up=v_mla_w_q_up,
                 mla_w_kv_up=v_mla_w_kv_up, mla_w_out=v_mla_w_out)
    names = list(big)
    mla_names = ["mla_w_in", "mla_w_q_up", "mla_w_kv_up", "mla_w_out"]

    blocks = {(k, l): big[k][l].astype(BF) for k in names for l in range(big[k].shape[0])}
    first = ("ffn1_w_in", 0)
    groups = {}
    for i in range(L):
        if i > 0:
            groups[f"ffn1_in_{i}"] = [("ffn1_w_in", i)]
        groups[f"ffn1_out_{i}"] = [("ffn1_w_out", i)]
        groups[f"mix_{i}"] = [("sgu_w_in", i // 2), ("sgu_w_out", i // 2)] if i % 2 == 0 else [(k, i // 2) for k in mla_names]
        groups[f"ffn2_{i}"] = [("ffn2_w_in", i), ("ffn2_w_out", i)]
    gathered = {first: _all_gather([blocks[first]])[0]}
    started, order_after = {}, (gathered[first],)
    for tag, grp in groups.items():
        started[tag] = _xstart(f"gather_start_{tag}", [blocks[k] for k in grp], True, order_after)
        order_after = (started[tag]["token"],)

    def fetch(tag, after):
        own, lands = _xwait(f"gather_wait_{tag}", started[tag], after, True)
        for k, blk, land in zip(groups[tag], own, lands):
            gathered[k] = _put_own(f"gather_own_{k[0]}_{k[1]}", land, blk, me_arr)

    def w_out_of(name, tag):
        def get(after):
            fetch(tag, after)
            return gathered[name]
        return get

    norm_rows = jnp.zeros((N_DEV, LANES), F32)
    mine = jnp.concatenate([mla_q_norm[0], mla_kv_norm[0]])
    norm_rows = lax.dynamic_update_slice(norm_rows, mine[None, :], (me, 0))
    norm_all = _all_reduce_small("norm_gains_gather", norm_rows)
    nq_sh = mla_q_norm.shape[1]
    q_gain = norm_all[:, :nq_sh].reshape(1, Q_LORA)
    kv_gain = norm_all[:, nq_sh:2 * nq_sh].reshape(1, KV_LORA)
    cos, sa, sb = _rope_tables(positions[0])
    scale = float((QK_NOPE + QK_ROPE) ** -0.5)

    xs = x[0]
    w_sp = sgu_w_spatial[0]
    b_sp = sgu_b_spatial[0][:, :, None]
    saved = []
    mla_w = {}
    deps = order_after
    for i in range(L):
        if i > 0:
            fetch(f"ffn1_in_{i}", xs)
        xs, s1 = _ffn_fwd(f"l{i}_ffn1", xs, ln_ffn1[i:i + 1], gathered[("ffn1_w_in", i)],
                          w_out_of(("ffn1_w_out", i), f"ffn1_out_{i}"), deps)
        deps = ()
        fetch(f"mix_{i}", xs)
        x_mix = xs
        h = _rms_fwd(f"l{i}_mix_norm", xs, ln_mix[i:i + 1])
        j = i // 2
        if i % 2 == 0:
            puv = _pair_in(f"l{i}_sgu_in", h, gathered[("sgu_w_in", j)], F32, None)[0]
            gated = _sgu_mid_fwd(f"l{i}_sgu_mid", puv, sgu_v_gain, sgu_v_bias, w_sp, b_sp)
            xs = _rows_out(f"l{i}_sgu_out", gated, gathered[("sgu_w_out", j)], xs, 1.0)
            sm = (x_mix, h, puv, gated)
        else:
            w_in_nat = gathered[("mla_w_in", j)].reshape(D, Q_LORA + KV_LORA + QK_ROPE)
            w_in_pad = jnp.pad(w_in_nat, ((0, 0), (0, HEAD_PAD - QK_ROPE)))
            wq_nat = jnp.transpose(gathered[("mla_w_q_up", j)], (1, 0, 2)).reshape(Q_LORA, H, QK_NOPE + QK_ROPE)
            wq_t = jnp.stack([wq_nat[:, :, :QK_NOPE].reshape(Q_LORA, H * HEAD_PAD),
                              jnp.pad(wq_nat[:, :, QK_NOPE:], ((0, 0), (0, 0), (0, HEAD_PAD - QK_ROPE))).reshape(
                                  Q_LORA, H * HEAD_PAD)])
            wkv_nat = jnp.transpose(gathered[("mla_w_kv_up", j)], (1, 0, 2)).reshape(KV_LORA, H, QK_NOPE + V_DIM)
            wkv_t = jnp.stack([wkv_nat[:, :, :QK_NOPE].reshape(KV_LORA, H * HEAD_PAD),
                               wkv_nat[:, :, QK_NOPE:].reshape(KV_LORA, H * HEAD_PAD)])
            w_o_nat = gathered[("mla_w_out", j)].reshape(H * V_DIM, D)
            mla_w[i] = (w_in_pad, wq_t, wkv_t, w_o_nat)
            proj = _mm2(f"l{i}_mla_in", h, w_in_pad, False, False, F32, tn_cands=(384, 128))
            lat, kr = _mla_mid_fwd(f"l{i}_mla_mid", proj, q_gain, kv_gain, cos, sa, sb)

            def q_epi(accs, ex, orefs, ids):
                orefs[0][...] = (accs[0] * scale).astype(BF)

            def qr_epi(accs, ex, orefs, ids):
                for hh in range(accs[0].shape[1] // HEAD_PAD):
                    sl = slice(hh * HEAD_PAD, (hh + 1) * HEAD_PAD)
                    orefs[0][:, sl] = (_rope(accs[0][:, sl], *ex) * scale).astype(BF)

            q_all = _mm_halves(f"l{i}_mla_q", (lat, 0), wq_t, BF, lambda half: (q_epi, qr_epi)[half], extras=(cos, sa, sb))
            kv_all = _mm_halves(f"l{i}_mla_kv", (lat, 1), wkv_t, BF)
            o, lse = _attn_fwd(f"l{i}_attn", q_all, kv_all, kr)
            xs = _mm2(f"l{i}_mla_out", o, w_o_nat, False, False, F32, res=xs)
            sm = (x_mix, h, proj, lat, kr, q_all, kv_all, o, lse)
        fetch(f"ffn2_{i}", xs)
        xs, s2 = _ffn_fwd(f"l{i}_ffn2", xs, ln_ffn2[i:i + 1], gathered[("ffn2_w_in", i)],
                          lambda after, i=i: gathered[("ffn2_w_out", i)])
        saved.append((s1, sm, s2))

    loss_row, d, d_bf, g_ln_final = _final_loss("final_loss", xs, ln_final[None, :], loss_target[0])
    loss = lax.psum(loss_row[0, 0], ("x", "y", "c"))

    sent = []

    def send(tag, keys, grads):
        st = _xstart(f"scatter_start_{tag}", grads, False)
        sent.append((tag, keys, st))
        return st["token"]

    g_ln1, g_ln2, g_lnm = [None] * L, [None] * L, [None] * L
    small_g = {}
    for i in reversed(range(L)):
        s1, sm, s2 = saved[i]
        d, d_bf, g_ln2[i] = _ffn_bwd(
            f"l{i}_ffn2", d, d_bf, s2, ln_ffn2[i:i + 1], gathered[("ffn2_w_in", i)], gathered[("ffn2_w_out", i)],
            lambda g, i=i: send(f"l{i}_ffn2_out", [("ffn2_w_out", i)], [g]),
            lambda g, i=i: send(f"l{i}_ffn2_in", [("ffn2_w_in", i)], [g]))
        j = i // 2
        if i % 2 == 0:
            x_mix, h, puv, gated = sm
            dgated = _rows_dact(f"l{i}_sgu_dgated", d_bf, gathered[("sgu_w_out", j)], [], [_sd(gated.shape, BF)], _store())[0]
            g_so = _rows_wgrad(f"l{i}_sgu_wgrad_out", gated, d_bf, None)
            token_out = send(f"l{i}_sgu_out", [("sgu_w_out", j)], [g_so])
            dpuv, dgain, dbias, dwsp, dbsp = _sgu_mid_bwd(f"l{i}_sgu_mid_bwd", puv, dgated, sgu_v_gain, sgu_v_bias,
                                                          w_sp, b_sp)
            small_g.update(sgu_v_gain=dgain, sgu_v_bias=dbias, sgu_w_spatial=dwsp[None], sgu_b_spatial=dbsp[None, :, :, 0])
            g_si = _cols_wgrad(f"l{i}_sgu_wgrad_in", h, dpuv, deps=(token_out,))
            token = send(f"l{i}_sgu_in", [("sgu_w_in", j)], [g_si])
            dh = _cols_dh(f"l{i}_sgu_dh", dpuv, gathered[("sgu_w_in", j)], deps=(token,))
        else:
            x_mix, h, proj, lat, kr, q_all, kv_all, o, lse = sm
            w_in_pad, wq_t, wkv_t, w_o_nat = mla_w[i]
            t = _attn_tile(S)
            do = _mm2(f"l{i}_mla_do", d_bf, w_o_nat, False, True, F32)
            g_wo = _mm2(f"l{i}_mla_wgrad_out", o, d_bf, True, False, BF)
            dq_all, delta = _attn_dq(f"l{i}_attn_dq", q_all, kv_all, kr, do, o, lse, cos, sa, sb, scale)
            lse_row = lse[:, :, 0].reshape(H, S // t, 1, t)
            delta_row = delta[:, :, 0].reshape(H, S // t, 1, t)
            dkv_all, dkr_heads = _attn_dkv(f"l{i}_attn_dkv", q_all, kv_all, kr, do, lse_row, delta_row)
            dqn = _mm2(f"l{i}_mla_dqn", (dq_all, 0), (wq_t, 0), False, True, F32)
            dqn = _mm2(f"l{i}_mla_dqn2", (dq_all, 1), (wq_t, 1), False, True, F32, res=dqn)
            dkvn = _mm2(f"l{i}_mla_dkvn", (dkv_all, 0), (wkv_t, 0), False, True, F32)
            dkvn = _mm2(f"l{i}_mla_dkvn2", (dkv_all, 1), (wkv_t, 1), False, True, F32, res=dkvn)
            g_wq = [_mm2(f"l{i}_mla_wgrad_q{t2}", (lat, 0), (dq_all, t2), True, False, BF) for t2 in range(2)]
            g_wkv = [_mm2(f"l{i}_mla_wgrad_kv{t2}", (lat, 1), (dkv_all, t2), True, False, BF) for t2 in range(2)]
            dproj, g_qn, g_kvn = _mla_mid_bwd(f"l{i}_mla_mid_bwd", proj, dqn, dkvn, dkr_heads, q_gain, kv_gain, cos, sa, sb)
            g_win = _mm2(f"l{i}_mla_wgrad_in", h, dproj, True, False, BF, tn_cands=(384, 128))
            n_in = Q_LORA + KV_LORA + QK_ROPE
            gq_nat = jnp.concatenate([g_wq[0].reshape(Q_LORA, H, HEAD_PAD),
                                      g_wq[1].reshape(Q_LORA, H, HEAD_PAD)[:, :, :QK_ROPE]], axis=2)
            gkv_nat = jnp.concatenate([g_wkv[0].reshape(KV_LORA, H, HEAD_PAD), g_wkv[1].reshape(KV_LORA, H, HEAD_PAD)], axis=2)
            token = send(f"l{i}_mla", [(k, j) for k in mla_names],
                         [g_win[:, :n_in].reshape(N_DEV, D // N_DEV, n_in),
                          jnp.transpose(gq_nat.reshape(Q_LORA, N_DEV, -1), (1, 0, 2)),
                          jnp.transpose(gkv_nat.reshape(KV_LORA, N_DEV, -1), (1, 0, 2)),
                          g_wo.reshape(N_DEV, H * V_DIM // N_DEV, D)])
            small_g.update(mla_q_norm=g_qn, mla_kv_norm=g_kvn)
            dh = _mm2(f"l{i}_mla_dh", dproj, w_in_pad, False, True, F32, tn_cands=(512, 256, 128), deps=(token,))
        d, d_bf, g_lnm[i] = _rms_bwd(f"l{i}_mix_dnorm", dh, x_mix, ln_mix[i:i + 1], d)
        d, d_bf, g_ln1[i] = _ffn_bwd(
            f"l{i}_ffn1", d, d_bf, s1, ln_ffn1[i:i + 1], gathered[("ffn1_w_in", i)], gathered[("ffn1_w_out", i)],
            lambda g, i=i: send(f"l{i}_ffn1_out", [("ffn1_w_out", i)], [g]),
            lambda g, i=i: send(f"l{i}_ffn1_in", [("ffn1_w_in", i)], [g]))
    grad_x = d[None]

    landed, partial = {}, {}
    for tag, keys, st in sent:
        fulls, lands = _xwait(f"scatter_wait_{tag}", st, d, False)
        for k, full, land in zip(keys, fulls, lands):
            partial[k], landed[k] = full, land
    big_out = {}
    for k in names:
        w = big[k]
        rc = (math.prod(w.shape[1:-1]), w.shape[-1])
        flat = (w.shape[0] * rc[0], rc[1])
        lands = [landed[(k, l)].reshape((N_DEV - 1,) + rc) for l in range(w.shape[0])]
        fulls = [partial[(k, l)].reshape((N_DEV,) + rc) for l in range(w.shape[0])]
        res = _adamw_sharded(f"adamw_{k}", lands, fulls, me_arr, w.reshape(flat), big_m[k].reshape(flat),
                             big_v[k].reshape(flat))
        big_out[k] = [r.reshape(w.shape) for r in res]

    small_g.update(ln_ffn1=jnp.concatenate(g_ln1), ln_mix=jnp.concatenate(g_lnm), ln_ffn2=jnp.concatenate(g_ln2),
                   ln_final=g_ln_final[0])
    small_names = ["ln_ffn1", "ln_mix", "ln_ffn2", "sgu_v_gain", "sgu_v_bias", "sgu_w_spatial", "sgu_b_spatial",
                   "ln_final", "mla_q_norm", "mla_kv_norm"]
    summed = _unpack(_all_reduce_small("small_grads_all_reduce", _pack([small_g[k] for k in small_names])),
                     [small_g[k] for k in small_names])
    small_grad = dict(zip(small_names, summed))
    for k in ("mla_q_norm", "mla_kv_norm"):
        small_grad[k] = lax.dynamic_slice(small_grad[k], (0, me * nq_sh), (1, nq_sh))
    small_w = dict(ln_ffn1=ln_ffn1, ln_mix=ln_mix, ln_ffn2=ln_ffn2, sgu_v_gain=sgu_v_gain, sgu_v_bias=sgu_v_bias,
                   sgu_w_spatial=sgu_w_spatial, sgu_b_spatial=sgu_b_spatial, ln_final=ln_final, mla_q_norm=mla_q_norm,
                   mla_kv_norm=mla_kv_norm)
    small_m = dict(ln_ffn1=m_ln_ffn1, ln_mix=m_ln_mix, ln_ffn2=m_ln_ffn2, sgu_v_gain=m_sgu_v_gain, sgu_v_bias=m_sgu_v_bias,
                   sgu_w_spatial=m_sgu_w_spatial, sgu_b_spatial=m_sgu_b_spatial, ln_final=m_ln_final,
                   mla_q_norm=m_mla_q_norm, mla_kv_norm=m_mla_kv_norm)
    small_v = dict(ln_ffn1=v_ln_ffn1, ln_mix=v_ln_mix, ln_ffn2=v_ln_ffn2, sgu_v_gain=v_sgu_v_gain, sgu_v_bias=v_sgu_v_bias,
                   sgu_w_spatial=v_sgu_w_spatial, sgu_b_spatial=v_sgu_b_spatial, ln_final=v_ln_final,
                   mla_q_norm=v_mla_q_norm, mla_kv_norm=v_mla_kv_norm)
    like = [small_w[k] for k in small_names]
    packed = _adamw_packed("adamw_small", _pack(like), _pack([small_grad[k] for k in small_names]),
                           _pack([small_m[k] for k in small_names]), _pack([small_v[k] for k in small_names]))
    small_out = {}
    unpacked = [_unpack(p, like) for p in packed]
    for idx, k in enumerate(small_names):
        small_out[k] = [small_grad[k].reshape(small_w[k].shape)] + [u[idx] for u in unpacked]

    order = ["ln_ffn1", "ffn1_w_in", "ffn1_w_out", "ln_mix", "ln_ffn2", "ffn2_w_in", "ffn2_w_out", "sgu_w_in",
             "sgu_v_gain", "sgu_v_bias", "sgu_w_spatial", "sgu_b_spatial", "sgu_w_out", "mla_w_in", "mla_q_norm",
             "mla_w_q_up", "mla_kv_norm", "mla_w_kv_up", "mla_w_out", "ln_final"]
    res = {k: (big_out[k] if k in big_out else small_out[k]) for k in order}
    outs = [loss, grad_x]
    for part in range(4):
        outs.extend(res[k][part] for k in order)
    return tuple(outs)
```

```python
import functools
import math

import jax
import jax.numpy as jnp
from jax import lax
from jax.experimental import pallas as pl
from jax.experimental.pallas import tpu as pltpu

F32 = jnp.float32
BF = jnp.bfloat16
MESH = pl.DeviceIdType.MESH

N_DEV = 8
EPS = 1e-6
CHUNK = 64
SGU_BLOCK = 128
SGU_GROUPS = 8
Q_LORA = 512
KV_LORA = 512
QK_NOPE = 128
QK_ROPE = 64
V_DIM = 128
ROPE_THETA = 10000.0
HEAD_PAD = 128
LANES = 128
ADAM_LR = 0.001
ADAM_B1 = 0.9
ADAM_B2 = 0.999
ADAM_EPS = 1e-08
ADAM_WD = 0.01
ADAM_STEP = 10
V7X_VMEM_BYTES = 64 * 1024 * 1024
VMEM_CAP = V7X_VMEM_BYTES - 6 * 1024 * 1024
VMEM_FLOOR = 32 * 1024 * 1024
NEG = -1e30
ATTN_GROUPS = 1
ATTN_UNROLLS = (8, 4, 2)


def _pick(n, cands):
    for c in cands:
        if n % c == 0:
            return c
    return n


def _nbytes(shape, dtype):
    return math.prod(int(s) for s in shape if s is not None) * jnp.dtype(dtype).itemsize


def _params(sem, block_bytes):
    limit = int(min(VMEM_CAP, max(VMEM_FLOOR, block_bytes)))
    return pltpu.CompilerParams(dimension_semantics=sem, vmem_limit_bytes=limit)


def _spec(shape, fn):
    return pl.BlockSpec(shape, fn)


_ANY = pl.BlockSpec(memory_space=pl.ANY)


def _mm(name, grid, ops, pairs, extras, outs, epilogue, acc_shapes, deps=()):
    nk = grid[2]
    n_ops, n_ex, n_out = len(ops), len(extras), len(outs)

    def load(refs, idx):
        loader = ops[idx][2] if len(ops[idx]) > 2 else None
        return (refs[idx][...] if loader is None else loader(refs[idx])).astype(BF)

    def prod(refs, p):
        ia, ib, ta, tb, _ = p
        a = load(refs, ia)
        b = load(refs, ib)
        dims = (((0 if ta else 1,), (1 if tb else 0,)), ((), ()))
        return lax.dot_general(a, b, dims, preferred_element_type=F32)

    def body(*refs):
        op_refs = refs[:n_ops]
        ex_refs = refs[n_ops:n_ops + n_ex]
        n_in = n_ops + n_ex + len(deps)
        out_refs = refs[n_in:n_in + n_out]
        acc_refs = refs[n_in + n_out:]
        ids = (pl.program_id(0), pl.program_id(1))

        def finish(vals):
            epilogue(vals, [e[...] for e in ex_refs], out_refs, ids)

        if nk == 1:
            vals = [None] * len(acc_shapes)
            for p in pairs:
                r = prod(op_refs, p)
                vals[p[4]] = r if vals[p[4]] is None else vals[p[4]] + r
            finish(vals)
        else:
            k = pl.program_id(2)

            def products():
                vals = [None] * len(acc_shapes)
                for p in pairs:
                    r = prod(op_refs, p)
                    vals[p[4]] = r if vals[p[4]] is None else vals[p[4]] + r
                return vals

            @pl.when(k == 0)
            def _():
                for a, v in zip(acc_refs, products()):
                    a[...] = v

            @pl.when((k > 0) & (k < nk - 1))
            def _():
                for a, v in zip(acc_refs, products()):
                    a[...] += v

            @pl.when(k == nk - 1)
            def _():
                finish([a[...] + v for a, v in zip(acc_refs, products())])

    in_arrays = [o[0] for o in ops] + [e[0] for e in extras]
    in_specs = [o[1] for o in ops] + [e[1] for e in extras]
    in_arrays += list(deps)
    in_specs += [_ANY] * len(deps)
    blk = 0
    for entry in ops + extras:
        blk += 2 * _nbytes(entry[1].block_shape, entry[0].dtype)
    for sd, sp in outs:
        blk += 2 * _nbytes(sp.block_shape, sd.dtype)
    acc_b = sum(_nbytes(s, F32) for s in acc_shapes)
    blk += 6 * acc_b
    scratch = [pltpu.VMEM(s, F32) for s in acc_shapes] if nk > 1 else []
    res = pl.pallas_call(
        body, name=name, grid=grid, in_specs=in_specs,
        out_specs=[o[1] for o in outs], out_shape=[o[0] for o in outs],
        scratch_shapes=scratch,
        compiler_params=_params(("parallel", "parallel", "arbitrary"), blk))(*in_arrays)
    return res


def _store(scale=None):
    def epi(accs, ex, outs, ids):
        v = accs[0]
        if scale is not None:
            v = v * scale
        outs[0][...] = v.astype(outs[0].dtype)
    return epi


def _store_residual(scale):
    def epi(accs, ex, outs, ids):
        outs[0][...] = ex[0] + scale * accs[0]
    return epi


def _sd(shape, dtype):
    return jax.ShapeDtypeStruct(tuple(shape), dtype)


def _pair_in(name, h, w_sm, out_dtype, act):
    S, D = h.shape
    c = w_sm.shape[-1]
    tm = _pick(S, (512, 256, 128))
    half = N_DEV // 2
    ops = [(h, _spec((tm, D), lambda j, i, k: (i, 0))),
           (w_sm, _spec((None, D, c), lambda j, i, k: (j, 0, 0))),
           (w_sm, _spec((None, D, c), lambda j, i, k: (j + half, 0, 0)))]
    outs = [(_sd((2, S, half * c), out_dtype), _spec((2, tm, c), lambda j, i, k: (0, i, j)))]
    if act is not None:
        outs.append((_sd((S, half * c), BF), _spec((tm, c), lambda j, i, k: (i, j))))

    def epi(accs, ex, orefs, ids):
        if act is None:
            orefs[0][0] = accs[0].astype(out_dtype)
            orefs[0][1] = accs[1].astype(out_dtype)
        else:
            keep0, keep1, out = act(accs[0], accs[1])
            orefs[0][0] = keep0.astype(out_dtype)
            orefs[0][1] = keep1.astype(out_dtype)
            orefs[1][...] = out.astype(BF)

    return _mm(name, (half, S // tm, 1), ops, [(0, 1, False, False, 0), (0, 2, False, False, 1)], [], outs, epi,
               [(tm, c), (tm, c)])


def _two_slabs(ref):
    return jnp.concatenate([ref[0], ref[1]], axis=0)


def _rows_out(name, a, w_sm, res, scale):
    S = a.shape[0]
    r, D = w_sm.shape[-2], w_sm.shape[-1]
    tm = _pick(S, (1024, 512, 256, 128))
    tn = _pick(D, (1024, 512, 256, 128))
    ops = [(a, _spec((tm, 2 * r), lambda i, j, k: (i, k))),
           (w_sm, _spec((2, r, tn), lambda i, j, k: (k, 0, j)), _two_slabs)]
    extras = [(res, _spec((tm, tn), lambda i, j, k: (i, j)))]
    outs = [(_sd((S, D), F32), _spec((tm, tn), lambda i, j, k: (i, j)))]
    return _mm(name, (S // tm, D // tn, N_DEV // 2), ops, [(0, 1, False, False, 0)], extras, outs,
               _store_residual(scale), [(tm, tn)])[0]


def _rows_dact(name, d_bf, w_sm, extras_arrays, out_shapes, epi):
    S, D = d_bf.shape
    r = w_sm.shape[-2]
    tm = _pick(S, (1024, 512, 256, 128))
    ops = [(d_bf, _spec((tm, D), lambda j, i, k: (i, 0))),
           (w_sm, _spec((2, r, D), lambda j, i, k: (j, 0, 0)), _two_slabs)]
    extras = []
    for arr in extras_arrays:
        if arr.ndim == 3:
            extras.append((arr, _spec((arr.shape[0], tm, 2 * r), lambda j, i, k: (0, i, j))))
        else:
            extras.append((arr, _spec((tm, 2 * r), lambda j, i, k: (i, j))))
    outs = []
    for sd in out_shapes:
        if len(sd.shape) == 3:
            outs.append((sd, _spec((sd.shape[0], tm, 2 * r), lambda j, i, k: (0, i, j))))
        else:
            outs.append((sd, _spec((tm, 2 * r), lambda j, i, k: (i, j))))
    return _mm(name, (N_DEV // 2, S // tm, 1), ops, [(0, 1, False, True, 0)], extras, outs, epi, [(tm, 2 * r)])


def _rows_wgrad(name, a, d_bf, scale):
    S, D = d_bf.shape
    r = a.shape[1] // N_DEV
    tn = _pick(D, (1024, 512, 256, 128))
    tk = _pick(S, (2048, 1024, 512, 256, 128))
    ops = [(a, _spec((tk, 2 * r), lambda s, j, k: (k, s))),
           (d_bf, _spec((tk, tn), lambda s, j, k: (k, j)))]
    outs = [(_sd((N_DEV, r, D), BF), _spec((2, r, tn), lambda s, j, k: (s, 0, j)))]

    def epi(accs, ex, orefs, ids):
        v = accs[0] if scale is None else accs[0] * scale
        orefs[0][0] = v[:r].astype(BF)
        orefs[0][1] = v[r:].astype(BF)

    return _mm(name, (N_DEV // 2, D // tn, S // tk), ops, [(0, 1, True, False, 0)], [], outs, epi, [(2 * r, tn)])[0]


def _cols_dh(name, dpair, w_sm, deps=()):
    _, S, _ = dpair.shape
    D, c = w_sm.shape[-2], w_sm.shape[-1]
    half = N_DEV // 2
    tm = _pick(S, (1024, 512, 256, 128))
    tn = _pick(D, (1024, 512, 256, 128))
    ops = [(dpair, _spec((None, tm, c), lambda i, j, k: (k // half, i, k % half))),
           (w_sm, _spec((None, tn, c), lambda i, j, k: (k, j, 0)))]
    outs = [(_sd((S, D), F32), _spec((tm, tn), lambda i, j, k: (i, j)))]
    return _mm(name, (S // tm, D // tn, N_DEV), ops, [(0, 1, False, True, 0)], [], outs, _store(), [(tm, tn)],
               deps=deps)[0]


def _cols_wgrad(name, h, dpair, deps=()):
    S, D = h.shape
    half = N_DEV // 2
    c = dpair.shape[2] // half
    tm = _pick(D, (1024, 512, 256, 128))
    tk = _pick(S, (2048, 1024, 512, 256, 128))
    ops = [(h, _spec((tk, tm), lambda s, i, k: (k, i))),
           (dpair, _spec((None, tk, c), lambda s, i, k: (s // half, k, s % half)))]
    outs = [(_sd((N_DEV, D, c), BF), _spec((None, tm, c), lambda s, i, k: (s, i, 0)))]
    return _mm(name, (N_DEV, D // tm, S // tk), ops, [(0, 1, True, False, 0)], [], outs, _store(), [(tm, c)],
               deps=deps)[0]


def _mm2(name, a, b, ta, tb, out_dtype, epi=None, extras=(), res=None, tn_cands=(512, 384, 256, 128), deps=()):
    a, a_lead = a if isinstance(a, tuple) else (a, None)
    b, b_lead = b if isinstance(b, tuple) else (b, None)
    M = a.shape[-1] if ta else a.shape[-2]
    K = a.shape[-2] if ta else a.shape[-1]
    N = b.shape[-2] if tb else b.shape[-1]
    tm = _pick(M, (1024, 512, 256, 128))
    tn = _pick(N, tn_cands)
    tk = _pick(K, (2048, 1152, 1024, 512, 256, 128))

    def matrix_spec(lead, shape, fn):
        if lead is None:
            return _spec(shape, fn)
        return _spec((None,) + shape, lambda i, j, k: (lead,) + fn(i, j, k))

    a_spec = matrix_spec(a_lead, (tk, tm), lambda i, j, k: (k, i)) if ta else matrix_spec(a_lead, (tm, tk), lambda i, j, k: (i, k))
    b_spec = matrix_spec(b_lead, (tn, tk), lambda i, j, k: (j, k)) if tb else matrix_spec(b_lead, (tk, tn), lambda i, j, k: (k, j))
    ex = [(e, _spec((tm, e.shape[1]), lambda i, j, k: (i, 0))) for e in extras]
    if res is not None:
        ex = [(res, _spec((tm, tn), lambda i, j, k: (i, j)))]
        epi = _store_residual(1.0)
    outs = [(_sd((M, N), out_dtype), _spec((tm, tn), lambda i, j, k: (i, j)))]
    return _mm(name, (M // tm, N // tn, K // tk), [(a, a_spec), (b, b_spec)], [(0, 1, ta, tb, 0)], ex, outs,
               epi or _store(), [(tm, tn)], deps=deps)[0]


def _mm_halves(name, a, w_t, out_dtype, epi_of_half=None, extras=()):
    a, lead = a
    M, K = a.shape[1:]
    N = w_t.shape[2]
    tm = _pick(M, (1024, 512, 256, 128))
    tn = _pick(N, (512, 256, 128))
    tk = _pick(K, (2048, 1024, 512, 256, 128))
    nj = N // tn
    ops = [(a, _spec((None, tm, tk), lambda i, j, k: (lead, i, k))),
           (w_t, _spec((None, tk, tn), lambda i, j, k: (j // nj, k, j % nj)))]
    ex = [(e, _spec((tm, e.shape[1]), lambda i, j, k: (i, 0))) for e in extras]
    outs = [(_sd((2, M, N), out_dtype), _spec((None, tm, tn), lambda i, j, k: (j // nj, i, j % nj)))]

    def epi(accs, ex_tiles, orefs, ids):
        if epi_of_half is None:
            orefs[0][...] = accs[0].astype(out_dtype)
        else:
            for half in range(2):
                pl.when(ids[1] // nj == half)(functools.partial(epi_of_half(half), accs, ex_tiles, orefs, ids))

    return _mm(name, (M // tm, 2 * nj, K // tk), ops, [(0, 1, False, False, 0)], ex, outs, epi, [(tm, tn)])[0]


def _rms_fwd(name, x, g, deps=()):
    S, D = x.shape
    ts = _pick(S, (512, 256, 128))

    def body(x_ref, g_ref, *rest):
        h_ref = rest[-1]
        xv = x_ref[...]
        r = lax.rsqrt(jnp.mean(xv * xv, axis=-1, keepdims=True) + EPS)
        h_ref[...] = (xv * r * g_ref[...]).astype(BF)

    return pl.pallas_call(
        body, name=name, grid=(S // ts,),
        in_specs=[_spec((ts, D), lambda i: (i, 0)), _spec((1, D), lambda i: (0, 0))] + [_ANY] * len(deps),
        out_specs=_spec((ts, D), lambda i: (i, 0)), out_shape=_sd((S, D), BF),
        compiler_params=_params(("parallel",), 12 * ts * D * 4))(x, g, *deps)


def _rms_bwd(name, dh, x, g, dres, deps=()):
    S, D = x.shape
    ts = _pick(S, (256, 128))

    def body(dh_ref, x_ref, g_ref, dres_ref, *rest):
        dx_ref, dxb_ref, dg_ref = rest[len(deps):]
        xv = x_ref[...]
        dhv = dh_ref[...]
        r = lax.rsqrt(jnp.mean(xv * xv, axis=-1, keepdims=True) + EPS)
        xhat = xv * r
        dxh = dhv * g_ref[...]
        cm = jnp.mean(dxh * xhat, axis=-1, keepdims=True)
        dx = r * (dxh - xhat * cm) + dres_ref[...]
        dx_ref[...] = dx
        dxb_ref[...] = dx.astype(BF)

        @pl.when(pl.program_id(0) == 0)
        def _():
            dg_ref[...] = jnp.zeros_like(dg_ref)

        dg_ref[...] += jnp.sum(dhv * xhat, axis=0, keepdims=True)

    row = _spec((ts, D), lambda i: (i, 0))
    vec = _spec((1, D), lambda i: (0, 0))
    return pl.pallas_call(
        body, name=name, grid=(S // ts,),
        in_specs=[row, row, vec, row] + [_ANY] * len(deps), out_specs=[row, row, vec],
        out_shape=[_sd((S, D), F32), _sd((S, D), BF), _sd((1, D), F32)],
        compiler_params=_params(("arbitrary",), 20 * ts * D * 4))(dh, x, g, dres, *deps)


def _final_loss(name, x, g, target):
    S, D = x.shape
    ts = _pick(S, (256, 128))

    def body(x_ref, g_ref, t_ref, loss_ref, dx_ref, dxb_ref, dg_ref):
        xv = x_ref[...]
        gv = g_ref[...]
        r = lax.rsqrt(jnp.mean(xv * xv, axis=-1, keepdims=True) + EPS)
        xhat = xv * r
        err = xhat * gv - t_ref[...]
        part = 0.5 * jnp.sum(jnp.mean(err * err, axis=-1, keepdims=True), axis=0, keepdims=True)
        dy = err * (1.0 / D)
        dxh = dy * gv
        cm = jnp.mean(dxh * xhat, axis=-1, keepdims=True)
        dx = r * (dxh - xhat * cm)
        dx_ref[...] = dx
        dxb_ref[...] = dx.astype(BF)

        @pl.when(pl.program_id(0) == 0)
        def _():
            dg_ref[...] = jnp.zeros_like(dg_ref)
            loss_ref[...] = jnp.zeros_like(loss_ref)

        dg_ref[...] += jnp.sum(dy * xhat, axis=0, keepdims=True)
        loss_ref[...] += jnp.broadcast_to(part, loss_ref.shape)

    row = _spec((ts, D), lambda i: (i, 0))
    vec = _spec((1, D), lambda i: (0, 0))
    return pl.pallas_call(
        body, name=name, grid=(S // ts,),
        in_specs=[row, vec, row], out_specs=[_spec((1, LANES), lambda i: (0, 0)), row, row, vec],
        out_shape=[_sd((1, LANES), F32), _sd((S, D), F32), _sd((S, D), BF), _sd((1, D), F32)],
        compiler_params=_params(("arbitrary",), 20 * ts * D * 4))(x, g, target)


def _swiglu(gate, up):
    sg = jax.nn.sigmoid(gate)
    silu = gate * sg
    return up * (sg * (1.0 + gate * (1.0 - sg))), silu, silu * up


def _swiglu_bwd_epi(accs, ex, orefs, ids):
    da = 0.5 * accs[0]
    orefs[0][0] = (da * ex[0][0].astype(F32)).astype(BF)
    orefs[0][1] = (da * ex[0][1].astype(F32)).astype(BF)


_GELU_C = math.sqrt(2.0 / math.pi)


def _gelu(x):
    return x * (0.5 * (1.0 + jnp.tanh(_GELU_C * (x + 0.044715 * (x * x * x)))))


def _gelu_and_grad(x):
    x2 = x * x
    t = jnp.tanh(_GELU_C * (x + 0.044715 * (x2 * x)))
    cdf = 0.5 * (1.0 + t)
    return x * cdf, cdf + x * ((0.5 * _GELU_C) * (1.0 - t * t) * (1.0 + (3.0 * 0.044715) * x2))


def _causal_block_mask():
    row = lax.broadcasted_iota(jnp.int32, (SGU_BLOCK, SGU_BLOCK), 0) // CHUNK
    col = lax.broadcasted_iota(jnp.int32, (SGU_BLOCK, SGU_BLOCK), 1) // CHUNK
    return row >= col


def _sgu_mid_fwd(name, puv, gain, bias, w_sp, b_sp):
    _, S, W = puv.shape
    G = SGU_GROUPS
    C = W // G
    T = SGU_BLOCK

    def body(puv_ref, gain_ref, bias_ref, w_ref, b_ref, out_ref):
        mask = _causal_block_mask()
        v = _gelu(puv_ref[1])
        mu = jnp.mean(v, axis=-1, keepdims=True)
        vc = v - mu
        rs = lax.rsqrt(jnp.mean(vc * vc, axis=-1, keepdims=True) + EPS)
        vln = (vc * rs * gain_ref[...] + bias_ref[...]).astype(BF)
        for g in range(G):
            wg = jnp.where(mask, w_ref[g], 0.0).astype(BF)
            mixed = jnp.dot(wg, vln[:, g * C:(g + 1) * C], preferred_element_type=F32) + b_ref[g]
            out_ref[:, g * C:(g + 1) * C] = (_gelu(puv_ref[0, :, g * C:(g + 1) * C]) * mixed).astype(BF)

    return pl.pallas_call(
        body, name=name, grid=(S // T,),
        in_specs=[_spec((2, T, W), lambda i: (0, i, 0)), _spec((1, W), lambda i: (0, 0)), _spec((1, W), lambda i: (0, 0)),
                  _spec((G, T, T), lambda i: (0, 0, 0)), _spec((G, T, 1), lambda i: (0, 0, 0))],
        out_specs=_spec((T, W), lambda i: (i, 0)), out_shape=_sd((S, W), BF),
        compiler_params=_params(("parallel",), 16 * T * W * 4))(puv, gain, bias, w_sp, b_sp)


def _sgu_mid_bwd(name, puv, dgated, gain, bias, w_sp, b_sp):
    _, S, W = puv.shape
    G = SGU_GROUPS
    C = W // G
    T = SGU_BLOCK

    def body(puv_ref, dg_ref, gain_ref, bias_ref, w_ref, b_ref, dpuv_ref, dgain_ref, dbias_ref, dw_ref, db_ref, dvln_ref):
        @pl.when(pl.program_id(0) == 0)
        def _():
            dgain_ref[...] = jnp.zeros_like(dgain_ref)
            dbias_ref[...] = jnp.zeros_like(dbias_ref)
            dw_ref[...] = jnp.zeros_like(dw_ref)
            db_ref[...] = jnp.zeros_like(db_ref)

        mask = _causal_block_mask()
        v, v_grad = _gelu_and_grad(puv_ref[1])
        mu = jnp.mean(v, axis=-1, keepdims=True)
        vc = v - mu
        rs = lax.rsqrt(jnp.mean(vc * vc, axis=-1, keepdims=True) + EPS)
        vhat = vc * rs
        gain_v = gain_ref[...]
        vln = (vhat * gain_v + bias_ref[...]).astype(BF)
        for g in range(G):
            sl = slice(g * C, (g + 1) * C)
            wg = jnp.where(mask, w_ref[g], 0.0).astype(BF)
            vg = vln[:, sl]
            mixed = jnp.dot(wg, vg, preferred_element_type=F32) + b_ref[g]
            u, u_grad = _gelu_and_grad(puv_ref[0, :, sl])
            dgt = dg_ref[:, sl].astype(F32)
            dpuv_ref[0, :, sl] = (dgt * mixed * u_grad).astype(BF)
            dmix = dgt * u
            db_ref[g] += jnp.sum(dmix, axis=-1, keepdims=True)
            dmb = dmix.astype(BF)
            dwg = lax.dot_general(dmb, vg, (((1,), (1,)), ((), ())), preferred_element_type=F32)
            dw_ref[g] += jnp.where(mask, dwg, 0.0)
            dvln_ref[:, sl] = lax.dot_general(wg, dmb, (((0,), (0,)), ((), ())), preferred_element_type=F32)
        dvln = dvln_ref[...]
        dgain_ref[...] += jnp.sum(dvln * vhat, axis=0, keepdims=True)
        dbias_ref[...] += jnp.sum(dvln, axis=0, keepdims=True)
        dvh = dvln * gain_v
        m1 = jnp.mean(dvh, axis=-1, keepdims=True)
        m2 = jnp.mean(dvh * vhat, axis=-1, keepdims=True)
        dv = rs * (dvh - m1 - vhat * m2)
        dpuv_ref[1] = (dv * v_grad).astype(BF)

    vec = _spec((1, W), lambda i: (0, 0))
    wsp = _spec((G, T, T), lambda i: (0, 0, 0))
    bsp = _spec((G, T, 1), lambda i: (0, 0, 0))
    return pl.pallas_call(
        body, name=name, grid=(S // T,),
        in_specs=[_spec((2, T, W), lambda i: (0, i, 0)), _spec((T, W), lambda i: (i, 0)), vec, vec, wsp, bsp],
        out_specs=[_spec((2, T, W), lambda i: (0, i, 0)), vec, vec, wsp, bsp],
        out_shape=[_sd((2, S, W), BF), _sd((1, W), F32), _sd((1, W), F32), _sd((G, T, T), F32), _sd((G, T, 1), F32)],
        scratch_shapes=[pltpu.VMEM((T, W), F32)],
        compiler_params=_params(("arbitrary",), 24 * T * W * 4))(puv, dgated, gain, bias, w_sp, b_sp)


def _rope_tables(positions):
    half = QK_ROPE // 2
    inv_freq = 1.0 / (ROPE_THETA ** (jnp.arange(half, dtype=F32) / half))
    ang = positions.astype(F32)[:, None] * inv_freq[None, :]
    cos, sin = jnp.cos(ang), jnp.sin(ang)
    z = jnp.zeros_like(cos)
    return (jnp.concatenate([cos, cos, z, z], axis=1), jnp.concatenate([-sin, z, z, z], axis=1),
            jnp.concatenate([z, sin, z, z], axis=1))


def _rope(x, cos, sa, sb):
    return x * cos + pltpu.roll(x, HEAD_PAD - QK_ROPE // 2, 1) * sa + pltpu.roll(x, QK_ROPE // 2, 1) * sb


def _rope_t(dy, cos, sa, sb):
    return dy * cos + pltpu.roll(dy * sa, QK_ROPE // 2, 1) + pltpu.roll(dy * sb, HEAD_PAD - QK_ROPE // 2, 1)


def _rms_rows(x, g):
    r = lax.rsqrt(jnp.mean(x * x, axis=-1, keepdims=True) + EPS)
    return x * r * g


def _rms_rows_bwd(dy, x, g):
    r = lax.rsqrt(jnp.mean(x * x, axis=-1, keepdims=True) + EPS)
    xhat = x * r
    dxh = dy * g
    cm = jnp.mean(dxh * xhat, axis=-1, keepdims=True)
    return r * (dxh - xhat * cm), jnp.sum(dy * xhat, axis=0, keepdims=True)


def _mla_mid_fwd(name, proj, qg, kvg, cos, sa, sb):
    S, P = proj.shape
    ts = _pick(S, (512, 256, 128))

    def body(p_ref, qg_ref, kvg_ref, cos_ref, sa_ref, sb_ref, lat_ref, kr_ref):
        lat_ref[0] = _rms_rows(p_ref[:, :Q_LORA], qg_ref[...]).astype(BF)
        lat_ref[1] = _rms_rows(p_ref[:, Q_LORA:Q_LORA + KV_LORA], kvg_ref[...]).astype(BF)
        kr_ref[...] = _rope(p_ref[:, Q_LORA + KV_LORA:], cos_ref[...], sa_ref[...], sb_ref[...]).astype(BF)

    tab = _spec((ts, HEAD_PAD), lambda i: (i, 0))
    return pl.pallas_call(
        body, name=name, grid=(S // ts,),
        in_specs=[_spec((ts, P), lambda i: (i, 0)), _spec((1, Q_LORA), lambda i: (0, 0)),
                  _spec((1, KV_LORA), lambda i: (0, 0)), tab, tab, tab],
        out_specs=[_spec((2, ts, Q_LORA), lambda i: (0, i, 0)), tab],
        out_shape=[_sd((2, S, Q_LORA), BF), _sd((S, HEAD_PAD), BF)],
        compiler_params=_params(("parallel",), 16 * ts * P * 4))(proj, qg, kvg, cos, sa, sb)


def _mla_mid_bwd(name, proj, dqn, dkvn, dkr_heads, qg, kvg, cos, sa, sb):
    S, P = proj.shape
    H = dkr_heads.shape[0]
    ts = _pick(S, (256, 128))

    def body(p_ref, dqn_ref, dkvn_ref, dkr_ref, qg_ref, kvg_ref, cos_ref, sa_ref, sb_ref, dp_ref, dqg_ref, dkvg_ref):
        @pl.when(pl.program_id(0) == 0)
        def _():
            dqg_ref[...] = jnp.zeros_like(dqg_ref)
            dkvg_ref[...] = jnp.zeros_like(dkvg_ref)

        dq, dqg = _rms_rows_bwd(dqn_ref[...], p_ref[:, :Q_LORA], qg_ref[...])
        dkv, dkvg = _rms_rows_bwd(dkvn_ref[...], p_ref[:, Q_LORA:Q_LORA + KV_LORA], kvg_ref[...])
        dqg_ref[...] += dqg
        dkvg_ref[...] += dkvg
        dkr = dkr_ref[0]
        for h in range(1, H):
            dkr = dkr + dkr_ref[h]
        dp_ref[:, :Q_LORA] = dq.astype(BF)
        dp_ref[:, Q_LORA:Q_LORA + KV_LORA] = dkv.astype(BF)
        dp_ref[:, Q_LORA + KV_LORA:] = _rope_t(dkr, cos_ref[...], sa_ref[...], sb_ref[...]).astype(BF)

    tab = _spec((ts, HEAD_PAD), lambda i: (i, 0))
    lat = _spec((ts, Q_LORA), lambda i: (i, 0))
    gq = _spec((1, Q_LORA), lambda i: (0, 0))
    return pl.pallas_call(
        body, name=name, grid=(S // ts,),
        in_specs=[_spec((ts, P), lambda i: (i, 0)), lat, lat, _spec((H, ts, HEAD_PAD), lambda i: (0, i, 0)),
                  gq, gq, tab, tab, tab],
        out_specs=[_spec((ts, P), lambda i: (i, 0)), gq, gq],
        out_shape=[_sd((S, P), BF), _sd((1, Q_LORA), F32), _sd((1, KV_LORA), F32)],
        compiler_params=_params(("arbitrary",), 24 * ts * P * 4))(proj, dqn, dkvn, dkr_heads, qg, kvg, cos, sa, sb)


def _attn_tile(S):
    return _pick(S, (1024,)) if S >= 2048 else _pick(S, (128,))


def _diag_mask(t, transposed):
    q = lax.broadcasted_iota(jnp.int32, (t, t), 1 if transposed else 0) // CHUNK
    k = lax.broadcasted_iota(jnp.int32, (t, t), 0 if transposed else 1) // CHUNK
    return k <= q


_NT = (((1,), (1,)), ((), ()))


def _tile_loop(lo, hi, step, carry):
    pos = lo
    for unroll in ATTN_UNROLLS:
        trips = (hi - pos) // unroll

        def several(tt, c, unroll=unroll, pos=pos):
            for u in range(unroll):
                c = step(pos + tt * unroll + u, c)
            return c

        carry = lax.fori_loop(0, trips, several, carry)
        pos = pos + trips * unroll
    return lax.fori_loop(pos, hi, step, carry)


def _attn_fwd(name, q_all, kv_all, kr):
    _, S, HP = q_all.shape
    H = HP // HEAD_PAD
    t = _attn_tile(S)
    nq = S // t
    ng = ATTN_GROUPS
    tg = t // ng

    def body(q_ref, kv_ref, kr_ref, o_ref, lse_ref, kcat_ref):
        i = pl.program_id(1)

        @pl.when(i == 0)
        def _():
            kcat_ref[:, :HEAD_PAD] = kv_ref[0]
            kcat_ref[:, HEAD_PAD:] = kr_ref[...]

        qs = [jnp.concatenate([q_ref[0, g * tg:(g + 1) * tg], q_ref[1, g * tg:(g + 1) * tg]], axis=1) for g in range(ng)]

        def step(j, carry, masked):
            off = pl.multiple_of(j * t, t)
            kj = kcat_ref[pl.ds(off, t), :]
            vj = kv_ref[1, pl.ds(off, t), :]
            out = []
            for g in range(ng):
                m, l, acc = carry[g]
                s = lax.dot_general(qs[g], kj, _NT, preferred_element_type=F32)
                if masked:
                    s = jnp.where(_diag_mask(t, False)[g * tg:(g + 1) * tg], s, NEG)
                m2 = jnp.maximum(m, jnp.max(s, axis=-1, keepdims=True))
                al = jnp.exp(m - m2)
                p = jnp.exp(s - m2)
                l2 = al * l + jnp.sum(p, axis=-1, keepdims=True)
                acc2 = al * acc + jnp.dot(p.astype(BF), vj, preferred_element_type=F32)
                out.append((m2, l2, acc2))
            return tuple(out)

        init = tuple((jnp.full((tg, 1), NEG, F32), jnp.zeros((tg, 1), F32), jnp.zeros((tg, V_DIM), F32))
                     for _ in range(ng))
        carry = _tile_loop(0, i, lambda j, c: step(j, c, False), init)
        carry = step(i, carry, True)
        for g in range(ng):
            m, l, acc = carry[g]
            o_ref[g * tg:(g + 1) * tg, :] = acc / l
            lse_ref[g * tg:(g + 1) * tg, :] = jnp.broadcast_to(m + jnp.log(l), (tg, LANES))

    return pl.pallas_call(
        body, name=name, grid=(H, nq),
        in_specs=[_spec((2, t, HEAD_PAD), lambda h, i: (0, i, h)), _spec((2, S, HEAD_PAD), lambda h, i: (0, 0, h)),
                  _spec((S, HEAD_PAD), lambda h, i: (0, 0))],
        out_specs=[_spec((t, HEAD_PAD), lambda h, i: (i, h)), _spec((None, t, LANES), lambda h, i: (h, i, 0))],
        out_shape=[_sd((S, HP), F32), _sd((H, S, LANES), F32)],
        scratch_shapes=[pltpu.VMEM((S, 2 * HEAD_PAD), BF)],
        compiler_params=_params(("parallel", "arbitrary"), 8 * S * HEAD_PAD * 2 + 24 * t * t * 4))(q_all, kv_all, kr)


def _attn_dq(name, q_all, kv_all, kr, do, o, lse, cos, sa, sb, scale):
    _, S, HP = q_all.shape
    H = HP // HEAD_PAD
    t = _attn_tile(S)
    nq = S // t

    def body(q_ref, kv_ref, kr_ref, do_ref, o_ref, lse_ref, cos_ref, sa_ref, sb_ref, dq_ref, dl_ref, kcat_ref):
        i = pl.program_id(1)

        @pl.when(i == 0)
        def _():
            kcat_ref[:, :HEAD_PAD] = kv_ref[0]
            kcat_ref[:, HEAD_PAD:] = kr_ref[...]

        ng = ATTN_GROUPS
        tg = t // ng
        rows = [slice(g * tg, (g + 1) * tg) for g in range(ng)]
        qs = [jnp.concatenate([q_ref[0, r], q_ref[1, r]], axis=1) for r in rows]
        dobs = [do_ref[r, :].astype(BF) for r in rows]
        lses = [lse_ref[r, 0:1] for r in rows]
        dls = [jnp.sum(do_ref[r, :] * o_ref[r, :], axis=-1, keepdims=True) for r in rows]
        for g in range(ng):
            dl_ref[rows[g], :] = jnp.broadcast_to(dls[g], (tg, LANES))

        def step(j, dqs, masked):
            off = pl.multiple_of(j * t, t)
            kj = kcat_ref[pl.ds(off, t), :]
            vj = kv_ref[1, pl.ds(off, t), :]
            out = []
            for g in range(ng):
                s = lax.dot_general(qs[g], kj, _NT, preferred_element_type=F32)
                if masked:
                    s = jnp.where(_diag_mask(t, False)[rows[g]], s, NEG)
                p = jnp.exp(s - lses[g])
                dp = lax.dot_general(dobs[g], vj, _NT, preferred_element_type=F32)
                ds = (p * (dp - dls[g])).astype(BF)
                out.append(dqs[g] + jnp.dot(ds, kj, preferred_element_type=F32))
            return tuple(out)

        init = tuple(jnp.zeros((tg, 2 * HEAD_PAD), F32) for _ in range(ng))
        dqs = _tile_loop(0, i, lambda j, c: step(j, c, False), init)
        dqs = step(i, dqs, True)
        for g in range(ng):
            dq_ref[0, rows[g]] = (dqs[g][:, :HEAD_PAD] * scale).astype(BF)
            dq_ref[1, rows[g]] = (_rope_t(dqs[g][:, HEAD_PAD:], cos_ref[rows[g], :], sa_ref[rows[g], :],
                                          sb_ref[rows[g], :]) * scale).astype(BF)

    tab = _spec((t, HEAD_PAD), lambda h, i: (i, 0))
    stat = _spec((None, t, LANES), lambda h, i: (h, i, 0))
    head_tile = _spec((t, HEAD_PAD), lambda h, i: (i, h))
    return pl.pallas_call(
        body, name=name, grid=(H, nq),
        in_specs=[_spec((2, t, HEAD_PAD), lambda h, i: (0, i, h)), _spec((2, S, HEAD_PAD), lambda h, i: (0, 0, h)),
                  _spec((S, HEAD_PAD), lambda h, i: (0, 0)), head_tile, head_tile, stat, tab, tab, tab],
        out_specs=[_spec((2, t, HEAD_PAD), lambda h, i: (0, i, h)), stat],
        out_shape=[_sd((2, S, HP), BF), _sd((H, S, LANES), F32)],
        scratch_shapes=[pltpu.VMEM((S, 2 * HEAD_PAD), BF)],
        compiler_params=_params(("parallel", "arbitrary"), 8 * S * HEAD_PAD * 2 + 32 * t * t * 4))(
            q_all, kv_all, kr, do, o, lse, cos, sa, sb)


def _attn_dkv(name, q_all, kv_all, kr, do, lse_row, delta_row):
    _, S, HP = q_all.shape
    H = HP // HEAD_PAD
    t = _attn_tile(S)
    nq = S // t

    def body(q_ref, kv_ref, kr_ref, do_ref, lse_ref, dl_ref, dkv_ref, dkr_ref, qcat_ref):
        j = pl.program_id(1)

        @pl.when(j == 0)
        def _():
            qcat_ref[:, :HEAD_PAD] = q_ref[0]
            qcat_ref[:, HEAD_PAD:] = q_ref[1]

        ng = ATTN_GROUPS
        tg = t // ng
        rows = [slice(g * tg, (g + 1) * tg) for g in range(ng)]
        kjs = [jnp.concatenate([kv_ref[0, r], kr_ref[r, :]], axis=1) for r in rows]
        vjs = [kv_ref[1, r] for r in rows]

        def step(i, carry, masked):
            off = pl.multiple_of(i * t, t)
            qi = qcat_ref[pl.ds(off, t), :]
            doi = do_ref[pl.ds(off, t), :].astype(BF)
            lse_i = lse_ref[i]
            dl_i = dl_ref[i]
            out = []
            for g in range(ng):
                dk, dv = carry[g]
                st = lax.dot_general(kjs[g], qi, _NT, preferred_element_type=F32)
                if masked:
                    st = jnp.where(_diag_mask(t, True)[rows[g]], st, NEG)
                pt = jnp.exp(st - lse_i)
                dv2 = dv + jnp.dot(pt.astype(BF), doi, preferred_element_type=F32)
                dpt = lax.dot_general(vjs[g], doi, _NT, preferred_element_type=F32)
                dst = (pt * (dpt - dl_i)).astype(BF)
                out.append((dk + jnp.dot(dst, qi, preferred_element_type=F32), dv2))
            return tuple(out)

        init = tuple((jnp.zeros((tg, 2 * HEAD_PAD), F32), jnp.zeros((tg, V_DIM), F32)) for _ in range(ng))
        carry = step(j, init, True)
        carry = _tile_loop(j + 1, nq, lambda i, c: step(i, c, False), carry)
        for g in range(ng):
            dk, dv = carry[g]
            dkv_ref[0, rows[g]] = dk[:, :HEAD_PAD].astype(BF)
            dkv_ref[1, rows[g]] = dv.astype(BF)
            dkr_ref[rows[g], :] = dk[:, HEAD_PAD:]

    stat = _spec((None, nq, 1, t), lambda h, j: (h, 0, 0, 0))
    return pl.pallas_call(
        body, name=name, grid=(H, nq),
        in_specs=[_spec((2, S, HEAD_PAD), lambda h, j: (0, 0, h)), _spec((2, t, HEAD_PAD), lambda h, j: (0, j, h)),
                  _spec((t, HEAD_PAD), lambda h, j: (j, 0)), _spec((S, HEAD_PAD), lambda h, j: (0, h)), stat, stat],
        out_specs=[_spec((2, t, HEAD_PAD), lambda h, j: (0, j, h)), _spec((None, t, HEAD_PAD), lambda h, j: (h, j, 0))],
        out_shape=[_sd((2, S, HP), BF), _sd((H, S, HEAD_PAD), F32)],
        scratch_shapes=[pltpu.VMEM((S, 2 * HEAD_PAD), BF)],
        compiler_params=_params(("parallel", "arbitrary"), 8 * S * HEAD_PAD * 4 + 32 * t * t * 4))(
            q_all, kv_all, kr, do, lse_row, delta_row)


def _place():
    x, y, c = lax.axis_index("x"), lax.axis_index("y"), lax.axis_index("c")
    return x, y, c


def _all_gather(blocks):
    n = len(blocks)

    def body(*refs):
        ins, outs = refs[:n], refs[n:2 * n]
        send_sems, recv_sems, local_sems = refs[2 * n:]
        x, y, c = _place()
        me = 4 * x + 2 * y + c
        sibling = (x, y, 1 - c)
        chips = [(1 - x, y), (x, 1 - y), (1 - x, 1 - y)]

        def slab(a, px, py, pc):
            return outs[a].at[4 * px + 2 * py + pc]

        def copy(a, k, src, dst, to):
            return pltpu.make_async_remote_copy(src_ref=src, dst_ref=dst, send_sem=send_sems.at[a, k],
                                                recv_sem=recv_sems.at[a, k], device_id=to, device_id_type=MESH)

        local = [pltpu.make_async_copy(ins[a], outs[a].at[me], local_sems.at[a]) for a in range(n)]
        for cp in local:
            cp.start()
        sends = []
        for a in range(n):
            mine = slab(a, x, y, c)
            sends.append(copy(a, 0, ins[a], mine, sibling))
            for j, chip in enumerate(chips):
                sends.append(copy(a, 1 + j, ins[a], mine, (*chip, c)))
        for cp in sends:
            cp.start()
        for j, chip in enumerate(chips):
            for a in range(n):
                got = slab(a, *chip, c)
                copy(a, 1 + j, got, got, (x, y, c)).wait_recv()
                fwd = copy(a, 4 + j, got, got, sibling)
                fwd.start()
                sends.append(fwd)
        for a in range(n):
            got = slab(a, x, y, 1 - c)
            copy(a, 0, got, got, (x, y, c)).wait_recv()
            for j, chip in enumerate(chips):
                got = slab(a, *chip, 1 - c)
                copy(a, 4 + j, got, got, (x, y, c)).wait_recv()
        for cp in sends:
            cp.wait_send()
        for cp in local:
            cp.wait()

    return pl.pallas_call(
        body, name="weights_all_gather", in_specs=[_ANY] * n, out_specs=[_ANY] * n,
        out_shape=[_sd((N_DEV,) + b.shape, b.dtype) for b in blocks],
        scratch_shapes=[pltpu.SemaphoreType.DMA((n, 7)), pltpu.SemaphoreType.DMA((n, 7)), pltpu.SemaphoreType.DMA((n,))],
    )(*blocks)


_HBM = pl.BlockSpec(memory_space=pltpu.HBM)
_SEM = pl.BlockSpec(memory_space=pltpu.SEMAPHORE)
_EFFECT = pltpu.SideEffectType.DATAFLOW_SIDE_EFFECTING


def _peers():
    x, y, c = _place()
    out = []
    for m in range(1, N_DEV):
        px, py, pc = x ^ (m >> 2), y ^ ((m >> 1) & 1), c ^ (m & 1)
        out.append((m, (px, py, pc), 4 * px + 2 * py + pc))
    return 4 * x + 2 * y + c, out


def _exchange_copies(src, land, send_sem, recv_sem, gather):
    me, peers = _peers()
    cps = []
    for a in range(len(src)):
        for m, pos, idx in peers:
            s_ref, d_ref = (src[a], land[a].at[me]) if gather else (src[a].at[idx], land[a].at[m - 1])
            k = a * (N_DEV - 1) + m - 1
            cps.append(pltpu.make_async_remote_copy(src_ref=s_ref, dst_ref=d_ref, send_sem=send_sem.at[k],
                                                    recv_sem=recv_sem.at[k], device_id=pos, device_id_type=MESH))
    return cps


def _xstart(name, srcs, gather, after=()):
    n = len(srcs)
    if gather:
        land_shapes = [(N_DEV,) + s.shape for s in srcs]
    else:
        land_shapes = [(N_DEV - 1,) + s.shape[1:] for s in srcs]

    def body(*refs):
        src, land = refs[:n], refs[n:2 * n]
        send_sem, recv_sem = refs[2 * n + len(after)], refs[2 * n + len(after) + 1]
        token = refs[-1]
        for cp in _exchange_copies(src, land, send_sem, recv_sem, gather):
            cp.start()
        token[...] = jnp.zeros_like(token)

    sem = pltpu.SemaphoreType.DMA((n * (N_DEV - 1),))
    out_shape = ([sem, sem] + [pltpu.HBM(s.shape, s.dtype) for s in srcs]
                 + [pltpu.HBM(sh, s.dtype) for sh, s in zip(land_shapes, srcs)] + [_sd((8, LANES), F32)])
    args = [pltpu.with_memory_space_constraint(s, pltpu.HBM) for s in srcs]
    args += [pltpu.with_memory_space_constraint(lax.empty(sh, s.dtype), pltpu.HBM) for sh, s in zip(land_shapes, srcs)]
    res = pl.pallas_call(
        body, name=name, out_shape=out_shape, in_specs=[_HBM] * (2 * n) + [_ANY] * len(after),
        out_specs=[_SEM, _SEM] + [_HBM] * (2 * n) + [pl.BlockSpec(memory_space=pltpu.VMEM)],
        input_output_aliases={i: 2 + i for i in range(2 * n)},
        compiler_params=pltpu.CompilerParams(has_side_effects=_EFFECT))(*args, *after)
    return dict(send=res[0], recv=res[1], srcs=list(res[2:2 + n]), lands=list(res[2 + n:2 + 2 * n]), token=res[-1])


def _xwait(name, st, after, gather):
    n = len(st["srcs"])

    def body(*refs):
        src, land = refs[:n], refs[n:2 * n]
        send_sem, recv_sem = refs[2 * n], refs[2 * n + 1]
        for cp in _exchange_copies(src, land, send_sem, recv_sem, gather):
            cp.wait_send()
            cp.wait_recv()

    arrays = st["srcs"] + st["lands"]
    res = pl.pallas_call(
        body, name=name, out_shape=[pltpu.HBM(a.shape, a.dtype) for a in arrays],
        in_specs=[_HBM] * (2 * n) + [_SEM, _SEM, _ANY], out_specs=[_HBM] * (2 * n),
        input_output_aliases={i: i for i in range(2 * n)},
        compiler_params=pltpu.CompilerParams(has_side_effects=_EFFECT))(*arrays, st["send"], st["recv"], after)
    return list(res[:n]), list(res[n:])


def _put_own(name, land, block, me_arr):
    R, C = block.shape
    tr = _pick(R, (512, 256, 128, 64, 32, 16))

    def body(me_ref, b_ref, land_ref, o_ref):
        o_ref[...] = b_ref[...]

    return pl.pallas_call(
        body, name=name, out_shape=_sd(land.shape, land.dtype),
        grid_spec=pltpu.PrefetchScalarGridSpec(
            num_scalar_prefetch=1, grid=(R // tr,),
            in_specs=[_spec((tr, C), lambda i, me_ref: (i, 0)), _ANY],
            out_specs=_spec((None, tr, C), lambda i, me_ref: (me_ref[0], i, 0))),
        input_output_aliases={2: 0},
        compiler_params=_params(("parallel",), 8 * tr * C * 2))(me_arr, block, land)


def _all_reduce_small(name, part):
    R = part.shape[0]

    def body(p_ref, out_ref, gath_ref, send_sems, recv_sems):
        x, y, c = _place()
        me = 4 * x + 2 * y + c
        gath_ref[me] = p_ref[...]
        cps = []
        for m in range(1, N_DEV):
            to = (x ^ (m >> 2), y ^ ((m >> 1) & 1), c ^ (m & 1))
            cps.append(pltpu.make_async_remote_copy(
                src_ref=p_ref, dst_ref=gath_ref.at[me], send_sem=send_sems.at[m - 1], recv_sem=recv_sems.at[m - 1],
                device_id=to, device_id_type=MESH))
        for cp in cps:
            cp.start()
        for m in range(1, N_DEV):
            frm = 4 * (x ^ (m >> 2)) + 2 * (y ^ ((m >> 1) & 1)) + (c ^ (m & 1))
            pltpu.make_async_remote_copy(
                src_ref=p_ref, dst_ref=gath_ref.at[frm], send_sem=send_sems.at[m - 1], recv_sem=recv_sems.at[m - 1],
                device_id=(x, y, c), device_id_type=MESH).wait_recv()
        for cp in cps:
            cp.wait_send()
        tot = gath_ref[0]
        for k in range(1, N_DEV):
            tot = tot + gath_ref[k]
        out_ref[...] = tot

    vm = pl.BlockSpec(memory_space=pltpu.VMEM)
    return pl.pallas_call(
        body, name=name, in_specs=[vm], out_specs=vm, out_shape=_sd((R, LANES), F32),
        scratch_shapes=[pltpu.VMEM((N_DEV, R, LANES), F32), pltpu.SemaphoreType.DMA((N_DEV - 1,)),
                        pltpu.SemaphoreType.DMA((N_DEV - 1,))],
        compiler_params=pltpu.CompilerParams(vmem_limit_bytes=VMEM_FLOOR),
    )(part)


def _adam_math(w, g, m, v):
    m2 = ADAM_B1 * m + (1.0 - ADAM_B1) * g
    v2 = ADAM_B2 * v + (1.0 - ADAM_B2) * (g * g)
    m_hat = m2 / (1.0 - ADAM_B1 ** ADAM_STEP)
    v_hat = v2 / (1.0 - ADAM_B2 ** ADAM_STEP)
    delta = -ADAM_LR * (m_hat / (jnp.sqrt(v_hat) + ADAM_EPS) + ADAM_WD * w)
    return delta, m2, v2


def _adamw_sharded(name, lands, fulls, me_arr, w, m, v):
    n_l = len(lands)
    R, C = lands[0].shape[1], lands[0].shape[2]
    tr = _pick(R, (128, 64, 32, 16, 8))
    nr = R // tr

    def body(me_ref, *refs):
        land_refs, own_refs = refs[:n_l], refs[n_l:2 * n_l]
        w_ref, m_ref, v_ref, g_ref, d_ref, m2_ref, v2_ref = refs[2 * n_l:]
        layer = pl.program_id(0)
        for ll in range(n_l):
            @pl.when(layer == ll)
            def _(ll=ll):
                g = own_refs[ll][...].astype(F32)
                for j in range(N_DEV - 1):
                    g = g + land_refs[ll][j].astype(F32)
                delta, m2, v2 = _adam_math(w_ref[...], g, m_ref[...], v_ref[...])
                g_ref[...] = g
                d_ref[...] = delta
                m2_ref[...] = m2
                v2_ref[...] = v2

    def row_of(ll):
        return lambda l, i, me_ref: jnp.where(l == ll, i, 0)

    in_specs = [_spec((N_DEV - 1, tr, C), lambda l, i, me_ref, f=row_of(ll): (0, f(l, i, me_ref), 0)) for ll in range(n_l)]
    in_specs += [_spec((None, tr, C), lambda l, i, me_ref, f=row_of(ll): (me_ref[0], f(l, i, me_ref), 0))
                 for ll in range(n_l)]
    blk = _spec((tr, C), lambda l, i, me_ref: (l * nr + i, 0))
    return pl.pallas_call(
        body, name=name, out_shape=[_sd(w.shape, F32)] * 4,
        grid_spec=pltpu.PrefetchScalarGridSpec(num_scalar_prefetch=1, grid=(n_l, nr), in_specs=in_specs + [blk] * 3,
                                               out_specs=[blk] * 4),
        compiler_params=_params(("parallel", "parallel"), (4 * n_l * N_DEV + 40) * tr * C * 4))(
            me_arr, *lands, *fulls, w, m, v)


def _adamw_packed(name, w, g, m, v):
    R = w.shape[0]

    def body(w_ref, g_ref, m_ref, v_ref, d_ref, m2_ref, v2_ref):
        delta, m2, v2 = _adam_math(w_ref[...], g_ref[...], m_ref[...], v_ref[...])
        d_ref[...] = delta
        m2_ref[...] = m2
        v2_ref[...] = v2

    vm = pl.BlockSpec(memory_space=pltpu.VMEM)
    return pl.pallas_call(
        body, name=name, in_specs=[vm] * 4, out_specs=[vm] * 3, out_shape=[_sd((R, LANES), F32)] * 3,
        compiler_params=pltpu.CompilerParams(vmem_limit_bytes=VMEM_FLOOR))(w, g, m, v)


def _pack(arrays):
    flat = jnp.concatenate([a.reshape(-1).astype(F32) for a in arrays])
    pad = (-flat.shape[0]) % (8 * LANES)
    return jnp.pad(flat, (0, pad)).reshape(-1, LANES)


def _unpack(packed, like):
    flat = packed.reshape(-1)
    out, pos = [], 0
    for a in like:
        n = math.prod(a.shape)
        out.append(flat[pos:pos + n].reshape(a.shape))
        pos += n
    return out


def _ffn_fwd(tag, x, gain, w_in_sm, w_out_of, deps=()):
    h = _rms_fwd(tag + "_norm", x, gain, deps)
    gu, act = _pair_in(tag + "_in", h, w_in_sm, BF, _swiglu)
    x_new = _rows_out(tag + "_out", act, w_out_of(act), x, 0.5)
    return x_new, (x, h, gu, act)


def _ffn_bwd(tag, d, d_bf, saved, gain, w_in_sm, w_out_sm, send_out, send_in):
    x, h, gu, act = saved
    dgu = _rows_dact(tag + "_dact", d_bf, w_out_sm, [gu], [_sd(gu.shape, BF)], _swiglu_bwd_epi)[0]
    g_out = _rows_wgrad(tag + "_wgrad_out", act, d_bf, 0.5)
    token_out = send_out(g_out)
    g_in = _cols_wgrad(tag + "_wgrad_in", h, dgu, deps=(token_out,))
    token = send_in(g_in)
    dh = _cols_dh(tag + "_dh", dgu, w_in_sm, deps=(token,))
    dx, dx_bf, dgain = _rms_bwd(tag + "_dnorm", dh, x, gain, d)
    return dx, dx_bf, dgain


def kernel(x, positions, ln_ffn1, ffn1_w_in, ffn1_w_out, ln_mix, ln_ffn2, ffn2_w_in, ffn2_w_out, sgu_w_in, sgu_v_gain, sgu_v_bias, sgu_w_spatial, sgu_b_spatial, sgu_w_out, mla_w_in, mla_q_norm, mla_w_q_up, mla_kv_norm, mla_w_kv_up, mla_w_out, ln_final, loss_target, m_ln_ffn1, m_ffn1_w_in, m_ffn1_w_out, m_ln_mix, m_ln_ffn2, m_ffn2_w_in, m_ffn2_w_out, m_sgu_w_in, m_sgu_v_gain, m_sgu_v_bias, m_sgu_w_spatial, m_sgu_b_spatial, m_sgu_w_out, m_mla_w_in, m_mla_q_norm, m_mla_w_q_up, m_mla_kv_norm, m_mla_w_kv_up, m_mla_w_out, m_ln_final, v_ln_ffn1, v_ffn1_w_in, v_ffn1_w_out, v_ln_mix, v_ln_ffn2, v_ffn2_w_in, v_ffn2_w_out, v_sgu_w_in, v_sgu_v_gain, v_sgu_v_bias, v_sgu_w_spatial, v_sgu_b_spatial, v_sgu_w_out, v_mla_w_in, v_mla_q_norm, v_mla_w_q_up, v_mla_kv_norm, v_mla_w_kv_up, v_mla_w_out, v_ln_final):
    S, D = x.shape[1], x.shape[2]
    L = ln_ffn1.shape[0]
    H = mla_w_q_up.shape[-1] * N_DEV // (QK_NOPE + QK_ROPE)
    xi, yi, ci = _place()
    me = 4 * xi + 2 * yi + ci
    me_arr = jnp.reshape(me, (1,)).astype(jnp.int32)
    big = dict(ffn1_w_in=ffn1_w_in, ffn1_w_out=ffn1_w_out, ffn2_w_in=ffn2_w_in, ffn2_w_out=ffn2_w_out,
               sgu_w_in=sgu_w_in, sgu_w_out=sgu_w_out, mla_w_in=mla_w_in, mla_w_q_up=mla_w_q_up,
               mla_w_kv_up=mla_w_kv_up, mla_w_out=mla_w_out)
    big_m = dict(ffn1_w_in=m_ffn1_w_in, ffn1_w_out=m_ffn1_w_out, ffn2_w_in=m_ffn2_w_in, ffn2_w_out=m_ffn2_w_out,
                 sgu_w_in=m_sgu_w_in, sgu_w_out=m_sgu_w_out, mla_w_in=m_mla_w_in, mla_w_q_up=m_mla_w_q_up,
                 mla_w_kv_up=m_mla_w_kv_up, mla_w_out=m_mla_w_out)
    big_v = dict(ffn1_w_in=v_ffn1_w_in, ffn1_w_out=v_ffn1_w_out, ffn2_w_in=v_ffn2_w_in, ffn2_w_out=v_ffn2_w_out,
                 sgu_w_in=v_sgu_w_in, sgu_w_out=v_sgu_w_out, mla_w_in=v_mla_w_in, mla_w_q_up=v_mla_w_q_up,
                 mla_w_kv_up=v_mla_w_kv_up, mla_w_out=v_mla_w_out)
    names = list(big)
    mla_names = ["mla_w_in", "mla_w_q_up", "mla_w_kv_up", "mla_w_out"]

    blocks = {(k, l): big[k][l].astype(BF) for k in names for l in range(big[k].shape[0])}
    first = ("ffn1_w_in", 0)
    groups = {}
    for i in range(L):
        if i > 0:
            groups[f"ffn1_in_{i}"] = [("ffn1_w_in", i)]
        groups[f"ffn1_out_{i}"] = [("ffn1_w_out", i)]
        groups[f"mix_{i}"] = [("sgu_w_in", i // 2), ("sgu_w_out", i // 2)] if i % 2 == 0 else [(k, i // 2) for k in mla_names]
        groups[f"ffn2_{i}"] = [("ffn2_w_in", i), ("ffn2_w_out", i)]
    gathered = {first: _all_gather([blocks[first]])[0]}
    started, order_after = {}, (gathered[first],)
    for tag, grp in groups.items():
        started[tag] = _xstart(f"gather_start_{tag}", [blocks[k] for k in grp], True, order_after)
        order_after = (started[tag]["token"],)

    def fetch(tag, after):
        own, lands = _xwait(f"gather_wait_{tag}", started[tag], after, True)
        for k, blk, land in zip(groups[tag], own, lands):
            gathered[k] = _put_own(f"gather_own_{k[0]}_{k[1]}", land, blk, me_arr)

    def w_out_of(name, tag):
        def get(after):
            fetch(tag, after)
            return gathered[name]
        return get

    norm_rows = jnp.zeros((N_DEV, LANES), F32)
    mine = jnp.concatenate([mla_q_norm[0], mla_kv_norm[0]])
    norm_rows = lax.dynamic_update_slice(norm_rows, mine[None, :], (me, 0))
    norm_all = _all_reduce_small("norm_gains_gather", norm_rows)
    nq_sh = mla_q_norm.shape[1]
    q_gain = norm_all[:, :nq_sh].reshape(1, Q_LORA)
    kv_gain = norm_all[:, nq_sh:2 * nq_sh].reshape(1, KV_LORA)
    cos, sa, sb = _rope_tables(positions[0])
    scale = float((QK_NOPE + QK_ROPE) ** -0.5)

    xs = x[0]
    w_sp = sgu_w_spatial[0]
    b_sp = sgu_b_spatial[0][:, :, None]
    saved = []
    mla_w = {}
    deps = order_after
    for i in range(L):
        if i > 0:
            fetch(f"ffn1_in_{i}", xs)
        xs, s1 = _ffn_fwd(f"l{i}_ffn1", xs, ln_ffn1[i:i + 1], gathered[("ffn1_w_in", i)],
                          w_out_of(("ffn1_w_out", i), f"ffn1_out_{i}"), deps)
        deps = ()
        fetch(f"mix_{i}", xs)
        x_mix = xs
        h = _rms_fwd(f"l{i}_mix_norm", xs, ln_mix[i:i + 1])
        j = i // 2
        if i % 2 == 0:
            puv = _pair_in(f"l{i}_sgu_in", h, gathered[("sgu_w_in", j)], F32, None)[0]
            gated = _sgu_mid_fwd(f"l{i}_sgu_mid", puv, sgu_v_gain, sgu_v_bias, w_sp, b_sp)
            xs = _rows_out(f"l{i}_sgu_out", gated, gathered[("sgu_w_out", j)], xs, 1.0)
            sm = (x_mix, h, puv, gated)
        else:
            w_in_nat = gathered[("mla_w_in", j)].reshape(D, Q_LORA + KV_LORA + QK_ROPE)
            w_in_pad = jnp.pad(w_in_nat, ((0, 0), (0, HEAD_PAD - QK_ROPE)))
            wq_nat = jnp.transpose(gathered[("mla_w_q_up", j)], (1, 0, 2)).reshape(Q_LORA, H, QK_NOPE + QK_ROPE)
            wq_t = jnp.stack([wq_nat[:, :, :QK_NOPE].reshape(Q_LORA, H * HEAD_PAD),
                              jnp.pad(wq_nat[:, :, QK_NOPE:], ((0, 0), (0, 0), (0, HEAD_PAD - QK_ROPE))).reshape(
                                  Q_LORA, H * HEAD_PAD)])
            wkv_nat = jnp.transpose(gathered[("mla_w_kv_up", j)], (1, 0, 2)).reshape(KV_LORA, H, QK_NOPE + V_DIM)
            wkv_t = jnp.stack([wkv_nat[:, :, :QK_NOPE].reshape(KV_LORA, H * HEAD_PAD),
                               wkv_nat[:, :, QK_NOPE:].reshape(KV_LORA, H * HEAD_PAD)])
            w_o_nat = gathered[("mla_w_out", j)].reshape(H * V_DIM, D)
            mla_w[i] = (w_in_pad, wq_t, wkv_t, w_o_nat)
            proj = _mm2(f"l{i}_mla_in", h, w_in_pad, False, False, F32, tn_cands=(384, 128))
            lat, kr = _mla_mid_fwd(f"l{i}_mla_mid", proj, q_gain, kv_gain, cos, sa, sb)

            def q_epi(accs, ex, orefs, ids):
                orefs[0][...] = (accs[0] * scale).astype(BF)

            def qr_epi(accs, ex, orefs, ids):
                for hh in range(accs[0].shape[1] // HEAD_PAD):
                    sl = slice(hh * HEAD_PAD, (hh + 1) * HEAD_PAD)
                    orefs[0][:, sl] = (_rope(accs[0][:, sl], *ex) * scale).astype(BF)

            q_all = _mm_halves(f"l{i}_mla_q", (lat, 0), wq_t, BF, lambda half: (q_epi, qr_epi)[half], extras=(cos, sa, sb))
            kv_all = _mm_halves(f"l{i}_mla_kv", (lat, 1), wkv_t, BF)
            o, lse = _attn_fwd(f"l{i}_attn", q_all, kv_all, kr)
            xs = _mm2(f"l{i}_mla_out", o, w_o_nat, False, False, F32, res=xs)
            sm = (x_mix, h, proj, lat, kr, q_all, kv_all, o, lse)
        fetch(f"ffn2_{i}", xs)
        xs, s2 = _ffn_fwd(f"l{i}_ffn2", xs, ln_ffn2[i:i + 1], gathered[("ffn2_w_in", i)],
                          lambda after, i=i: gathered[("ffn2_w_out", i)])
        saved.append((s1, sm, s2))

    loss_row, d, d_bf, g_ln_final = _final_loss("final_loss", xs, ln_final[None, :], loss_target[0])
    loss = lax.psum(loss_row[0, 0], ("x", "y", "c"))

    sent = []

    def send(tag, keys, grads):
        st = _xstart(f"scatter_start_{tag}", grads, False)
        sent.append((tag, keys, st))
        return st["token"]

    g_ln1, g_ln2, g_lnm = [None] * L, [None] * L, [None] * L
    small_g = {}
    for i in reversed(range(L)):
        s1, sm, s2 = saved[i]
        d, d_bf, g_ln2[i] = _ffn_bwd(
            f"l{i}_ffn2", d, d_bf, s2, ln_ffn2[i:i + 1], gathered[("ffn2_w_in", i)], gathered[("ffn2_w_out", i)],
            lambda g, i=i: send(f"l{i}_ffn2_out", [("ffn2_w_out", i)], [g]),
            lambda g, i=i: send(f"l{i}_ffn2_in", [("ffn2_w_in", i)], [g]))
        j = i // 2
        if i % 2 == 0:
            x_mix, h, puv, gated = sm
            dgated = _rows_dact(f"l{i}_sgu_dgated", d_bf, gathered[("sgu_w_out", j)], [], [_sd(gated.shape, BF)], _store())[0]
            g_so = _rows_wgrad(f"l{i}_sgu_wgrad_out", gated, d_bf, None)
            token_out = send(f"l{i}_sgu_out", [("sgu_w_out", j)], [g_so])
            dpuv, dgain, dbias, dwsp, dbsp = _sgu_mid_bwd(f"l{i}_sgu_mid_bwd", puv, dgated, sgu_v_gain, sgu_v_bias,
                                                          w_sp, b_sp)
            small_g.update(sgu_v_gain=dgain, sgu_v_bias=dbias, sgu_w_spatial=dwsp[None], sgu_b_spatial=dbsp[None, :, :, 0])
            g_si = _cols_wgrad(f"l{i}_sgu_wgrad_in", h, dpuv, deps=(token_out,))
            token = send(f"l{i}_sgu_in", [("sgu_w_in", j)], [g_si])
            dh = _cols_dh(f"l{i}_sgu_dh", dpuv, gathered[("sgu_w_in", j)], deps=(token,))
        else:
            x_mix, h, proj, lat, kr, q_all, kv_all, o, lse = sm
            w_in_pad, wq_t, wkv_t, w_o_nat = mla_w[i]
            t = _attn_tile(S)
            do = _mm2(f"l{i}_mla_do", d_bf, w_o_nat, False, True, F32)
            g_wo = _mm2(f"l{i}_mla_wgrad_out", o, d_bf, True, False, BF)
            dq_all, delta = _attn_dq(f"l{i}_attn_dq", q_all, kv_all, kr, do, o, lse, cos, sa, sb, scale)
            lse_row = lse[:, :, 0].reshape(H, S // t, 1, t)
            delta_row = delta[:, :, 0].reshape(H, S // t, 1, t)
            dkv_all, dkr_heads = _attn_dkv(f"l{i}_attn_dkv", q_all, kv_all, kr, do, lse_row, delta_row)
            dqn = _mm2(f"l{i}_mla_dqn", (dq_all, 0), (wq_t, 0), False, True, F32)
            dqn = _mm2(f"l{i}_mla_dqn2", (dq_all, 1), (wq_t, 1), False, True, F32, res=dqn)
            dkvn = _mm2(f"l{i}_mla_dkvn", (dkv_all, 0), (wkv_t, 0), False, True, F32)
            dkvn = _mm2(f"l{i}_mla_dkvn2", (dkv_all, 1), (wkv_t, 1), False, True, F32, res=dkvn)
            g_wq = [_mm2(f"l{i}_mla_wgrad_q{t2}", (lat, 0), (dq_all, t2), True, False, BF) for t2 in range(2)]
            g_wkv = [_mm2(f"l{i}_mla_wgrad_kv{t2}", (lat, 1), (dkv_all, t2), True, False, BF) for t2 in range(2)]
            dproj, g_qn, g_kvn = _mla_mid_bwd(f"l{i}_mla_mid_bwd", proj, dqn, dkvn, dkr_heads, q_gain, kv_gain, cos, sa, sb)
            g_win = _mm2(f"l{i}_mla_wgrad_in", h, dproj, True, False, BF, tn_cands=(384, 128))
            n_in = Q_LORA + KV_LORA + QK_ROPE
            gq_nat = jnp.concatenate([g_wq[0].reshape(Q_LORA, H, HEAD_PAD),
                                      g_wq[1].reshape(Q_LORA, H, HEAD_PAD)[:, :, :QK_ROPE]], axis=2)
            gkv_nat = jnp.concatenate([g_wkv[0].reshape(KV_LORA, H, HEAD_PAD), g_wkv[1].reshape(KV_LORA, H, HEAD_PAD)], axis=2)
            token = send(f"l{i}_mla", [(k, j) for k in mla_names],
                         [g_win[:, :n_in].reshape(N_DEV, D // N_DEV, n_in),
                          jnp.transpose(gq_nat.reshape(Q_LORA, N_DEV, -1), (1, 0, 2)),
                          jnp.transpose(gkv_nat.reshape(KV_LORA, N_DEV, -1), (1, 0, 2)),
                          g_wo.reshape(N_DEV, H * V_DIM // N_DEV, D)])
            small_g.update(mla_q_norm=g_qn, mla_kv_norm=g_kvn)
            dh = _mm2(f"l{i}_mla_dh", dproj, w_in_pad, False, True, F32, tn_cands=(512, 256, 128), deps=(token,))
        d, d_bf, g_lnm[i] = _rms_bwd(f"l{i}_mix_dnorm", dh, x_mix, ln_mix[i:i + 1], d)
        d, d_bf, g_ln1[i] = _ffn_bwd(
            f"l{i}_ffn1", d, d_bf, s1, ln_ffn1[i:i + 1], gathered[("ffn1_w_in", i)], gathered[("ffn1_w_out", i)],
            lambda g, i=i: send(f"l{i}_ffn1_out", [("ffn1_w_out", i)], [g]),
            lambda g, i=i: send(f"l{i}_ffn1_in", [("ffn1_w_in", i)], [g]))
    grad_x = d[None]

    landed, partial = {}, {}
    for tag, keys, st in sent:
        fulls, lands = _xwait(f"scatter_wait_{tag}", st, d, False)
        for k, full, land in zip(keys, fulls, lands):
            partial[k], landed[k] = full, land
    big_out = {}
    for k in names:
        w = big[k]
        rc = (math.prod(w.shape[1:-1]), w.shape[-1])
        flat = (w.shape[0] * rc[0], rc[1])
        lands = [landed[(k, l)].reshape((N_DEV - 1,) + rc) for l in range(w.shape[0])]
        fulls = [partial[(k, l)].reshape((N_DEV,) + rc) for l in range(w.shape[0])]
        res = _adamw_sharded(f"adamw_{k}", lands, fulls, me_arr, w.reshape(flat), big_m[k].reshape(flat),
                             big_v[k].reshape(flat))
        big_out[k] = [r.reshape(w.shape) for r in res]

    small_g.update(ln_ffn1=jnp.concatenate(g_ln1), ln_mix=jnp.concatenate(g_lnm), ln_ffn2=jnp.concatenate(g_ln2),
                   ln_final=g_ln_final[0])
    small_names = ["ln_ffn1", "ln_mix", "ln_ffn2", "sgu_v_gain", "sgu_v_bias", "sgu_w_spatial", "sgu_b_spatial",
                   "ln_final", "mla_q_norm", "mla_kv_norm"]
    summed = _unpack(_all_reduce_small("small_grads_all_reduce", _pack([small_g[k] for k in small_names])),
                     [small_g[k] for k in small_names])
    small_grad = dict(zip(small_names, summed))
    for k in ("mla_q_norm", "mla_kv_norm"):
        small_grad[k] = lax.dynamic_slice(small_grad[k], (0, me * nq_sh), (1, nq_sh))
    small_w = dict(ln_ffn1=ln_ffn1, ln_mix=ln_mix, ln_ffn2=ln_ffn2, sgu_v_gain=sgu_v_gain, sgu_v_bias=sgu_v_bias,
                   sgu_w_spatial=sgu_w_spatial, sgu_b_spatial=sgu_b_spatial, ln_final=ln_final, mla_q_norm=mla_q_norm,
                   mla_kv_norm=mla_kv_norm)
    small_m = dict(ln_ffn1=m_ln_ffn1, ln_mix=m_ln_mix, ln_ffn2=m_ln_ffn2, sgu_v_gain=m_sgu_v_gain, sgu_v_bias=m_sgu_v_bias,
                   sgu_w_spatial=m_sgu_w_spatial, sgu_b_spatial=m_sgu_b_spatial, ln_final=m_ln_final,
                   mla_q_norm=m_mla_q_norm, mla_kv_norm=m_mla_kv_norm)
    small_v = dict(ln_ffn1=v_ln_ffn1, ln_mix=v_ln_mix, ln_ffn2=v_ln_ffn2, sgu_v_gain=v_sgu_v_gain, sgu_v_bias=v_sgu_v_bias,
                   sgu_w_spatial=v_sgu_w_spatial, sgu_b_spatial=v_sgu_b_spatial, ln_final=v_ln_final,
                   mla_q_norm=v_mla_q_norm, mla_kv_norm=v_mla_kv_norm)
    like = [small_w[k] for k in small_names]
    packed = _adamw_packed("adamw_small", _pack(like), _pack([small_grad[k] for k in small_names]),
                           _pack([small_m[k] for k in small_names]), _pack([small_v[k] for k in small_names]))
    small_out = {}
    unpacked = [_unpack(p, like) for p in packed]
    for idx, k in enumerate(small_names):
        small_out[k] = [small_grad[k].reshape(small_w[k].shape)] + [u[idx] for u in unpacked]

    order = ["ln_ffn1", "ffn1_w_in", "ffn1_w_out", "ln_mix", "ln_ffn2", "ffn2_w_in", "ffn2_w_out", "sgu_w_in",
             "sgu_v_gain", "sgu_v_bias", "sgu_w_spatial", "sgu_b_spatial", "sgu_w_out", "mla_w_in", "mla_q_norm",
             "mla_w_q_up", "mla_kv_norm", "mla_w_kv_up", "mla_w_out", "ln_final"]
    res = {k: (big_out[k] if k in big_out else small_out[k]) for k in order}
    outs = [loss, grad_x]
    for part in range(4):
        outs.extend(res[k][part] for k in order)
    return tuple(outs)
```
